```python
import math
import jax, jax.numpy as jnp
from jax import lax
import numpy as np

D_MODEL = 1024
BATCH = 2
SEQ = 8192
DEPTH = 1
DEC_BATCH = 4
DEC_SEQ = 4096
PAST_LEN = 128

EPS = 1e-6
HEAD_DIM = 64
ATTN_GROUPS = ((128, 1), (512, 4), (2048, 16))
N_GROUPS = 3
HEADS_PER_GROUP = D_MODEL // 128
N_ATTN_HEADS = N_GROUPS * HEADS_PER_GROUP
ATTN_QKV_WIDTH = N_ATTN_HEADS * HEAD_DIM
ATTN_WIDTH = HEADS_PER_GROUP * HEAD_DIM
HYENA_WIDTH = D_MODEL
HYENA_ORDER = 2
SHORT_CONV = 3
FILTER_EMB = 33
N_BANDS = (FILTER_EMB - 1) // 2
FILTER_HIDDEN = 64
DECAY_TARGET = 1e-2
FAST_DECAY_PCT = 0.3
SLOW_DECAY_PCT = 1.5
NUM_BUCKETS = 32
MAX_DISTANCE = 1024
NEG_INF = -1e30
IN_SIZES = (ATTN_QKV_WIDTH, ATTN_QKV_WIDTH, ATTN_QKV_WIDTH, ATTN_WIDTH, 3 * HYENA_WIDTH, HYENA_WIDTH, 2 * D_MODEL)
IN_WIDTH = 3 * ATTN_QKV_WIDTH + ATTN_WIDTH + 4 * HYENA_WIDTH + 2 * D_MODEL

kernel_name = 'hybrid_dilated_attn_hyena_encoder'


def _split_points():
    return [int(c) for c in np.cumsum(IN_SIZES)[:-1]]


def _rms_norm(x, g):
    xf = x.astype(jnp.float32)
    y = xf * lax.rsqrt(jnp.mean(xf * xf, axis=-1, keepdims=True) + EPS)
    return (y * g.astype(jnp.float32)).astype(x.dtype)


def _t5_bucket(rel):
    half = NUM_BUCKETS // 2
    max_exact = half // 2
    n = jnp.abs(rel)
    nf = jnp.maximum(n, 1).astype(jnp.float32)
    large = max_exact + (jnp.log(nf / max_exact) / math.log(MAX_DISTANCE / max_exact) * (half - max_exact)).astype(jnp.int32)
    large = jnp.minimum(large, half - 1)
    return jnp.where(rel > 0, half, 0) + jnp.where(n < max_exact, n, large)


def _dilated_window_attention(q, k, v, bias_table, window, dilation):
    b, s, h, hd = q.shape
    r = dilation
    blk = window // (2 * dilation)
    lr = s // r
    nb = -(-lr // blk)
    lp = nb * blk

    def to_sub(t):
        t = t.reshape(b, lr, r, h, hd).transpose(0, 2, 1, 3, 4)
        return jnp.pad(t, ((0, 0), (0, 0), (0, lp - lr), (0, 0), (0, 0)))

    def neighbours(t):
        t = jnp.pad(t, ((0, 0), (0, 0), (blk, blk), (0, 0), (0, 0))).reshape(b, r, nb + 2, blk, h, hd)
        return jnp.concatenate([t[:, :, :-2], t[:, :, 1:-1], t[:, :, 2:]], axis=3)

    qb = to_sub(q).reshape(b, r, nb, blk, h, hd).astype(jnp.float32)
    kb = neighbours(to_sub(k)).astype(jnp.float32)
    vb = neighbours(to_sub(v)).astype(jnp.float32)
    qi = jnp.arange(blk)[:, None]
    kj = jnp.arange(3 * blk)[None, :] - blk
    delta = kj - qi
    bias = bias_table.astype(jnp.float32)[_t5_bucket(delta * r)].transpose(2, 0, 1)
    key_sub = jnp.arange(nb)[:, None, None] * blk + kj[None]
    valid = (jnp.abs(delta) <= blk)[None] & (key_sub >= 0) & (key_sub < lr)
    logits = jnp.einsum('brnqhd,brnkhd->brnhqk', qb, kb) * (hd ** -0.5) + bias
    logits = jnp.where(valid[:, None], logits, NEG_INF)
    m = jnp.max(logits, axis=-1, keepdims=True)
    p = jnp.exp(logits - m)
    den = jnp.sum(p, axis=-1)
    o = jnp.einsum('brnhqk,brnkhd->brnqhd', p, vb) / jnp.swapaxes(den, -1, -2)[..., None]
    lse = m[..., 0] + jnp.log(den)
    o = o.reshape(b, r, lp, h, hd)[:, :, :lr].transpose(0, 2, 1, 3, 4).reshape(b, s, h, hd)
    lse = jnp.swapaxes(lse, -1, -2).reshape(b, r, lp, h)[:, :, :lr].transpose(0, 2, 1, 3).reshape(b, s, h)
    return o, lse


def _hyena_filters(length, w1, b1, w2, b2, w3, b3, w4, freq):
    f32 = jnp.float32
    t = jnp.linspace(0.0, 1.0, length, dtype=f32)[:, None]
    ang = (2.0 * math.pi / length) * jnp.arange(length, dtype=f32)[:, None]
    bands = jnp.linspace(1e-4, N_BANDS - 1, N_BANDS, dtype=f32)[None]
    z = jnp.concatenate([t, jnp.cos(ang * bands), -jnp.sin(ang * bands)], axis=-1)
    fr = freq.astype(f32)
    hdn = jnp.sin(fr * (z @ w1.astype(f32) + b1.astype(f32)))
    hdn = jnp.sin(fr * (hdn @ w2.astype(f32) + b2.astype(f32)))
    hdn = jnp.sin(fr * (hdn @ w3.astype(f32) + b3.astype(f32)))
    filt = hdn @ w4.astype(f32)
    max_decay = math.log(DECAY_TARGET) / FAST_DECAY_PCT
    min_decay = math.log(DECAY_TARGET) / SLOW_DECAY_PCT
    deltas = jnp.abs(jnp.linspace(min_decay, max_decay, HYENA_WIDTH, dtype=f32))
    decay = jnp.exp(-t * deltas[None])
    return filt.reshape(length, HYENA_ORDER, 2, HYENA_WIDTH) * decay[:, None, None]


def _long_conv(z, h_fwd, h_bwd, skip):
    length = z.shape[1]
    c = z.shape[-1]
    kcirc = jnp.concatenate([h_fwd, jnp.zeros((1, c), jnp.float32), h_bwd[:0:-1]], axis=0)
    zf = z.astype(jnp.float32)
    spec = jnp.fft.rfft(zf, n=2 * length, axis=1) * jnp.fft.rfft(kcirc, axis=0)[None]
    y = jnp.fft.irfft(spec, n=2 * length, axis=1)[:, :length]
    return y + zf * skip.astype(jnp.float32)


def _layer(x, rel_bias, pre_g, post_g, w_in, conv_w, conv_b, fw1, fb1, fw2, fb2, fw3, fb3, fw4, ffreq, skip, w_br_a, w_br_h, w_out):
    b, s, _ = x.shape
    f32 = jnp.float32
    hn = _rms_norm(x, pre_g)
    proj = jnp.einsum('bsd,de->bse', hn, w_in)
    q, k, v, gate_a, u_hy, gate_h, merge = jnp.split(proj, _split_points(), axis=-1)
    q = q.reshape(b, s, N_GROUPS, HEADS_PER_GROUP, HEAD_DIM)
    k = k.reshape(b, s, N_GROUPS, HEADS_PER_GROUP, HEAD_DIM)
    v = v.reshape(b, s, N_GROUPS, HEADS_PER_GROUP, HEAD_DIM)

    outs, lses = [], []
    for g, (window, dilation) in enumerate(ATTN_GROUPS):
        o_g, lse_g = _dilated_window_attention(q[:, :, g], k[:, :, g], v[:, :, g], rel_bias[:, g * HEADS_PER_GROUP:(g + 1) * HEADS_PER_GROUP], window, dilation)
        outs.append(o_g)
        lses.append(lse_g)
    wts = jax.nn.softmax(jnp.stack(lses), axis=0)
    attn = jnp.einsum('gbsh,gbshd->bshd', wts, jnp.stack(outs)).reshape(b, s, ATTN_WIDTH).astype(x.dtype)
    attn_branch = jnp.einsum('bse,ed->bsd', attn * jax.nn.silu(gate_a), w_br_a)

    up = jnp.pad(u_hy, ((0, 0), (1, 1), (0, 0)))
    u_c = up[:, :-2] * conv_w[0] + up[:, 1:-1] * conv_w[1] + up[:, 2:] * conv_w[2] + conv_b
    hv, hx1, hx2 = jnp.split(u_c, 3, axis=-1)
    filt = _hyena_filters(s, fw1, fb1, fw2, fb2, fw3, fb3, fw4, ffreq)
    zz = hx1.astype(f32) * _long_conv(hv, filt[:, 0, 0], filt[:, 0, 1], skip[0])
    zz = hx2.astype(f32) * _long_conv(zz, filt[:, 1, 0], filt[:, 1, 1], skip[1])
    hy = zz.astype(x.dtype)
    hy_branch = jnp.einsum('bse,ed->bsd', hy * jax.nn.silu(gate_h), w_br_h)

    g_a, g_h = jnp.split(merge, 2, axis=-1)
    merged = jax.nn.sigmoid(g_a) * attn_branch + jax.nn.sigmoid(g_h) * hy_branch
    out = jnp.einsum('bsd,de->bse', merged, w_out)
    return x + _rms_norm(out, post_g)


def setup_inputs(seed: int = 0) -> dict:
    key = jax.random.key(seed)
    ks = jax.random.split(key, 20)

    def nrm(k, shape, scale):
        return scale * jax.random.normal(k, shape, jnp.float32)

    D, L = D_MODEL, DEPTH
    return {
        'x_prompt': nrm(ks[0], (BATCH, SEQ, D), 1.0),
        'x_sample': nrm(ks[1], (DEC_BATCH, DEC_SEQ, D), 1.0),
        'rel_bias': nrm(ks[2], (NUM_BUCKETS, N_ATTN_HEADS), 0.2),
        'pre_norm_g': 1.0 + nrm(ks[3], (L, D), 0.1),
        'post_norm_g': 1.0 + nrm(ks[4], (L, D), 0.1),
        'w_in': nrm(ks[5], (L, D, IN_WIDTH), D ** -0.5),
        'conv_w': nrm(ks[6], (L, SHORT_CONV, 3 * HYENA_WIDTH), SHORT_CONV ** -0.5),
        'conv_b': nrm(ks[7], (L, 3 * HYENA_WIDTH), 0.02),
        'filt_w1': nrm(ks[8], (L, FILTER_EMB, FILTER_HIDDEN), FILTER_EMB ** -0.5),
        'filt_b1': nrm(ks[9], (L, FILTER_HIDDEN), 0.1),
        'filt_w2': nrm(ks[10], (L, FILTER_HIDDEN, FILTER_HIDDEN), FILTER_HIDDEN ** -0.5),
        'filt_b2': nrm(ks[11], (L, FILTER_HIDDEN), 0.1),
        'filt_w3': nrm(ks[12], (L, FILTER_HIDDEN, FILTER_HIDDEN), FILTER_HIDDEN ** -0.5),
        'filt_b3': nrm(ks[13], (L, FILTER_HIDDEN), 0.1),
        'filt_w4': nrm(ks[14], (L, FILTER_HIDDEN, HYENA_ORDER * 2 * HYENA_WIDTH), 0.05 * FILTER_HIDDEN ** -0.5),
        'filt_freq': 1.0 + nrm(ks[15], (L, FILTER_HIDDEN), 0.1),
        'hyena_skip': nrm(ks[16], (L, HYENA_ORDER, HYENA_WIDTH), 1.0),
        'w_branch_a': nrm(ks[17], (L, ATTN_WIDTH, D), ATTN_WIDTH ** -0.5),
        'w_branch_h': nrm(ks[18], (L, HYENA_WIDTH, D), HYENA_WIDTH ** -0.5),
        'w_out': nrm(ks[19], (L, D, D), D ** -0.5),
    }


def reference(x_prompt, x_sample, rel_bias, pre_norm_g, post_norm_g, w_in, conv_w, conv_b, filt_w1, filt_b1, filt_w2, filt_b2, filt_w3, filt_b3, filt_w4, filt_freq, hyena_skip, w_branch_a, w_branch_h, w_out):
    def run(x):
        for l in range(DEPTH):
            x = _layer(x, rel_bias, pre_norm_g[l], post_norm_g[l], w_in[l], conv_w[l], conv_b[l], filt_w1[l], filt_b1[l], filt_w2[l], filt_b2[l], filt_w3[l], filt_b3[l], filt_w4[l], filt_freq[l], hyena_skip[l], w_branch_a[l], w_branch_h[l], w_out[l])
        return x

    y_prompt = run(x_prompt)
    y_sample = run(x_sample)
    return (y_prompt, y_sample)
```

```python
import functools
import math

import numpy as np
import jax
import jax.numpy as jnp
from jax import lax
from jax.experimental import pallas as pl
from jax.experimental.pallas import tpu as pltpu

F32 = jnp.float32
BF16 = jnp.bfloat16

D_MODEL = 1024
EPS = 1e-6
HEAD_DIM = 64
ATTN_GROUPS = ((128, 1), (512, 4), (2048, 16))
N_GROUPS = 3
HEADS_PER_GROUP = 8
ATTN_WIDTH = HEADS_PER_GROUP * HEAD_DIM
HYENA_WIDTH = 1024
FILTER_EMB = 33
N_BANDS = 16
FILTER_HIDDEN = 64
NUM_BUCKETS = 32
MAX_DISTANCE = 1024
NEG_INF = -1e30
IN_WIDTH = 11264
Q_OFF, K_OFF, V_OFF, GA_OFF, U_OFF, GH_OFF, MG_OFF = 0, 1536, 3072, 4608, 5120, 8192, 9216

WIN = 64
LANES = 128
DFT_N2 = 128
VMEM_LIMIT = 48 * 1024 * 1024


def _cparams(sem):
    return pltpu.CompilerParams(dimension_semantics=sem, vmem_limit_bytes=VMEM_LIMIT)


def _inproj_kernel(x_ref, g_ref, w_ref, o_ref, hn_ref):
    @pl.when(pl.program_id(1) == 0)
    def _():
        x = x_ref[...]
        ms = jnp.mean(x * x, axis=-1, keepdims=True)
        hn_ref[...] = (x * lax.rsqrt(ms + EPS) * g_ref[...]).astype(BF16)

    o_ref[...] = jnp.dot(hn_ref[...], w_ref[...], preferred_element_type=F32).astype(o_ref.dtype)


def _inproj(x2d, g, w_bf16, tm=1024, tn=1024):
    rows = x2d.shape[0]
    tm = min(tm, rows)
    return pl.pallas_call(
        _inproj_kernel,
        grid=(rows // tm, IN_WIDTH // tn),
        in_specs=[
            pl.BlockSpec((tm, D_MODEL), lambda i, j: (i, 0)),
            pl.BlockSpec((1, D_MODEL), lambda i, j: (0, 0)),
            pl.BlockSpec((D_MODEL, tn), lambda i, j: (0, j)),
        ],
        out_specs=pl.BlockSpec((tm, tn), lambda i, j: (i, j)),
        out_shape=jax.ShapeDtypeStruct((rows, IN_WIDTH), BF16),
        scratch_shapes=[pltpu.VMEM((tm, D_MODEL), BF16)],
        compiler_params=_cparams(("parallel", "arbitrary")),
        name="inproj",
    )(x2d, g, w_bf16)


def _t5_bucket_np(rel):
    half = NUM_BUCKETS // 2
    max_exact = half // 2
    n = np.abs(rel)
    nf = np.maximum(n, 1).astype(np.float64)
    large = max_exact + (np.log(nf / max_exact) / math.log(MAX_DISTANCE / max_exact) * (half - max_exact)).astype(np.int64)
    large = np.minimum(large, half - 1)
    return np.where(rel > 0, half, 0) + np.where(n < max_exact, n, large)


def _bias_mask(rel_bias_g, dilation):
    qi = np.arange(2 * WIN)[:, None]
    kj = np.arange(4 * WIN)[None, :] - WIN
    delta = kj - qi
    bucket = _t5_bucket_np(delta * dilation).astype(np.int32)
    bias = jnp.transpose(rel_bias_g.astype(F32)[bucket], (2, 0, 1))
    return jnp.where(jnp.asarray(np.abs(delta) <= WIN)[None], bias, NEG_INF)


def _attn_kernel(q_ref, kp_ref, kc_ref, kn_ref, vp_ref, vc_ref, vn_ref, bm_ref, o_ref, l_ref,
                 kbuf, vbuf, *, tq, sub_len):
    t = pl.program_id(2)
    kbuf[0:WIN] = kp_ref[...]
    kbuf[WIN:WIN + tq] = kc_ref[...]
    kbuf[WIN + tq:] = kn_ref[...]
    vbuf[0:WIN] = vp_ref[...]
    vbuf[WIN:WIN + tq] = vc_ref[...]
    vbuf[WIN + tq:] = vn_ref[...]
    lane = lax.broadcasted_iota(jnp.int32, (1, LANES), 1)
    is_lo = lane < HEAD_DIM
    kcol = lax.broadcasted_iota(jnp.int32, (1, 4 * WIN), 1)
    qt = 2 * WIN

    def body(i, carry):
        r0 = pl.multiple_of(i * qt, qt)
        kpos = t * tq + r0 - WIN + kcol
        pen = jnp.where((kpos >= 0) & (kpos < sub_len), 0.0, NEG_INF).astype(F32)
        for hp in range(ATTN_WIDTH // LANES):
            cs = slice(hp * LANES, (hp + 1) * LANES)
            q = q_ref[pl.ds(r0, qt), cs] * jnp.asarray(HEAD_DIM ** -0.5, BF16)
            k = kbuf[pl.ds(r0, 2 * qt), cs]
            v = vbuf[pl.ds(r0, 2 * qt), cs]
            outs, lses = [], []
            for hh in range(2):
                sel = is_lo if hh == 0 else jnp.logical_not(is_lo)
                qm = jnp.where(sel, q, jnp.zeros_like(q))
                s = lax.dot_general(qm, k, (((1,), (1,)), ((), ())), preferred_element_type=F32)
                s = s + bm_ref[2 * hp + hh] + pen
                m = jnp.max(s, axis=-1, keepdims=True)
                p = jnp.exp(s - m)
                den = jnp.sum(p, axis=-1, keepdims=True)
                pv = jnp.dot(p.astype(BF16), v, preferred_element_type=F32)
                outs.append(pv / den)
                lses.append(m + jnp.log(den))
            o_ref[pl.ds(r0, qt), cs] = jnp.where(is_lo, outs[0], outs[1])
            l_ref[pl.ds(r0, qt), cs] = jnp.where(is_lo, lses[0], lses[1])
        return carry

    lax.fori_loop(0, tq // qt, body, 0)


def _attention_group(proj, rel_bias, g, batch, seq):
    _, dil = ATTN_GROUPS[g]
    sub_len = seq // dil
    tq = min(512, sub_len)
    nt = sub_len // tq
    nblk = IN_WIDTH // ATTN_WIDTH
    pv = proj.reshape(batch, sub_len, dil * IN_WIDTH)
    bm = _bias_mask(rel_bias[:, g * HEADS_PER_GROUP:(g + 1) * HEADS_PER_GROUP], dil)
    qb, kb, vb = (Q_OFF // ATTN_WIDTH + g, K_OFF // ATTN_WIDTH + g, V_OFF // ATTN_WIDTH + g)
    hb = tq // WIN
    nhb = sub_len // WIN

    def cur(off):
        return pl.BlockSpec((None, tq, ATTN_WIDTH), lambda b, c, t: (b, t, c * nblk + off))

    def prev(off):
        return pl.BlockSpec((None, WIN, ATTN_WIDTH),
                            lambda b, c, t: (b, jnp.maximum(t * hb - 1, 0), c * nblk + off))

    def nxt(off):
        return pl.BlockSpec((None, WIN, ATTN_WIDTH),
                            lambda b, c, t: (b, jnp.minimum((t + 1) * hb, nhb - 1), c * nblk + off))

    out_spec = pl.BlockSpec((None, tq, ATTN_WIDTH), lambda b, c, t: (b, t, c))
    o, l = pl.pallas_call(
        functools.partial(_attn_kernel, tq=tq, sub_len=sub_len),
        grid=(batch, dil, nt),
        in_specs=[cur(qb), prev(kb), cur(kb), nxt(kb), prev(vb), cur(vb), nxt(vb),
                  pl.BlockSpec((HEADS_PER_GROUP, 2 * WIN, 4 * WIN), lambda b, c, t: (0, 0, 0))],
        out_specs=[out_spec, out_spec],
        out_shape=[jax.ShapeDtypeStruct((batch, sub_len, dil * ATTN_WIDTH), F32)] * 2,
        scratch_shapes=[pltpu.VMEM((tq + 2 * WIN, ATTN_WIDTH), BF16)] * 2,
        compiler_params=_cparams(("parallel", "parallel", "arbitrary")),
        name=f"attn_g{g}",
    )(pv, pv, pv, pv, pv, pv, pv, bm)
    return o.reshape(batch * seq, ATTN_WIDTH), l.reshape(batch * seq, ATTN_WIDTH)


def _shortconv_kernel(p_ref, c_ref, n_ref, w_ref, b_ref, o_ref, *, rows, halo):
    t = pl.program_id(1)
    nt = pl.num_programs(1)
    x = c_ref[...].astype(F32)
    prev_row = jnp.where(t > 0, p_ref[...].astype(F32)[halo - 1:halo, :], 0.0)
    next_row = jnp.where(t < nt - 1, n_ref[...].astype(F32)[0:1, :], 0.0)
    row = lax.broadcasted_iota(jnp.int32, (rows, 1), 0)
    up = jnp.where(row == 0, prev_row, pltpu.roll(x, 1, axis=0))
    dn = jnp.where(row == rows - 1, next_row, pltpu.roll(x, rows - 1, axis=0))
    o_ref[...] = up * w_ref[0:1, :] + x * w_ref[1:2, :] + dn * w_ref[2:3, :] + b_ref[...]


def _shortconv(proj, conv_w, conv_b, batch, seq, rows=512):
    rows = min(rows, seq)
    halo = 16
    pv = proj.reshape(batch, seq, IN_WIDTH)
    cb = U_OFF // HYENA_WIDTH
    hb = rows // halo
    nhb = seq // halo
    return pl.pallas_call(
        functools.partial(_shortconv_kernel, rows=rows, halo=halo),
        grid=(batch, seq // rows, 3),
        in_specs=[
            pl.BlockSpec((None, halo, HYENA_WIDTH), lambda b, t, j: (b, jnp.maximum(t * hb - 1, 0), cb + j)),
            pl.BlockSpec((None, rows, HYENA_WIDTH), lambda b, t, j: (b, t, cb + j)),
            pl.BlockSpec((None, halo, HYENA_WIDTH), lambda b, t, j: (b, jnp.minimum((t + 1) * hb, nhb - 1), cb + j)),
            pl.BlockSpec((3, HYENA_WIDTH), lambda b, t, j: (0, j)),
            pl.BlockSpec((1, HYENA_WIDTH), lambda b, t, j: (0, j)),
        ],
        out_specs=pl.BlockSpec((None, None, rows, HYENA_WIDTH), lambda b, t, j: (j, b, t, 0)),
        out_shape=jax.ShapeDtypeStruct((3, batch, seq, HYENA_WIDTH), F32),
        compiler_params=_cparams(("parallel", "parallel", "arbitrary")),
        name="shortconv",
    )(pv, pv, pv, conv_w, conv_b.reshape(1, -1))


def _filter_features(length):
    t = np.linspace(0.0, 1.0, length)[:, None]
    ang = (2.0 * math.pi / length) * np.arange(length, dtype=np.float64)[:, None]
    bands = np.linspace(1e-4, N_BANDS - 1, N_BANDS)[None]
    z = np.concatenate([t, np.cos(ang * bands), -np.sin(ang * bands)], axis=-1)
    zp = np.zeros((length, LANES), np.float32)
    zp[:, :FILTER_EMB] = z
    return zp


def _decay_rates():
    max_decay = math.log(1e-2) / 0.3
    min_decay = math.log(1e-2) / 1.5
    return np.abs(np.linspace(min_decay, max_decay, HYENA_WIDTH)).astype(np.float32)[None]


def _filter_kernel(z_ref, w1_ref, b1_ref, w2_ref, b2_ref, w3_ref, b3_ref, w4_ref, fr_ref, dl_ref, o_ref, *, rows):
    hi = lax.Precision.HIGHEST
    z = z_ref[...]
    fr = fr_ref[...]
    h = jnp.sin(fr * (jnp.dot(z, w1_ref[...], precision=hi, preferred_element_type=F32) + b1_ref[...]))
    h = jnp.sin(fr * (jnp.dot(h, w2_ref[...], precision=hi, preferred_element_type=F32) + b2_ref[...]))
    h = jnp.sin(fr * (jnp.dot(h, w3_ref[...], precision=hi, preferred_element_type=F32) + b3_ref[...]))
    filt = jnp.dot(h, w4_ref[...], precision=hi, preferred_element_type=F32)
    decay = jnp.exp(-z[:, 0:1] * dl_ref[...])
    row = pl.program_id(0) * rows + lax.broadcasted_iota(jnp.int32, (rows, 1), 0)
    for j in range(4):
        cs = slice(j * HYENA_WIDTH, (j + 1) * HYENA_WIDTH)
        val = filt[:, cs] * decay
        if j % 2 == 1:
            val = jnp.where(row == 0, 0.0, val)
        o_ref[:, cs] = val


def _filters(length, fw1, fb1, fw2, fb2, fw3, fb3, fw4, ffreq, rows=256):
    rows = min(rows, length)
    zfeat = jnp.asarray(_filter_features(length))
    w1p = jnp.zeros((LANES, FILTER_HIDDEN), F32).at[:FILTER_EMB].set(fw1.astype(F32))
    full = lambda shape: pl.BlockSpec(shape, lambda i: (0,) * len(shape))
    return pl.pallas_call(
        functools.partial(_filter_kernel, rows=rows),
        grid=(length // rows,),
        in_specs=[pl.BlockSpec((rows, LANES), lambda i: (i, 0)),
                  full((LANES, FILTER_HIDDEN)), full((1, FILTER_HIDDEN)),
                  full((FILTER_HIDDEN, FILTER_HIDDEN)), full((1, FILTER_HIDDEN)),
                  full((FILTER_HIDDEN, FILTER_HIDDEN)), full((1, FILTER_HIDDEN)),
                  full((FILTER_HIDDEN, 4 * HYENA_WIDTH)), full((1, FILTER_HIDDEN)),
                  full((1, HYENA_WIDTH))],
        out_specs=pl.BlockSpec((rows, 4 * HYENA_WIDTH), lambda i: (i, 0)),
        out_shape=jax.ShapeDtypeStruct((length, 4 * HYENA_WIDTH), F32),
        compiler_params=_cparams(("parallel",)),
        name="hyena_filters",
    )(zfeat, w1p, fb1.reshape(1, -1), fw2, fb2.reshape(1, -1), fw3, fb3.reshape(1, -1), fw4,
      ffreq.reshape(1, -1), jnp.asarray(_decay_rates()))


def _dft_tables(length):
    n = 2 * length
    n1 = n // DFT_N2
    h1 = n1 // 2
    kk = np.arange(h1)[:, None] + 0.5
    th = 2.0 * math.pi * kk * np.arange(h1)[None, :] / n1
    fa = np.concatenate([np.cos(th), -np.sin(th)], axis=0)
    ga = (2.0 / n) * fa.T
    k = np.arange(h1)[:, None, None] + n1 * np.arange(DFT_N2)[None, :, None] + 0.5
    ph = 2.0 * math.pi * k * np.arange(DFT_N2)[None, None, :] / n
    f = lambda a: jnp.asarray(a.astype(np.float32))
    er, ei = f(np.cos(ph)), f(-np.sin(ph))
    m = jnp.concatenate([jnp.concatenate([er, -ei], axis=2), jnp.concatenate([ei, er], axis=2)], axis=1)
    mt = jnp.transpose(m, (0, 2, 1))
    return f(fa), f(ga), m, mt


def _split(a):
    hi = a.astype(BF16)
    lo = (a - hi.astype(F32)).astype(BF16)
    return hi, lo


def _dot3(a, b):
    ah, al = _split(a)
    bh, bl = _split(b)
    d = lambda x, y: jnp.dot(x, y, preferred_element_type=F32)
    return d(ah, bh) + (d(al, bh) + d(ah, bl))


def _stage_a_kernel(f_ref, x_ref, o_ref):
    o_ref[...] = _dot3(f_ref[...], x_ref[...])


def _stage_a(fa, x4, sel, tn=8192):
    _, batch, h1, cols = x4.shape
    n1 = fa.shape[0]
    tn = min(tn, cols)
    return pl.pallas_call(
        _stage_a_kernel,
        grid=(batch, cols // tn),
        in_specs=[pl.BlockSpec((n1, h1), lambda b, j: (0, 0)),
                  pl.BlockSpec((None, None, h1, tn), lambda b, j: (sel, b, 0, j))],
        out_specs=pl.BlockSpec((None, n1, tn), lambda b, j: (b, 0, j)),
        out_shape=jax.ShapeDtypeStruct((batch, n1, cols), F32),
        compiler_params=_cparams(("parallel", "parallel")),
        name="dft_stage_a",
    )(fa, x4)


def _spectrum_kernel(m_ref, xf_ref, xb_ref, o_ref):
    m = m_ref[...]
    half = DFT_N2
    zf = _dot3(m, xf_ref[...].reshape(2 * half, -1))
    zb = _dot3(m, xb_ref[...].reshape(2 * half, -1))
    o_ref[0] = zf[:half] + zb[:half]
    o_ref[1] = zf[half:] - zb[half:]


def _filter_spectrum(m_tab, a_filt):
    n1 = a_filt.shape[1]
    h1 = n1 // 2
    c = HYENA_WIDTH
    av = a_filt.reshape(2, h1, DFT_N2, 4 * c)
    return pl.pallas_call(
        _spectrum_kernel,
        grid=(2, h1),
        in_specs=[pl.BlockSpec((None, 2 * DFT_N2, 2 * DFT_N2), lambda o, k: (k, 0, 0)),
                  pl.BlockSpec((2, None, DFT_N2, c), lambda o, k: (0, k, 0, 2 * o)),
                  pl.BlockSpec((2, None, DFT_N2, c), lambda o, k: (0, k, 0, 2 * o + 1))],
        out_specs=pl.BlockSpec((None, None, 2, DFT_N2, c), lambda o, k: (o, k, 0, 0, 0)),
        out_shape=jax.ShapeDtypeStruct((2, h1, 2, DFT_N2, c), F32),
        compiler_params=_cparams(("parallel", "parallel")),
        name="filter_spectrum",
    )(m_tab, av, av)


def _freq_kernel(m_ref, mt_ref, x_ref, k_ref, o_ref):
    half = DFT_N2
    z = _dot3(m_ref[...], x_ref[...].reshape(2 * half, -1))
    zr, zi = z[:half], z[half:]
    kr, ki = k_ref[0], k_ref[1]
    y = jnp.concatenate([zr * kr - zi * ki, zr * ki + zi * kr], axis=0)
    o_ref[...] = _dot3(mt_ref[...], y).reshape(2, half, -1)


def _freq_stage(m_tab, mt_tab, a, kspec, order):
    batch, n1, _ = a.shape
    h1 = n1 // 2
    c = HYENA_WIDTH
    av = a.reshape(batch, 2, h1, DFT_N2, c)
    mspec = pl.BlockSpec((None, 2 * DFT_N2, 2 * DFT_N2), lambda k, b: (k, 0, 0))
    out = pl.pallas_call(
        _freq_kernel,
        grid=(h1, batch),
        in_specs=[mspec, mspec,
                  pl.BlockSpec((None, 2, None, DFT_N2, c), lambda k, b: (b, 0, k, 0, 0)),
                  pl.BlockSpec((None, None, 2, DFT_N2, c), lambda k, b: (order, k, 0, 0, 0))],
        out_specs=pl.BlockSpec((None, 2, None, DFT_N2, c), lambda k, b: (b, 0, k, 0, 0)),
        out_shape=jax.ShapeDtypeStruct((batch, 2, h1, DFT_N2, c), F32),
        compiler_params=_cparams(("parallel", "parallel")),
        name="dft_freq_stage",
    )(m_tab, mt_tab, av, kspec)
    return out.reshape(batch, n1, DFT_N2 * c)


def _stage_inv_kernel(g_ref, bp_ref, z_ref, hx_ref, sk_ref, o_ref):
    y = _dot3(g_ref[...], bp_ref[...])
    o_ref[...] = hx_ref[...] * (y + z_ref[...] * sk_ref[...])


def _stage_inv(ga, bp, z4, zsel, hx4, hxsel, skip_row, tn=8192):
    batch, n1, cols = bp.shape
    h1 = n1 // 2
    tn = min(tn, cols)
    return pl.pallas_call(
        _stage_inv_kernel,
        grid=(batch, cols // tn),
        in_specs=[pl.BlockSpec((h1, n1), lambda b, j: (0, 0)),
                  pl.BlockSpec((None, n1, tn), lambda b, j: (b, 0, j)),
                  pl.BlockSpec((None, None, h1, tn), lambda b, j: (zsel, b, 0, j)),
                  pl.BlockSpec((None, None, h1, tn), lambda b, j: (hxsel, b, 0, j)),
                  pl.BlockSpec((1, tn), lambda b, j: (0, 0))],
        out_specs=pl.BlockSpec((None, h1, tn), lambda b, j: (b, 0, j)),
        out_shape=jax.ShapeDtypeStruct((batch, h1, cols), F32),
        compiler_params=_cparams(("parallel", "parallel")),
        name="dft_stage_inv",
    )(ga, bp, z4, hx4, skip_row)


def _hyena(proj, batch, seq, conv_w, conv_b, fparams, skip):
    c = HYENA_WIDTH
    fa, ga, m_tab, mt_tab = _dft_tables(seq)
    n1 = fa.shape[0]
    h1 = n1 // 2
    tn = min(8192, DFT_N2 * c)
    filt = _filters(seq, *fparams)
    a_f = _stage_a(fa, filt.reshape(1, 1, h1, DFT_N2 * 4 * c), 0)
    kspec = _filter_spectrum(m_tab, a_f)
    u3 = _shortconv(proj, conv_w, conv_b, batch, seq)
    u4 = u3.reshape(3, batch, h1, DFT_N2 * c)
    skip_rows = jnp.tile(skip.astype(F32), (1, tn // c))
    a1 = _stage_a(fa, u4, 0)
    b1 = _freq_stage(m_tab, mt_tab, a1, kspec, 0)
    z1 = _stage_inv(ga, b1, u4, 0, u4, 1, skip_rows[0:1])
    z1v = z1.reshape(1, batch, h1, DFT_N2 * c)
    a2 = _stage_a(fa, z1v, 0)
    b2 = _freq_stage(m_tab, mt_tab, a2, kspec, 1)
    hy = _stage_inv(ga, b2, z1v, 0, u4, 2, skip_rows[1:2])
    return hy.reshape(batch * seq, c)


def _final_kernel(o1_ref, o2_ref, o3_ref, l1_ref, l2_ref, l3_ref, ga_ref, hy_ref, gh_ref, ma_ref, mh_ref,
                  x_ref, wa_ref, wh_ref, wo_ref, pg_ref, y_ref):
    l1, l2, l3 = l1_ref[...], l2_ref[...], l3_ref[...]
    mx = jnp.maximum(jnp.maximum(l1, l2), l3)
    e1, e2, e3 = jnp.exp(l1 - mx), jnp.exp(l2 - mx), jnp.exp(l3 - mx)
    attn = (e1 * o1_ref[...] + e2 * o2_ref[...] + e3 * o3_ref[...]) / (e1 + e2 + e3)
    ga = ga_ref[...].astype(F32)
    a_in = (attn * (ga * jax.nn.sigmoid(ga))).astype(BF16)
    a_br = jnp.dot(a_in, wa_ref[...], preferred_element_type=F32)
    gh = gh_ref[...].astype(F32)
    h_in = (hy_ref[...] * (gh * jax.nn.sigmoid(gh))).astype(BF16)
    h_br = jnp.dot(h_in, wh_ref[...], preferred_element_type=F32)
    merged = jax.nn.sigmoid(ma_ref[...].astype(F32)) * a_br + jax.nn.sigmoid(mh_ref[...].astype(F32)) * h_br
    out = jnp.dot(merged.astype(BF16), wo_ref[...], preferred_element_type=F32)
    ms = jnp.mean(out * out, axis=-1, keepdims=True)
    y_ref[...] = x_ref[...] + out * lax.rsqrt(ms + EPS) * pg_ref[...]


def _final(os_, ls_, proj, hy, x2d, wa, wh, wo, pg, tm=256):
    rows = x2d.shape[0]
    tm = min(tm, rows)
    a512 = lambda off: pl.BlockSpec((tm, ATTN_WIDTH), lambda i: (i, off))
    a1024 = lambda off: pl.BlockSpec((tm, D_MODEL), lambda i: (i, off))
    full = lambda shape: pl.BlockSpec(shape, lambda i: (0, 0))
    return pl.pallas_call(
        _final_kernel,
        grid=(rows // tm,),
        in_specs=[a512(0)] * 6 + [a512(GA_OFF // ATTN_WIDTH), a1024(0), a1024(GH_OFF // D_MODEL),
                                  a1024(MG_OFF // D_MODEL), a1024(MG_OFF // D_MODEL + 1), a1024(0),
                                  full((ATTN_WIDTH, D_MODEL)), full((D_MODEL, D_MODEL)), full((D_MODEL, D_MODEL)),
                                  full((1, D_MODEL))],
        out_specs=pl.BlockSpec((tm, D_MODEL), lambda i: (i, 0)),
        out_shape=jax.ShapeDtypeStruct((rows, D_MODEL), F32),
        compiler_params=_cparams(("parallel",)),
        name="merge_out",
    )(*os_, *ls_, proj, hy, proj, proj, proj, x2d, wa, wh, wo, pg)


def _layer(x, rel_bias, pre_g, post_g, w_in, conv_w, conv_b, fparams, skip, w_br_a, w_br_h, w_out):
    batch, seq, _ = x.shape
    x2d = x.reshape(batch * seq, D_MODEL)
    proj = _inproj(x2d, pre_g.reshape(1, -1).astype(F32), w_in)
    os_, ls_ = [], []
    for g in range(N_GROUPS):
        o, l = _attention_group(proj, rel_bias, g, batch, seq)
        os_.append(o)
        ls_.append(l)
    hy = _hyena(proj, batch, seq, conv_w, conv_b, fparams, skip)
    y = _final(os_, ls_, proj, hy, x2d, w_br_a, w_br_h, w_out, post_g.reshape(1, -1).astype(F32))
    return y.reshape(batch, seq, D_MODEL)


def kernel(x_prompt, x_sample, rel_bias, pre_norm_g, post_norm_g, w_in, conv_w, conv_b, filt_w1, filt_b1, filt_w2, filt_b2, filt_w3, filt_b3, filt_w4, filt_freq, hyena_skip, w_branch_a, w_branch_h, w_out):
    depth = w_in.shape[0]

    def run(x):
        for l in range(depth):
            fparams = (filt_w1[l], filt_b1[l], filt_w2[l], filt_b2[l], filt_w3[l], filt_b3[l], filt_w4[l], filt_freq[l])
            x = _layer(x, rel_bias, pre_norm_g[l], post_norm_g[l], w_in[l].astype(BF16), conv_w[l], conv_b[l],
                       fparams, hyena_skip[l], w_branch_a[l].astype(BF16), w_branch_h[l].astype(BF16),
                       w_out[l].astype(BF16))
        return x

    return (run(x_prompt), run(x_sample))
```

```python
import functools
import math

import numpy as np
import jax
import jax.numpy as jnp
from jax import lax
from jax.experimental import pallas as pl
from jax.experimental.pallas import tpu as pltpu

F32 = jnp.float32
BF16 = jnp.bfloat16

D_MODEL = 1024
EPS = 1e-6
HEAD_DIM = 64
ATTN_GROUPS = ((128, 1), (512, 4), (2048, 16))
N_GROUPS = 3
HEADS_PER_GROUP = 8
ATTN_WIDTH = HEADS_PER_GROUP * HEAD_DIM
HYENA_WIDTH = 1024
FILTER_EMB = 33
N_BANDS = 16
FILTER_HIDDEN = 64
NUM_BUCKETS = 32
MAX_DISTANCE = 1024
NEG_INF = -1e30
QKV_WIDTH = 4608
GA_OFF, U_OFF, GH_OFF, MG_OFF = 4608, 5120, 8192, 9216
REST_WIDTH = 6656
R_U, R_GH, R_MG, R_GA = 0, 3072, 4096, 6144

WIN = 64
LANES = 128
DFT_N2 = 128
VMEM_LIMIT = 48 * 1024 * 1024


def _cparams(sem):
    return pltpu.CompilerParams(dimension_semantics=sem, vmem_limit_bytes=VMEM_LIMIT)


def _inproj_kernel(x_ref, g_ref, w_ref, o_ref, hn_ref, *, slabs):
    @pl.when(pl.program_id(1) == 0)
    def _():
        x = x_ref[...]
        ms = jnp.mean(x * x, axis=-1, keepdims=True)
        hn_ref[...] = (x * lax.rsqrt(ms + EPS) * g_ref[...]).astype(BF16)

    acc = jnp.dot(hn_ref[...], w_ref[...], preferred_element_type=F32)
    if slabs:
        for s in range(slabs):
            o_ref[s] = acc[:, s * LANES:(s + 1) * LANES]
    else:
        o_ref[...] = acc.astype(o_ref.dtype)


def _inproj(x2d, g, w_bf16, tn, slab_out, tm=1024):
    rows = x2d.shape[0]
    width = w_bf16.shape[1]
    tm = min(tm, rows)
    if slab_out:
        ns = tn // LANES
        out_spec = pl.BlockSpec((ns, tm, LANES), lambda i, j: (j, i, 0))
        out_shape = jax.ShapeDtypeStruct((width // LANES, rows, LANES), F32)
    else:
        ns = 0
        out_spec = pl.BlockSpec((tm, tn), lambda i, j: (i, j))
        out_shape = jax.ShapeDtypeStruct((rows, width), BF16)
    return pl.pallas_call(
        functools.partial(_inproj_kernel, slabs=ns),
        grid=(rows // tm, width // tn),
        in_specs=[
            pl.BlockSpec((tm, D_MODEL), lambda i, j: (i, 0)),
            pl.BlockSpec((1, D_MODEL), lambda i, j: (0, 0)),
            pl.BlockSpec((D_MODEL, tn), lambda i, j: (0, j)),
        ],
        out_specs=out_spec,
        out_shape=out_shape,
        scratch_shapes=[pltpu.VMEM((tm, D_MODEL), BF16)],
        compiler_params=_cparams(("parallel", "arbitrary")),
        name="inproj_qkv" if slab_out else "inproj_rest",
    )(x2d, g, w_bf16)


def _t5_bucket_np(rel):
    half = NUM_BUCKETS // 2
    max_exact = half // 2
    n = np.abs(rel)
    nf = np.maximum(n, 1).astype(np.float64)
    large = max_exact + (np.log(nf / max_exact) / math.log(MAX_DISTANCE / max_exact) * (half - max_exact)).astype(np.int64)
    large = np.minimum(large, half - 1)
    return np.where(rel > 0, half, 0) + np.where(n < max_exact, n, large)


def _bias_mask(rel_bias_g, dilation):
    qi = np.arange(2 * WIN)[:, None]
    kj = np.arange(4 * WIN)[None, :] - WIN
    delta = kj - qi
    bucket = _t5_bucket_np(delta * dilation).astype(np.int32)
    onehot = jax.nn.one_hot(jnp.asarray(bucket), NUM_BUCKETS, dtype=F32)
    bias = jnp.einsum('qkb,bh->hqk', onehot, rel_bias_g.astype(F32), precision=lax.Precision.HIGHEST)
    return jnp.where(jnp.asarray(np.abs(delta) <= WIN)[None], bias, NEG_INF)


ATTN_POS_PER_STEP = 4096


def _attn_kernel(q_ref, kp_ref, kc_ref, kn_ref, vp_ref, vc_ref, vn_ref, bm_ref, o_ref, l_ref,
                 kbuf, vbuf, *, tq, dil, sub_len):
    t = pl.program_id(2)
    halo = WIN * dil
    span = tq * dil
    kbuf[0:halo] = kp_ref[...]
    kbuf[halo:halo + span] = kc_ref[...]
    kbuf[halo + span:] = kn_ref[...]
    vbuf[0:halo] = vp_ref[...]
    vbuf[halo:halo + span] = vc_ref[...]
    vbuf[halo + span:] = vn_ref[...]
    lane = lax.broadcasted_iota(jnp.int32, (1, LANES), 1)
    is_lo = lane < HEAD_DIM
    kcol = lax.broadcasted_iota(jnp.int32, (1, 4 * WIN), 1)
    qt = 2 * WIN
    nq = tq // qt

    def tile(c, i):
        row0 = i * (qt * dil) + c
        kpos = t * tq + i * qt - WIN + kcol
        pen = jnp.where((kpos >= 0) & (kpos < sub_len), 0.0, NEG_INF).astype(F32)
        q = (q_ref[pl.ds(row0, qt, stride=dil), :] * (HEAD_DIM ** -0.5)).astype(BF16)
        k = kbuf[pl.ds(row0, 2 * qt, stride=dil), :].astype(BF16)
        v = vbuf[pl.ds(row0, 2 * qt, stride=dil), :].astype(BF16)
        outs, lses = [], []
        for hh in range(2):
            sel = is_lo if hh == 0 else jnp.logical_not(is_lo)
            qm = jnp.where(sel, q, jnp.zeros_like(q))
            s = lax.dot_general(qm, k, (((1,), (1,)), ((), ())), preferred_element_type=F32)
            s = s + bm_ref[hh] + pen
            m = jnp.max(s, axis=-1, keepdims=True)
            p = jnp.exp(s - m)
            den = jnp.sum(p, axis=-1, keepdims=True)
            pv = jnp.dot(p.astype(BF16), v, preferred_element_type=F32)
            outs.append(pv / den)
            lses.append(m + jnp.log(den))
        o_ref[pl.ds(row0, qt, stride=dil), :] = jnp.where(is_lo, outs[0], outs[1])
        l_ref[pl.ds(row0, qt, stride=dil), :] = jnp.where(is_lo, lses[0], lses[1])

    def body(idx, carry):
        tile(idx // nq, idx % nq)
        return carry

    lax.fori_loop(0, dil * nq, body, 0)


def _attention_group(qkv, rel_bias, g, batch, seq):
    _, dil = ATTN_GROUPS[g]
    sub_len = seq // dil
    tq = min(ATTN_POS_PER_STEP // dil, sub_len)
    nt = sub_len // tq
    span = tq * dil
    halo = WIN * dil
    n_hp = ATTN_WIDTH // LANES
    n_slab = N_GROUPS * n_hp
    bm = _bias_mask(rel_bias[:, g * HEADS_PER_GROUP:(g + 1) * HEADS_PER_GROUP], dil)
    hb = tq // WIN
    nhb = sub_len // WIN

    def cur(which):
        return pl.BlockSpec((None, span, LANES), lambda b, hp, t: (which * n_slab + g * n_hp + hp, b * nt + t, 0))

    def prev(which):
        return pl.BlockSpec((None, halo, LANES),
                            lambda b, hp, t: (which * n_slab + g * n_hp + hp, b * nhb + jnp.maximum(t * hb - 1, 0), 0))

    def nxt(which):
        return pl.BlockSpec((None, halo, LANES),
                            lambda b, hp, t: (which * n_slab + g * n_hp + hp, b * nhb + jnp.minimum((t + 1) * hb, nhb - 1), 0))

    out_spec = pl.BlockSpec((None, span, LANES), lambda b, hp, t: (hp, b * nt + t, 0))
    return pl.pallas_call(
        functools.partial(_attn_kernel, tq=tq, dil=dil, sub_len=sub_len),
        grid=(batch, n_hp, nt),
        in_specs=[cur(0), prev(1), cur(1), nxt(1), prev(2), cur(2), nxt(2),
                  pl.BlockSpec((2, 2 * WIN, 4 * WIN), lambda b, hp, t: (hp, 0, 0))],
        out_specs=[out_spec, out_spec],
        out_shape=[jax.ShapeDtypeStruct((n_hp, batch * seq, LANES), F32)] * 2,
        scratch_shapes=[pltpu.VMEM((span + 2 * halo, LANES), F32)] * 2,
        compiler_params=_cparams(("parallel", "parallel", "arbitrary")),
        name=f"attn_g{g}",
    )(qkv, qkv, qkv, qkv, qkv, qkv, qkv, bm)


def _shortconv_kernel(p_ref, c_ref, n_ref, w_ref, b_ref, o_ref, *, rows, halo):
    t = pl.program_id(1)
    nt = pl.num_programs(1)
    x = c_ref[...].astype(F32)
    prev_row = jnp.where(t > 0, p_ref[...].astype(F32)[halo - 1:halo, :], 0.0)
    next_row = jnp.where(t < nt - 1, n_ref[...].astype(F32)[0:1, :], 0.0)
    row = lax.broadcasted_iota(jnp.int32, (rows, 1), 0)
    up = jnp.where(row == 0, prev_row, pltpu.roll(x, 1, axis=0))
    dn = jnp.where(row == rows - 1, next_row, pltpu.roll(x, rows - 1, axis=0))
    o_ref[...] = up * w_ref[0:1, :] + x * w_ref[1:2, :] + dn * w_ref[2:3, :] + b_ref[...]


def _shortconv(proj, conv_w, conv_b, batch, seq, rows=512):
    rows = min(rows, seq)
    halo = 16
    pv = proj.reshape(batch, seq, REST_WIDTH)
    cb = R_U // HYENA_WIDTH
    hb = rows // halo
    nhb = seq // halo
    return pl.pallas_call(
        functools.partial(_shortconv_kernel, rows=rows, halo=halo),
        grid=(batch, seq // rows, 3),
        in_specs=[
            pl.BlockSpec((None, halo, HYENA_WIDTH), lambda b, t, j: (b, jnp.maximum(t * hb - 1, 0), cb + j)),
            pl.BlockSpec((None, rows, HYENA_WIDTH), lambda b, t, j: (b, t, cb + j)),
            pl.BlockSpec((None, halo, HYENA_WIDTH), lambda b, t, j: (b, jnp.minimum((t + 1) * hb, nhb - 1), cb + j)),
            pl.BlockSpec((3, HYENA_WIDTH), lambda b, t, j: (0, j)),
            pl.BlockSpec((1, HYENA_WIDTH), lambda b, t, j: (0, j)),
        ],
        out_specs=pl.BlockSpec((None, None, rows, HYENA_WIDTH), lambda b, t, j: (j, b, t, 0)),
        out_shape=jax.ShapeDtypeStruct((3, batch, seq, HYENA_WIDTH), F32),
        compiler_params=_cparams(("parallel", "parallel", "arbitrary")),
        name="shortconv",
    )(pv, pv, pv, conv_w, conv_b.reshape(1, -1))


def _filter_features(length):
    t = np.linspace(0.0, 1.0, length)[:, None]
    ang = (2.0 * math.pi / length) * np.arange(length, dtype=np.float64)[:, None]
    bands = np.linspace(1e-4, N_BANDS - 1, N_BANDS)[None]
    z = np.concatenate([t, np.cos(ang * bands), -np.sin(ang * bands)], axis=-1)
    zp = np.zeros((length, LANES), np.float32)
    zp[:, :FILTER_EMB] = z
    return zp


def _decay_rates():
    max_decay = math.log(1e-2) / 0.3
    min_decay = math.log(1e-2) / 1.5
    return np.abs(np.linspace(min_decay, max_decay, HYENA_WIDTH)).astype(np.float32)[None]


def _filter_kernel(z_ref, w1_ref, b1_ref, w2_ref, b2_ref, w3_ref, b3_ref, w4_ref, fr_ref, dl_ref, o_ref, *, rows):
    hi = lax.Precision.HIGHEST
    z = z_ref[...]
    fr = fr_ref[...]
    h = jnp.sin(fr * (jnp.dot(z, w1_ref[...], precision=hi, preferred_element_type=F32) + b1_ref[...]))
    h = jnp.sin(fr * (jnp.dot(h, w2_ref[...], precision=hi, preferred_element_type=F32) + b2_ref[...]))
    h = jnp.sin(fr * (jnp.dot(h, w3_ref[...], precision=hi, preferred_element_type=F32) + b3_ref[...]))
    filt = jnp.dot(h, w4_ref[...], precision=hi, preferred_element_type=F32)
    decay = jnp.exp(-z[:, 0:1] * dl_ref[...])
    row = pl.program_id(0) * rows + lax.broadcasted_iota(jnp.int32, (rows, 1), 0)
    for j in range(4):
        cs = slice(j * HYENA_WIDTH, (j + 1) * HYENA_WIDTH)
        val = filt[:, cs] * decay
        if j % 2 == 1:
            val = jnp.where(row == 0, 0.0, val)
        o_ref[:, cs] = val


def _filters(length, fw1, fb1, fw2, fb2, fw3, fb3, fw4, ffreq, rows=256):
    rows = min(rows, length)
    zfeat = jnp.asarray(_filter_features(length))
    w1p = jnp.zeros((LANES, FILTER_HIDDEN), F32).at[:FILTER_EMB].set(fw1.astype(F32))
    full = lambda shape: pl.BlockSpec(shape, lambda i: (0,) * len(shape))
    return pl.pallas_call(
        functools.partial(_filter_kernel, rows=rows),
        grid=(length // rows,),
        in_specs=[pl.BlockSpec((rows, LANES), lambda i: (i, 0)),
                  full((LANES, FILTER_HIDDEN)), full((1, FILTER_HIDDEN)),
                  full((FILTER_HIDDEN, FILTER_HIDDEN)), full((1, FILTER_HIDDEN)),
                  full((FILTER_HIDDEN, FILTER_HIDDEN)), full((1, FILTER_HIDDEN)),
                  full((FILTER_HIDDEN, 4 * HYENA_WIDTH)), full((1, FILTER_HIDDEN)),
                  full((1, HYENA_WIDTH))],
        out_specs=pl.BlockSpec((rows, 4 * HYENA_WIDTH), lambda i: (i, 0)),
        out_shape=jax.ShapeDtypeStruct((length, 4 * HYENA_WIDTH), F32),
        compiler_params=_cparams(("parallel",)),
        name="hyena_filters",
    )(zfeat, w1p, fb1.reshape(1, -1), fw2, fb2.reshape(1, -1), fw3, fb3.reshape(1, -1), fw4,
      ffreq.reshape(1, -1), jnp.asarray(_decay_rates()))


def _dft_tables(length):
    n = 2 * length
    n1 = n // DFT_N2
    h1 = n1 // 2
    kk = np.arange(h1)[:, None] + 0.5
    th = 2.0 * math.pi * kk * np.arange(h1)[None, :] / n1
    fa = np.concatenate([np.cos(th), -np.sin(th)], axis=0)
    ga = (2.0 / n) * fa.T
    k = np.arange(h1)[:, None, None] + n1 * np.arange(DFT_N2)[None, :, None] + 0.5
    ph = 2.0 * math.pi * k * np.arange(DFT_N2)[None, None, :] / n
    f = lambda a: jnp.asarray(a.astype(np.float32))
    er, ei = f(np.cos(ph)), f(-np.sin(ph))
    m = jnp.concatenate([jnp.concatenate([er, -ei], axis=2), jnp.concatenate([ei, er], axis=2)], axis=1)
    mt = jnp.transpose(m, (0, 2, 1))
    return f(fa), f(ga), m, mt


def _split(a):
    hi = a.astype(BF16)
    lo = (a - hi.astype(F32)).astype(BF16)
    return hi, lo


def _dot3(a, b):
    ah, al = _split(a)
    bh, bl = _split(b)
    d = lambda x, y: jnp.dot(x, y, preferred_element_type=F32)
    return d(ah, bh) + (d(al, bh) + d(ah, bl))


def _stage_a_kernel(f_ref, x_ref, o_ref):
    o_ref[...] = _dot3(f_ref[...], x_ref[...])


def _stage_a(fa, x4, sel, tn=8192):
    _, batch, h1, cols = x4.shape
    n1 = fa.shape[0]
    tn = min(tn, cols)
    return pl.pallas_call(
        _stage_a_kernel,
        grid=(batch, cols // tn),
        in_specs=[pl.BlockSpec((n1, h1), lambda b, j: (0, 0)),
                  pl.BlockSpec((None, None, h1, tn), lambda b, j: (sel, b, 0, j))],
        out_specs=pl.BlockSpec((None, n1, tn), lambda b, j: (b, 0, j)),
        out_shape=jax.ShapeDtypeStruct((batch, n1, cols), F32),
        compiler_params=_cparams(("parallel", "parallel")),
        name="dft_stage_a",
    )(fa, x4)


def _spectrum_kernel(m_ref, xf_ref, xb_ref, o_ref):
    m = m_ref[...]
    half = DFT_N2
    zf = _dot3(m, xf_ref[...].reshape(2 * half, -1))
    zb = _dot3(m, xb_ref[...].reshape(2 * half, -1))
    o_ref[0] = zf[:half] + zb[:half]
    o_ref[1] = zf[half:] - zb[half:]


def _filter_spectrum(m_tab, a_filt):
    n1 = a_filt.shape[1]
    h1 = n1 // 2
    c = HYENA_WIDTH
    av = a_filt.reshape(2, h1, DFT_N2, 4 * c)
    return pl.pallas_call(
        _spectrum_kernel,
        grid=(2, h1),
        in_specs=[pl.BlockSpec((None, 2 * DFT_N2, 2 * DFT_N2), lambda o, k: (k, 0, 0)),
                  pl.BlockSpec((2, None, DFT_N2, c), lambda o, k: (0, k, 0, 2 * o)),
                  pl.BlockSpec((2, None, DFT_N2, c), lambda o, k: (0, k, 0, 2 * o + 1))],
        out_specs=pl.BlockSpec((None, None, 2, DFT_N2, c), lambda o, k: (o, k, 0, 0, 0)),
        out_shape=jax.ShapeDtypeStruct((2, h1, 2, DFT_N2, c), F32),
        compiler_params=_cparams(("parallel", "parallel")),
        name="filter_spectrum",
    )(m_tab, av, av)


def _freq_kernel(m_ref, mt_ref, x_ref, k_ref, o_ref):
    half = DFT_N2
    z = _dot3(m_ref[...], x_ref[...].reshape(2 * half, -1))
    zr, zi = z[:half], z[half:]
    kr, ki = k_ref[0], k_ref[1]
    y = jnp.concatenate([zr * kr - zi * ki, zr * ki + zi * kr], axis=0)
    o_ref[...] = _dot3(mt_ref[...], y).reshape(2, half, -1)


def _freq_stage(m_tab, mt_tab, a, kspec, order):
    batch, n1, _ = a.shape
    h1 = n1 // 2
    c = HYENA_WIDTH
    av = a.reshape(batch, 2, h1, DFT_N2, c)
    mspec = pl.BlockSpec((None, 2 * DFT_N2, 2 * DFT_N2), lambda k, b: (k, 0, 0))
    out = pl.pallas_call(
        _freq_kernel,
        grid=(h1, batch),
        in_specs=[mspec, mspec,
                  pl.BlockSpec((None, 2, None, DFT_N2, c), lambda k, b: (b, 0, k, 0, 0)),
                  pl.BlockSpec((None, None, 2, DFT_N2, c), lambda k, b: (order, k, 0, 0, 0))],
        out_specs=pl.BlockSpec((None, 2, None, DFT_N2, c), lambda k, b: (b, 0, k, 0, 0)),
        out_shape=jax.ShapeDtypeStruct((batch, 2, h1, DFT_N2, c), F32),
        compiler_params=_cparams(("parallel", "parallel")),
        name="dft_freq_stage",
    )(m_tab, mt_tab, av, kspec)
    return out.reshape(batch, n1, DFT_N2 * c)


def _stage_inv_kernel(g_ref, bp_ref, z_ref, hx_ref, sk_ref, o_ref):
    y = _dot3(g_ref[...], bp_ref[...])
    o_ref[...] = hx_ref[...] * (y + z_ref[...] * sk_ref[...])


def _stage_inv(ga, bp, z4, zsel, hx4, hxsel, skip_row, tn=8192):
    batch, n1, cols = bp.shape
    h1 = n1 // 2
    tn = min(tn, cols)
    return pl.pallas_call(
        _stage_inv_kernel,
        grid=(batch, cols // tn),
        in_specs=[pl.BlockSpec((h1, n1), lambda b, j: (0, 0)),
                  pl.BlockSpec((None, n1, tn), lambda b, j: (b, 0, j)),
                  pl.BlockSpec((None, None, h1, tn), lambda b, j: (zsel, b, 0, j)),
                  pl.BlockSpec((None, None, h1, tn), lambda b, j: (hxsel, b, 0, j)),
                  pl.BlockSpec((1, tn), lambda b, j: (0, 0))],
        out_specs=pl.BlockSpec((None, h1, tn), lambda b, j: (b, 0, j)),
        out_shape=jax.ShapeDtypeStruct((batch, h1, cols), F32),
        compiler_params=_cparams(("parallel", "parallel")),
        name="dft_stage_inv",
    )(ga, bp, z4, hx4, skip_row)


def _hyena(proj, batch, seq, conv_w, conv_b, fparams, skip):
    c = HYENA_WIDTH
    fa, ga, m_tab, mt_tab = _dft_tables(seq)
    n1 = fa.shape[0]
    h1 = n1 // 2
    tn = min(8192, DFT_N2 * c)
    filt = _filters(seq, *fparams)
    a_f = _stage_a(fa, filt.reshape(1, 1, h1, DFT_N2 * 4 * c), 0)
    kspec = _filter_spectrum(m_tab, a_f)
    u3 = _shortconv(proj, conv_w, conv_b, batch, seq)
    u4 = u3.reshape(3, batch, h1, DFT_N2 * c)
    skip_rows = jnp.tile(skip.astype(F32), (1, tn // c))
    a1 = _stage_a(fa, u4, 0)
    b1 = _freq_stage(m_tab, mt_tab, a1, kspec, 0)
    z1 = _stage_inv(ga, b1, u4, 0, u4, 1, skip_rows[0:1])
    z1v = z1.reshape(1, batch, h1, DFT_N2 * c)
    a2 = _stage_a(fa, z1v, 0)
    b2 = _freq_stage(m_tab, mt_tab, a2, kspec, 1)
    hy = _stage_inv(ga, b2, z1v, 0, u4, 2, skip_rows[1:2])
    return hy.reshape(batch * seq, c)


def _final_kernel(o1_ref, o2_ref, o3_ref, l1_ref, l2_ref, l3_ref, ga_ref, hy_ref, gh_ref, ma_ref, mh_ref,
                  x_ref, wa_ref, wh_ref, wo_ref, pg_ref, y_ref):
    def cat(ref):
        return jnp.concatenate([ref[s] for s in range(ATTN_WIDTH // LANES)], axis=1)

    l1, l2, l3 = cat(l1_ref), cat(l2_ref), cat(l3_ref)
    mx = jnp.maximum(jnp.maximum(l1, l2), l3)
    e1, e2, e3 = jnp.exp(l1 - mx), jnp.exp(l2 - mx), jnp.exp(l3 - mx)
    attn = (e1 * cat(o1_ref) + e2 * cat(o2_ref) + e3 * cat(o3_ref)) / (e1 + e2 + e3)
    ga = ga_ref[...].astype(F32)
    a_in = (attn * (ga * jax.nn.sigmoid(ga))).astype(BF16)
    a_br = jnp.dot(a_in, wa_ref[...], preferred_element_type=F32)
    gh = gh_ref[...].astype(F32)
    h_in = (hy_ref[...] * (gh * jax.nn.sigmoid(gh))).astype(BF16)
    h_br = jnp.dot(h_in, wh_ref[...], preferred_element_type=F32)
    merged = jax.nn.sigmoid(ma_ref[...].astype(F32)) * a_br + jax.nn.sigmoid(mh_ref[...].astype(F32)) * h_br
    out = jnp.dot(merged.astype(BF16), wo_ref[...], preferred_element_type=F32)
    ms = jnp.mean(out * out, axis=-1, keepdims=True)
    y_ref[...] = x_ref[...] + out * lax.rsqrt(ms + EPS) * pg_ref[...]


def _final(os_, ls_, proj, hy, x2d, wa, wh, wo, pg, tm=256):
    rows = x2d.shape[0]
    tm = min(tm, rows)
    slab = pl.BlockSpec((ATTN_WIDTH // LANES, tm, LANES), lambda i: (0, i, 0))
    a512 = lambda off: pl.BlockSpec((tm, ATTN_WIDTH), lambda i: (i, off))
    a1024 = lambda off: pl.BlockSpec((tm, D_MODEL), lambda i: (i, off))
    full = lambda shape: pl.BlockSpec(shape, lambda i: (0, 0))
    return pl.pallas_call(
        _final_kernel,
        grid=(rows // tm,),
        in_specs=[slab] * 6 + [a512(R_GA // ATTN_WIDTH), a1024(0), a1024(R_GH // D_MODEL),
                               a1024(R_MG // D_MODEL), a1024(R_MG // D_MODEL + 1), a1024(0),
                                  full((ATTN_WIDTH, D_MODEL)), full((D_MODEL, D_MODEL)), full((D_MODEL, D_MODEL)),
                                  full((1, D_MODEL))],
        out_specs=pl.BlockSpec((tm, D_MODEL), lambda i: (i, 0)),
        out_shape=jax.ShapeDtypeStruct((rows, D_MODEL), F32),
        compiler_params=_cparams(("parallel",)),
        name="merge_out",
    )(*os_, *ls_, proj, hy, proj, proj, proj, x2d, wa, wh, wo, pg)


def _layer(x, rel_bias, pre_g, post_g, w_in, conv_w, conv_b, fparams, skip, w_br_a, w_br_h, w_out):
    batch, seq, _ = x.shape
    x2d = x.reshape(batch * seq, D_MODEL)
    w_qkv, w_rest = w_in
    gain = pre_g.reshape(1, -1).astype(F32)
    qkv = _inproj(x2d, gain, w_qkv, tn=QKV_WIDTH // 3, slab_out=True)
    proj = _inproj(x2d, gain, w_rest, tn=REST_WIDTH // 4, slab_out=False)
    os_, ls_ = [], []
    for g in range(N_GROUPS):
        o, l = _attention_group(qkv, rel_bias, g, batch, seq)
        os_.append(o)
        ls_.append(l)
    hy = _hyena(proj, batch, seq, conv_w, conv_b, fparams, skip)
    y = _final(os_, ls_, proj, hy, x2d, w_br_a, w_br_h, w_out, post_g.reshape(1, -1).astype(F32))
    return y.reshape(batch, seq, D_MODEL)


def kernel(x_prompt, x_sample, rel_bias, pre_norm_g, post_norm_g, w_in, conv_w, conv_b, filt_w1, filt_b1, filt_w2, filt_b2, filt_w3, filt_b3, filt_w4, filt_freq, hyena_skip, w_branch_a, w_branch_h, w_out):
    depth = w_in.shape[0]

    def run(x):
        for l in range(depth):
            fparams = (filt_w1[l], filt_b1[l], filt_w2[l], filt_b2[l], filt_w3[l], filt_b3[l], filt_w4[l], filt_freq[l])
            w = w_in[l]
            w_qkv = w[:, :QKV_WIDTH].astype(BF16)
            w_rest = jnp.concatenate([w[:, U_OFF:], w[:, GA_OFF:U_OFF]], axis=1).astype(BF16)
            x = _layer(x, rel_bias, pre_norm_g[l], post_norm_g[l], (w_qkv, w_rest), conv_w[l], conv_b[l],
                       fparams, hyena_skip[l], w_branch_a[l].astype(BF16), w_branch_h[l].astype(BF16),
                       w_out[l].astype(BF16))
        return x

    return (run(x_prompt), run(x_sample))
```

```python
import functools
import math

import numpy as np
import jax
import jax.numpy as jnp
from jax import lax
from jax.experimental import pallas as pl
from jax.experimental.pallas import tpu as pltpu

F32 = jnp.float32
BF16 = jnp.bfloat16

D_MODEL = 1024
EPS = 1e-6
HEAD_DIM = 64
ATTN_GROUPS = ((128, 1), (512, 4), (2048, 16))
N_GROUPS = 3
HEADS_PER_GROUP = 8
ATTN_WIDTH = HEADS_PER_GROUP * HEAD_DIM
HYENA_WIDTH = 1024
FILTER_EMB = 33
N_BANDS = 16
FILTER_HIDDEN = 64
NUM_BUCKETS = 32
MAX_DISTANCE = 1024
NEG_INF = -1e30
QKV_WIDTH = 4608
GA_OFF, U_OFF, GH_OFF, MG_OFF = 4608, 5120, 8192, 9216
REST_WIDTH = 6656
R_U, R_GH, R_MG, R_GA = 0, 3072, 4096, 6144

WIN = 64
LANES = 128
DFT_N2 = 128
PITCH_Z = 136
PITCH_S = 264
VMEM_LIMIT = 48 * 1024 * 1024
CONV_VMEM_LIMIT = 58 * 1024 * 1024


def _cparams(sem):
    return pltpu.CompilerParams(dimension_semantics=sem, vmem_limit_bytes=VMEM_LIMIT)


def _inproj_kernel(x_ref, g_ref, w_ref, o_ref, hn_ref, *, slabs):
    @pl.when(pl.program_id(1) == 0)
    def _():
        x = x_ref[...]
        ms = jnp.mean(x * x, axis=-1, keepdims=True)
        hn_ref[...] = (x * lax.rsqrt(ms + EPS) * g_ref[...]).astype(BF16)

    acc = jnp.dot(hn_ref[...], w_ref[...], preferred_element_type=F32)
    if slabs:
        for s in range(slabs):
            o_ref[s] = acc[:, s * LANES:(s + 1) * LANES]
    else:
        o_ref[...] = acc.astype(o_ref.dtype)


def _inproj(x2d, g, w_bf16, tn, slab_out, tm=1024):
    rows = x2d.shape[0]
    width = w_bf16.shape[1]
    tm = min(tm, rows)
    if slab_out:
        ns = tn // LANES
        out_spec = pl.BlockSpec((ns, tm, LANES), lambda i, j: (j, i, 0))
        out_shape = jax.ShapeDtypeStruct((width // LANES, rows, LANES), F32)
    else:
        ns = 0
        out_spec = pl.BlockSpec((tm, tn), lambda i, j: (i, j))
        out_shape = jax.ShapeDtypeStruct((rows, width), BF16)
    return pl.pallas_call(
        functools.partial(_inproj_kernel, slabs=ns),
        grid=(rows // tm, width // tn),
        in_specs=[
            pl.BlockSpec((tm, D_MODEL), lambda i, j: (i, 0)),
            pl.BlockSpec((1, D_MODEL), lambda i, j: (0, 0)),
            pl.BlockSpec((D_MODEL, tn), lambda i, j: (0, j)),
        ],
        out_specs=out_spec,
        out_shape=out_shape,
        scratch_shapes=[pltpu.VMEM((tm, D_MODEL), BF16)],
        compiler_params=_cparams(("parallel", "arbitrary")),
        name="inproj_qkv" if slab_out else "inproj_rest",
    )(x2d, g, w_bf16)


def _t5_bucket_np(rel):
    half = NUM_BUCKETS // 2
    max_exact = half // 2
    n = np.abs(rel)
    nf = np.maximum(n, 1).astype(np.float64)
    large = max_exact + (np.log(nf / max_exact) / math.log(MAX_DISTANCE / max_exact) * (half - max_exact)).astype(np.int64)
    large = np.minimum(large, half - 1)
    return np.where(rel > 0, half, 0) + np.where(n < max_exact, n, large)


def _bias_mask(rel_bias_g, dilation):
    qi = np.arange(2 * WIN)[:, None]
    kj = np.arange(4 * WIN)[None, :] - WIN
    delta = kj - qi
    bucket = _t5_bucket_np(delta * dilation).astype(np.int32)
    onehot = jax.nn.one_hot(jnp.asarray(bucket), NUM_BUCKETS, dtype=F32)
    bias = jnp.einsum('qkb,bh->hqk', onehot, rel_bias_g.astype(F32), precision=lax.Precision.HIGHEST)
    return jnp.where(jnp.asarray(np.abs(delta) <= WIN)[None], bias, NEG_INF)


ATTN_POS_PER_STEP = 4096
ATTN_TILES_PER_ITER = 4


def _attn_kernel(q_ref, kp_ref, kc_ref, kn_ref, vp_ref, vc_ref, vn_ref, bm_ref, o_ref, l_ref,
                 kbuf, vbuf, *, tq, dil, sub_len):
    t = pl.program_id(2)
    halo = WIN * dil
    span = tq * dil
    kbuf[0:halo] = kp_ref[...]
    kbuf[halo:halo + span] = kc_ref[...]
    kbuf[halo + span:] = kn_ref[...]
    vbuf[0:halo] = vp_ref[...]
    vbuf[halo:halo + span] = vc_ref[...]
    vbuf[halo + span:] = vn_ref[...]
    lane = lax.broadcasted_iota(jnp.int32, (1, LANES), 1)
    is_lo = lane < HEAD_DIM
    kcol = lax.broadcasted_iota(jnp.int32, (1, 4 * WIN), 1)
    qt = 2 * WIN
    nq = tq // qt

    def tile(c, i):
        row0 = i * (qt * dil) + c
        kpos = t * tq + i * qt - WIN + kcol
        pen = jnp.where((kpos >= 0) & (kpos < sub_len), 0.0, NEG_INF).astype(F32)
        q = (q_ref[pl.ds(row0, qt, stride=dil), :] * (HEAD_DIM ** -0.5)).astype(BF16)
        k = kbuf[pl.ds(row0, 2 * qt, stride=dil), :].astype(BF16)
        v = vbuf[pl.ds(row0, 2 * qt, stride=dil), :].astype(BF16)
        outs, lses = [], []
        for hh in range(2):
            sel = is_lo if hh == 0 else jnp.logical_not(is_lo)
            qm = jnp.where(sel, q, jnp.zeros_like(q))
            s = lax.dot_general(qm, k, (((1,), (1,)), ((), ())), preferred_element_type=F32)
            s = s + bm_ref[hh] + pen
            m = jnp.max(s, axis=-1, keepdims=True)
            p = jnp.exp(s - m)
            den = jnp.sum(p, axis=-1, keepdims=True)
            pv = jnp.dot(p.astype(BF16), v, preferred_element_type=F32)
            outs.append(pv / den)
            lses.append(m + jnp.log(den))
        o_ref[pl.ds(row0, qt, stride=dil), :] = jnp.where(is_lo, outs[0], outs[1])
        l_ref[pl.ds(row0, qt, stride=dil), :] = jnp.where(is_lo, lses[0], lses[1])

    def body(it, carry):
        for u in range(ATTN_TILES_PER_ITER):
            idx = it * ATTN_TILES_PER_ITER + u
            tile(idx // nq, idx % nq)
        return carry

    lax.fori_loop(0, dil * nq // ATTN_TILES_PER_ITER, body, 0)


def _attention_group(qkv, rel_bias, g, batch, seq):
    _, dil = ATTN_GROUPS[g]
    sub_len = seq // dil
    tq = min(ATTN_POS_PER_STEP // dil, sub_len)
    nt = sub_len // tq
    span = tq * dil
    halo = WIN * dil
    n_hp = ATTN_WIDTH // LANES
    n_slab = N_GROUPS * n_hp
    bm = _bias_mask(rel_bias[:, g * HEADS_PER_GROUP:(g + 1) * HEADS_PER_GROUP], dil)
    hb = tq // WIN
    nhb = sub_len // WIN

    def cur(which):
        return pl.BlockSpec((None, span, LANES), lambda b, hp, t: (which * n_slab + g * n_hp + hp, b * nt + t, 0))

    def prev(which):
        return pl.BlockSpec((None, halo, LANES),
                            lambda b, hp, t: (which * n_slab + g * n_hp + hp, b * nhb + jnp.maximum(t * hb - 1, 0), 0))

    def nxt(which):
        return pl.BlockSpec((None, halo, LANES),
                            lambda b, hp, t: (which * n_slab + g * n_hp + hp, b * nhb + jnp.minimum((t + 1) * hb, nhb - 1), 0))

    out_spec = pl.BlockSpec((None, span, LANES), lambda b, hp, t: (hp, b * nt + t, 0))
    return pl.pallas_call(
        functools.partial(_attn_kernel, tq=tq, dil=dil, sub_len=sub_len),
        grid=(batch, n_hp, nt),
        in_specs=[cur(0), prev(1), cur(1), nxt(1), prev(2), cur(2), nxt(2),
                  pl.BlockSpec((2, 2 * WIN, 4 * WIN), lambda b, hp, t: (hp, 0, 0))],
        out_specs=[out_spec, out_spec],
        out_shape=[jax.ShapeDtypeStruct((n_hp, batch * seq, LANES), F32)] * 2,
        scratch_shapes=[pltpu.VMEM((span + 2 * halo, LANES), F32)] * 2,
        compiler_params=_cparams(("parallel", "parallel", "arbitrary")),
        name=f"attn_g{g}",
    )(qkv, qkv, qkv, qkv, qkv, qkv, qkv, bm)


def _shortconv_kernel(p_ref, c_ref, n_ref, w_ref, b_ref, o_ref, *, rows, halo):
    t = pl.program_id(1)
    nt = pl.num_programs(1)
    x = c_ref[...].astype(F32)
    prev_row = jnp.where(t > 0, p_ref[...].astype(F32)[halo - 1:halo, :], 0.0)
    next_row = jnp.where(t < nt - 1, n_ref[...].astype(F32)[0:1, :], 0.0)
    row = lax.broadcasted_iota(jnp.int32, (rows, 1), 0)
    up = jnp.where(row == 0, prev_row, pltpu.roll(x, 1, axis=0))
    dn = jnp.where(row == rows - 1, next_row, pltpu.roll(x, rows - 1, axis=0))
    res = up * w_ref[0:1, :] + x * w_ref[1:2, :] + dn * w_ref[2:3, :] + b_ref[...]
    _store_padded_slabs(o_ref, res, rows // DFT_N2)


def _store_padded_slabs(o_ref, val, nblk):
    pad = jnp.zeros((PITCH_Z - DFT_N2, LANES), F32)
    for s in range(val.shape[1] // LANES):
        for jb in range(nblk):
            o_ref[s, jb * PITCH_Z:jb * PITCH_Z + DFT_N2, :] = val[jb * DFT_N2:(jb + 1) * DFT_N2, s * LANES:(s + 1) * LANES]
            o_ref[s, jb * PITCH_Z + DFT_N2:(jb + 1) * PITCH_Z, :] = pad


def _load_padded_slabs(ref, nblk):
    return jnp.concatenate(
        [jnp.concatenate([ref[s, jb * PITCH_Z:jb * PITCH_Z + DFT_N2, :] for jb in range(nblk)], axis=0)
         for s in range(ref.shape[0])], axis=1)


def _shortconv(proj, conv_w, conv_b, batch, seq, rows=512):
    rows = min(rows, seq)
    prow = rows // DFT_N2 * PITCH_Z
    n_slab = HYENA_WIDTH // LANES
    halo = 16
    pv = proj.reshape(batch, seq, REST_WIDTH)
    cb = R_U // HYENA_WIDTH
    hb = rows // halo
    nhb = seq // halo
    return pl.pallas_call(
        functools.partial(_shortconv_kernel, rows=rows, halo=halo),
        grid=(batch, seq // rows, 3),
        in_specs=[
            pl.BlockSpec((None, halo, HYENA_WIDTH), lambda b, t, j: (b, jnp.maximum(t * hb - 1, 0), cb + j)),
            pl.BlockSpec((None, rows, HYENA_WIDTH), lambda b, t, j: (b, t, cb + j)),
            pl.BlockSpec((None, halo, HYENA_WIDTH), lambda b, t, j: (b, jnp.minimum((t + 1) * hb, nhb - 1), cb + j)),
            pl.BlockSpec((3, HYENA_WIDTH), lambda b, t, j: (0, j)),
            pl.BlockSpec((1, HYENA_WIDTH), lambda b, t, j: (0, j)),
        ],
        out_specs=pl.BlockSpec((None, n_slab, None, prow, LANES), lambda b, t, j: (j, 0, b, t, 0)),
        out_shape=jax.ShapeDtypeStruct((3, n_slab, batch, seq // DFT_N2 * PITCH_Z, LANES), F32),
        compiler_params=_cparams(("parallel", "parallel", "arbitrary")),
        name="shortconv",
    )(pv, pv, pv, conv_w, conv_b.reshape(1, -1))


def _filter_features(length):
    t = np.linspace(0.0, 1.0, length)[:, None]
    ang = (2.0 * math.pi / length) * np.arange(length, dtype=np.float64)[:, None]
    bands = np.linspace(1e-4, N_BANDS - 1, N_BANDS)[None]
    z = np.concatenate([t, np.cos(ang * bands), -np.sin(ang * bands)], axis=-1)
    zp = np.zeros((length, LANES), np.float32)
    zp[:, :FILTER_EMB] = z
    return zp


def _decay_rates():
    max_decay = math.log(1e-2) / 0.3
    min_decay = math.log(1e-2) / 1.5
    return np.abs(np.linspace(min_decay, max_decay, HYENA_WIDTH)).astype(np.float32)[None]


def _filter_kernel(z_ref, w1_ref, b1_ref, w2_ref, b2_ref, w3_ref, b3_ref, w4_ref, fr_ref, dl_ref, o_ref, *, rows):
    hi = lax.Precision.HIGHEST
    z = z_ref[...]
    fr = fr_ref[...]
    h = jnp.sin(fr * (jnp.dot(z, w1_ref[...], precision=hi, preferred_element_type=F32) + b1_ref[...]))
    h = jnp.sin(fr * (jnp.dot(h, w2_ref[...], precision=hi, preferred_element_type=F32) + b2_ref[...]))
    h = jnp.sin(fr * (jnp.dot(h, w3_ref[...], precision=hi, preferred_element_type=F32) + b3_ref[...]))
    filt = jnp.dot(h, w4_ref[...], precision=hi, preferred_element_type=F32)
    decay = jnp.exp(-z[:, 0:1] * dl_ref[...])
    row = pl.program_id(0) * rows + lax.broadcasted_iota(jnp.int32, (rows, 1), 0)
    for j in range(4):
        cs = slice(j * HYENA_WIDTH, (j + 1) * HYENA_WIDTH)
        val = filt[:, cs] * decay
        if j % 2 == 1:
            val = jnp.where(row == 0, 0.0, val)
        o_ref[:, cs] = val


def _filters(length, fw1, fb1, fw2, fb2, fw3, fb3, fw4, ffreq, rows=256):
    rows = min(rows, length)
    zfeat = jnp.asarray(_filter_features(length))
    w1p = jnp.zeros((LANES, FILTER_HIDDEN), F32).at[:FILTER_EMB].set(fw1.astype(F32))
    full = lambda shape: pl.BlockSpec(shape, lambda i: (0,) * len(shape))
    return pl.pallas_call(
        functools.partial(_filter_kernel, rows=rows),
        grid=(length // rows,),
        in_specs=[pl.BlockSpec((rows, LANES), lambda i: (i, 0)),
                  full((LANES, FILTER_HIDDEN)), full((1, FILTER_HIDDEN)),
                  full((FILTER_HIDDEN, FILTER_HIDDEN)), full((1, FILTER_HIDDEN)),
                  full((FILTER_HIDDEN, FILTER_HIDDEN)), full((1, FILTER_HIDDEN)),
                  full((FILTER_HIDDEN, 4 * HYENA_WIDTH)), full((1, FILTER_HIDDEN)),
                  full((1, HYENA_WIDTH))],
        out_specs=pl.BlockSpec((rows, 4 * HYENA_WIDTH), lambda i: (i, 0)),
        out_shape=jax.ShapeDtypeStruct((length, 4 * HYENA_WIDTH), F32),
        compiler_params=_cparams(("parallel",)),
        name="hyena_filters",
    )(zfeat, w1p, fb1.reshape(1, -1), fw2, fb2.reshape(1, -1), fw3, fb3.reshape(1, -1), fw4,
      ffreq.reshape(1, -1), jnp.asarray(_decay_rates()))


def _dft_tables(length):
    n = 2 * length
    n1 = n // DFT_N2
    h1 = n1 // 2
    kk = np.arange(h1)[:, None] + 0.5
    th = 2.0 * math.pi * kk * np.arange(h1)[None, :] / n1
    fa = np.concatenate([np.cos(th), -np.sin(th)], axis=0)
    ga = (2.0 / n) * fa.T
    k = np.arange(h1)[:, None, None] + n1 * np.arange(DFT_N2)[None, :, None] + 0.5
    ph = 2.0 * math.pi * k * np.arange(DFT_N2)[None, None, :] / n
    f = lambda a: jnp.asarray(a.astype(np.float32))
    er, ei = f(np.cos(ph)), f(-np.sin(ph))
    m = jnp.concatenate([jnp.concatenate([er, -ei], axis=2), jnp.concatenate([ei, er], axis=2)], axis=1)
    mt = jnp.transpose(m, (0, 2, 1))
    return f(fa), f(ga), m, mt


def _split(a):
    hi = a.astype(BF16)
    lo = (a - hi.astype(F32)).astype(BF16)
    return hi, lo


def _dot3(a, b):
    ah, al = _split(a)
    bh, bl = _split(b)
    d = lambda x, y: jnp.dot(x, y, preferred_element_type=F32)
    return d(ah, bh) + (d(al, bh) + d(ah, bl))


def _stage_a_kernel(f_ref, x_ref, o_ref):
    o_ref[...] = _dot3(f_ref[...], x_ref[...])


def _stage_a(fa, x4, sel, tn=8192):
    _, batch, h1, cols = x4.shape
    n1 = fa.shape[0]
    tn = min(tn, cols)
    return pl.pallas_call(
        _stage_a_kernel,
        grid=(batch, cols // tn),
        in_specs=[pl.BlockSpec((n1, h1), lambda b, j: (0, 0)),
                  pl.BlockSpec((None, None, h1, tn), lambda b, j: (sel, b, 0, j))],
        out_specs=pl.BlockSpec((None, n1, tn), lambda b, j: (b, 0, j)),
        out_shape=jax.ShapeDtypeStruct((batch, n1, cols), F32),
        compiler_params=_cparams(("parallel", "parallel")),
        name="dft_stage_a",
    )(fa, x4)


def _spectrum_kernel(m_ref, xf_ref, xb_ref, o_ref):
    m = m_ref[...]
    half = DFT_N2
    zf = _dot3(m, xf_ref[...].reshape(2 * half, -1))
    zb = _dot3(m, xb_ref[...].reshape(2 * half, -1))
    kr = zf[:half] + zb[:half]
    ki = zf[half:] - zb[half:]
    for s in range(o_ref.shape[0]):
        o_ref[s, 0:half, :] = kr[:, s * LANES:(s + 1) * LANES]
        o_ref[s, half:, :] = ki[:, s * LANES:(s + 1) * LANES]


def _filter_spectrum(m_tab, a_filt):
    n1 = a_filt.shape[1]
    h1 = n1 // 2
    c = HYENA_WIDTH
    n_slab = c // LANES
    av = a_filt.reshape(2, h1, DFT_N2, 4 * c)
    return pl.pallas_call(
        _spectrum_kernel,
        grid=(2, h1),
        in_specs=[pl.BlockSpec((None, 2 * DFT_N2, 2 * DFT_N2), lambda o, k: (k, 0, 0)),
                  pl.BlockSpec((2, None, DFT_N2, c), lambda o, k: (0, k, 0, 2 * o)),
                  pl.BlockSpec((2, None, DFT_N2, c), lambda o, k: (0, k, 0, 2 * o + 1))],
        out_specs=pl.BlockSpec((None, n_slab, None, 2 * DFT_N2, LANES), lambda o, k: (o, 0, k, 0, 0)),
        out_shape=jax.ShapeDtypeStruct((2, n_slab, h1, 2 * DFT_N2, LANES), F32),
        compiler_params=_cparams(("parallel", "parallel")),
        name="filter_spectrum",
    )(m_tab, av, av)


CONV_K1_CHUNK = 32
CONV_PAIR_UNROLL = 8
CONV_FREQ_UNROLL = 4


def _conv_kernel(fa_ref, ga_ref, m_ref, k_ref, z_ref, hx_ref, sk_ref, o_ref, s_ref, *, h1, kc):
    hf = pl.program_id(2)
    half = DFT_N2
    pairs = DFT_N2 // 2

    @pl.when(hf == 0)
    def _():
        fa = fa_ref[...]

        def body(j, carry):
            n2 = 2 * j
            x = jnp.concatenate([z_ref[pl.ds(n2, h1, stride=PITCH_Z), :],
                                 z_ref[pl.ds(n2 + 1, h1, stride=PITCH_Z), :]], axis=1).astype(BF16)
            a = jnp.dot(fa, x, preferred_element_type=F32)
            for ri in range(2):
                for u in range(2):
                    s_ref[pl.ds(ri * half + n2 + u, h1, stride=PITCH_S), :] = (
                        a[ri * h1:(ri + 1) * h1, u * LANES:(u + 1) * LANES])
            return carry

        lax.fori_loop(0, pairs, body, 0, unroll=CONV_PAIR_UNROLL)

    def freq_body(k1, carry):
        row = pl.multiple_of((hf * kc + k1) * PITCH_S, 8)
        x = s_ref[pl.ds(row, 2 * half), :].astype(BF16)
        m = m_ref[k1]
        z = jnp.dot(m, x, preferred_element_type=F32)
        zr, zi = z[:half], z[half:]
        kr, ki = k_ref[k1, 0:half, :], k_ref[k1, half:, :]
        y = jnp.concatenate([zr * kr - zi * ki, zr * ki + zi * kr], axis=0).astype(BF16)
        s_ref[pl.ds(row, 2 * half), :] = lax.dot_general(m, y, (((0,), (0,)), ((), ())),
                                                         preferred_element_type=F32)
        return carry

    lax.fori_loop(0, kc, freq_body, 0, unroll=CONV_FREQ_UNROLL)

    @pl.when(hf == pl.num_programs(2) - 1)
    def _():
        ga = ga_ref[...]
        sk = sk_ref[...]

        def body(j, carry):
            n2 = 2 * j
            cols = []
            for u in range(2):
                re = s_ref[pl.ds(n2 + u, h1, stride=PITCH_S), :]
                im = s_ref[pl.ds(half + n2 + u, h1, stride=PITCH_S), :]
                cols.append(jnp.concatenate([re, im], axis=0))
            bn = jnp.concatenate(cols, axis=1).astype(BF16)
            y = jnp.dot(ga, bn, preferred_element_type=F32)
            for u in range(2):
                zs = z_ref[pl.ds(n2 + u, h1, stride=PITCH_Z), :]
                hs = hx_ref[pl.ds(n2 + u, h1, stride=PITCH_Z), :]
                o_ref[pl.ds(n2 + u, h1, stride=PITCH_Z), :] = hs * (y[:, u * LANES:(u + 1) * LANES] + sk * zs)
            return carry

        lax.fori_loop(0, pairs, body, 0, unroll=CONV_PAIR_UNROLL)
        pad = jnp.zeros((PITCH_Z - DFT_N2, LANES), F32)
        for blk in range(h1):
            o_ref[blk * PITCH_Z + DFT_N2:(blk + 1) * PITCH_Z, :] = pad


def _long_conv_gate(tabs, kspec, order, z5, zsel, hx5, hxsel, skip):
    fa, ga, m_tab = tabs
    _, n_slab, batch, prow, _ = z5.shape
    n1, h1 = fa.shape
    kc = min(CONV_K1_CHUNK, h1)
    slab = lambda sel: pl.BlockSpec((None, None, None, prow, LANES), lambda s, b, hf: (sel, s, b, 0, 0))
    return pl.pallas_call(
        functools.partial(_conv_kernel, h1=h1, kc=kc),
        grid=(n_slab, batch, h1 // kc),
        in_specs=[pl.BlockSpec((n1, h1), lambda s, b, hf: (0, 0)),
                  pl.BlockSpec((h1, n1), lambda s, b, hf: (0, 0)),
                  pl.BlockSpec((kc, 2 * DFT_N2, 2 * DFT_N2), lambda s, b, hf: (hf, 0, 0)),
                  pl.BlockSpec((None, None, kc, 2 * DFT_N2, LANES), lambda s, b, hf: (order, s, hf, 0, 0)),
                  slab(zsel), slab(hxsel),
                  pl.BlockSpec((None, None, 1, LANES), lambda s, b, hf: (order, s, 0, 0))],
        out_specs=pl.BlockSpec((None, None, prow, LANES), lambda s, b, hf: (s, b, 0, 0)),
        out_shape=jax.ShapeDtypeStruct((n_slab, batch, prow, LANES), F32),
        scratch_shapes=[pltpu.VMEM((h1 * PITCH_S, LANES), F32)],
        compiler_params=pltpu.CompilerParams(dimension_semantics=("parallel", "parallel", "arbitrary"),
                                             vmem_limit_bytes=CONV_VMEM_LIMIT),
        name="hyena_long_conv",
    )(fa.astype(BF16), ga.astype(BF16), m_tab.astype(BF16), kspec, z5, hx5, skip)


def _hyena(proj, batch, seq, conv_w, conv_b, fparams, skip):
    c = HYENA_WIDTH
    fa, ga, m_tab, _ = _dft_tables(seq)
    h1 = fa.shape[1]
    filt = _filters(seq, *fparams)
    a_f = _stage_a(fa, filt.reshape(1, 1, h1, DFT_N2 * 4 * c), 0)
    kspec = _filter_spectrum(m_tab, a_f)
    u5 = _shortconv(proj, conv_w, conv_b, batch, seq)
    skip4 = skip.astype(F32).reshape(2, c // LANES, 1, LANES)
    tabs = (fa, ga, m_tab)
    z1 = _long_conv_gate(tabs, kspec, 0, u5, 0, u5, 1, skip4)
    return _long_conv_gate(tabs, kspec, 1, z1[None], 0, u5, 2, skip4)


def _final_kernel(o1_ref, o2_ref, o3_ref, l1_ref, l2_ref, l3_ref, ga_ref, hy_ref, gh_ref, ma_ref, mh_ref,
                  x_ref, wa_ref, wh_ref, wo_ref, pg_ref, y_ref):
    def cat(ref):
        return jnp.concatenate([ref[s] for s in range(ATTN_WIDTH // LANES)], axis=1)

    l1, l2, l3 = cat(l1_ref), cat(l2_ref), cat(l3_ref)
    mx = jnp.maximum(jnp.maximum(l1, l2), l3)
    e1, e2, e3 = jnp.exp(l1 - mx), jnp.exp(l2 - mx), jnp.exp(l3 - mx)
    attn = (e1 * cat(o1_ref) + e2 * cat(o2_ref) + e3 * cat(o3_ref)) / (e1 + e2 + e3)
    ga = ga_ref[...].astype(F32)
    a_in = (attn * (ga * jax.nn.sigmoid(ga))).astype(BF16)
    a_br = jnp.dot(a_in, wa_ref[...], preferred_element_type=F32)
    gh = gh_ref[...].astype(F32)
    hy = _load_padded_slabs(hy_ref, gh.shape[0] // DFT_N2)
    h_in = (hy * (gh * jax.nn.sigmoid(gh))).astype(BF16)
    h_br = jnp.dot(h_in, wh_ref[...], preferred_element_type=F32)
    merged = jax.nn.sigmoid(ma_ref[...].astype(F32)) * a_br + jax.nn.sigmoid(mh_ref[...].astype(F32)) * h_br
    out = jnp.dot(merged.astype(BF16), wo_ref[...], preferred_element_type=F32)
    ms = jnp.mean(out * out, axis=-1, keepdims=True)
    y_ref[...] = x_ref[...] + out * lax.rsqrt(ms + EPS) * pg_ref[...]


def _final(os_, ls_, proj, hy, x2d, wa, wh, wo, pg, seq, tm=256):
    rows = x2d.shape[0]
    tm = min(tm, rows)
    tiles_per_seq = seq // tm
    hy_spec = pl.BlockSpec((HYENA_WIDTH // LANES, None, tm // DFT_N2 * PITCH_Z, LANES),
                           lambda i: (0, i // tiles_per_seq, i % tiles_per_seq, 0))
    slab = pl.BlockSpec((ATTN_WIDTH // LANES, tm, LANES), lambda i: (0, i, 0))
    a512 = lambda off: pl.BlockSpec((tm, ATTN_WIDTH), lambda i: (i, off))
    a1024 = lambda off: pl.BlockSpec((tm, D_MODEL), lambda i: (i, off))
    full = lambda shape: pl.BlockSpec(shape, lambda i: (0, 0))
    return pl.pallas_call(
        _final_kernel,
        grid=(rows // tm,),
        in_specs=[slab] * 6 + [a512(R_GA // ATTN_WIDTH), hy_spec, a1024(R_GH // D_MODEL),
                               a1024(R_MG // D_MODEL), a1024(R_MG // D_MODEL + 1), a1024(0),
                               full((ATTN_WIDTH, D_MODEL)), full((D_MODEL, D_MODEL)), full((D_MODEL, D_MODEL)),
                               full((1, D_MODEL))],
        out_specs=pl.BlockSpec((tm, D_MODEL), lambda i: (i, 0)),
        out_shape=jax.ShapeDtypeStruct((rows, D_MODEL), F32),
        compiler_params=_cparams(("parallel",)),
        name="merge_out",
    )(*os_, *ls_, proj, hy, proj, proj, proj, x2d, wa, wh, wo, pg)


def _layer(x, rel_bias, pre_g, post_g, w_in, conv_w, conv_b, fparams, skip, w_br_a, w_br_h, w_out):
    batch, seq, _ = x.shape
    x2d = x.reshape(batch * seq, D_MODEL)
    w_qkv, w_rest = w_in
    gain = pre_g.reshape(1, -1).astype(F32)
    qkv = _inproj(x2d, gain, w_qkv, tn=QKV_WIDTH // 3, slab_out=True)
    proj = _inproj(x2d, gain, w_rest, tn=REST_WIDTH // 4, slab_out=False)
    os_, ls_ = [], []
    for g in range(N_GROUPS):
        o, l = _attention_group(qkv, rel_bias, g, batch, seq)
        os_.append(o)
        ls_.append(l)
    hy = _hyena(proj, batch, seq, conv_w, conv_b, fparams, skip)
    y = _final(os_, ls_, proj, hy, x2d, w_br_a, w_br_h, w_out, post_g.reshape(1, -1).astype(F32), seq)
    return y.reshape(batch, seq, D_MODEL)


def kernel(x_prompt, x_sample, rel_bias, pre_norm_g, post_norm_g, w_in, conv_w, conv_b, filt_w1, filt_b1, filt_w2, filt_b2, filt_w3, filt_b3, filt_w4, filt_freq, hyena_skip, w_branch_a, w_branch_h, w_out):
    depth = w_in.shape[0]

    def run(x):
        for l in range(depth):
            fparams = (filt_w1[l], filt_b1[l], filt_w2[l], filt_b2[l], filt_w3[l], filt_b3[l], filt_w4[l], filt_freq[l])
            w = w_in[l]
            w_qkv = w[:, :QKV_WIDTH].astype(BF16)
            w_rest = jnp.concatenate([w[:, U_OFF:], w[:, GA_OFF:U_OFF]], axis=1).astype(BF16)
            x = _layer(x, rel_bias, pre_norm_g[l], post_norm_g[l], (w_qkv, w_rest), conv_w[l], conv_b[l],
                       fparams, hyena_skip[l], w_branch_a[l].astype(BF16), w_branch_h[l].astype(BF16),
                       w_out[l].astype(BF16))
        return x

    return (run(x_prompt), run(x_sample))
```

```python
import functools
import math

import numpy as np
import jax
import jax.numpy as jnp
from jax import lax
from jax.experimental import pallas as pl
from jax.experimental.pallas import tpu as pltpu

F32 = jnp.float32
BF16 = jnp.bfloat16

D_MODEL = 1024
EPS = 1e-6
HEAD_DIM = 64
ATTN_GROUPS = ((128, 1), (512, 4), (2048, 16))
N_GROUPS = 3
HEADS_PER_GROUP = 8
ATTN_WIDTH = HEADS_PER_GROUP * HEAD_DIM
HYENA_WIDTH = 1024
FILTER_EMB = 33
N_BANDS = 16
FILTER_HIDDEN = 64
NUM_BUCKETS = 32
MAX_DISTANCE = 1024
NEG_INF = -1e30
QKV_WIDTH = 4608
GA_OFF, U_OFF, GH_OFF, MG_OFF = 4608, 5120, 8192, 9216
REST_WIDTH = 6656
R_U, R_GH, R_MG, R_GA = 0, 3072, 4096, 6144

WIN = 64
LANES = 128
DFT_N2 = 128
PITCH_Z = 136
PITCH_S = 264
VMEM_LIMIT = 48 * 1024 * 1024
CONV_VMEM_LIMIT = 58 * 1024 * 1024


def _cparams(sem):
    return pltpu.CompilerParams(dimension_semantics=sem, vmem_limit_bytes=VMEM_LIMIT)


def _inproj_kernel(x_ref, g_ref, w_ref, o_ref, hn_ref, *, slabs):
    @pl.when(pl.program_id(1) == 0)
    def _():
        x = x_ref[...]
        ms = jnp.mean(x * x, axis=-1, keepdims=True)
        hn_ref[...] = (x * lax.rsqrt(ms + EPS) * g_ref[...]).astype(BF16)

    acc = jnp.dot(hn_ref[...], w_ref[...], preferred_element_type=F32)
    if slabs:
        for s in range(slabs):
            o_ref[s] = acc[:, s * LANES:(s + 1) * LANES]
    else:
        o_ref[...] = acc.astype(o_ref.dtype)


def _inproj(x2d, g, w_bf16, tn, slab_out, tm=1024):
    rows = x2d.shape[0]
    width = w_bf16.shape[1]
    tm = min(tm, rows)
    if slab_out:
        ns = tn // LANES
        out_spec = pl.BlockSpec((ns, tm, LANES), lambda i, j: (j, i, 0))
        out_shape = jax.ShapeDtypeStruct((width // LANES, rows, LANES), F32)
    else:
        ns = 0
        out_spec = pl.BlockSpec((tm, tn), lambda i, j: (i, j))
        out_shape = jax.ShapeDtypeStruct((rows, width), BF16)
    return pl.pallas_call(
        functools.partial(_inproj_kernel, slabs=ns),
        grid=(rows // tm, width // tn),
        in_specs=[
            pl.BlockSpec((tm, D_MODEL), lambda i, j: (i, 0)),
            pl.BlockSpec((1, D_MODEL), lambda i, j: (0, 0)),
            pl.BlockSpec((D_MODEL, tn), lambda i, j: (0, j)),
        ],
        out_specs=out_spec,
        out_shape=out_shape,
        scratch_shapes=[pltpu.VMEM((tm, D_MODEL), BF16)],
        compiler_params=_cparams(("parallel", "arbitrary")),
        name="inproj_qkv" if slab_out else "inproj_rest",
    )(x2d, g, w_bf16)


def _t5_bucket_np(rel):
    half = NUM_BUCKETS // 2
    max_exact = half // 2
    n = np.abs(rel)
    nf = np.maximum(n, 1).astype(np.float64)
    large = max_exact + (np.log(nf / max_exact) / math.log(MAX_DISTANCE / max_exact) * (half - max_exact)).astype(np.int64)
    large = np.minimum(large, half - 1)
    return np.where(rel > 0, half, 0) + np.where(n < max_exact, n, large)


def _bias_mask(rel_bias_g, dilation):
    qi = np.arange(2 * WIN)[:, None]
    kj = np.arange(4 * WIN)[None, :] - WIN
    delta = kj - qi
    bucket = _t5_bucket_np(delta * dilation).astype(np.int32)
    onehot = jax.nn.one_hot(jnp.asarray(bucket), NUM_BUCKETS, dtype=F32)
    bias = jnp.einsum('qkb,bh->hqk', onehot, rel_bias_g.astype(F32), precision=lax.Precision.HIGHEST)
    return jnp.where(jnp.asarray(np.abs(delta) <= WIN)[None], bias, NEG_INF)


ATTN_POS_PER_STEP = 4096
ATTN_TILES_PER_ITER = 4


def _attn_kernel(q_ref, kp_ref, kc_ref, kn_ref, vp_ref, vc_ref, vn_ref, bm_ref, o_ref, l_ref,
                 kbuf, vbuf, *, tq, dil, sub_len):
    t = pl.program_id(2)
    halo = WIN * dil
    span = tq * dil
    kbuf[0:halo] = kp_ref[...]
    kbuf[halo:halo + span] = kc_ref[...]
    kbuf[halo + span:] = kn_ref[...]
    vbuf[0:halo] = vp_ref[...]
    vbuf[halo:halo + span] = vc_ref[...]
    vbuf[halo + span:] = vn_ref[...]
    lane = lax.broadcasted_iota(jnp.int32, (1, LANES), 1)
    is_lo = lane < HEAD_DIM
    kcol = lax.broadcasted_iota(jnp.int32, (1, 4 * WIN), 1)
    qt = 2 * WIN
    nq = tq // qt

    def tile(c, i):
        row0 = i * (qt * dil) + c
        kpos = t * tq + i * qt - WIN + kcol
        pen = jnp.where((kpos >= 0) & (kpos < sub_len), 0.0, NEG_INF).astype(F32)
        q = (q_ref[pl.ds(row0, qt, stride=dil), :] * (HEAD_DIM ** -0.5)).astype(BF16)
        k = kbuf[pl.ds(row0, 2 * qt, stride=dil), :].astype(BF16)
        v = vbuf[pl.ds(row0, 2 * qt, stride=dil), :].astype(BF16)
        outs, lses = [], []
        for hh in range(2):
            sel = is_lo if hh == 0 else jnp.logical_not(is_lo)
            qm = jnp.where(sel, q, jnp.zeros_like(q))
            s = lax.dot_general(qm, k, (((1,), (1,)), ((), ())), preferred_element_type=F32)
            s = s + bm_ref[hh] + pen
            m = jnp.max(s, axis=-1, keepdims=True)
            p = jnp.exp(s - m)
            den = jnp.sum(p, axis=-1, keepdims=True)
            pv = jnp.dot(p.astype(BF16), v, preferred_element_type=F32)
            outs.append(pv / den)
            lses.append(m + jnp.log(den))
        o_ref[pl.ds(row0, qt, stride=dil), :] = jnp.where(is_lo, outs[0], outs[1])
        l_ref[pl.ds(row0, qt, stride=dil), :] = jnp.where(is_lo, lses[0], lses[1])

    def body(it, carry):
        for u in range(ATTN_TILES_PER_ITER):
            idx = it * ATTN_TILES_PER_ITER + u
            tile(idx // nq, idx % nq)
        return carry

    lax.fori_loop(0, dil * nq // ATTN_TILES_PER_ITER, body, 0)


def _attention_group(qkv, rel_bias, g, batch, seq):
    _, dil = ATTN_GROUPS[g]
    sub_len = seq // dil
    tq = min(ATTN_POS_PER_STEP // dil, sub_len)
    nt = sub_len // tq
    span = tq * dil
    halo = WIN * dil
    n_hp = ATTN_WIDTH // LANES
    n_slab = N_GROUPS * n_hp
    bm = _bias_mask(rel_bias[:, g * HEADS_PER_GROUP:(g + 1) * HEADS_PER_GROUP], dil)
    hb = tq // WIN
    nhb = sub_len // WIN

    def cur(which):
        return pl.BlockSpec((None, span, LANES), lambda b, hp, t: (which * n_slab + g * n_hp + hp, b * nt + t, 0))

    def prev(which):
        return pl.BlockSpec((None, halo, LANES),
                            lambda b, hp, t: (which * n_slab + g * n_hp + hp, b * nhb + jnp.maximum(t * hb - 1, 0), 0))

    def nxt(which):
        return pl.BlockSpec((None, halo, LANES),
                            lambda b, hp, t: (which * n_slab + g * n_hp + hp, b * nhb + jnp.minimum((t + 1) * hb, nhb - 1), 0))

    out_spec = pl.BlockSpec((None, span, LANES), lambda b, hp, t: (hp, b * nt + t, 0))
    return pl.pallas_call(
        functools.partial(_attn_kernel, tq=tq, dil=dil, sub_len=sub_len),
        grid=(batch, n_hp, nt),
        in_specs=[cur(0), prev(1), cur(1), nxt(1), prev(2), cur(2), nxt(2),
                  pl.BlockSpec((2, 2 * WIN, 4 * WIN), lambda b, hp, t: (hp, 0, 0))],
        out_specs=[out_spec, out_spec],
        out_shape=[jax.ShapeDtypeStruct((n_hp, batch * seq, LANES), F32)] * 2,
        scratch_shapes=[pltpu.VMEM((span + 2 * halo, LANES), F32)] * 2,
        compiler_params=_cparams(("parallel", "parallel", "arbitrary")),
        name=f"attn_g{g}",
    )(qkv, qkv, qkv, qkv, qkv, qkv, qkv, bm)


def _shortconv_kernel(p_ref, c_ref, n_ref, w_ref, b_ref, o_ref, *, rows, halo):
    t = pl.program_id(1)
    nt = pl.num_programs(1)
    x = c_ref[...].astype(F32)
    prev_row = jnp.where(t > 0, p_ref[...].astype(F32)[halo - 1:halo, :], 0.0)
    next_row = jnp.where(t < nt - 1, n_ref[...].astype(F32)[0:1, :], 0.0)
    row = lax.broadcasted_iota(jnp.int32, (rows, 1), 0)
    up = jnp.where(row == 0, prev_row, pltpu.roll(x, 1, axis=0))
    dn = jnp.where(row == rows - 1, next_row, pltpu.roll(x, rows - 1, axis=0))
    res = up * w_ref[0:1, :] + x * w_ref[1:2, :] + dn * w_ref[2:3, :] + b_ref[...]
    _store_padded_slabs(o_ref, res, rows // DFT_N2)


def _store_padded_slabs(o_ref, val, nblk):
    pad = jnp.zeros((PITCH_Z - DFT_N2, LANES), F32)
    for s in range(val.shape[1] // LANES):
        for jb in range(nblk):
            o_ref[s, jb * PITCH_Z:jb * PITCH_Z + DFT_N2, :] = val[jb * DFT_N2:(jb + 1) * DFT_N2, s * LANES:(s + 1) * LANES]
            o_ref[s, jb * PITCH_Z + DFT_N2:(jb + 1) * PITCH_Z, :] = pad


def _load_padded_slabs(ref, nblk):
    return jnp.concatenate(
        [jnp.concatenate([ref[s, jb * PITCH_Z:jb * PITCH_Z + DFT_N2, :] for jb in range(nblk)], axis=0)
         for s in range(ref.shape[0])], axis=1)


def _shortconv(proj, conv_w, conv_b, batch, seq, rows=512):
    rows = min(rows, seq)
    prow = rows // DFT_N2 * PITCH_Z
    n_slab = HYENA_WIDTH // LANES
    halo = 16
    pv = proj.reshape(batch, seq, REST_WIDTH)
    cb = R_U // HYENA_WIDTH
    hb = rows // halo
    nhb = seq // halo
    return pl.pallas_call(
        functools.partial(_shortconv_kernel, rows=rows, halo=halo),
        grid=(batch, seq // rows, 3),
        in_specs=[
            pl.BlockSpec((None, halo, HYENA_WIDTH), lambda b, t, j: (b, jnp.maximum(t * hb - 1, 0), cb + j)),
            pl.BlockSpec((None, rows, HYENA_WIDTH), lambda b, t, j: (b, t, cb + j)),
            pl.BlockSpec((None, halo, HYENA_WIDTH), lambda b, t, j: (b, jnp.minimum((t + 1) * hb, nhb - 1), cb + j)),
            pl.BlockSpec((3, HYENA_WIDTH), lambda b, t, j: (0, j)),
            pl.BlockSpec((1, HYENA_WIDTH), lambda b, t, j: (0, j)),
        ],
        out_specs=pl.BlockSpec((None, n_slab, None, prow, LANES), lambda b, t, j: (j, 0, b, t, 0)),
        out_shape=jax.ShapeDtypeStruct((3, n_slab, batch, seq // DFT_N2 * PITCH_Z, LANES), F32),
        compiler_params=_cparams(("parallel", "parallel", "arbitrary")),
        name="shortconv",
    )(pv, pv, pv, conv_w, conv_b.reshape(1, -1))


def _filter_features(length):
    t = np.linspace(0.0, 1.0, length)[:, None]
    ang = (2.0 * math.pi / length) * np.arange(length, dtype=np.float64)[:, None]
    bands = np.linspace(1e-4, N_BANDS - 1, N_BANDS)[None]
    z = np.concatenate([t, np.cos(ang * bands), -np.sin(ang * bands)], axis=-1)
    zp = np.zeros((length, LANES), np.float32)
    zp[:, :FILTER_EMB] = z
    return zp


def _decay_rates():
    max_decay = math.log(1e-2) / 0.3
    min_decay = math.log(1e-2) / 1.5
    return np.abs(np.linspace(min_decay, max_decay, HYENA_WIDTH)).astype(np.float32)[None]


def _filter_kernel(z_ref, w1_ref, b1_ref, w2_ref, b2_ref, w3_ref, b3_ref, w4_ref, fr_ref, dl_ref, o_ref, *, rows):
    hi = lax.Precision.HIGHEST
    z = z_ref[...]
    fr = fr_ref[...]
    h = jnp.sin(fr * (jnp.dot(z, w1_ref[...], precision=hi, preferred_element_type=F32) + b1_ref[...]))
    h = jnp.sin(fr * (jnp.dot(h, w2_ref[...], precision=hi, preferred_element_type=F32) + b2_ref[...]))
    h = jnp.sin(fr * (jnp.dot(h, w3_ref[...], precision=hi, preferred_element_type=F32) + b3_ref[...]))
    hh, hl = _split(h)
    filt = jnp.dot(jnp.concatenate([hh, hl, hh], axis=1), w4_ref[...], preferred_element_type=F32)
    decay = jnp.exp(-z[:, 0:1] * dl_ref[...])
    row = pl.program_id(0) * rows + lax.broadcasted_iota(jnp.int32, (rows, 1), 0)
    for j in range(4):
        cs = slice(j * HYENA_WIDTH, (j + 1) * HYENA_WIDTH)
        val = filt[:, cs] * decay
        if j % 2 == 1:
            val = jnp.where(row == 0, 0.0, val)
        _store_padded_slabs(o_ref.at[j], val, rows // DFT_N2)


def _filters(length, fw1, fb1, fw2, fb2, fw3, fb3, fw4, ffreq, rows=256):
    rows = min(rows, length)
    n_slab = HYENA_WIDTH // LANES
    prow = rows // DFT_N2 * PITCH_Z
    w4h, w4l = _split(fw4.astype(F32))
    w4s = jnp.concatenate([w4h, w4h, w4l], axis=0)
    zfeat = jnp.asarray(_filter_features(length))
    w1p = jnp.zeros((LANES, FILTER_HIDDEN), F32).at[:FILTER_EMB].set(fw1.astype(F32))
    full = lambda shape: pl.BlockSpec(shape, lambda i: (0,) * len(shape))
    return pl.pallas_call(
        functools.partial(_filter_kernel, rows=rows),
        grid=(length // rows,),
        in_specs=[pl.BlockSpec((rows, LANES), lambda i: (i, 0)),
                  full((LANES, FILTER_HIDDEN)), full((1, FILTER_HIDDEN)),
                  full((FILTER_HIDDEN, FILTER_HIDDEN)), full((1, FILTER_HIDDEN)),
                  full((FILTER_HIDDEN, FILTER_HIDDEN)), full((1, FILTER_HIDDEN)),
                  full((3 * FILTER_HIDDEN, 4 * HYENA_WIDTH)), full((1, FILTER_HIDDEN)),
                  full((1, HYENA_WIDTH))],
        out_specs=pl.BlockSpec((4, n_slab, prow, LANES), lambda i: (0, 0, i, 0)),
        out_shape=jax.ShapeDtypeStruct((4, n_slab, length // DFT_N2 * PITCH_Z, LANES), F32),
        compiler_params=_cparams(("parallel",)),
        name="hyena_filters",
    )(zfeat, w1p, fb1.reshape(1, -1), fw2, fb2.reshape(1, -1), fw3, fb3.reshape(1, -1), w4s,
      ffreq.reshape(1, -1), jnp.asarray(_decay_rates()))


def _dft_tables(length):
    n = 2 * length
    n1 = n // DFT_N2
    h1 = n1 // 2
    kk = np.arange(h1)[:, None] + 0.5
    th = 2.0 * math.pi * kk * np.arange(h1)[None, :] / n1
    fa = np.concatenate([np.cos(th), -np.sin(th)], axis=0)
    ga = (2.0 / n) * fa.T
    k = np.arange(h1)[:, None, None] + n1 * np.arange(DFT_N2)[None, :, None] + 0.5
    ph = 2.0 * math.pi * k * np.arange(DFT_N2)[None, None, :] / n
    f = lambda a: jnp.asarray(a.astype(np.float32))
    er, ei = f(np.cos(ph)), f(-np.sin(ph))
    m = jnp.concatenate([jnp.concatenate([er, -ei], axis=2), jnp.concatenate([ei, er], axis=2)], axis=1)
    return f(fa), f(ga), m.astype(BF16)


def _split(a):
    hi = a.astype(BF16)
    lo = (a - hi.astype(F32)).astype(BF16)
    return hi, lo


CONV_K1_CHUNK = 32
CONV_PAIR_UNROLL = 8
CONV_FREQ_UNROLL = 4


def _spectrum_kernel(fa_ref, m_ref, xf_ref, xb_ref, o_ref, s_ref, *, h1, kc):
    hf = pl.program_id(2)
    half = DFT_N2

    @pl.when(hf == 0)
    def _():
        fa = fa_ref[...]

        def body(n2, carry):
            x = jnp.concatenate([xf_ref[pl.ds(n2, h1, stride=PITCH_Z), :],
                                 xb_ref[pl.ds(n2, h1, stride=PITCH_Z), :]], axis=1)
            xh, xl = _split(x)
            a = jnp.dot(fa, jnp.concatenate([xh, xh, xl], axis=0), preferred_element_type=F32)
            for ri in range(2):
                for d in range(2):
                    s_ref.at[d][pl.ds(ri * half + n2, h1, stride=PITCH_S), :] = (
                        a[ri * h1:(ri + 1) * h1, d * LANES:(d + 1) * LANES])
            return carry

        lax.fori_loop(0, DFT_N2, body, 0, unroll=CONV_PAIR_UNROLL)

    def freq_body(k1, carry):
        row = pl.multiple_of((hf * kc + k1) * PITCH_S, 8)
        x = jnp.concatenate([s_ref.at[0][pl.ds(row, 2 * half), :],
                             s_ref.at[1][pl.ds(row, 2 * half), :]], axis=1).astype(BF16)
        z = jnp.dot(m_ref[k1], x, preferred_element_type=F32)
        o_ref[k1, 0:half, :] = z[:half, :LANES] + z[:half, LANES:]
        o_ref[k1, half:, :] = z[half:, :LANES] - z[half:, LANES:]
        return carry

    lax.fori_loop(0, kc, freq_body, 0, unroll=CONV_FREQ_UNROLL)


def _filter_spectrum(fa, m_bf16, filt):
    _, n_slab, prow, _ = filt.shape
    n1, h1 = fa.shape
    kc = min(CONV_K1_CHUNK, h1)
    fh, fl = _split(fa)
    fa3 = jnp.concatenate([fh, fl, fh], axis=1)
    return pl.pallas_call(
        functools.partial(_spectrum_kernel, h1=h1, kc=kc),
        grid=(2, n_slab, h1 // kc),
        in_specs=[pl.BlockSpec((n1, 3 * h1), lambda o, s, hf: (0, 0)),
                  pl.BlockSpec((kc, 2 * DFT_N2, 2 * DFT_N2), lambda o, s, hf: (hf, 0, 0)),
                  pl.BlockSpec((None, None, prow, LANES), lambda o, s, hf: (2 * o, s, 0, 0)),
                  pl.BlockSpec((None, None, prow, LANES), lambda o, s, hf: (2 * o + 1, s, 0, 0))],
        out_specs=pl.BlockSpec((None, None, kc, 2 * DFT_N2, LANES), lambda o, s, hf: (o, s, hf, 0, 0)),
        out_shape=jax.ShapeDtypeStruct((2, n_slab, h1, 2 * DFT_N2, LANES), F32),
        scratch_shapes=[pltpu.VMEM((2, h1 * PITCH_S, LANES), F32)],
        compiler_params=pltpu.CompilerParams(dimension_semantics=("parallel", "parallel", "arbitrary"),
                                             vmem_limit_bytes=CONV_VMEM_LIMIT),
        name="filter_spectrum",
    )(fa3, m_bf16, filt, filt)


def _conv_kernel(fa_ref, ga_ref, m_ref, k_ref, z_ref, hx_ref, sk_ref, o_ref, s_ref, *, h1, kc):
    hf = pl.program_id(2)
    half = DFT_N2
    pairs = DFT_N2 // 2

    @pl.when(hf == 0)
    def _():
        fa = fa_ref[...]

        def body(j, carry):
            n2 = 2 * j
            x = jnp.concatenate([z_ref[pl.ds(n2, h1, stride=PITCH_Z), :],
                                 z_ref[pl.ds(n2 + 1, h1, stride=PITCH_Z), :]], axis=1).astype(BF16)
            a = jnp.dot(fa, x, preferred_element_type=F32)
            for ri in range(2):
                for u in range(2):
                    s_ref[pl.ds(ri * half + n2 + u, h1, stride=PITCH_S), :] = (
                        a[ri * h1:(ri + 1) * h1, u * LANES:(u + 1) * LANES])
            return carry

        lax.fori_loop(0, pairs, body, 0, unroll=CONV_PAIR_UNROLL)

    def freq_body(k1, carry):
        row = pl.multiple_of((hf * kc + k1) * PITCH_S, 8)
        x = s_ref[pl.ds(row, 2 * half), :].astype(BF16)
        m = m_ref[k1]
        z = jnp.dot(m, x, preferred_element_type=F32)
        zr, zi = z[:half], z[half:]
        kr, ki = k_ref[k1, 0:half, :], k_ref[k1, half:, :]
        y = jnp.concatenate([zr * kr - zi * ki, zr * ki + zi * kr], axis=0).astype(BF16)
        s_ref[pl.ds(row, 2 * half), :] = lax.dot_general(m, y, (((0,), (0,)), ((), ())),
                                                         preferred_element_type=F32)
        return carry

    lax.fori_loop(0, kc, freq_body, 0, unroll=CONV_FREQ_UNROLL)

    @pl.when(hf == pl.num_programs(2) - 1)
    def _():
        ga = ga_ref[...]
        sk = sk_ref[...]

        def body(j, carry):
            n2 = 2 * j
            cols = []
            for u in range(2):
                re = s_ref[pl.ds(n2 + u, h1, stride=PITCH_S), :]
                im = s_ref[pl.ds(half + n2 + u, h1, stride=PITCH_S), :]
                cols.append(jnp.concatenate([re, im], axis=0))
            bn = jnp.concatenate(cols, axis=1).astype(BF16)
            y = jnp.dot(ga, bn, preferred_element_type=F32)
            for u in range(2):
                zs = z_ref[pl.ds(n2 + u, h1, stride=PITCH_Z), :]
                hs = hx_ref[pl.ds(n2 + u, h1, stride=PITCH_Z), :]
                o_ref[pl.ds(n2 + u, h1, stride=PITCH_Z), :] = hs * (y[:, u * LANES:(u + 1) * LANES] + sk * zs)
            return carry

        lax.fori_loop(0, pairs, body, 0, unroll=CONV_PAIR_UNROLL)
        pad = jnp.zeros((PITCH_Z - DFT_N2, LANES), F32)
        for blk in range(h1):
            o_ref[blk * PITCH_Z + DFT_N2:(blk + 1) * PITCH_Z, :] = pad


def _long_conv_gate(tabs, kspec, order, z5, zsel, hx5, hxsel, skip):
    fa, ga, m_tab = tabs
    _, n_slab, batch, prow, _ = z5.shape
    n1, h1 = fa.shape
    kc = min(CONV_K1_CHUNK, h1)
    slab = lambda sel: pl.BlockSpec((None, None, None, prow, LANES), lambda s, b, hf: (sel, s, b, 0, 0))
    return pl.pallas_call(
        functools.partial(_conv_kernel, h1=h1, kc=kc),
        grid=(n_slab, batch, h1 // kc),
        in_specs=[pl.BlockSpec((n1, h1), lambda s, b, hf: (0, 0)),
                  pl.BlockSpec((h1, n1), lambda s, b, hf: (0, 0)),
                  pl.BlockSpec((kc, 2 * DFT_N2, 2 * DFT_N2), lambda s, b, hf: (hf, 0, 0)),
                  pl.BlockSpec((None, None, kc, 2 * DFT_N2, LANES), lambda s, b, hf: (order, s, hf, 0, 0)),
                  slab(zsel), slab(hxsel),
                  pl.BlockSpec((None, None, 1, LANES), lambda s, b, hf: (order, s, 0, 0))],
        out_specs=pl.BlockSpec((None, None, prow, LANES), lambda s, b, hf: (s, b, 0, 0)),
        out_shape=jax.ShapeDtypeStruct((n_slab, batch, prow, LANES), F32),
        scratch_shapes=[pltpu.VMEM((h1 * PITCH_S, LANES), F32)],
        compiler_params=pltpu.CompilerParams(dimension_semantics=("parallel", "parallel", "arbitrary"),
                                             vmem_limit_bytes=CONV_VMEM_LIMIT),
        name="hyena_long_conv",
    )(fa.astype(BF16), ga.astype(BF16), m_tab, kspec, z5, hx5, skip)


def _hyena(proj, batch, seq, conv_w, conv_b, fparams, skip):
    c = HYENA_WIDTH
    fa, ga, m_tab = _dft_tables(seq)
    kspec = _filter_spectrum(fa, m_tab, _filters(seq, *fparams))
    u5 = _shortconv(proj, conv_w, conv_b, batch, seq)
    skip4 = skip.astype(F32).reshape(2, c // LANES, 1, LANES)
    tabs = (fa, ga, m_tab)
    z1 = _long_conv_gate(tabs, kspec, 0, u5, 0, u5, 1, skip4)
    return _long_conv_gate(tabs, kspec, 1, z1[None], 0, u5, 2, skip4)


def _final_kernel(o1_ref, o2_ref, o3_ref, l1_ref, l2_ref, l3_ref, ga_ref, hy_ref, gh_ref, ma_ref, mh_ref,
                  x_ref, wa_ref, wh_ref, wo_ref, pg_ref, y_ref):
    def cat(ref):
        return jnp.concatenate([ref[s] for s in range(ATTN_WIDTH // LANES)], axis=1)

    l1, l2, l3 = cat(l1_ref), cat(l2_ref), cat(l3_ref)
    mx = jnp.maximum(jnp.maximum(l1, l2), l3)
    e1, e2, e3 = jnp.exp(l1 - mx), jnp.exp(l2 - mx), jnp.exp(l3 - mx)
    attn = (e1 * cat(o1_ref) + e2 * cat(o2_ref) + e3 * cat(o3_ref)) / (e1 + e2 + e3)
    ga = ga_ref[...].astype(F32)
    a_in = (attn * (ga * jax.nn.sigmoid(ga))).astype(BF16)
    a_br = jnp.dot(a_in, wa_ref[...], preferred_element_type=F32)
    gh = gh_ref[...].astype(F32)
    hy = _load_padded_slabs(hy_ref, gh.shape[0] // DFT_N2)
    h_in = (hy * (gh * jax.nn.sigmoid(gh))).astype(BF16)
    h_br = jnp.dot(h_in, wh_ref[...], preferred_element_type=F32)
    merged = jax.nn.sigmoid(ma_ref[...].astype(F32)) * a_br + jax.nn.sigmoid(mh_ref[...].astype(F32)) * h_br
    out = jnp.dot(merged.astype(BF16), wo_ref[...], preferred_element_type=F32)
    ms = jnp.mean(out * out, axis=-1, keepdims=True)
    y_ref[...] = x_ref[...] + out * lax.rsqrt(ms + EPS) * pg_ref[...]


def _final(os_, ls_, proj, hy, x2d, wa, wh, wo, pg, seq, tm=256):
    rows = x2d.shape[0]
    tm = min(tm, rows)
    tiles_per_seq = seq // tm
    hy_spec = pl.BlockSpec((HYENA_WIDTH // LANES, None, tm // DFT_N2 * PITCH_Z, LANES),
                           lambda i: (0, i // tiles_per_seq, i % tiles_per_seq, 0))
    slab = pl.BlockSpec((ATTN_WIDTH // LANES, tm, LANES), lambda i: (0, i, 0))
    a512 = lambda off: pl.BlockSpec((tm, ATTN_WIDTH), lambda i: (i, off))
    a1024 = lambda off: pl.BlockSpec((tm, D_MODEL), lambda i: (i, off))
    full = lambda shape: pl.BlockSpec(shape, lambda i: (0, 0))
    return pl.pallas_call(
        _final_kernel,
        grid=(rows // tm,),
        in_specs=[slab] * 6 + [a512(R_GA // ATTN_WIDTH), hy_spec, a1024(R_GH // D_MODEL),
                               a1024(R_MG // D_MODEL), a1024(R_MG // D_MODEL + 1), a1024(0),
                               full((ATTN_WIDTH, D_MODEL)), full((D_MODEL, D_MODEL)), full((D_MODEL, D_MODEL)),
                               full((1, D_MODEL))],
        out_specs=pl.BlockSpec((tm, D_MODEL), lambda i: (i, 0)),
        out_shape=jax.ShapeDtypeStruct((rows, D_MODEL), F32),
        compiler_params=_cparams(("parallel",)),
        name="merge_out",
    )(*os_, *ls_, proj, hy, proj, proj, proj, x2d, wa, wh, wo, pg)


def _layer(x, rel_bias, pre_g, post_g, w_in, conv_w, conv_b, fparams, skip, w_br_a, w_br_h, w_out):
    batch, seq, _ = x.shape
    x2d = x.reshape(batch * seq, D_MODEL)
    w_qkv, w_rest = w_in
    gain = pre_g.reshape(1, -1).astype(F32)
    qkv = _inproj(x2d, gain, w_qkv, tn=QKV_WIDTH // 3, slab_out=True)
    proj = _inproj(x2d, gain, w_rest, tn=REST_WIDTH // 4, slab_out=False)
    os_, ls_ = [], []
    for g in range(N_GROUPS):
        o, l = _attention_group(qkv, rel_bias, g, batch, seq)
        os_.append(o)
        ls_.append(l)
    hy = _hyena(proj, batch, seq, conv_w, conv_b, fparams, skip)
    y = _final(os_, ls_, proj, hy, x2d, w_br_a, w_br_h, w_out, post_g.reshape(1, -1).astype(F32), seq)
    return y.reshape(batch, seq, D_MODEL)


def kernel(x_prompt, x_sample, rel_bias, pre_norm_g, post_norm_g, w_in, conv_w, conv_b, filt_w1, filt_b1, filt_w2, filt_b2, filt_w3, filt_b3, filt_w4, filt_freq, hyena_skip, w_branch_a, w_branch_h, w_out):
    depth = w_in.shape[0]

    def run(x):
        for l in range(depth):
            fparams = (filt_w1[l], filt_b1[l], filt_w2[l], filt_b2[l], filt_w3[l], filt_b3[l], filt_w4[l], filt_freq[l])
            w = w_in[l]
            w_qkv = w[:, :QKV_WIDTH].astype(BF16)
            w_rest = jnp.concatenate([w[:, U_OFF:], w[:, GA_OFF:U_OFF]], axis=1).astype(BF16)
            x = _layer(x, rel_bias, pre_norm_g[l], post_norm_g[l], (w_qkv, w_rest), conv_w[l], conv_b[l],
                       fparams, hyena_skip[l], w_branch_a[l].astype(BF16), w_branch_h[l].astype(BF16),
                       w_out[l].astype(BF16))
        return x

    return (run(x_prompt), run(x_sample))
```

```python
import functools
import math

import numpy as np
import jax
import jax.numpy as jnp
from jax import lax
from jax.experimental import pallas as pl
from jax.experimental.pallas import tpu as pltpu

F32 = jnp.float32
BF16 = jnp.bfloat16

D_MODEL = 1024
EPS = 1e-6
HEAD_DIM = 64
ATTN_GROUPS = ((128, 1), (512, 4), (2048, 16))
N_GROUPS = 3
HEADS_PER_GROUP = 8
ATTN_WIDTH = HEADS_PER_GROUP * HEAD_DIM
HYENA_WIDTH = 1024
FILTER_EMB = 33
N_BANDS = 16
FILTER_HIDDEN = 64
NUM_BUCKETS = 32
MAX_DISTANCE = 1024
NEG_INF = -1e30
QKV_WIDTH = 4608
GA_OFF, U_OFF, GH_OFF, MG_OFF = 4608, 5120, 8192, 9216
REST_WIDTH = 6656
R_U, R_GH, R_MG, R_GA = 0, 3072, 4096, 6144

WIN = 64
LANES = 128
DFT_N2 = 128
PITCH_Z = 136
PITCH_S = 264
VMEM_LIMIT = 48 * 1024 * 1024
CONV_VMEM_LIMIT = 58 * 1024 * 1024


def _cparams(sem):
    return pltpu.CompilerParams(dimension_semantics=sem, vmem_limit_bytes=VMEM_LIMIT)


def _inproj_kernel(x_ref, g_ref, w_ref, o_ref, hn_ref, *, slabs):
    @pl.when(pl.program_id(1) == 0)
    def _():
        x = x_ref[...]
        ms = jnp.mean(x * x, axis=-1, keepdims=True)
        hn_ref[...] = (x * lax.rsqrt(ms + EPS) * g_ref[...]).astype(BF16)

    acc = jnp.dot(hn_ref[...], w_ref[...], preferred_element_type=F32)
    if slabs:
        for s in range(slabs):
            o_ref[s] = acc[:, s * LANES:(s + 1) * LANES]
    else:
        o_ref[...] = acc.astype(o_ref.dtype)


def _inproj(x2d, g, w_bf16, tn, slab_out, tm=1024):
    rows = x2d.shape[0]
    width = w_bf16.shape[1]
    tm = min(tm, rows)
    if slab_out:
        ns = tn // LANES
        out_spec = pl.BlockSpec((ns, tm, LANES), lambda i, j: (j, i, 0))
        out_shape = jax.ShapeDtypeStruct((width // LANES, rows, LANES), F32)
    else:
        ns = 0
        out_spec = pl.BlockSpec((tm, tn), lambda i, j: (i, j))
        out_shape = jax.ShapeDtypeStruct((rows, width), BF16)
    return pl.pallas_call(
        functools.partial(_inproj_kernel, slabs=ns),
        grid=(rows // tm, width // tn),
        in_specs=[
            pl.BlockSpec((tm, D_MODEL), lambda i, j: (i, 0)),
            pl.BlockSpec((1, D_MODEL), lambda i, j: (0, 0)),
            pl.BlockSpec((D_MODEL, tn), lambda i, j: (0, j)),
        ],
        out_specs=out_spec,
        out_shape=out_shape,
        scratch_shapes=[pltpu.VMEM((tm, D_MODEL), BF16)],
        compiler_params=_cparams(("parallel", "arbitrary")),
        name="inproj_qkv" if slab_out else "inproj_rest",
    )(x2d, g, w_bf16)


def _t5_bucket_np(rel):
    half = NUM_BUCKETS // 2
    max_exact = half // 2
    n = np.abs(rel)
    nf = np.maximum(n, 1).astype(np.float64)
    large = max_exact + (np.log(nf / max_exact) / math.log(MAX_DISTANCE / max_exact) * (half - max_exact)).astype(np.int64)
    large = np.minimum(large, half - 1)
    return np.where(rel > 0, half, 0) + np.where(n < max_exact, n, large)


def _bias_mask(rel_bias_g, dilation):
    qi = np.arange(2 * WIN)[:, None]
    kj = np.arange(4 * WIN)[None, :] - WIN
    delta = kj - qi
    bucket = _t5_bucket_np(delta * dilation).astype(np.int32)
    onehot = jax.nn.one_hot(jnp.asarray(bucket), NUM_BUCKETS, dtype=F32)
    bias = jnp.einsum('qkb,bh->hqk', onehot, rel_bias_g.astype(F32), precision=lax.Precision.HIGHEST)
    return jnp.where(jnp.asarray(np.abs(delta) <= WIN)[None], bias, NEG_INF)


ATTN_POS_PER_STEP = 4096
ATTN_TILES_PER_ITER = 4


def _attn_kernel(q_ref, kp_ref, kc_ref, kn_ref, vp_ref, vc_ref, vn_ref, bm_ref, o_ref, l_ref,
                 kbuf, vbuf, *, tq, dil, sub_len):
    t = pl.program_id(2)
    halo = WIN * dil
    span = tq * dil
    kbuf[0:halo] = kp_ref[...]
    kbuf[halo:halo + span] = kc_ref[...]
    kbuf[halo + span:] = kn_ref[...]
    vbuf[0:halo] = vp_ref[...]
    vbuf[halo:halo + span] = vc_ref[...]
    vbuf[halo + span:] = vn_ref[...]
    lane = lax.broadcasted_iota(jnp.int32, (1, LANES), 1)
    is_lo = lane < HEAD_DIM
    kcol = lax.broadcasted_iota(jnp.int32, (1, 4 * WIN), 1)
    qt = 2 * WIN
    nq = tq // qt

    def tile(c, i):
        row0 = i * (qt * dil) + c
        kpos = t * tq + i * qt - WIN + kcol
        pen = jnp.where((kpos >= 0) & (kpos < sub_len), 0.0, NEG_INF).astype(F32)
        q = (q_ref[pl.ds(row0, qt, stride=dil), :] * (HEAD_DIM ** -0.5)).astype(BF16)
        k = kbuf[pl.ds(row0, 2 * qt, stride=dil), :].astype(BF16)
        v = vbuf[pl.ds(row0, 2 * qt, stride=dil), :].astype(BF16)
        outs, lses = [], []
        for hh in range(2):
            sel = is_lo if hh == 0 else jnp.logical_not(is_lo)
            qm = jnp.where(sel, q, jnp.zeros_like(q))
            s = lax.dot_general(qm, k, (((1,), (1,)), ((), ())), preferred_element_type=F32)
            s = s + bm_ref[hh] + pen
            m = jnp.max(s, axis=-1, keepdims=True)
            p = jnp.exp(s - m)
            den = jnp.sum(p, axis=-1, keepdims=True)
            pv = jnp.dot(p.astype(BF16), v, preferred_element_type=F32)
            outs.append(pv / den)
            lses.append(m + jnp.log(den))
        o_ref[pl.ds(row0, qt, stride=dil), :] = jnp.where(is_lo, outs[0], outs[1])
        l_ref[pl.ds(row0, qt, stride=dil), :] = jnp.where(is_lo, lses[0], lses[1])

    def body(it, carry):
        for u in range(ATTN_TILES_PER_ITER):
            idx = it * ATTN_TILES_PER_ITER + u
            tile(idx // nq, idx % nq)
        return carry

    lax.fori_loop(0, dil * nq // ATTN_TILES_PER_ITER, body, 0)


def _attention_group(qkv, rel_bias, g, batch, seq):
    _, dil = ATTN_GROUPS[g]
    sub_len = seq // dil
    tq = min(ATTN_POS_PER_STEP // dil, sub_len)
    nt = sub_len // tq
    span = tq * dil
    halo = WIN * dil
    n_hp = ATTN_WIDTH // LANES
    n_slab = N_GROUPS * n_hp
    bm = _bias_mask(rel_bias[:, g * HEADS_PER_GROUP:(g + 1) * HEADS_PER_GROUP], dil)
    hb = tq // WIN
    nhb = sub_len // WIN

    def cur(which):
        return pl.BlockSpec((None, span, LANES), lambda b, hp, t: (which * n_slab + g * n_hp + hp, b * nt + t, 0))

    def prev(which):
        return pl.BlockSpec((None, halo, LANES),
                            lambda b, hp, t: (which * n_slab + g * n_hp + hp, b * nhb + jnp.maximum(t * hb - 1, 0), 0))

    def nxt(which):
        return pl.BlockSpec((None, halo, LANES),
                            lambda b, hp, t: (which * n_slab + g * n_hp + hp, b * nhb + jnp.minimum((t + 1) * hb, nhb - 1), 0))

    out_spec = pl.BlockSpec((None, span, LANES), lambda b, hp, t: (hp, b * nt + t, 0))
    return pl.pallas_call(
        functools.partial(_attn_kernel, tq=tq, dil=dil, sub_len=sub_len),
        grid=(batch, n_hp, nt),
        in_specs=[cur(0), prev(1), cur(1), nxt(1), prev(2), cur(2), nxt(2),
                  pl.BlockSpec((2, 2 * WIN, 4 * WIN), lambda b, hp, t: (hp, 0, 0))],
        out_specs=[out_spec, out_spec],
        out_shape=[jax.ShapeDtypeStruct((n_hp, batch * seq, LANES), F32)] * 2,
        scratch_shapes=[pltpu.VMEM((span + 2 * halo, LANES), F32)] * 2,
        compiler_params=_cparams(("parallel", "parallel", "arbitrary")),
        name=f"attn_g{g}",
    )(qkv, qkv, qkv, qkv, qkv, qkv, qkv, bm)


def _shortconv_kernel(p_ref, c_ref, n_ref, w_ref, b_ref, o_ref, *, rows, halo):
    t = pl.program_id(1)
    nt = pl.num_programs(1)
    x = c_ref[...].astype(F32)
    prev_row = jnp.where(t > 0, p_ref[...].astype(F32)[halo - 1:halo, :], 0.0)
    next_row = jnp.where(t < nt - 1, n_ref[...].astype(F32)[0:1, :], 0.0)
    row = lax.broadcasted_iota(jnp.int32, (rows, 1), 0)
    up = jnp.where(row == 0, prev_row, pltpu.roll(x, 1, axis=0))
    dn = jnp.where(row == rows - 1, next_row, pltpu.roll(x, rows - 1, axis=0))
    res = up * w_ref[0:1, :] + x * w_ref[1:2, :] + dn * w_ref[2:3, :] + b_ref[...]
    _store_padded_slabs(o_ref, res, rows // DFT_N2)


def _store_padded_slabs(o_ref, val, nblk):
    pad = jnp.zeros((PITCH_Z - DFT_N2, LANES), F32)
    for s in range(val.shape[1] // LANES):
        for jb in range(nblk):
            o_ref[s, jb * PITCH_Z:jb * PITCH_Z + DFT_N2, :] = val[jb * DFT_N2:(jb + 1) * DFT_N2, s * LANES:(s + 1) * LANES]
            o_ref[s, jb * PITCH_Z + DFT_N2:(jb + 1) * PITCH_Z, :] = pad


def _load_padded_slabs(ref, nblk):
    return jnp.concatenate(
        [jnp.concatenate([ref[s, jb * PITCH_Z:jb * PITCH_Z + DFT_N2, :] for jb in range(nblk)], axis=0)
         for s in range(ref.shape[0])], axis=1)


def _shortconv(proj, conv_w, conv_b, batch, seq, rows=1024):
    rows = min(rows, seq)
    prow = rows // DFT_N2 * PITCH_Z
    n_slab = HYENA_WIDTH // LANES
    halo = 16
    pv = proj.reshape(batch, seq, REST_WIDTH)
    cb = R_U // HYENA_WIDTH
    hb = rows // halo
    nhb = seq // halo
    return pl.pallas_call(
        functools.partial(_shortconv_kernel, rows=rows, halo=halo),
        grid=(batch, seq // rows, 3),
        in_specs=[
            pl.BlockSpec((None, halo, HYENA_WIDTH), lambda b, t, j: (b, jnp.maximum(t * hb - 1, 0), cb + j)),
            pl.BlockSpec((None, rows, HYENA_WIDTH), lambda b, t, j: (b, t, cb + j)),
            pl.BlockSpec((None, halo, HYENA_WIDTH), lambda b, t, j: (b, jnp.minimum((t + 1) * hb, nhb - 1), cb + j)),
            pl.BlockSpec((3, HYENA_WIDTH), lambda b, t, j: (0, j)),
            pl.BlockSpec((1, HYENA_WIDTH), lambda b, t, j: (0, j)),
        ],
        out_specs=pl.BlockSpec((None, n_slab, None, prow, LANES), lambda b, t, j: (j, 0, b, t, 0)),
        out_shape=jax.ShapeDtypeStruct((3, n_slab, batch, seq // DFT_N2 * PITCH_Z, LANES), F32),
        compiler_params=_cparams(("parallel", "parallel", "arbitrary")),
        name="shortconv",
    )(pv, pv, pv, conv_w, conv_b.reshape(1, -1))


def _filter_features(length):
    t = np.linspace(0.0, 1.0, length)[:, None]
    ang = (2.0 * math.pi / length) * np.arange(length, dtype=np.float64)[:, None]
    bands = np.linspace(1e-4, N_BANDS - 1, N_BANDS)[None]
    z = np.concatenate([t, np.cos(ang * bands), -np.sin(ang * bands)], axis=-1)
    zp = np.zeros((length, LANES), np.float32)
    zp[:, :FILTER_EMB] = z
    return zp


def _decay_rates():
    max_decay = math.log(1e-2) / 0.3
    min_decay = math.log(1e-2) / 1.5
    return np.abs(np.linspace(min_decay, max_decay, HYENA_WIDTH)).astype(np.float32)[None]


def _filter_kernel(z_ref, w1_ref, b1_ref, w2_ref, b2_ref, w3_ref, b3_ref, w4_ref, fr_ref, dl_ref, o_ref, *, rows):
    hi = lax.Precision.HIGHEST
    z = z_ref[...]
    fr = fr_ref[...]
    h = jnp.sin(fr * (jnp.dot(z, w1_ref[...], precision=hi, preferred_element_type=F32) + b1_ref[...]))
    h = jnp.sin(fr * (jnp.dot(h, w2_ref[...], precision=hi, preferred_element_type=F32) + b2_ref[...]))
    h = jnp.sin(fr * (jnp.dot(h, w3_ref[...], precision=hi, preferred_element_type=F32) + b3_ref[...]))
    hh, hl = _split(h)
    filt = jnp.dot(jnp.concatenate([hh, hl, hh], axis=1), w4_ref[...], preferred_element_type=F32)
    decay = jnp.exp(-z[:, 0:1] * dl_ref[...])
    row = pl.program_id(0) * rows + lax.broadcasted_iota(jnp.int32, (rows, 1), 0)
    for j in range(4):
        cs = slice(j * HYENA_WIDTH, (j + 1) * HYENA_WIDTH)
        val = filt[:, cs] * decay
        if j % 2 == 1:
            val = jnp.where(row == 0, 0.0, val)
        _store_padded_slabs(o_ref.at[j], val, rows // DFT_N2)


def _filters(length, fw1, fb1, fw2, fb2, fw3, fb3, fw4, ffreq, rows=256):
    rows = min(rows, length)
    n_slab = HYENA_WIDTH // LANES
    prow = rows // DFT_N2 * PITCH_Z
    w4h, w4l = _split(fw4.astype(F32))
    w4s = jnp.concatenate([w4h, w4h, w4l], axis=0)
    zfeat = jnp.asarray(_filter_features(length))
    w1p = jnp.zeros((LANES, FILTER_HIDDEN), F32).at[:FILTER_EMB].set(fw1.astype(F32))
    full = lambda shape: pl.BlockSpec(shape, lambda i: (0,) * len(shape))
    return pl.pallas_call(
        functools.partial(_filter_kernel, rows=rows),
        grid=(length // rows,),
        in_specs=[pl.BlockSpec((rows, LANES), lambda i: (i, 0)),
                  full((LANES, FILTER_HIDDEN)), full((1, FILTER_HIDDEN)),
                  full((FILTER_HIDDEN, FILTER_HIDDEN)), full((1, FILTER_HIDDEN)),
                  full((FILTER_HIDDEN, FILTER_HIDDEN)), full((1, FILTER_HIDDEN)),
                  full((3 * FILTER_HIDDEN, 4 * HYENA_WIDTH)), full((1, FILTER_HIDDEN)),
                  full((1, HYENA_WIDTH))],
        out_specs=pl.BlockSpec((4, n_slab, prow, LANES), lambda i: (0, 0, i, 0)),
        out_shape=jax.ShapeDtypeStruct((4, n_slab, length // DFT_N2 * PITCH_Z, LANES), F32),
        compiler_params=_cparams(("parallel",)),
        name="hyena_filters",
    )(zfeat, w1p, fb1.reshape(1, -1), fw2, fb2.reshape(1, -1), fw3, fb3.reshape(1, -1), w4s,
      ffreq.reshape(1, -1), jnp.asarray(_decay_rates()))


def _dft_tables(length):
    n = 2 * length
    n1 = n // DFT_N2
    h1 = n1 // 2
    kk = np.arange(h1)[:, None] + 0.5
    th = 2.0 * math.pi * kk * np.arange(h1)[None, :] / n1
    fa = np.concatenate([np.cos(th), -np.sin(th)], axis=0)
    ga = (2.0 / n) * fa.T
    k = np.arange(h1)[:, None, None] + n1 * np.arange(DFT_N2)[None, :, None] + 0.5
    ph = 2.0 * math.pi * k * np.arange(DFT_N2)[None, None, :] / n
    f = lambda a: jnp.asarray(a.astype(np.float32))
    er, ei = f(np.cos(ph)), f(-np.sin(ph))
    m = jnp.concatenate([jnp.concatenate([er, -ei], axis=2), jnp.concatenate([ei, er], axis=2)], axis=1)
    return f(fa), f(ga), m.astype(BF16)


def _split(a):
    hi = a.astype(BF16)
    lo = (a - hi.astype(F32)).astype(BF16)
    return hi, lo


CONV_K1_CHUNK = 32
CONV_PAIR_UNROLL = 8
CONV_FREQ_UNROLL = 8
CONV_VMEM_MARGIN = 4 * 1024 * 1024


def _spectrum_kernel(fa_ref, m_ref, xf_ref, xb_ref, o_ref, s_ref, *, h1, kc):
    hf = pl.program_id(2)
    half = DFT_N2

    @pl.when(hf == 0)
    def _():
        fa = fa_ref[...]

        def body(n2, carry):
            x = jnp.concatenate([xf_ref[pl.ds(n2, h1, stride=PITCH_Z), :],
                                 xb_ref[pl.ds(n2, h1, stride=PITCH_Z), :]], axis=1)
            xh, xl = _split(x)
            a = jnp.dot(fa, jnp.concatenate([xh, xh, xl], axis=0), preferred_element_type=F32)
            for ri in range(2):
                for d in range(2):
                    s_ref.at[d][pl.ds(ri * half + n2, h1, stride=PITCH_S), :] = (
                        a[ri * h1:(ri + 1) * h1, d * LANES:(d + 1) * LANES])
            return carry

        lax.fori_loop(0, DFT_N2, body, 0, unroll=CONV_PAIR_UNROLL)

    def freq_body(k1, carry):
        row = pl.multiple_of((hf * kc + k1) * PITCH_S, 8)
        x = jnp.concatenate([s_ref.at[0][pl.ds(row, 2 * half), :],
                             s_ref.at[1][pl.ds(row, 2 * half), :]], axis=1).astype(BF16)
        z = jnp.dot(m_ref[k1], x, preferred_element_type=F32)
        o_ref[k1, 0:half, :] = z[:half, :LANES] + z[:half, LANES:]
        o_ref[k1, half:, :] = z[half:, :LANES] - z[half:, LANES:]
        return carry

    lax.fori_loop(0, kc, freq_body, 0, unroll=CONV_FREQ_UNROLL)


def _filter_spectrum(fa, m_bf16, filt):
    _, n_slab, prow, _ = filt.shape
    n1, h1 = fa.shape
    kc = min(CONV_K1_CHUNK, h1)
    fh, fl = _split(fa)
    fa3 = jnp.concatenate([fh, fl, fh], axis=1)
    return pl.pallas_call(
        functools.partial(_spectrum_kernel, h1=h1, kc=kc),
        grid=(2, n_slab, h1 // kc),
        in_specs=[pl.BlockSpec((n1, 3 * h1), lambda o, s, hf: (0, 0)),
                  pl.BlockSpec((kc, 2 * DFT_N2, 2 * DFT_N2), lambda o, s, hf: (hf, 0, 0)),
                  pl.BlockSpec((None, None, prow, LANES), lambda o, s, hf: (2 * o, s, 0, 0)),
                  pl.BlockSpec((None, None, prow, LANES), lambda o, s, hf: (2 * o + 1, s, 0, 0))],
        out_specs=pl.BlockSpec((None, None, kc, 2 * DFT_N2, LANES), lambda o, s, hf: (o, s, hf, 0, 0)),
        out_shape=jax.ShapeDtypeStruct((2, n_slab, h1, 2 * DFT_N2, LANES), F32),
        scratch_shapes=[pltpu.VMEM((2, h1 * PITCH_S, LANES), F32)],
        compiler_params=pltpu.CompilerParams(dimension_semantics=("parallel", "parallel", "arbitrary"),
                                             vmem_limit_bytes=CONV_VMEM_LIMIT),
        name="filter_spectrum",
    )(fa3, m_bf16, filt, filt)


def _conv_kernel(fa_ref, ga_ref, m_ref, k_ref, z_ref, hx_ref, sk_ref, o_ref, s_ref, *, h1, kc, nb):
    hf = pl.program_id(2)
    half = DFT_N2
    pairs = DFT_N2 // 2

    @pl.when(hf == 0)
    def _():
        fa = fa_ref[...]

        def body(j, carry):
            n2 = 2 * j
            for bi in range(nb):
                zb, sb = z_ref.at[bi], s_ref.at[bi]
                x = jnp.concatenate([zb[pl.ds(n2, h1, stride=PITCH_Z), :],
                                     zb[pl.ds(n2 + 1, h1, stride=PITCH_Z), :]], axis=1).astype(BF16)
                a = jnp.dot(fa, x, preferred_element_type=F32)
                for ri in range(2):
                    for u in range(2):
                        sb[pl.ds(ri * half + n2 + u, h1, stride=PITCH_S), :] = (
                            a[ri * h1:(ri + 1) * h1, u * LANES:(u + 1) * LANES])
            return carry

        lax.fori_loop(0, pairs, body, 0, unroll=CONV_PAIR_UNROLL // nb)

    def freq_body(k1, carry):
        row = pl.multiple_of((hf * kc + k1) * PITCH_S, 8)
        x = jnp.concatenate([s_ref[bi, pl.ds(row, 2 * half), :] for bi in range(nb)], axis=1).astype(BF16)
        m = m_ref[k1]
        z = jnp.dot(m, x, preferred_element_type=F32)
        zr, zi = z[:half], z[half:]
        kr = jnp.concatenate([k_ref[k1, 0:half, :]] * nb, axis=1)
        ki = jnp.concatenate([k_ref[k1, half:, :]] * nb, axis=1)
        y = jnp.concatenate([zr * kr - zi * ki, zr * ki + zi * kr], axis=0).astype(BF16)
        bq = lax.dot_general(m, y, (((0,), (0,)), ((), ())), preferred_element_type=F32)
        for bi in range(nb):
            s_ref[bi, pl.ds(row, 2 * half), :] = bq[:, bi * LANES:(bi + 1) * LANES]
        return carry

    lax.fori_loop(0, kc, freq_body, 0, unroll=CONV_FREQ_UNROLL // nb)

    @pl.when(hf == pl.num_programs(2) - 1)
    def _():
        ga = ga_ref[...]
        sk = sk_ref[...]

        def body(j, carry):
            n2 = 2 * j
            for bi in range(nb):
                zb, hb, ob, sb = z_ref.at[bi], hx_ref.at[bi], o_ref.at[bi], s_ref.at[bi]
                cols = []
                for u in range(2):
                    re = sb[pl.ds(n2 + u, h1, stride=PITCH_S), :]
                    im = sb[pl.ds(half + n2 + u, h1, stride=PITCH_S), :]
                    cols.append(jnp.concatenate([re, im], axis=0))
                bn = jnp.concatenate(cols, axis=1).astype(BF16)
                y = jnp.dot(ga, bn, preferred_element_type=F32)
                for u in range(2):
                    zs = zb[pl.ds(n2 + u, h1, stride=PITCH_Z), :]
                    hs = hb[pl.ds(n2 + u, h1, stride=PITCH_Z), :]
                    ob[pl.ds(n2 + u, h1, stride=PITCH_Z), :] = hs * (y[:, u * LANES:(u + 1) * LANES] + sk * zs)
            return carry

        lax.fori_loop(0, pairs, body, 0, unroll=CONV_PAIR_UNROLL // nb)
        pad = jnp.zeros((PITCH_Z - DFT_N2, LANES), F32)
        for bi in range(nb):
            for blk in range(h1):
                o_ref[bi, blk * PITCH_Z + DFT_N2:(blk + 1) * PITCH_Z, :] = pad


def _conv_vmem_bytes(nb, prow, h1, kc):
    slabs = 3 * 2 * nb * prow * LANES * 4
    scratch = nb * h1 * PITCH_S * LANES * 4
    tables = 2 * kc * 2 * DFT_N2 * (2 * DFT_N2 * 2 + LANES * 4)
    return slabs + scratch + tables


def _long_conv_gate(tabs, kspec, order, z5, zsel, hx5, hxsel, skip):
    fa, ga, m_tab = tabs
    _, n_slab, batch, prow, _ = z5.shape
    n1, h1 = fa.shape
    kc = min(CONV_K1_CHUNK, h1)
    nb = 2 if (batch % 2 == 0 and _conv_vmem_bytes(2, prow, h1, kc) <= CONV_VMEM_LIMIT - CONV_VMEM_MARGIN) else 1
    slab = lambda sel: pl.BlockSpec((None, None, nb, prow, LANES), lambda s, b, hf: (sel, s, b, 0, 0))
    return pl.pallas_call(
        functools.partial(_conv_kernel, h1=h1, kc=kc, nb=nb),
        grid=(n_slab, batch // nb, h1 // kc),
        in_specs=[pl.BlockSpec((n1, h1), lambda s, b, hf: (0, 0)),
                  pl.BlockSpec((h1, n1), lambda s, b, hf: (0, 0)),
                  pl.BlockSpec((kc, 2 * DFT_N2, 2 * DFT_N2), lambda s, b, hf: (hf, 0, 0)),
                  pl.BlockSpec((None, None, kc, 2 * DFT_N2, LANES), lambda s, b, hf: (order, s, hf, 0, 0)),
                  slab(zsel), slab(hxsel),
                  pl.BlockSpec((None, None, 1, LANES), lambda s, b, hf: (order, s, 0, 0))],
        out_specs=pl.BlockSpec((None, nb, prow, LANES), lambda s, b, hf: (s, b, 0, 0)),
        out_shape=jax.ShapeDtypeStruct((n_slab, batch, prow, LANES), F32),
        scratch_shapes=[pltpu.VMEM((nb, h1 * PITCH_S, LANES), F32)],
        compiler_params=pltpu.CompilerParams(dimension_semantics=("parallel", "parallel", "arbitrary"),
                                             vmem_limit_bytes=CONV_VMEM_LIMIT),
        name="hyena_long_conv",
    )(fa.astype(BF16), ga.astype(BF16), m_tab, kspec, z5, hx5, skip)


def _hyena(proj, batch, seq, conv_w, conv_b, fparams, skip):
    c = HYENA_WIDTH
    fa, ga, m_tab = _dft_tables(seq)
    kspec = _filter_spectrum(fa, m_tab, _filters(seq, *fparams))
    u5 = _shortconv(proj, conv_w, conv_b, batch, seq)
    skip4 = skip.astype(F32).reshape(2, c // LANES, 1, LANES)
    tabs = (fa, ga, m_tab)
    z1 = _long_conv_gate(tabs, kspec, 0, u5, 0, u5, 1, skip4)
    return _long_conv_gate(tabs, kspec, 1, z1[None], 0, u5, 2, skip4)


def _final_kernel(o1_ref, o2_ref, o3_ref, l1_ref, l2_ref, l3_ref, ga_ref, hy_ref, gh_ref, ma_ref, mh_ref,
                  x_ref, wa_ref, wh_ref, wo_ref, pg_ref, y_ref):
    def cat(ref):
        return jnp.concatenate([ref[s] for s in range(ATTN_WIDTH // LANES)], axis=1)

    l1, l2, l3 = cat(l1_ref), cat(l2_ref), cat(l3_ref)
    mx = jnp.maximum(jnp.maximum(l1, l2), l3)
    e1, e2, e3 = jnp.exp(l1 - mx), jnp.exp(l2 - mx), jnp.exp(l3 - mx)
    attn = (e1 * cat(o1_ref) + e2 * cat(o2_ref) + e3 * cat(o3_ref)) / (e1 + e2 + e3)
    ga = ga_ref[...].astype(F32)
    a_in = (attn * (ga * jax.nn.sigmoid(ga))).astype(BF16)
    a_br = jnp.dot(a_in, wa_ref[...], preferred_element_type=F32)
    gh = gh_ref[...].astype(F32)
    hy = _load_padded_slabs(hy_ref, gh.shape[0] // DFT_N2)
    h_in = (hy * (gh * jax.nn.sigmoid(gh))).astype(BF16)
    h_br = jnp.dot(h_in, wh_ref[...], preferred_element_type=F32)
    merged = jax.nn.sigmoid(ma_ref[...].astype(F32)) * a_br + jax.nn.sigmoid(mh_ref[...].astype(F32)) * h_br
    out = jnp.dot(merged.astype(BF16), wo_ref[...], preferred_element_type=F32)
    ms = jnp.mean(out * out, axis=-1, keepdims=True)
    y_ref[...] = x_ref[...] + out * lax.rsqrt(ms + EPS) * pg_ref[...]


def _final(os_, ls_, proj, hy, x2d, wa, wh, wo, pg, seq, tm=256):
    rows = x2d.shape[0]
    tm = min(tm, rows)
    tiles_per_seq = seq // tm
    hy_spec = pl.BlockSpec((HYENA_WIDTH // LANES, None, tm // DFT_N2 * PITCH_Z, LANES),
                           lambda i: (0, i // tiles_per_seq, i % tiles_per_seq, 0))
    slab = pl.BlockSpec((ATTN_WIDTH // LANES, tm, LANES), lambda i: (0, i, 0))
    a512 = lambda off: pl.BlockSpec((tm, ATTN_WIDTH), lambda i: (i, off))
    a1024 = lambda off: pl.BlockSpec((tm, D_MODEL), lambda i: (i, off))
    full = lambda shape: pl.BlockSpec(shape, lambda i: (0, 0))
    return pl.pallas_call(
        _final_kernel,
        grid=(rows // tm,),
        in_specs=[slab] * 6 + [a512(R_GA // ATTN_WIDTH), hy_spec, a1024(R_GH // D_MODEL),
                               a1024(R_MG // D_MODEL), a1024(R_MG // D_MODEL + 1), a1024(0),
                               full((ATTN_WIDTH, D_MODEL)), full((D_MODEL, D_MODEL)), full((D_MODEL, D_MODEL)),
                               full((1, D_MODEL))],
        out_specs=pl.BlockSpec((tm, D_MODEL), lambda i: (i, 0)),
        out_shape=jax.ShapeDtypeStruct((rows, D_MODEL), F32),
        compiler_params=_cparams(("parallel",)),
        name="merge_out",
    )(*os_, *ls_, proj, hy, proj, proj, proj, x2d, wa, wh, wo, pg)


def _layer(x, rel_bias, pre_g, post_g, w_in, conv_w, conv_b, fparams, skip, w_br_a, w_br_h, w_out):
    batch, seq, _ = x.shape
    x2d = x.reshape(batch * seq, D_MODEL)
    w_qkv, w_rest = w_in
    gain = pre_g.reshape(1, -1).astype(F32)
    qkv = _inproj(x2d, gain, w_qkv, tn=QKV_WIDTH // 3, slab_out=True)
    proj = _inproj(x2d, gain, w_rest, tn=REST_WIDTH // 2, slab_out=False)
    os_, ls_ = [], []
    for g in range(N_GROUPS):
        o, l = _attention_group(qkv, rel_bias, g, batch, seq)
        os_.append(o)
        ls_.append(l)
    hy = _hyena(proj, batch, seq, conv_w, conv_b, fparams, skip)
    y = _final(os_, ls_, proj, hy, x2d, w_br_a, w_br_h, w_out, post_g.reshape(1, -1).astype(F32), seq)
    return y.reshape(batch, seq, D_MODEL)


def kernel(x_prompt, x_sample, rel_bias, pre_norm_g, post_norm_g, w_in, conv_w, conv_b, filt_w1, filt_b1, filt_w2, filt_b2, filt_w3, filt_b3, filt_w4, filt_freq, hyena_skip, w_branch_a, w_branch_h, w_out):
    depth = w_in.shape[0]

    def run(x):
        for l in range(depth):
            fparams = (filt_w1[l], filt_b1[l], filt_w2[l], filt_b2[l], filt_w3[l], filt_b3[l], filt_w4[l], filt_freq[l])
            w = w_in[l]
            w_qkv = w[:, :QKV_WIDTH].astype(BF16)
            w_rest = jnp.concatenate([w[:, U_OFF:], w[:, GA_OFF:U_OFF]], axis=1).astype(BF16)
            x = _layer(x, rel_bias, pre_norm_g[l], post_norm_g[l], (w_qkv, w_rest), conv_w[l], conv_b[l],
                       fparams, hyena_skip[l], w_branch_a[l].astype(BF16), w_branch_h[l].astype(BF16),
                       w_out[l].astype(BF16))
        return x

    return (run(x_prompt), run(x_sample))
```

```python
import functools
import math

import numpy as np
import jax
import jax.numpy as jnp
from jax import lax
from jax.experimental import pallas as pl
from jax.experimental.pallas import tpu as pltpu

F32 = jnp.float32
BF16 = jnp.bfloat16

D_MODEL = 1024
EPS = 1e-6
HEAD_DIM = 64
ATTN_GROUPS = ((128, 1), (512, 4), (2048, 16))
N_GROUPS = 3
HEADS_PER_GROUP = 8
ATTN_WIDTH = HEADS_PER_GROUP * HEAD_DIM
HYENA_WIDTH = 1024
FILTER_EMB = 33
N_BANDS = 16
FILTER_HIDDEN = 64
NUM_BUCKETS = 32
MAX_DISTANCE = 1024
NEG_INF = -1e30
QKV_WIDTH = 4608
GA_OFF, U_OFF, GH_OFF, MG_OFF = 4608, 5120, 8192, 9216
REST_WIDTH = 6656
R_U, R_GH, R_MG, R_GA = 0, 3072, 4096, 6144

WIN = 64
LANES = 128
DFT_N2 = 128
PITCH_Z = 136
PITCH_S = 264
VMEM_LIMIT = 48 * 1024 * 1024
CONV_VMEM_LIMIT = 58 * 1024 * 1024


def _cparams(sem):
    return pltpu.CompilerParams(dimension_semantics=sem, vmem_limit_bytes=VMEM_LIMIT)


def _inproj_kernel(x_ref, g_ref, w_ref, o_ref, hn_ref, *, slabs):
    @pl.when(pl.program_id(1) == 0)
    def _():
        x = x_ref[...]
        ms = jnp.mean(x * x, axis=-1, keepdims=True)
        hn_ref[...] = (x * lax.rsqrt(ms + EPS) * g_ref[...]).astype(BF16)

    acc = jnp.dot(hn_ref[...], w_ref[...], preferred_element_type=F32)
    if slabs:
        for s in range(slabs):
            o_ref[s] = _pack_bf16_pair(acc[:, (2 * s) * LANES:(2 * s + 1) * LANES],
                                       acc[:, (2 * s + 1) * LANES:(2 * s + 2) * LANES])
    else:
        o_ref[...] = acc.astype(o_ref.dtype)


def _pack_bf16_pair(a, b):
    ua = lax.bitcast_convert_type(a.astype(BF16).astype(F32), jnp.uint32)
    ub = lax.bitcast_convert_type(b.astype(BF16).astype(F32), jnp.uint32)
    return (ua >> 16) | (ub & jnp.uint32(0xFFFF0000))


def _unpack_bf16_pair(w, idx):
    bits = (w << 16) if idx == 0 else (w & jnp.uint32(0xFFFF0000))
    return lax.bitcast_convert_type(bits, F32).astype(BF16)


def _inproj(x2d, g, w_bf16, tn, slab_out, tm=1024):
    rows = x2d.shape[0]
    width = w_bf16.shape[1]
    tm = min(tm, rows)
    if slab_out:
        ns = tn // (2 * LANES)
        out_spec = pl.BlockSpec((ns, tm, LANES), lambda i, j: (j, i, 0))
        out_shape = jax.ShapeDtypeStruct((width // (2 * LANES), rows, LANES), jnp.uint32)
    else:
        ns = 0
        out_spec = pl.BlockSpec((tm, tn), lambda i, j: (i, j))
        out_shape = jax.ShapeDtypeStruct((rows, width), BF16)
    return pl.pallas_call(
        functools.partial(_inproj_kernel, slabs=ns),
        grid=(rows // tm, width // tn),
        in_specs=[
            pl.BlockSpec((tm, D_MODEL), lambda i, j: (i, 0)),
            pl.BlockSpec((1, D_MODEL), lambda i, j: (0, 0)),
            pl.BlockSpec((D_MODEL, tn), lambda i, j: (0, j)),
        ],
        out_specs=out_spec,
        out_shape=out_shape,
        scratch_shapes=[pltpu.VMEM((tm, D_MODEL), BF16)],
        compiler_params=_cparams(("parallel", "arbitrary")),
        name="inproj_qkv" if slab_out else "inproj_rest",
    )(x2d, g, w_bf16)


def _t5_bucket_np(rel):
    half = NUM_BUCKETS // 2
    max_exact = half // 2
    n = np.abs(rel)
    nf = np.maximum(n, 1).astype(np.float64)
    large = max_exact + (np.log(nf / max_exact) / math.log(MAX_DISTANCE / max_exact) * (half - max_exact)).astype(np.int64)
    large = np.minimum(large, half - 1)
    return np.where(rel > 0, half, 0) + np.where(n < max_exact, n, large)


def _bias_mask(rel_bias_g, dilation):
    qi = np.arange(2 * WIN)[:, None]
    kj = np.arange(4 * WIN)[None, :] - WIN
    delta = kj - qi
    bucket = _t5_bucket_np(delta * dilation).astype(np.int32)
    onehot = jax.nn.one_hot(jnp.asarray(bucket), NUM_BUCKETS, dtype=F32)
    bias = jnp.einsum('qkb,bh->hqk', onehot, rel_bias_g.astype(F32), precision=lax.Precision.HIGHEST)
    bias = jnp.where(jnp.asarray(np.abs(delta) <= WIN)[None], bias, NEG_INF)
    return bias.reshape(HEADS_PER_GROUP // 2, 2, 2 * WIN, 4 * WIN)


ATTN_POS_PER_STEP = 4096
ATTN_TILES_PER_ITER = 2
PAIRS_PER_SLAB = 2


def _attn_kernel(q_ref, kp_ref, kc_ref, kn_ref, vp_ref, vc_ref, vn_ref, bm_ref, o_ref, l_ref,
                 kbuf, vbuf, *, tq, dil, sub_len):
    t = pl.program_id(2)
    halo = WIN * dil
    span = tq * dil
    kbuf[0:halo] = kp_ref[...]
    kbuf[halo:halo + span] = kc_ref[...]
    kbuf[halo + span:] = kn_ref[...]
    vbuf[0:halo] = vp_ref[...]
    vbuf[halo:halo + span] = vc_ref[...]
    vbuf[halo + span:] = vn_ref[...]
    lane = lax.broadcasted_iota(jnp.int32, (1, LANES), 1)
    is_lo = lane < HEAD_DIM
    kcol = lax.broadcasted_iota(jnp.int32, (1, 4 * WIN), 1)
    qt = 2 * WIN
    nq = tq // qt

    def tile(c, i):
        row0 = i * (qt * dil) + c
        kpos = t * tq + i * qt - WIN + kcol
        pen = jnp.where((kpos >= 0) & (kpos < sub_len), 0.0, NEG_INF).astype(F32)
        qw = q_ref[pl.ds(row0, qt, stride=dil), :]
        kw = kbuf[pl.ds(row0, 2 * qt, stride=dil), :]
        vw = vbuf[pl.ds(row0, 2 * qt, stride=dil), :]
        for hp in range(PAIRS_PER_SLAB):
            q = _unpack_bf16_pair(qw, hp)
            k = _unpack_bf16_pair(kw, hp)
            v = _unpack_bf16_pair(vw, hp)
            outs, lses = [], []
            for hh in range(2):
                sel = is_lo if hh == 0 else jnp.logical_not(is_lo)
                qm = jnp.where(sel, q, jnp.zeros_like(q))
                s = lax.dot_general(qm, k, (((1,), (1,)), ((), ())), preferred_element_type=F32)
                s = s + bm_ref[hp, hh] + pen
                m = jnp.max(s, axis=-1, keepdims=True)
                p = jnp.exp(s - m)
                den = jnp.sum(p, axis=-1, keepdims=True)
                pv = jnp.dot(p.astype(BF16), v, preferred_element_type=F32)
                outs.append(pv / den)
                lses.append(m + jnp.log(den))
            o_ref[hp, pl.ds(row0, qt, stride=dil), :] = jnp.where(is_lo, outs[0], outs[1])
            l_ref[hp, pl.ds(row0, qt, stride=dil), :] = jnp.where(is_lo, lses[0], lses[1])

    def body(it, carry):
        for u in range(ATTN_TILES_PER_ITER):
            idx = it * ATTN_TILES_PER_ITER + u
            tile(idx // nq, idx % nq)
        return carry

    lax.fori_loop(0, dil * nq // ATTN_TILES_PER_ITER, body, 0)


def _attention_group(qkv, rel_bias, g, batch, seq):
    _, dil = ATTN_GROUPS[g]
    sub_len = seq // dil
    tq = min(ATTN_POS_PER_STEP // dil, sub_len)
    nt = sub_len // tq
    span = tq * dil
    halo = WIN * dil
    n_hp = ATTN_WIDTH // LANES
    n_ps = n_hp // PAIRS_PER_SLAB
    n_slab = N_GROUPS * n_ps
    bm = _bias_mask(rel_bias[:, g * HEADS_PER_GROUP:(g + 1) * HEADS_PER_GROUP], dil)
    hb = tq // WIN
    nhb = sub_len // WIN

    def cur(which):
        return pl.BlockSpec((None, span, LANES), lambda b, ps, t: (which * n_slab + g * n_ps + ps, b * nt + t, 0))

    def prev(which):
        return pl.BlockSpec((None, halo, LANES),
                            lambda b, ps, t: (which * n_slab + g * n_ps + ps, b * nhb + jnp.maximum(t * hb - 1, 0), 0))

    def nxt(which):
        return pl.BlockSpec((None, halo, LANES),
                            lambda b, ps, t: (which * n_slab + g * n_ps + ps, b * nhb + jnp.minimum((t + 1) * hb, nhb - 1), 0))

    out_spec = pl.BlockSpec((PAIRS_PER_SLAB, span, LANES), lambda b, ps, t: (ps, b * nt + t, 0))
    return pl.pallas_call(
        functools.partial(_attn_kernel, tq=tq, dil=dil, sub_len=sub_len),
        grid=(batch, n_ps, nt),
        in_specs=[cur(0), prev(1), cur(1), nxt(1), prev(2), cur(2), nxt(2),
                  pl.BlockSpec((PAIRS_PER_SLAB, 2, 2 * WIN, 4 * WIN), lambda b, ps, t: (ps, 0, 0, 0))],
        out_specs=[out_spec, out_spec],
        out_shape=[jax.ShapeDtypeStruct((n_hp, batch * seq, LANES), F32)] * 2,
        scratch_shapes=[pltpu.VMEM((span + 2 * halo, LANES), jnp.uint32)] * 2,
        compiler_params=_cparams(("parallel", "parallel", "arbitrary")),
        name=f"attn_g{g}",
    )(qkv, qkv, qkv, qkv, qkv, qkv, qkv, bm)


def _shortconv_kernel(p_ref, c_ref, n_ref, w_ref, b_ref, o_ref, *, rows, halo):
    t = pl.program_id(1)
    nt = pl.num_programs(1)
    x = c_ref[...].astype(F32)
    prev_row = jnp.where(t > 0, p_ref[...].astype(F32)[halo - 1:halo, :], 0.0)
    next_row = jnp.where(t < nt - 1, n_ref[...].astype(F32)[0:1, :], 0.0)
    row = lax.broadcasted_iota(jnp.int32, (rows, 1), 0)
    up = jnp.where(row == 0, prev_row, pltpu.roll(x, 1, axis=0))
    dn = jnp.where(row == rows - 1, next_row, pltpu.roll(x, rows - 1, axis=0))
    res = up * w_ref[0:1, :] + x * w_ref[1:2, :] + dn * w_ref[2:3, :] + b_ref[...]
    _store_padded_slabs(o_ref, res, rows // DFT_N2)


def _store_padded_slabs(o_ref, val, nblk):
    pad = jnp.zeros((PITCH_Z - DFT_N2, LANES), F32)
    for s in range(val.shape[1] // LANES):
        for jb in range(nblk):
            o_ref[s, jb * PITCH_Z:jb * PITCH_Z + DFT_N2, :] = val[jb * DFT_N2:(jb + 1) * DFT_N2, s * LANES:(s + 1) * LANES]
            o_ref[s, jb * PITCH_Z + DFT_N2:(jb + 1) * PITCH_Z, :] = pad


def _load_padded_slabs(ref, nblk):
    return jnp.concatenate(
        [jnp.concatenate([ref[s, jb * PITCH_Z:jb * PITCH_Z + DFT_N2, :] for jb in range(nblk)], axis=0)
         for s in range(ref.shape[0])], axis=1)


def _shortconv(proj, conv_w, conv_b, batch, seq, rows=1024):
    rows = min(rows, seq)
    prow = rows // DFT_N2 * PITCH_Z
    n_slab = HYENA_WIDTH // LANES
    halo = 16
    pv = proj.reshape(batch, seq, REST_WIDTH)
    cb = R_U // HYENA_WIDTH
    hb = rows // halo
    nhb = seq // halo
    return pl.pallas_call(
        functools.partial(_shortconv_kernel, rows=rows, halo=halo),
        grid=(batch, seq // rows, 3),
        in_specs=[
            pl.BlockSpec((None, halo, HYENA_WIDTH), lambda b, t, j: (b, jnp.maximum(t * hb - 1, 0), cb + j)),
            pl.BlockSpec((None, rows, HYENA_WIDTH), lambda b, t, j: (b, t, cb + j)),
            pl.BlockSpec((None, halo, HYENA_WIDTH), lambda b, t, j: (b, jnp.minimum((t + 1) * hb, nhb - 1), cb + j)),
            pl.BlockSpec((3, HYENA_WIDTH), lambda b, t, j: (0, j)),
            pl.BlockSpec((1, HYENA_WIDTH), lambda b, t, j: (0, j)),
        ],
        out_specs=pl.BlockSpec((None, n_slab, None, prow, LANES), lambda b, t, j: (j, 0, b, t, 0)),
        out_shape=jax.ShapeDtypeStruct((3, n_slab, batch, seq // DFT_N2 * PITCH_Z, LANES), F32),
        compiler_params=_cparams(("parallel", "parallel", "arbitrary")),
        name="shortconv",
    )(pv, pv, pv, conv_w, conv_b.reshape(1, -1))


def _filter_features(length):
    t = np.linspace(0.0, 1.0, length)[:, None]
    ang = (2.0 * math.pi / length) * np.arange(length, dtype=np.float64)[:, None]
    bands = np.linspace(1e-4, N_BANDS - 1, N_BANDS)[None]
    z = np.concatenate([t, np.cos(ang * bands), -np.sin(ang * bands)], axis=-1)
    zp = np.zeros((length, LANES), np.float32)
    zp[:, :FILTER_EMB] = z
    return zp


def _decay_rates():
    max_decay = math.log(1e-2) / 0.3
    min_decay = math.log(1e-2) / 1.5
    return np.abs(np.linspace(min_decay, max_decay, HYENA_WIDTH)).astype(np.float32)[None]


def _filter_kernel(z_ref, w1_ref, b1_ref, w2_ref, b2_ref, w3_ref, b3_ref, w4_ref, fr_ref, dl_ref, o_ref, *, rows):
    hi = lax.Precision.HIGHEST
    z = z_ref[...]
    fr = fr_ref[...]
    h = jnp.sin(fr * (jnp.dot(z, w1_ref[...], precision=hi, preferred_element_type=F32) + b1_ref[...]))
    h = jnp.sin(fr * (jnp.dot(h, w2_ref[...], precision=hi, preferred_element_type=F32) + b2_ref[...]))
    h = jnp.sin(fr * (jnp.dot(h, w3_ref[...], precision=hi, preferred_element_type=F32) + b3_ref[...]))
    hh, hl = _split(h)
    filt = jnp.dot(jnp.concatenate([hh, hl, hh], axis=1), w4_ref[...], preferred_element_type=F32)
    decay = jnp.exp(-z[:, 0:1] * dl_ref[...])
    row = pl.program_id(0) * rows + lax.broadcasted_iota(jnp.int32, (rows, 1), 0)
    for j in range(4):
        cs = slice(j * HYENA_WIDTH, (j + 1) * HYENA_WIDTH)
        val = filt[:, cs] * decay
        if j % 2 == 1:
            val = jnp.where(row == 0, 0.0, val)
        _store_padded_slabs(o_ref.at[j], val, rows // DFT_N2)


def _filters(length, fw1, fb1, fw2, fb2, fw3, fb3, fw4, ffreq, rows=256):
    rows = min(rows, length)
    n_slab = HYENA_WIDTH // LANES
    prow = rows // DFT_N2 * PITCH_Z
    w4h, w4l = _split(fw4.astype(F32))
    w4s = jnp.concatenate([w4h, w4h, w4l], axis=0)
    zfeat = jnp.asarray(_filter_features(length))
    w1p = jnp.zeros((LANES, FILTER_HIDDEN), F32).at[:FILTER_EMB].set(fw1.astype(F32))
    full = lambda shape: pl.BlockSpec(shape, lambda i: (0,) * len(shape))
    return pl.pallas_call(
        functools.partial(_filter_kernel, rows=rows),
        grid=(length // rows,),
        in_specs=[pl.BlockSpec((rows, LANES), lambda i: (i, 0)),
                  full((LANES, FILTER_HIDDEN)), full((1, FILTER_HIDDEN)),
                  full((FILTER_HIDDEN, FILTER_HIDDEN)), full((1, FILTER_HIDDEN)),
                  full((FILTER_HIDDEN, FILTER_HIDDEN)), full((1, FILTER_HIDDEN)),
                  full((3 * FILTER_HIDDEN, 4 * HYENA_WIDTH)), full((1, FILTER_HIDDEN)),
                  full((1, HYENA_WIDTH))],
        out_specs=pl.BlockSpec((4, n_slab, prow, LANES), lambda i: (0, 0, i, 0)),
        out_shape=jax.ShapeDtypeStruct((4, n_slab, length // DFT_N2 * PITCH_Z, LANES), F32),
        compiler_params=_cparams(("parallel",)),
        name="hyena_filters",
    )(zfeat, w1p, fb1.reshape(1, -1), fw2, fb2.reshape(1, -1), fw3, fb3.reshape(1, -1), w4s,
      ffreq.reshape(1, -1), jnp.asarray(_decay_rates()))


def _dft_tables(length):
    n = 2 * length
    n1 = n // DFT_N2
    h1 = n1 // 2
    kk = np.arange(h1)[:, None] + 0.5
    th = 2.0 * math.pi * kk * np.arange(h1)[None, :] / n1
    fa = np.concatenate([np.cos(th), -np.sin(th)], axis=0)
    ga = (2.0 / n) * fa.T
    k = np.arange(h1)[:, None, None] + n1 * np.arange(DFT_N2)[None, :, None] + 0.5
    ph = 2.0 * math.pi * k * np.arange(DFT_N2)[None, None, :] / n
    f = lambda a: jnp.asarray(a.astype(np.float32))
    er, ei = f(np.cos(ph)), f(-np.sin(ph))
    m = jnp.concatenate([jnp.concatenate([er, -ei], axis=2), jnp.concatenate([ei, er], axis=2)], axis=1)
    return f(fa), f(ga), m.astype(BF16)


def _split(a):
    hi = a.astype(BF16)
    lo = (a - hi.astype(F32)).astype(BF16)
    return hi, lo


CONV_K1_CHUNK = 32
CONV_PAIR_UNROLL = 8
CONV_FREQ_UNROLL = 8
CONV_VMEM_MARGIN = 4 * 1024 * 1024


def _spectrum_kernel(fa_ref, m_ref, xf_ref, xb_ref, o_ref, s_ref, *, h1, kc):
    hf = pl.program_id(2)
    half = DFT_N2

    @pl.when(hf == 0)
    def _():
        fa = fa_ref[...]

        def body(n2, carry):
            x = jnp.concatenate([xf_ref[pl.ds(n2, h1, stride=PITCH_Z), :],
                                 xb_ref[pl.ds(n2, h1, stride=PITCH_Z), :]], axis=1)
            xh, xl = _split(x)
            a = jnp.dot(fa, jnp.concatenate([xh, xh, xl], axis=0), preferred_element_type=F32)
            for ri in range(2):
                for d in range(2):
                    s_ref.at[d][pl.ds(ri * half + n2, h1, stride=PITCH_S), :] = (
                        a[ri * h1:(ri + 1) * h1, d * LANES:(d + 1) * LANES])
            return carry

        lax.fori_loop(0, DFT_N2, body, 0, unroll=CONV_PAIR_UNROLL)

    def freq_body(k1, carry):
        row = pl.multiple_of((hf * kc + k1) * PITCH_S, 8)
        x = jnp.concatenate([s_ref.at[0][pl.ds(row, 2 * half), :],
                             s_ref.at[1][pl.ds(row, 2 * half), :]], axis=1).astype(BF16)
        z = jnp.dot(m_ref[k1], x, preferred_element_type=F32)
        o_ref[k1, 0:half, :] = z[:half, :LANES] + z[:half, LANES:]
        o_ref[k1, half:, :] = z[half:, :LANES] - z[half:, LANES:]
        return carry

    lax.fori_loop(0, kc, freq_body, 0, unroll=CONV_FREQ_UNROLL)


def _filter_spectrum(fa, m_bf16, filt):
    _, n_slab, prow, _ = filt.shape
    n1, h1 = fa.shape
    kc = min(CONV_K1_CHUNK, h1)
    fh, fl = _split(fa)
    fa3 = jnp.concatenate([fh, fl, fh], axis=1)
    return pl.pallas_call(
        functools.partial(_spectrum_kernel, h1=h1, kc=kc),
        grid=(2, n_slab, h1 // kc),
        in_specs=[pl.BlockSpec((n1, 3 * h1), lambda o, s, hf: (0, 0)),
                  pl.BlockSpec((kc, 2 * DFT_N2, 2 * DFT_N2), lambda o, s, hf: (hf, 0, 0)),
                  pl.BlockSpec((None, None, prow, LANES), lambda o, s, hf: (2 * o, s, 0, 0)),
                  pl.BlockSpec((None, None, prow, LANES), lambda o, s, hf: (2 * o + 1, s, 0, 0))],
        out_specs=pl.BlockSpec((None, None, kc, 2 * DFT_N2, LANES), lambda o, s, hf: (o, s, hf, 0, 0)),
        out_shape=jax.ShapeDtypeStruct((2, n_slab, h1, 2 * DFT_N2, LANES), F32),
        scratch_shapes=[pltpu.VMEM((2, h1 * PITCH_S, LANES), F32)],
        compiler_params=pltpu.CompilerParams(dimension_semantics=("parallel", "parallel", "arbitrary"),
                                             vmem_limit_bytes=CONV_VMEM_LIMIT),
        name="filter_spectrum",
    )(fa3, m_bf16, filt, filt)


def _conv_kernel(fa_ref, ga_ref, m_ref, k_ref, z_ref, hx_ref, sk_ref, o_ref, s_ref, *, h1, kc, nb):
    hf = pl.program_id(2)
    half = DFT_N2
    pairs = DFT_N2 // 2

    @pl.when(hf == 0)
    def _():
        fa = fa_ref[...]

        def body(j, carry):
            n2 = 2 * j
            for bi in range(nb):
                zb, sb = z_ref.at[bi], s_ref.at[bi]
                x = jnp.concatenate([zb[pl.ds(n2, h1, stride=PITCH_Z), :],
                                     zb[pl.ds(n2 + 1, h1, stride=PITCH_Z), :]], axis=1).astype(BF16)
                a = jnp.dot(fa, x, preferred_element_type=F32)
                for ri in range(2):
                    for u in range(2):
                        sb[pl.ds(ri * half + n2 + u, h1, stride=PITCH_S), :] = (
                            a[ri * h1:(ri + 1) * h1, u * LANES:(u + 1) * LANES])
            return carry

        lax.fori_loop(0, pairs, body, 0, unroll=CONV_PAIR_UNROLL // nb)

    def freq_body(k1, carry):
        row = pl.multiple_of((hf * kc + k1) * PITCH_S, 8)
        x = jnp.concatenate([s_ref[bi, pl.ds(row, 2 * half), :] for bi in range(nb)], axis=1).astype(BF16)
        m = m_ref[k1]
        z = jnp.dot(m, x, preferred_element_type=F32)
        zr, zi = z[:half], z[half:]
        kr = jnp.concatenate([k_ref[k1, 0:half, :]] * nb, axis=1)
        ki = jnp.concatenate([k_ref[k1, half:, :]] * nb, axis=1)
        y = jnp.concatenate([zr * kr - zi * ki, zr * ki + zi * kr], axis=0).astype(BF16)
        bq = lax.dot_general(m, y, (((0,), (0,)), ((), ())), preferred_element_type=F32)
        for bi in range(nb):
            s_ref[bi, pl.ds(row, 2 * half), :] = bq[:, bi * LANES:(bi + 1) * LANES]
        return carry

    lax.fori_loop(0, kc, freq_body, 0, unroll=CONV_FREQ_UNROLL // nb)

    @pl.when(hf == pl.num_programs(2) - 1)
    def _():
        ga = ga_ref[...]
        sk = sk_ref[...]

        def body(j, carry):
            n2 = 2 * j
            for bi in range(nb):
                zb, hb, ob, sb = z_ref.at[bi], hx_ref.at[bi], o_ref.at[bi], s_ref.at[bi]
                cols = []
                for u in range(2):
                    re = sb[pl.ds(n2 + u, h1, stride=PITCH_S), :]
                    im = sb[pl.ds(half + n2 + u, h1, stride=PITCH_S), :]
                    cols.append(jnp.concatenate([re, im], axis=0))
                bn = jnp.concatenate(cols, axis=1).astype(BF16)
                y = jnp.dot(ga, bn, preferred_element_type=F32)
                for u in range(2):
                    zs = zb[pl.ds(n2 + u, h1, stride=PITCH_Z), :]
                    hs = hb[pl.ds(n2 + u, h1, stride=PITCH_Z), :]
                    ob[pl.ds(n2 + u, h1, stride=PITCH_Z), :] = hs * (y[:, u * LANES:(u + 1) * LANES] + sk * zs)
            return carry

        lax.fori_loop(0, pairs, body, 0, unroll=CONV_PAIR_UNROLL // nb)
        pad = jnp.zeros((PITCH_Z - DFT_N2, LANES), F32)
        for bi in range(nb):
            for blk in range(h1):
                o_ref[bi, blk * PITCH_Z + DFT_N2:(blk + 1) * PITCH_Z, :] = pad


def _conv_vmem_bytes(nb, prow, h1, kc):
    slabs = 3 * 2 * nb * prow * LANES * 4
    scratch = nb * h1 * PITCH_S * LANES * 4
    tables = 2 * kc * 2 * DFT_N2 * (2 * DFT_N2 * 2 + LANES * 4)
    return slabs + scratch + tables


def _long_conv_gate(tabs, kspec, order, z5, zsel, hx5, hxsel, skip):
    fa, ga, m_tab = tabs
    _, n_slab, batch, prow, _ = z5.shape
    n1, h1 = fa.shape
    kc = min(CONV_K1_CHUNK, h1)
    nb = 2 if (batch % 2 == 0 and _conv_vmem_bytes(2, prow, h1, kc) <= CONV_VMEM_LIMIT - CONV_VMEM_MARGIN) else 1
    slab = lambda sel: pl.BlockSpec((None, None, nb, prow, LANES), lambda s, b, hf: (sel, s, b, 0, 0))
    return pl.pallas_call(
        functools.partial(_conv_kernel, h1=h1, kc=kc, nb=nb),
        grid=(n_slab, batch // nb, h1 // kc),
        in_specs=[pl.BlockSpec((n1, h1), lambda s, b, hf: (0, 0)),
                  pl.BlockSpec((h1, n1), lambda s, b, hf: (0, 0)),
                  pl.BlockSpec((kc, 2 * DFT_N2, 2 * DFT_N2), lambda s, b, hf: (hf, 0, 0)),
                  pl.BlockSpec((None, None, kc, 2 * DFT_N2, LANES), lambda s, b, hf: (order, s, hf, 0, 0)),
                  slab(zsel), slab(hxsel),
                  pl.BlockSpec((None, None, 1, LANES), lambda s, b, hf: (order, s, 0, 0))],
        out_specs=pl.BlockSpec((None, nb, prow, LANES), lambda s, b, hf: (s, b, 0, 0)),
        out_shape=jax.ShapeDtypeStruct((n_slab, batch, prow, LANES), F32),
        scratch_shapes=[pltpu.VMEM((nb, h1 * PITCH_S, LANES), F32)],
        compiler_params=pltpu.CompilerParams(dimension_semantics=("parallel", "parallel", "arbitrary"),
                                             vmem_limit_bytes=CONV_VMEM_LIMIT),
        name="hyena_long_conv",
    )(fa.astype(BF16), ga.astype(BF16), m_tab, kspec, z5, hx5, skip)


def _hyena(proj, batch, seq, conv_w, conv_b, fparams, skip):
    c = HYENA_WIDTH
    fa, ga, m_tab = _dft_tables(seq)
    kspec = _filter_spectrum(fa, m_tab, _filters(seq, *fparams))
    u5 = _shortconv(proj, conv_w, conv_b, batch, seq)
    skip4 = skip.astype(F32).reshape(2, c // LANES, 1, LANES)
    tabs = (fa, ga, m_tab)
    z1 = _long_conv_gate(tabs, kspec, 0, u5, 0, u5, 1, skip4)
    return _long_conv_gate(tabs, kspec, 1, z1[None], 0, u5, 2, skip4)


def _final_kernel(o1_ref, o2_ref, o3_ref, l1_ref, l2_ref, l3_ref, ga_ref, hy_ref, gh_ref, ma_ref, mh_ref,
                  x_ref, wa_ref, wh_ref, wo_ref, pg_ref, y_ref):
    def cat(ref):
        return jnp.concatenate([ref[s] for s in range(ATTN_WIDTH // LANES)], axis=1)

    l1, l2, l3 = cat(l1_ref), cat(l2_ref), cat(l3_ref)
    mx = jnp.maximum(jnp.maximum(l1, l2), l3)
    e1, e2, e3 = jnp.exp(l1 - mx), jnp.exp(l2 - mx), jnp.exp(l3 - mx)
    attn = (e1 * cat(o1_ref) + e2 * cat(o2_ref) + e3 * cat(o3_ref)) / (e1 + e2 + e3)
    ga = ga_ref[...].astype(F32)
    a_in = (attn * (ga * jax.nn.sigmoid(ga))).astype(BF16)
    a_br = jnp.dot(a_in, wa_ref[...], preferred_element_type=F32)
    gh = gh_ref[...].astype(F32)
    hy = _load_padded_slabs(hy_ref, gh.shape[0] // DFT_N2)
    h_in = (hy * (gh * jax.nn.sigmoid(gh))).astype(BF16)
    h_br = jnp.dot(h_in, wh_ref[...], preferred_element_type=F32)
    merged = jax.nn.sigmoid(ma_ref[...].astype(F32)) * a_br + jax.nn.sigmoid(mh_ref[...].astype(F32)) * h_br
    out = jnp.dot(merged.astype(BF16), wo_ref[...], preferred_element_type=F32)
    ms = jnp.mean(out * out, axis=-1, keepdims=True)
    y_ref[...] = x_ref[...] + out * lax.rsqrt(ms + EPS) * pg_ref[...]


def _final(os_, ls_, proj, hy, x2d, wa, wh, wo, pg, seq, tm=256):
    rows = x2d.shape[0]
    tm = min(tm, rows)
    tiles_per_seq = seq // tm
    hy_spec = pl.BlockSpec((HYENA_WIDTH // LANES, None, tm // DFT_N2 * PITCH_Z, LANES),
                           lambda i: (0, i // tiles_per_seq, i % tiles_per_seq, 0))
    slab = pl.BlockSpec((ATTN_WIDTH // LANES, tm, LANES), lambda i: (0, i, 0))
    a512 = lambda off: pl.BlockSpec((tm, ATTN_WIDTH), lambda i: (i, off))
    a1024 = lambda off: pl.BlockSpec((tm, D_MODEL), lambda i: (i, off))
    full = lambda shape: pl.BlockSpec(shape, lambda i: (0, 0))
    return pl.pallas_call(
        _final_kernel,
        grid=(rows // tm,),
        in_specs=[slab] * 6 + [a512(R_GA // ATTN_WIDTH), hy_spec, a1024(R_GH // D_MODEL),
                               a1024(R_MG // D_MODEL), a1024(R_MG // D_MODEL + 1), a1024(0),
                               full((ATTN_WIDTH, D_MODEL)), full((D_MODEL, D_MODEL)), full((D_MODEL, D_MODEL)),
                               full((1, D_MODEL))],
        out_specs=pl.BlockSpec((tm, D_MODEL), lambda i: (i, 0)),
        out_shape=jax.ShapeDtypeStruct((rows, D_MODEL), F32),
        compiler_params=_cparams(("parallel",)),
        name="merge_out",
    )(*os_, *ls_, proj, hy, proj, proj, proj, x2d, wa, wh, wo, pg)


def _layer(x, rel_bias, pre_g, post_g, w_in, conv_w, conv_b, fparams, skip, w_br_a, w_br_h, w_out):
    batch, seq, _ = x.shape
    x2d = x.reshape(batch * seq, D_MODEL)
    w_qkv, w_rest = w_in
    gain = pre_g.reshape(1, -1).astype(F32)
    qkv = _inproj(x2d, gain, w_qkv, tn=QKV_WIDTH // 3, slab_out=True)
    proj = _inproj(x2d, gain, w_rest, tn=REST_WIDTH // 2, slab_out=False)
    os_, ls_ = [], []
    for g in range(N_GROUPS):
        o, l = _attention_group(qkv, rel_bias, g, batch, seq)
        os_.append(o)
        ls_.append(l)
    hy = _hyena(proj, batch, seq, conv_w, conv_b, fparams, skip)
    y = _final(os_, ls_, proj, hy, x2d, w_br_a, w_br_h, w_out, post_g.reshape(1, -1).astype(F32), seq)
    return y.reshape(batch, seq, D_MODEL)


def kernel(x_prompt, x_sample, rel_bias, pre_norm_g, post_norm_g, w_in, conv_w, conv_b, filt_w1, filt_b1, filt_w2, filt_b2, filt_w3, filt_b3, filt_w4, filt_freq, hyena_skip, w_branch_a, w_branch_h, w_out):
    depth = w_in.shape[0]

    def run(x):
        for l in range(depth):
            fparams = (filt_w1[l], filt_b1[l], filt_w2[l], filt_b2[l], filt_w3[l], filt_b3[l], filt_w4[l], filt_freq[l])
            w = w_in[l]
            col_scale = jnp.where(jnp.arange(QKV_WIDTH) < QKV_WIDTH // 3, HEAD_DIM ** -0.5, 1.0).astype(F32)
            w_qkv = (w[:, :QKV_WIDTH] * col_scale).astype(BF16)
            w_rest = jnp.concatenate([w[:, U_OFF:], w[:, GA_OFF:U_OFF]], axis=1).astype(BF16)
            x = _layer(x, rel_bias, pre_norm_g[l], post_norm_g[l], (w_qkv, w_rest), conv_w[l], conv_b[l],
                       fparams, hyena_skip[l], w_branch_a[l].astype(BF16), w_branch_h[l].astype(BF16),
                       w_out[l].astype(BF16))
        return x

    return (run(x_prompt), run(x_sample))
```

```python
import functools
import math

import numpy as np
import jax
import jax.numpy as jnp
from jax import lax
from jax.experimental import pallas as pl
from jax.experimental.pallas import tpu as pltpu

F32 = jnp.float32
BF16 = jnp.bfloat16

D_MODEL = 1024
EPS = 1e-6
HEAD_DIM = 64
ATTN_GROUPS = ((128, 1), (512, 4), (2048, 16))
N_GROUPS = 3
HEADS_PER_GROUP = 8
ATTN_WIDTH = HEADS_PER_GROUP * HEAD_DIM
HYENA_WIDTH = 1024
FILTER_EMB = 33
N_BANDS = 16
FILTER_HIDDEN = 64
NUM_BUCKETS = 32
MAX_DISTANCE = 1024
NEG_INF = -1e30
QKV_WIDTH = 4608
GA_OFF, U_OFF, GH_OFF, MG_OFF = 4608, 5120, 8192, 9216
REST_WIDTH = 6656
R_U, R_GH, R_MG, R_GA = 0, 3072, 4096, 6144

WIN = 64
LANES = 128
DFT_N2 = 128
PITCH_Z = 136
PITCH_S = 264
VMEM_LIMIT = 48 * 1024 * 1024
CONV_VMEM_LIMIT = 58 * 1024 * 1024


def _cparams(sem):
    return pltpu.CompilerParams(dimension_semantics=sem, vmem_limit_bytes=VMEM_LIMIT)


def _inproj_kernel(x_ref, g_ref, w_ref, o_ref, hn_ref, *, slabs):
    @pl.when(pl.program_id(1) == 0)
    def _():
        x = x_ref[...]
        ms = jnp.mean(x * x, axis=-1, keepdims=True)
        hn_ref[...] = (x * lax.rsqrt(ms + EPS) * g_ref[...]).astype(BF16)

    acc = jnp.dot(hn_ref[...], w_ref[...], preferred_element_type=F32)
    if slabs:
        for s in range(slabs):
            o_ref[s] = _pack_bf16_pair(acc[:, (2 * s) * LANES:(2 * s + 1) * LANES],
                                       acc[:, (2 * s + 1) * LANES:(2 * s + 2) * LANES])
    else:
        o_ref[...] = acc.astype(o_ref.dtype)


def _pack_bf16_pair(a, b):
    ua = lax.bitcast_convert_type(a.astype(BF16).astype(F32), jnp.uint32)
    ub = lax.bitcast_convert_type(b.astype(BF16).astype(F32), jnp.uint32)
    return (ua >> 16) | (ub & jnp.uint32(0xFFFF0000))


def _unpack_bf16_pair(w, idx, dtype=BF16):
    bits = (w << 16) if idx == 0 else (w & jnp.uint32(0xFFFF0000))
    return lax.bitcast_convert_type(bits, F32).astype(dtype)


def _inproj(x2d, g, w_bf16, tn, slab_out, tm=1024):
    rows = x2d.shape[0]
    width = w_bf16.shape[1]
    tm = min(tm, rows)
    if slab_out:
        ns = tn // (2 * LANES)
        out_spec = pl.BlockSpec((ns, tm, LANES), lambda i, j: (j, i, 0))
        out_shape = jax.ShapeDtypeStruct((width // (2 * LANES), rows, LANES), jnp.uint32)
    else:
        ns = 0
        out_spec = pl.BlockSpec((tm, tn), lambda i, j: (i, j))
        out_shape = jax.ShapeDtypeStruct((rows, width), BF16)
    return pl.pallas_call(
        functools.partial(_inproj_kernel, slabs=ns),
        grid=(rows // tm, width // tn),
        in_specs=[
            pl.BlockSpec((tm, D_MODEL), lambda i, j: (i, 0)),
            pl.BlockSpec((1, D_MODEL), lambda i, j: (0, 0)),
            pl.BlockSpec((D_MODEL, tn), lambda i, j: (0, j)),
        ],
        out_specs=out_spec,
        out_shape=out_shape,
        scratch_shapes=[pltpu.VMEM((tm, D_MODEL), BF16)],
        compiler_params=_cparams(("parallel", "arbitrary")),
        name="inproj_qkv" if slab_out else "inproj_rest",
    )(x2d, g, w_bf16)


def _t5_bucket_np(rel):
    half = NUM_BUCKETS // 2
    max_exact = half // 2
    n = np.abs(rel)
    nf = np.maximum(n, 1).astype(np.float64)
    large = max_exact + (np.log(nf / max_exact) / math.log(MAX_DISTANCE / max_exact) * (half - max_exact)).astype(np.int64)
    large = np.minimum(large, half - 1)
    return np.where(rel > 0, half, 0) + np.where(n < max_exact, n, large)


def _bias_mask(rel_bias_g, dilation):
    qi = np.arange(2 * WIN)[:, None]
    kj = np.arange(4 * WIN)[None, :] - WIN
    delta = kj - qi
    bucket = _t5_bucket_np(delta * dilation).astype(np.int32)
    onehot = jax.nn.one_hot(jnp.asarray(bucket), NUM_BUCKETS, dtype=F32)
    bias = jnp.einsum('qkb,bh->hqk', onehot, rel_bias_g.astype(F32), precision=lax.Precision.HIGHEST)
    bias = jnp.where(jnp.asarray(np.abs(delta) <= WIN)[None], bias, NEG_INF)
    return bias.reshape(HEADS_PER_GROUP // 2, 2, 2 * WIN, 4 * WIN)


ATTN_POS_PER_STEP = 4096
ATTN_TILES_PER_ITER = 2
PAIRS_PER_SLAB = 2


def _attn_kernel(q_ref, kp_ref, kc_ref, kn_ref, vp_ref, vc_ref, vn_ref, bm_ref, o_ref, l_ref,
                 kbuf, vbuf, *, tq, dil, sub_len):
    t = pl.program_id(2)
    halo = WIN * dil
    span = tq * dil
    kbuf[0:halo] = kp_ref[...]
    kbuf[halo:halo + span] = kc_ref[...]
    kbuf[halo + span:] = kn_ref[...]
    vbuf[0:halo] = vp_ref[...]
    vbuf[halo:halo + span] = vc_ref[...]
    vbuf[halo + span:] = vn_ref[...]
    lane = lax.broadcasted_iota(jnp.int32, (1, LANES), 1)
    is_lo = lane < HEAD_DIM
    kcol = lax.broadcasted_iota(jnp.int32, (1, 4 * WIN), 1)
    qt = 2 * WIN
    nq = tq // qt

    def tile(c, i):
        row0 = i * (qt * dil) + c
        kpos = t * tq + i * qt - WIN + kcol
        pen = jnp.where((kpos >= 0) & (kpos < sub_len), 0.0, NEG_INF).astype(F32)
        qw = q_ref[pl.ds(row0, qt, stride=dil), :]
        kw = kbuf[pl.ds(row0, 2 * qt, stride=dil), :]
        vw = vbuf[pl.ds(row0, 2 * qt, stride=dil), :]
        for hp in range(PAIRS_PER_SLAB):
            q = _unpack_bf16_pair(qw, hp)
            k = _unpack_bf16_pair(kw, hp)
            v = _unpack_bf16_pair(vw, hp)
            outs, lses = [], []
            for hh in range(2):
                sel = is_lo if hh == 0 else jnp.logical_not(is_lo)
                qm = jnp.where(sel, q, jnp.zeros_like(q))
                s = lax.dot_general(qm, k, (((1,), (1,)), ((), ())), preferred_element_type=F32)
                s = s + bm_ref[hp, hh] + pen
                m = jnp.max(s, axis=-1, keepdims=True)
                p = jnp.exp(s - m)
                den = jnp.sum(p, axis=-1, keepdims=True)
                pv = jnp.dot(p.astype(BF16), v, preferred_element_type=F32)
                outs.append(pv / den)
                lses.append(m + jnp.log(den))
            o_ref[hp, pl.ds(row0, qt, stride=dil), :] = jnp.where(is_lo, outs[0], outs[1])
            l_ref[hp, pl.ds(row0, qt, stride=dil), :] = jnp.where(is_lo, lses[0], lses[1])

    def body(it, carry):
        for u in range(ATTN_TILES_PER_ITER):
            idx = it * ATTN_TILES_PER_ITER + u
            tile(idx // nq, idx % nq)
        return carry

    lax.fori_loop(0, dil * nq // ATTN_TILES_PER_ITER, body, 0)


def _attention_group(qkv, rel_bias, g, batch, seq):
    _, dil = ATTN_GROUPS[g]
    sub_len = seq // dil
    tq = min(ATTN_POS_PER_STEP // dil, sub_len)
    nt = sub_len // tq
    span = tq * dil
    halo = WIN * dil
    n_hp = ATTN_WIDTH // LANES
    n_ps = n_hp // PAIRS_PER_SLAB
    n_slab = N_GROUPS * n_ps
    bm = _bias_mask(rel_bias[:, g * HEADS_PER_GROUP:(g + 1) * HEADS_PER_GROUP], dil)
    hb = tq // WIN
    nhb = sub_len // WIN

    def cur(which):
        return pl.BlockSpec((None, span, LANES), lambda b, ps, t: (which * n_slab + g * n_ps + ps, b * nt + t, 0))

    def prev(which):
        return pl.BlockSpec((None, halo, LANES),
                            lambda b, ps, t: (which * n_slab + g * n_ps + ps, b * nhb + jnp.maximum(t * hb - 1, 0), 0))

    def nxt(which):
        return pl.BlockSpec((None, halo, LANES),
                            lambda b, ps, t: (which * n_slab + g * n_ps + ps, b * nhb + jnp.minimum((t + 1) * hb, nhb - 1), 0))

    out_spec = pl.BlockSpec((PAIRS_PER_SLAB, span, LANES), lambda b, ps, t: (ps, b * nt + t, 0))
    return pl.pallas_call(
        functools.partial(_attn_kernel, tq=tq, dil=dil, sub_len=sub_len),
        grid=(batch, n_ps, nt),
        in_specs=[cur(0), prev(1), cur(1), nxt(1), prev(2), cur(2), nxt(2),
                  pl.BlockSpec((PAIRS_PER_SLAB, 2, 2 * WIN, 4 * WIN), lambda b, ps, t: (ps, 0, 0, 0))],
        out_specs=[out_spec, out_spec],
        out_shape=[jax.ShapeDtypeStruct((n_hp, batch * seq, LANES), F32)] * 2,
        scratch_shapes=[pltpu.VMEM((span + 2 * halo, LANES), jnp.uint32)] * 2,
        compiler_params=_cparams(("parallel", "parallel", "arbitrary")),
        name=f"attn_g{g}",
    )(qkv, qkv, qkv, qkv, qkv, qkv, qkv, bm)


def _shortconv_kernel(p_ref, c_ref, n_ref, w_ref, b_ref, o_ref, *, rows, halo):
    t = pl.program_id(1)
    nt = pl.num_programs(1)
    x = c_ref[...].astype(F32)
    prev_row = jnp.where(t > 0, p_ref[...].astype(F32)[halo - 1:halo, :], 0.0)
    next_row = jnp.where(t < nt - 1, n_ref[...].astype(F32)[0:1, :], 0.0)
    row = lax.broadcasted_iota(jnp.int32, (rows, 1), 0)
    up = jnp.where(row == 0, prev_row, pltpu.roll(x, 1, axis=0))
    dn = jnp.where(row == rows - 1, next_row, pltpu.roll(x, rows - 1, axis=0))
    res = up * w_ref[0:1, :] + x * w_ref[1:2, :] + dn * w_ref[2:3, :] + b_ref[...]
    _store_padded_slabs(o_ref, res, rows // DFT_N2)


def _store_padded_slabs(o_ref, val, nblk):
    pad = jnp.zeros((PITCH_Z - DFT_N2, LANES), F32)
    for s in range(val.shape[1] // LANES):
        for jb in range(nblk):
            o_ref[s, jb * PITCH_Z:jb * PITCH_Z + DFT_N2, :] = val[jb * DFT_N2:(jb + 1) * DFT_N2, s * LANES:(s + 1) * LANES]
            o_ref[s, jb * PITCH_Z + DFT_N2:(jb + 1) * PITCH_Z, :] = pad


def _load_padded_slabs(ref, nblk):
    return jnp.concatenate(
        [jnp.concatenate([ref[s, jb * PITCH_Z:jb * PITCH_Z + DFT_N2, :] for jb in range(nblk)], axis=0)
         for s in range(ref.shape[0])], axis=1)


def _shortconv(proj, conv_w, conv_b, batch, seq, rows=1024):
    rows = min(rows, seq)
    prow = rows // DFT_N2 * PITCH_Z
    n_slab = HYENA_WIDTH // LANES
    halo = 16
    pv = proj.reshape(batch, seq, REST_WIDTH)
    cb = R_U // HYENA_WIDTH
    hb = rows // halo
    nhb = seq // halo
    return pl.pallas_call(
        functools.partial(_shortconv_kernel, rows=rows, halo=halo),
        grid=(batch, seq // rows, 3),
        in_specs=[
            pl.BlockSpec((None, halo, HYENA_WIDTH), lambda b, t, j: (b, jnp.maximum(t * hb - 1, 0), cb + j)),
            pl.BlockSpec((None, rows, HYENA_WIDTH), lambda b, t, j: (b, t, cb + j)),
            pl.BlockSpec((None, halo, HYENA_WIDTH), lambda b, t, j: (b, jnp.minimum((t + 1) * hb, nhb - 1), cb + j)),
            pl.BlockSpec((3, HYENA_WIDTH), lambda b, t, j: (0, j)),
            pl.BlockSpec((1, HYENA_WIDTH), lambda b, t, j: (0, j)),
        ],
        out_specs=pl.BlockSpec((None, n_slab, None, prow, LANES), lambda b, t, j: (j, 0, b, t, 0)),
        out_shape=jax.ShapeDtypeStruct((3, n_slab, batch, seq // DFT_N2 * PITCH_Z, LANES), F32),
        compiler_params=_cparams(("parallel", "parallel", "arbitrary")),
        name="shortconv",
    )(pv, pv, pv, conv_w, conv_b.reshape(1, -1))


def _filter_features(length):
    t = np.linspace(0.0, 1.0, length)[:, None]
    ang = (2.0 * math.pi / length) * np.arange(length, dtype=np.float64)[:, None]
    bands = np.linspace(1e-4, N_BANDS - 1, N_BANDS)[None]
    z = np.concatenate([t, np.cos(ang * bands), -np.sin(ang * bands)], axis=-1)
    zp = np.zeros((length, LANES), np.float32)
    zp[:, :FILTER_EMB] = z
    return zp


def _decay_rates():
    max_decay = math.log(1e-2) / 0.3
    min_decay = math.log(1e-2) / 1.5
    return np.abs(np.linspace(min_decay, max_decay, HYENA_WIDTH)).astype(np.float32)[None]


def _filter_kernel(z_ref, w1_ref, b1_ref, w2_ref, b2_ref, w3_ref, b3_ref, w4_ref, fr_ref, dl_ref, o_ref, *, rows):
    hi = lax.Precision.HIGHEST
    z = z_ref[...]
    fr = fr_ref[...]
    h = jnp.sin(fr * (jnp.dot(z, w1_ref[...], precision=hi, preferred_element_type=F32) + b1_ref[...]))
    h = jnp.sin(fr * (jnp.dot(h, w2_ref[...], precision=hi, preferred_element_type=F32) + b2_ref[...]))
    h = jnp.sin(fr * (jnp.dot(h, w3_ref[...], precision=hi, preferred_element_type=F32) + b3_ref[...]))
    hh, hl = _split(h)
    filt = jnp.dot(jnp.concatenate([hh, hl, hh], axis=1), w4_ref[...], preferred_element_type=F32)
    decay = jnp.exp(-z[:, 0:1] * dl_ref[...])
    row = pl.program_id(0) * rows + lax.broadcasted_iota(jnp.int32, (rows, 1), 0)
    for j in range(4):
        cs = slice(j * HYENA_WIDTH, (j + 1) * HYENA_WIDTH)
        val = filt[:, cs] * decay
        if j % 2 == 1:
            val = jnp.where(row == 0, 0.0, val)
        _store_padded_slabs(o_ref.at[j], val, rows // DFT_N2)


def _filters(length, fw1, fb1, fw2, fb2, fw3, fb3, fw4, ffreq, rows=256):
    rows = min(rows, length)
    n_slab = HYENA_WIDTH // LANES
    prow = rows // DFT_N2 * PITCH_Z
    w4h, w4l = _split(fw4.astype(F32))
    w4s = jnp.concatenate([w4h, w4h, w4l], axis=0)
    zfeat = jnp.asarray(_filter_features(length))
    w1p = jnp.zeros((LANES, FILTER_HIDDEN), F32).at[:FILTER_EMB].set(fw1.astype(F32))
    full = lambda shape: pl.BlockSpec(shape, lambda i: (0,) * len(shape))
    return pl.pallas_call(
        functools.partial(_filter_kernel, rows=rows),
        grid=(length // rows,),
        in_specs=[pl.BlockSpec((rows, LANES), lambda i: (i, 0)),
                  full((LANES, FILTER_HIDDEN)), full((1, FILTER_HIDDEN)),
                  full((FILTER_HIDDEN, FILTER_HIDDEN)), full((1, FILTER_HIDDEN)),
                  full((FILTER_HIDDEN, FILTER_HIDDEN)), full((1, FILTER_HIDDEN)),
                  full((3 * FILTER_HIDDEN, 4 * HYENA_WIDTH)), full((1, FILTER_HIDDEN)),
                  full((1, HYENA_WIDTH))],
        out_specs=pl.BlockSpec((4, n_slab, prow, LANES), lambda i: (0, 0, i, 0)),
        out_shape=jax.ShapeDtypeStruct((4, n_slab, length // DFT_N2 * PITCH_Z, LANES), F32),
        compiler_params=_cparams(("parallel",)),
        name="hyena_filters",
    )(zfeat, w1p, fb1.reshape(1, -1), fw2, fb2.reshape(1, -1), fw3, fb3.reshape(1, -1), w4s,
      ffreq.reshape(1, -1), jnp.asarray(_decay_rates()))


def _dft_tables(length):
    n = 2 * length
    n1 = n // DFT_N2
    h1 = n1 // 2
    kk = np.arange(h1)[:, None] + 0.5
    th = 2.0 * math.pi * kk * np.arange(h1)[None, :] / n1
    fa = np.concatenate([np.cos(th), -np.sin(th)], axis=0)
    ga = (2.0 / n) * fa.T
    k = np.arange(h1)[:, None, None] + n1 * np.arange(DFT_N2)[None, :, None] + 0.5
    ph = 2.0 * math.pi * k * np.arange(DFT_N2)[None, None, :] / n
    f = lambda a: jnp.asarray(a.astype(np.float32))
    er, ei = f(np.cos(ph)), f(-np.sin(ph))
    m = jnp.concatenate([jnp.concatenate([er, -ei], axis=2), jnp.concatenate([ei, er], axis=2)], axis=1)
    return f(fa), f(ga), m.astype(BF16)


def _split(a):
    hi = a.astype(BF16)
    lo = (a - hi.astype(F32)).astype(BF16)
    return hi, lo


CONV_K1_CHUNK = 32
CONV_PAIR_UNROLL = 8
CONV_FREQ_UNROLL = 8
CONV_VMEM_MARGIN = 4 * 1024 * 1024


def _resident_table_spec(h1):
    return pl.BlockSpec((h1, 2 * DFT_N2, 2 * DFT_N2), lambda *_: (0, 0, 0), pipeline_mode=pl.Buffered(1))


def _spectrum_kernel(fa_ref, m_ref, xf_ref, xb_ref, o_ref, s_ref, *, h1, kc):
    hf = pl.program_id(2)
    half = DFT_N2

    @pl.when(hf == 0)
    def _():
        fa = fa_ref[...]

        def body(n2, carry):
            x = jnp.concatenate([xf_ref[pl.ds(n2, h1, stride=PITCH_Z), :],
                                 xb_ref[pl.ds(n2, h1, stride=PITCH_Z), :]], axis=1)
            xh, xl = _split(x)
            a = jnp.dot(fa, jnp.concatenate([xh, xh, xl], axis=0), preferred_element_type=F32)
            for ri in range(2):
                for d in range(2):
                    s_ref.at[d][pl.ds(ri * half + n2, h1, stride=PITCH_S), :] = (
                        a[ri * h1:(ri + 1) * h1, d * LANES:(d + 1) * LANES])
            return carry

        lax.fori_loop(0, DFT_N2, body, 0, unroll=CONV_PAIR_UNROLL)

    def freq_body(k1, carry):
        row = pl.multiple_of((hf * kc + k1) * PITCH_S, 8)
        x = jnp.concatenate([s_ref.at[0][pl.ds(row, 2 * half), :],
                             s_ref.at[1][pl.ds(row, 2 * half), :]], axis=1).astype(BF16)
        z = jnp.dot(m_ref[hf * kc + k1], x, preferred_element_type=F32)
        o_ref[k1] = _pack_bf16_pair(z[:half, :LANES] + z[:half, LANES:], z[half:, :LANES] - z[half:, LANES:])
        return carry

    lax.fori_loop(0, kc, freq_body, 0, unroll=CONV_FREQ_UNROLL)


def _filter_spectrum(fa, m_bf16, filt):
    _, n_slab, prow, _ = filt.shape
    n1, h1 = fa.shape
    kc = min(CONV_K1_CHUNK, h1)
    fh, fl = _split(fa)
    fa3 = jnp.concatenate([fh, fl, fh], axis=1)
    return pl.pallas_call(
        functools.partial(_spectrum_kernel, h1=h1, kc=kc),
        grid=(2, n_slab, h1 // kc),
        in_specs=[pl.BlockSpec((n1, 3 * h1), lambda o, s, hf: (0, 0)),
                  _resident_table_spec(h1),
                  pl.BlockSpec((None, None, prow, LANES), lambda o, s, hf: (2 * o, s, 0, 0)),
                  pl.BlockSpec((None, None, prow, LANES), lambda o, s, hf: (2 * o + 1, s, 0, 0))],
        out_specs=pl.BlockSpec((None, None, kc, DFT_N2, LANES), lambda o, s, hf: (o, s, hf, 0, 0)),
        out_shape=jax.ShapeDtypeStruct((2, n_slab, h1, DFT_N2, LANES), jnp.uint32),
        scratch_shapes=[pltpu.VMEM((2, h1 * PITCH_S, LANES), F32)],
        compiler_params=pltpu.CompilerParams(dimension_semantics=("parallel", "parallel", "arbitrary"),
                                             vmem_limit_bytes=CONV_VMEM_LIMIT),
        name="filter_spectrum",
    )(fa3, m_bf16, filt, filt)


def _conv_kernel(fa_ref, ga_ref, m_ref, k_ref, z_ref, hx_ref, sk_ref, o_ref, s_ref, *, h1, kc, nb):
    hf = pl.program_id(2)
    half = DFT_N2
    pairs = DFT_N2 // 2

    @pl.when(hf == 0)
    def _():
        fa = fa_ref[...]

        def body(j, carry):
            n2 = 2 * j
            for bi in range(nb):
                zb, sb = z_ref.at[bi], s_ref.at[bi]
                x = jnp.concatenate([zb[pl.ds(n2, h1, stride=PITCH_Z), :],
                                     zb[pl.ds(n2 + 1, h1, stride=PITCH_Z), :]], axis=1).astype(BF16)
                a = jnp.dot(fa, x, preferred_element_type=F32)
                for ri in range(2):
                    for u in range(2):
                        sb[pl.ds(ri * half + n2 + u, h1, stride=PITCH_S), :] = (
                            a[ri * h1:(ri + 1) * h1, u * LANES:(u + 1) * LANES])
            return carry

        lax.fori_loop(0, pairs, body, 0, unroll=CONV_PAIR_UNROLL // nb)

    def freq_body(k1, carry):
        row = pl.multiple_of((hf * kc + k1) * PITCH_S, 8)
        x = jnp.concatenate([s_ref[bi, pl.ds(row, 2 * half), :] for bi in range(nb)], axis=1).astype(BF16)
        m = m_ref[hf * kc + k1]
        z = jnp.dot(m, x, preferred_element_type=F32)
        zr, zi = z[:half], z[half:]
        kw = k_ref[k1]
        kr = jnp.concatenate([_unpack_bf16_pair(kw, 0, F32)] * nb, axis=1)
        ki = jnp.concatenate([_unpack_bf16_pair(kw, 1, F32)] * nb, axis=1)
        y = jnp.concatenate([zr * kr - zi * ki, zr * ki + zi * kr], axis=0).astype(BF16)
        bq = lax.dot_general(m, y, (((0,), (0,)), ((), ())), preferred_element_type=F32)
        for bi in range(nb):
            s_ref[bi, pl.ds(row, 2 * half), :] = bq[:, bi * LANES:(bi + 1) * LANES]
        return carry

    lax.fori_loop(0, kc, freq_body, 0, unroll=CONV_FREQ_UNROLL // nb)

    @pl.when(hf == pl.num_programs(2) - 1)
    def _():
        ga = ga_ref[...]
        sk = sk_ref[...]

        def body(j, carry):
            n2 = 2 * j
            for bi in range(nb):
                zb, hb, ob, sb = z_ref.at[bi], hx_ref.at[bi], o_ref.at[bi], s_ref.at[bi]
                cols = []
                for u in range(2):
                    re = sb[pl.ds(n2 + u, h1, stride=PITCH_S), :]
                    im = sb[pl.ds(half + n2 + u, h1, stride=PITCH_S), :]
                    cols.append(jnp.concatenate([re, im], axis=0))
                bn = jnp.concatenate(cols, axis=1).astype(BF16)
                y = jnp.dot(ga, bn, preferred_element_type=F32)
                for u in range(2):
                    zs = zb[pl.ds(n2 + u, h1, stride=PITCH_Z), :]
                    hs = hb[pl.ds(n2 + u, h1, stride=PITCH_Z), :]
                    ob[pl.ds(n2 + u, h1, stride=PITCH_Z), :] = hs * (y[:, u * LANES:(u + 1) * LANES] + sk * zs)
            return carry

        lax.fori_loop(0, pairs, body, 0, unroll=CONV_PAIR_UNROLL // nb)
        pad = jnp.zeros((PITCH_Z - DFT_N2, LANES), F32)
        for bi in range(nb):
            for blk in range(h1):
                o_ref[bi, blk * PITCH_Z + DFT_N2:(blk + 1) * PITCH_Z, :] = pad


def _conv_vmem_bytes(nb, prow, h1, kc):
    slabs = 3 * 2 * nb * prow * LANES * 4
    scratch = nb * h1 * PITCH_S * LANES * 4
    tables = h1 * (2 * DFT_N2) ** 2 * 2 + 2 * kc * DFT_N2 * LANES * 4
    return slabs + scratch + tables


def _long_conv_gate(tabs, kspec, order, z5, zsel, hx5, hxsel, skip):
    fa, ga, m_tab = tabs
    _, n_slab, batch, prow, _ = z5.shape
    n1, h1 = fa.shape
    kc = min(CONV_K1_CHUNK, h1)
    nb = 2 if (batch % 2 == 0 and _conv_vmem_bytes(2, prow, h1, kc) <= CONV_VMEM_LIMIT - CONV_VMEM_MARGIN) else 1
    slab = lambda sel: pl.BlockSpec((None, None, nb, prow, LANES), lambda s, b, hf: (sel, s, b, 0, 0))
    return pl.pallas_call(
        functools.partial(_conv_kernel, h1=h1, kc=kc, nb=nb),
        grid=(n_slab, batch // nb, h1 // kc),
        in_specs=[pl.BlockSpec((n1, h1), lambda s, b, hf: (0, 0)),
                  pl.BlockSpec((h1, n1), lambda s, b, hf: (0, 0)),
                  _resident_table_spec(h1),
                  pl.BlockSpec((None, None, kc, DFT_N2, LANES), lambda s, b, hf: (order, s, hf, 0, 0)),
                  slab(zsel), slab(hxsel),
                  pl.BlockSpec((None, None, 1, LANES), lambda s, b, hf: (order, s, 0, 0))],
        out_specs=pl.BlockSpec((None, nb, prow, LANES), lambda s, b, hf: (s, b, 0, 0)),
        out_shape=jax.ShapeDtypeStruct((n_slab, batch, prow, LANES), F32),
        scratch_shapes=[pltpu.VMEM((nb, h1 * PITCH_S, LANES), F32)],
        compiler_params=pltpu.CompilerParams(dimension_semantics=("parallel", "parallel", "arbitrary"),
                                             vmem_limit_bytes=CONV_VMEM_LIMIT),
        name="hyena_long_conv",
    )(fa.astype(BF16), ga.astype(BF16), m_tab, kspec, z5, hx5, skip)


def _hyena(proj, batch, seq, conv_w, conv_b, fparams, skip):
    c = HYENA_WIDTH
    fa, ga, m_tab = _dft_tables(seq)
    kspec = _filter_spectrum(fa, m_tab, _filters(seq, *fparams))
    u5 = _shortconv(proj, conv_w, conv_b, batch, seq)
    skip4 = skip.astype(F32).reshape(2, c // LANES, 1, LANES)
    tabs = (fa, ga, m_tab)
    z1 = _long_conv_gate(tabs, kspec, 0, u5, 0, u5, 1, skip4)
    return _long_conv_gate(tabs, kspec, 1, z1[None], 0, u5, 2, skip4)


def _final_kernel(o1_ref, o2_ref, o3_ref, l1_ref, l2_ref, l3_ref, ga_ref, hy_ref, gh_ref, ma_ref, mh_ref,
                  x_ref, wa_ref, wh_ref, wo_ref, pg_ref, y_ref):
    def cat(ref):
        return jnp.concatenate([ref[s] for s in range(ATTN_WIDTH // LANES)], axis=1)

    l1, l2, l3 = cat(l1_ref), cat(l2_ref), cat(l3_ref)
    mx = jnp.maximum(jnp.maximum(l1, l2), l3)
    e1, e2, e3 = jnp.exp(l1 - mx), jnp.exp(l2 - mx), jnp.exp(l3 - mx)
    attn = (e1 * cat(o1_ref) + e2 * cat(o2_ref) + e3 * cat(o3_ref)) / (e1 + e2 + e3)
    ga = ga_ref[...].astype(F32)
    a_in = (attn * (ga * jax.nn.sigmoid(ga))).astype(BF16)
    a_br = jnp.dot(a_in, wa_ref[...], preferred_element_type=F32)
    gh = gh_ref[...].astype(F32)
    hy = _load_padded_slabs(hy_ref, gh.shape[0] // DFT_N2)
    h_in = (hy * (gh * jax.nn.sigmoid(gh))).astype(BF16)
    h_br = jnp.dot(h_in, wh_ref[...], preferred_element_type=F32)
    merged = jax.nn.sigmoid(ma_ref[...].astype(F32)) * a_br + jax.nn.sigmoid(mh_ref[...].astype(F32)) * h_br
    out = jnp.dot(merged.astype(BF16), wo_ref[...], preferred_element_type=F32)
    ms = jnp.mean(out * out, axis=-1, keepdims=True)
    y_ref[...] = x_ref[...] + out * lax.rsqrt(ms + EPS) * pg_ref[...]


def _final(os_, ls_, proj, hy, x2d, wa, wh, wo, pg, seq, tm=256):
    rows = x2d.shape[0]
    tm = min(tm, rows)
    tiles_per_seq = seq // tm
    hy_spec = pl.BlockSpec((HYENA_WIDTH // LANES, None, tm // DFT_N2 * PITCH_Z, LANES),
                           lambda i: (0, i // tiles_per_seq, i % tiles_per_seq, 0))
    slab = pl.BlockSpec((ATTN_WIDTH // LANES, tm, LANES), lambda i: (0, i, 0))
    a512 = lambda off: pl.BlockSpec((tm, ATTN_WIDTH), lambda i: (i, off))
    a1024 = lambda off: pl.BlockSpec((tm, D_MODEL), lambda i: (i, off))
    full = lambda shape: pl.BlockSpec(shape, lambda i: (0, 0))
    return pl.pallas_call(
        _final_kernel,
        grid=(rows // tm,),
        in_specs=[slab] * 6 + [a512(R_GA // ATTN_WIDTH), hy_spec, a1024(R_GH // D_MODEL),
                               a1024(R_MG // D_MODEL), a1024(R_MG // D_MODEL + 1), a1024(0),
                               full((ATTN_WIDTH, D_MODEL)), full((D_MODEL, D_MODEL)), full((D_MODEL, D_MODEL)),
                               full((1, D_MODEL))],
        out_specs=pl.BlockSpec((tm, D_MODEL), lambda i: (i, 0)),
        out_shape=jax.ShapeDtypeStruct((rows, D_MODEL), F32),
        compiler_params=_cparams(("parallel",)),
        name="merge_out",
    )(*os_, *ls_, proj, hy, proj, proj, proj, x2d, wa, wh, wo, pg)


def _layer(x, rel_bias, pre_g, post_g, w_in, conv_w, conv_b, fparams, skip, w_br_a, w_br_h, w_out):
    batch, seq, _ = x.shape
    x2d = x.reshape(batch * seq, D_MODEL)
    w_qkv, w_rest = w_in
    gain = pre_g.reshape(1, -1).astype(F32)
    qkv = _inproj(x2d, gain, w_qkv, tn=QKV_WIDTH // 3, slab_out=True)
    proj = _inproj(x2d, gain, w_rest, tn=REST_WIDTH // 2, slab_out=False)
    os_, ls_ = [], []
    for g in range(N_GROUPS):
        o, l = _attention_group(qkv, rel_bias, g, batch, seq)
        os_.append(o)
        ls_.append(l)
    hy = _hyena(proj, batch, seq, conv_w, conv_b, fparams, skip)
    y = _final(os_, ls_, proj, hy, x2d, w_br_a, w_br_h, w_out, post_g.reshape(1, -1).astype(F32), seq)
    return y.reshape(batch, seq, D_MODEL)


def kernel(x_prompt, x_sample, rel_bias, pre_norm_g, post_norm_g, w_in, conv_w, conv_b, filt_w1, filt_b1, filt_w2, filt_b2, filt_w3, filt_b3, filt_w4, filt_freq, hyena_skip, w_branch_a, w_branch_h, w_out):
    depth = w_in.shape[0]

    def run(x):
        for l in range(depth):
            fparams = (filt_w1[l], filt_b1[l], filt_w2[l], filt_b2[l], filt_w3[l], filt_b3[l], filt_w4[l], filt_freq[l])
            w = w_in[l]
            col_scale = jnp.where(jnp.arange(QKV_WIDTH) < QKV_WIDTH // 3, HEAD_DIM ** -0.5, 1.0).astype(F32)
            w_qkv = (w[:, :QKV_WIDTH] * col_scale).astype(BF16)
            w_rest = jnp.concatenate([w[:, U_OFF:], w[:, GA_OFF:U_OFF]], axis=1).astype(BF16)
            x = _layer(x, rel_bias, pre_norm_g[l], post_norm_g[l], (w_qkv, w_rest), conv_w[l], conv_b[l],
                       fparams, hyena_skip[l], w_branch_a[l].astype(BF16), w_branch_h[l].astype(BF16),
                       w_out[l].astype(BF16))
        return x

    return (run(x_prompt), run(x_sample))
```

```python
import functools
import math

import numpy as np
import jax
import jax.numpy as jnp
from jax import lax
from jax.experimental import pallas as pl
from jax.experimental.pallas import tpu as pltpu

F32 = jnp.float32
BF16 = jnp.bfloat16

D_MODEL = 1024
EPS = 1e-6
HEAD_DIM = 64
ATTN_GROUPS = ((128, 1), (512, 4), (2048, 16))
N_GROUPS = 3
HEADS_PER_GROUP = 8
ATTN_WIDTH = HEADS_PER_GROUP * HEAD_DIM
HYENA_WIDTH = 1024
FILTER_EMB = 33
N_BANDS = 16
FILTER_HIDDEN = 64
NUM_BUCKETS = 32
MAX_DISTANCE = 1024
NEG_INF = -1e30
QKV_WIDTH = 4608
GA_OFF, U_OFF, GH_OFF, MG_OFF = 4608, 5120, 8192, 9216
REST_WIDTH = 6656
R_U, R_GH, R_MG, R_GA = 0, 3072, 4096, 6144

WIN = 64
LANES = 128
DFT_N2 = 128
PITCH_Z = 136
PITCH_S = 264
VMEM_LIMIT = 48 * 1024 * 1024
CONV_VMEM_LIMIT = 58 * 1024 * 1024


def _cparams(sem):
    return pltpu.CompilerParams(dimension_semantics=sem, vmem_limit_bytes=VMEM_LIMIT)


def _inproj_kernel(x_ref, g_ref, w_ref, o_ref, hn_ref, *, slabs):
    @pl.when(pl.program_id(1) == 0)
    def _():
        x = x_ref[...]
        ms = jnp.mean(x * x, axis=-1, keepdims=True)
        hn_ref[...] = (x * lax.rsqrt(ms + EPS) * g_ref[...]).astype(BF16)

    acc = jnp.dot(hn_ref[...], w_ref[...], preferred_element_type=F32)
    if slabs:
        for s in range(slabs):
            o_ref[s] = _pack_bf16_pair(acc[:, (2 * s) * LANES:(2 * s + 1) * LANES],
                                       acc[:, (2 * s + 1) * LANES:(2 * s + 2) * LANES])
    else:
        o_ref[...] = acc.astype(o_ref.dtype)


def _pack_bf16_pair(a, b):
    ua = lax.bitcast_convert_type(a.astype(BF16).astype(F32), jnp.uint32)
    ub = lax.bitcast_convert_type(b.astype(BF16).astype(F32), jnp.uint32)
    return (ua >> 16) | (ub & jnp.uint32(0xFFFF0000))


def _unpack_bf16_pair(w, idx, dtype=BF16):
    bits = (w << 16) if idx == 0 else (w & jnp.uint32(0xFFFF0000))
    return lax.bitcast_convert_type(bits, F32).astype(dtype)


def _inproj(x2d, g, w_bf16, tn, slab_out, tm=1024):
    rows = x2d.shape[0]
    width = w_bf16.shape[1]
    tm = min(tm, rows)
    if slab_out:
        ns = tn // (2 * LANES)
        out_spec = pl.BlockSpec((ns, tm, LANES), lambda i, j: (j, i, 0))
        out_shape = jax.ShapeDtypeStruct((width // (2 * LANES), rows, LANES), jnp.uint32)
    else:
        ns = 0
        out_spec = pl.BlockSpec((tm, tn), lambda i, j: (i, j))
        out_shape = jax.ShapeDtypeStruct((rows, width), BF16)
    return pl.pallas_call(
        functools.partial(_inproj_kernel, slabs=ns),
        grid=(rows // tm, width // tn),
        in_specs=[
            pl.BlockSpec((tm, D_MODEL), lambda i, j: (i, 0)),
            pl.BlockSpec((1, D_MODEL), lambda i, j: (0, 0)),
            pl.BlockSpec((D_MODEL, tn), lambda i, j: (0, j)),
        ],
        out_specs=out_spec,
        out_shape=out_shape,
        scratch_shapes=[pltpu.VMEM((tm, D_MODEL), BF16)],
        compiler_params=_cparams(("parallel", "arbitrary")),
        name="inproj_qkv" if slab_out else "inproj_rest",
    )(x2d, g, w_bf16)


def _t5_bucket_np(rel):
    half = NUM_BUCKETS // 2
    max_exact = half // 2
    n = np.abs(rel)
    nf = np.maximum(n, 1).astype(np.float64)
    large = max_exact + (np.log(nf / max_exact) / math.log(MAX_DISTANCE / max_exact) * (half - max_exact)).astype(np.int64)
    large = np.minimum(large, half - 1)
    return np.where(rel > 0, half, 0) + np.where(n < max_exact, n, large)


def _bias_mask(rel_bias_g, dilation):
    qi = np.arange(2 * WIN)[:, None]
    kj = np.arange(4 * WIN)[None, :] - WIN
    delta = kj - qi
    bucket = _t5_bucket_np(delta * dilation).astype(np.int32)
    onehot = jax.nn.one_hot(jnp.asarray(bucket), NUM_BUCKETS, dtype=F32)
    bias = jnp.einsum('qkb,bh->hqk', onehot, rel_bias_g.astype(F32), precision=lax.Precision.HIGHEST)
    bias = jnp.where(jnp.asarray(np.abs(delta) <= WIN)[None], bias, NEG_INF)
    return bias.reshape(HEADS_PER_GROUP // 2, 2, 2 * WIN, 4 * WIN)


ATTN_POS_PER_STEP = 4096
ATTN_TILES_PER_ITER = 2
PAIRS_PER_SLAB = 2


def _attn_kernel(q_ref, kp_ref, kc_ref, kn_ref, vp_ref, vc_ref, vn_ref, bm_ref, o_ref, l_ref,
                 kbuf, vbuf, *, tq, dil, sub_len):
    t = pl.program_id(2)
    halo = WIN * dil
    span = tq * dil
    kbuf[0:halo] = kp_ref[...]
    kbuf[halo:halo + span] = kc_ref[...]
    kbuf[halo + span:] = kn_ref[...]
    vbuf[0:halo] = vp_ref[...]
    vbuf[halo:halo + span] = vc_ref[...]
    vbuf[halo + span:] = vn_ref[...]
    lane = lax.broadcasted_iota(jnp.int32, (1, LANES), 1)
    is_lo = lane < HEAD_DIM
    kcol = lax.broadcasted_iota(jnp.int32, (1, 4 * WIN), 1)
    qt = 2 * WIN
    nq = tq // qt

    def tile(c, i):
        row0 = i * (qt * dil) + c
        kpos = t * tq + i * qt - WIN + kcol
        pen = jnp.where((kpos >= 0) & (kpos < sub_len), 0.0, NEG_INF).astype(F32)
        qw = q_ref[pl.ds(row0, qt, stride=dil), :]
        kw = kbuf[pl.ds(row0, 2 * qt, stride=dil), :]
        vw = vbuf[pl.ds(row0, 2 * qt, stride=dil), :]
        for hp in range(PAIRS_PER_SLAB):
            q = _unpack_bf16_pair(qw, hp)
            k = _unpack_bf16_pair(kw, hp)
            v = _unpack_bf16_pair(vw, hp)
            outs, lses = [], []
            for hh in range(2):
                sel = is_lo if hh == 0 else jnp.logical_not(is_lo)
                qm = jnp.where(sel, q, jnp.zeros_like(q))
                s = lax.dot_general(qm, k, (((1,), (1,)), ((), ())), preferred_element_type=F32)
                s = s + bm_ref[hp, hh] + pen
                m = jnp.max(s, axis=-1, keepdims=True)
                p = jnp.exp(s - m)
                den = jnp.sum(p, axis=-1, keepdims=True)
                pv = jnp.dot(p.astype(BF16), v, preferred_element_type=F32)
                outs.append(pv / den)
                lses.append(m + jnp.log(den))
            o_ref[hp, pl.ds(row0, qt, stride=dil), :] = jnp.where(is_lo, outs[0], outs[1])
            l_ref[hp, pl.ds(row0, qt, stride=dil), :] = jnp.where(is_lo, lses[0], lses[1])

    def body(it, carry):
        for u in range(ATTN_TILES_PER_ITER):
            idx = it * ATTN_TILES_PER_ITER + u
            tile(idx // nq, idx % nq)
        return carry

    lax.fori_loop(0, dil * nq // ATTN_TILES_PER_ITER, body, 0)


def _attention_group(qkv, rel_bias, g, batch, seq):
    _, dil = ATTN_GROUPS[g]
    sub_len = seq // dil
    tq = min(ATTN_POS_PER_STEP // dil, sub_len)
    nt = sub_len // tq
    span = tq * dil
    halo = WIN * dil
    n_hp = ATTN_WIDTH // LANES
    n_ps = n_hp // PAIRS_PER_SLAB
    n_slab = N_GROUPS * n_ps
    bm = _bias_mask(rel_bias[:, g * HEADS_PER_GROUP:(g + 1) * HEADS_PER_GROUP], dil)
    hb = tq // WIN
    nhb = sub_len // WIN

    def cur(which):
        return pl.BlockSpec((None, span, LANES), lambda b, ps, t: (which * n_slab + g * n_ps + ps, b * nt + t, 0))

    def prev(which):
        return pl.BlockSpec((None, halo, LANES),
                            lambda b, ps, t: (which * n_slab + g * n_ps + ps, b * nhb + jnp.maximum(t * hb - 1, 0), 0))

    def nxt(which):
        return pl.BlockSpec((None, halo, LANES),
                            lambda b, ps, t: (which * n_slab + g * n_ps + ps, b * nhb + jnp.minimum((t + 1) * hb, nhb - 1), 0))

    out_spec = pl.BlockSpec((PAIRS_PER_SLAB, span, LANES), lambda b, ps, t: (ps, b * nt + t, 0))
    return pl.pallas_call(
        functools.partial(_attn_kernel, tq=tq, dil=dil, sub_len=sub_len),
        grid=(batch, n_ps, nt),
        in_specs=[cur(0), prev(1), cur(1), nxt(1), prev(2), cur(2), nxt(2),
                  pl.BlockSpec((PAIRS_PER_SLAB, 2, 2 * WIN, 4 * WIN), lambda b, ps, t: (ps, 0, 0, 0))],
        out_specs=[out_spec, out_spec],
        out_shape=[jax.ShapeDtypeStruct((n_hp, batch * seq, LANES), F32)] * 2,
        scratch_shapes=[pltpu.VMEM((span + 2 * halo, LANES), jnp.uint32)] * 2,
        compiler_params=_cparams(("parallel", "parallel", "arbitrary")),
        name=f"attn_g{g}",
    )(qkv, qkv, qkv, qkv, qkv, qkv, qkv, bm)


def _shortconv_kernel(p_ref, c_ref, n_ref, w_ref, b_ref, o_ref, *, rows, halo):
    t = pl.program_id(1)
    nt = pl.num_programs(1)
    row = lax.broadcasted_iota(jnp.int32, (rows, 1), 0)
    res = []
    for e in range(2):
        x = c_ref[e].astype(F32)
        prev_row = jnp.where(t > 0, p_ref[e].astype(F32)[halo - 1:halo, :], 0.0)
        next_row = jnp.where(t < nt - 1, n_ref[e].astype(F32)[0:1, :], 0.0)
        up = jnp.where(row == 0, prev_row, pltpu.roll(x, 1, axis=0))
        dn = jnp.where(row == rows - 1, next_row, pltpu.roll(x, rows - 1, axis=0))
        res.append(up * w_ref[0:1, :] + x * w_ref[1:2, :] + dn * w_ref[2:3, :] + b_ref[...])
    _store_padded_slabs(o_ref, _pack_bf16_pair(res[0], res[1]), rows // DFT_N2)


def _store_padded_slabs(o_ref, val, nblk):
    pad = jnp.zeros((PITCH_Z - DFT_N2, LANES), val.dtype)
    for s in range(val.shape[1] // LANES):
        for jb in range(nblk):
            o_ref[s, jb * PITCH_Z:jb * PITCH_Z + DFT_N2, :] = val[jb * DFT_N2:(jb + 1) * DFT_N2, s * LANES:(s + 1) * LANES]
            o_ref[s, jb * PITCH_Z + DFT_N2:(jb + 1) * PITCH_Z, :] = pad


def _load_padded_slabs(ref, nblk):
    return jnp.concatenate(
        [jnp.concatenate([ref[s, jb * PITCH_Z:jb * PITCH_Z + DFT_N2, :] for jb in range(nblk)], axis=0)
         for s in range(ref.shape[0])], axis=1)


def _shortconv(proj, conv_w, conv_b, batch, seq, rows=512):
    assert batch % 2 == 0
    rows = min(rows, seq)
    prow = rows // DFT_N2 * PITCH_Z
    n_slab = HYENA_WIDTH // LANES
    halo = 16
    pv = proj.reshape(batch, seq, REST_WIDTH)
    cb = R_U // HYENA_WIDTH
    hb = rows // halo
    nhb = seq // halo
    return pl.pallas_call(
        functools.partial(_shortconv_kernel, rows=rows, halo=halo),
        grid=(batch // 2, seq // rows, 3),
        in_specs=[
            pl.BlockSpec((2, halo, HYENA_WIDTH), lambda b, t, j: (b, jnp.maximum(t * hb - 1, 0), cb + j)),
            pl.BlockSpec((2, rows, HYENA_WIDTH), lambda b, t, j: (b, t, cb + j)),
            pl.BlockSpec((2, halo, HYENA_WIDTH), lambda b, t, j: (b, jnp.minimum((t + 1) * hb, nhb - 1), cb + j)),
            pl.BlockSpec((3, HYENA_WIDTH), lambda b, t, j: (0, j)),
            pl.BlockSpec((1, HYENA_WIDTH), lambda b, t, j: (0, j)),
        ],
        out_specs=pl.BlockSpec((None, n_slab, None, prow, LANES), lambda b, t, j: (j, 0, b, t, 0)),
        out_shape=jax.ShapeDtypeStruct((3, n_slab, batch // 2, seq // DFT_N2 * PITCH_Z, LANES), jnp.uint32),
        compiler_params=_cparams(("parallel", "parallel", "arbitrary")),
        name="shortconv",
    )(pv, pv, pv, conv_w, conv_b.reshape(1, -1))


def _filter_features(length):
    t = np.linspace(0.0, 1.0, length)[:, None]
    ang = (2.0 * math.pi / length) * np.arange(length, dtype=np.float64)[:, None]
    bands = np.linspace(1e-4, N_BANDS - 1, N_BANDS)[None]
    z = np.concatenate([t, np.cos(ang * bands), -np.sin(ang * bands)], axis=-1)
    zp = np.zeros((length, LANES), np.float32)
    zp[:, :FILTER_EMB] = z
    return zp


def _decay_rates():
    max_decay = math.log(1e-2) / 0.3
    min_decay = math.log(1e-2) / 1.5
    return np.abs(np.linspace(min_decay, max_decay, HYENA_WIDTH)).astype(np.float32)[None]


def _filter_kernel(z_ref, w1_ref, b1_ref, w2_ref, b2_ref, w3_ref, b3_ref, w4_ref, fr_ref, dl_ref, o_ref, *, rows):
    hi = lax.Precision.HIGHEST
    z = z_ref[...]
    fr = fr_ref[...]
    h = jnp.sin(fr * (jnp.dot(z, w1_ref[...], precision=hi, preferred_element_type=F32) + b1_ref[...]))
    h = jnp.sin(fr * (jnp.dot(h, w2_ref[...], precision=hi, preferred_element_type=F32) + b2_ref[...]))
    h = jnp.sin(fr * (jnp.dot(h, w3_ref[...], precision=hi, preferred_element_type=F32) + b3_ref[...]))
    hh, hl = _split(h)
    filt = jnp.dot(jnp.concatenate([hh, hl, hh], axis=1), w4_ref[...], preferred_element_type=F32)
    decay = jnp.exp(-z[:, 0:1] * dl_ref[...])
    row = pl.program_id(0) * rows + lax.broadcasted_iota(jnp.int32, (rows, 1), 0)
    for j in range(4):
        cs = slice(j * HYENA_WIDTH, (j + 1) * HYENA_WIDTH)
        val = filt[:, cs] * decay
        if j % 2 == 1:
            val = jnp.where(row == 0, 0.0, val)
        _store_padded_slabs(o_ref.at[j], val, rows // DFT_N2)


def _filters(length, fw1, fb1, fw2, fb2, fw3, fb3, fw4, ffreq, rows=256):
    rows = min(rows, length)
    n_slab = HYENA_WIDTH // LANES
    prow = rows // DFT_N2 * PITCH_Z
    w4h, w4l = _split(fw4.astype(F32))
    w4s = jnp.concatenate([w4h, w4h, w4l], axis=0)
    zfeat = jnp.asarray(_filter_features(length))
    w1p = jnp.zeros((LANES, FILTER_HIDDEN), F32).at[:FILTER_EMB].set(fw1.astype(F32))
    full = lambda shape: pl.BlockSpec(shape, lambda i: (0,) * len(shape))
    return pl.pallas_call(
        functools.partial(_filter_kernel, rows=rows),
        grid=(length // rows,),
        in_specs=[pl.BlockSpec((rows, LANES), lambda i: (i, 0)),
                  full((LANES, FILTER_HIDDEN)), full((1, FILTER_HIDDEN)),
                  full((FILTER_HIDDEN, FILTER_HIDDEN)), full((1, FILTER_HIDDEN)),
                  full((FILTER_HIDDEN, FILTER_HIDDEN)), full((1, FILTER_HIDDEN)),
                  full((3 * FILTER_HIDDEN, 4 * HYENA_WIDTH)), full((1, FILTER_HIDDEN)),
                  full((1, HYENA_WIDTH))],
        out_specs=pl.BlockSpec((4, n_slab, prow, LANES), lambda i: (0, 0, i, 0)),
        out_shape=jax.ShapeDtypeStruct((4, n_slab, length // DFT_N2 * PITCH_Z, LANES), F32),
        compiler_params=_cparams(("parallel",)),
        name="hyena_filters",
    )(zfeat, w1p, fb1.reshape(1, -1), fw2, fb2.reshape(1, -1), fw3, fb3.reshape(1, -1), w4s,
      ffreq.reshape(1, -1), jnp.asarray(_decay_rates()))


def _dft_tables(length):
    n = 2 * length
    n1 = n // DFT_N2
    h1 = n1 // 2
    kk = np.arange(h1)[:, None] + 0.5
    th = 2.0 * math.pi * kk * np.arange(h1)[None, :] / n1
    fa = np.concatenate([np.cos(th), -np.sin(th)], axis=0)
    ga = (2.0 / n) * fa.T
    k = np.arange(h1)[:, None, None] + n1 * np.arange(DFT_N2)[None, :, None] + 0.5
    ph = 2.0 * math.pi * k * np.arange(DFT_N2)[None, None, :] / n
    f = lambda a: jnp.asarray(a.astype(np.float32))
    er, ei = f(np.cos(ph)), f(-np.sin(ph))
    m = jnp.concatenate([jnp.concatenate([er, -ei], axis=2), jnp.concatenate([ei, er], axis=2)], axis=1)
    return f(fa), f(ga), m.astype(BF16)


def _split(a):
    hi = a.astype(BF16)
    lo = (a - hi.astype(F32)).astype(BF16)
    return hi, lo


CONV_K1_CHUNK = 32
CONV_PAIR_UNROLL = 8
CONV_FREQ_UNROLL = 8


def _resident_table_spec(h1):
    return pl.BlockSpec((h1, 2 * DFT_N2, 2 * DFT_N2), lambda *_: (0, 0, 0), pipeline_mode=pl.Buffered(1))


def _spectrum_kernel(fa_ref, m_ref, xf_ref, xb_ref, o_ref, s_ref, *, h1, kc):
    hf = pl.program_id(2)
    half = DFT_N2

    @pl.when(hf == 0)
    def _():
        fa = fa_ref[...]

        def body(n2, carry):
            x = jnp.concatenate([xf_ref[pl.ds(n2, h1, stride=PITCH_Z), :],
                                 xb_ref[pl.ds(n2, h1, stride=PITCH_Z), :]], axis=1)
            xh, xl = _split(x)
            a = jnp.dot(fa, jnp.concatenate([xh, xh, xl], axis=0), preferred_element_type=F32)
            for ri in range(2):
                for d in range(2):
                    s_ref.at[d][pl.ds(ri * half + n2, h1, stride=PITCH_S), :] = (
                        a[ri * h1:(ri + 1) * h1, d * LANES:(d + 1) * LANES])
            return carry

        lax.fori_loop(0, DFT_N2, body, 0, unroll=CONV_PAIR_UNROLL)

    def freq_body(k1, carry):
        row = pl.multiple_of((hf * kc + k1) * PITCH_S, 8)
        x = jnp.concatenate([s_ref.at[0][pl.ds(row, 2 * half), :],
                             s_ref.at[1][pl.ds(row, 2 * half), :]], axis=1).astype(BF16)
        z = jnp.dot(m_ref[hf * kc + k1], x, preferred_element_type=F32)
        o_ref[k1] = _pack_bf16_pair(z[:half, :LANES] + z[:half, LANES:], z[half:, :LANES] - z[half:, LANES:])
        return carry

    lax.fori_loop(0, kc, freq_body, 0, unroll=CONV_FREQ_UNROLL)


def _filter_spectrum(fa, m_bf16, filt):
    _, n_slab, prow, _ = filt.shape
    n1, h1 = fa.shape
    kc = min(CONV_K1_CHUNK, h1)
    fh, fl = _split(fa)
    fa3 = jnp.concatenate([fh, fl, fh], axis=1)
    return pl.pallas_call(
        functools.partial(_spectrum_kernel, h1=h1, kc=kc),
        grid=(2, n_slab, h1 // kc),
        in_specs=[pl.BlockSpec((n1, 3 * h1), lambda o, s, hf: (0, 0)),
                  _resident_table_spec(h1),
                  pl.BlockSpec((None, None, prow, LANES), lambda o, s, hf: (2 * o, s, 0, 0)),
                  pl.BlockSpec((None, None, prow, LANES), lambda o, s, hf: (2 * o + 1, s, 0, 0))],
        out_specs=pl.BlockSpec((None, None, kc, DFT_N2, LANES), lambda o, s, hf: (o, s, hf, 0, 0)),
        out_shape=jax.ShapeDtypeStruct((2, n_slab, h1, DFT_N2, LANES), jnp.uint32),
        scratch_shapes=[pltpu.VMEM((2, h1 * PITCH_S, LANES), F32)],
        compiler_params=pltpu.CompilerParams(dimension_semantics=("parallel", "parallel", "arbitrary"),
                                             vmem_limit_bytes=CONV_VMEM_LIMIT),
        name="filter_spectrum",
    )(fa3, m_bf16, filt, filt)


def _conv_kernel(fa_ref, ga_ref, m_ref, k_ref, z_ref, hx_ref, sk_ref, o_ref, s_ref, *, h1):
    half = DFT_N2
    pairs = DFT_N2 // 2
    fa = fa_ref[...]
    ga = ga_ref[...]
    sk = sk_ref[...]

    def stage_a(j, carry):
        n2 = 2 * j
        zw = [z_ref[pl.ds(n2 + u, h1, stride=PITCH_Z), :] for u in range(2)]
        a = [jnp.dot(fa, jnp.concatenate([_unpack_bf16_pair(zw[0], e), _unpack_bf16_pair(zw[1], e)], axis=1),
                     preferred_element_type=F32) for e in range(2)]
        aw = _pack_bf16_pair(a[0], a[1])
        for ri in range(2):
            for u in range(2):
                s_ref[pl.ds(ri * half + n2 + u, h1, stride=PITCH_S), :] = (
                    aw[ri * h1:(ri + 1) * h1, u * LANES:(u + 1) * LANES])
        return carry

    lax.fori_loop(0, pairs, stage_a, 0, unroll=CONV_PAIR_UNROLL)

    def freq_body(k1, carry):
        row = pl.multiple_of(k1 * PITCH_S, 8)
        xw = s_ref[pl.ds(row, 2 * half), :]
        x = jnp.concatenate([_unpack_bf16_pair(xw, 0), _unpack_bf16_pair(xw, 1)], axis=1)
        m = m_ref[k1]
        z = jnp.dot(m, x, preferred_element_type=F32)
        zr, zi = z[:half], z[half:]
        kw = k_ref[k1]
        kr = jnp.concatenate([_unpack_bf16_pair(kw, 0, F32)] * 2, axis=1)
        ki = jnp.concatenate([_unpack_bf16_pair(kw, 1, F32)] * 2, axis=1)
        y = jnp.concatenate([zr * kr - zi * ki, zr * ki + zi * kr], axis=0).astype(BF16)
        bq = lax.dot_general(m, y, (((0,), (0,)), ((), ())), preferred_element_type=F32)
        s_ref[pl.ds(row, 2 * half), :] = _pack_bf16_pair(bq[:, :LANES], bq[:, LANES:])
        return carry

    lax.fori_loop(0, h1, freq_body, 0, unroll=CONV_FREQ_UNROLL)

    def stage_inv(j, carry):
        n2 = 2 * j
        rw = [s_ref[pl.ds(n2 + u, h1, stride=PITCH_S), :] for u in range(2)]
        iw = [s_ref[pl.ds(half + n2 + u, h1, stride=PITCH_S), :] for u in range(2)]
        zw = [z_ref[pl.ds(n2 + u, h1, stride=PITCH_Z), :] for u in range(2)]
        hw = [hx_ref[pl.ds(n2 + u, h1, stride=PITCH_Z), :] for u in range(2)]
        y = []
        for e in range(2):
            bn = jnp.concatenate(
                [jnp.concatenate([_unpack_bf16_pair(rw[u], e), _unpack_bf16_pair(iw[u], e)], axis=0)
                 for u in range(2)], axis=1)
            y.append(jnp.dot(ga, bn, preferred_element_type=F32))
        for u in range(2):
            res = [_unpack_bf16_pair(hw[u], e, F32) * (y[e][:, u * LANES:(u + 1) * LANES]
                                                       + sk * _unpack_bf16_pair(zw[u], e, F32)) for e in range(2)]
            o_ref[pl.ds(n2 + u, h1, stride=PITCH_Z), :] = _pack_bf16_pair(res[0], res[1])
        return carry

    lax.fori_loop(0, pairs, stage_inv, 0, unroll=CONV_PAIR_UNROLL // 2)
    pad = jnp.zeros((PITCH_Z - DFT_N2, LANES), jnp.uint32)
    for blk in range(h1):
        o_ref[blk * PITCH_Z + DFT_N2:(blk + 1) * PITCH_Z, :] = pad


def _long_conv_gate(tabs, kspec, order, z5, zsel, hx5, hxsel, skip):
    fa, ga, m_tab = tabs
    _, n_slab, n_pair, prow, _ = z5.shape
    n1, h1 = fa.shape
    slab = lambda sel: pl.BlockSpec((None, None, None, prow, LANES), lambda s, b: (sel, s, b, 0, 0))
    return pl.pallas_call(
        functools.partial(_conv_kernel, h1=h1),
        grid=(n_slab, n_pair),
        in_specs=[pl.BlockSpec((n1, h1), lambda s, b: (0, 0)),
                  pl.BlockSpec((h1, n1), lambda s, b: (0, 0)),
                  _resident_table_spec(h1),
                  pl.BlockSpec((None, None, h1, DFT_N2, LANES), lambda s, b: (order, s, 0, 0, 0)),
                  slab(zsel), slab(hxsel),
                  pl.BlockSpec((None, None, 1, LANES), lambda s, b: (order, s, 0, 0))],
        out_specs=pl.BlockSpec((None, None, prow, LANES), lambda s, b: (s, b, 0, 0)),
        out_shape=jax.ShapeDtypeStruct((n_slab, n_pair, prow, LANES), jnp.uint32),
        scratch_shapes=[pltpu.VMEM((h1 * PITCH_S, LANES), jnp.uint32)],
        compiler_params=pltpu.CompilerParams(dimension_semantics=("parallel", "arbitrary"),
                                             vmem_limit_bytes=CONV_VMEM_LIMIT),
        name="hyena_long_conv",
    )(fa.astype(BF16), ga.astype(BF16), m_tab, kspec, z5, hx5, skip)


def _hyena(proj, batch, seq, conv_w, conv_b, fparams, skip):
    c = HYENA_WIDTH
    fa, ga, m_tab = _dft_tables(seq)
    kspec = _filter_spectrum(fa, m_tab, _filters(seq, *fparams))
    u5 = _shortconv(proj, conv_w, conv_b, batch, seq)
    skip4 = skip.astype(F32).reshape(2, c // LANES, 1, LANES)
    tabs = (fa, ga, m_tab)
    z1 = _long_conv_gate(tabs, kspec, 0, u5, 0, u5, 1, skip4)
    return _long_conv_gate(tabs, kspec, 1, z1[None], 0, u5, 2, skip4)


def _final_kernel(o1_ref, o2_ref, o3_ref, l1_ref, l2_ref, l3_ref, ga_ref, hy_ref, gh_ref, ma_ref, mh_ref,
                  x_ref, wa_ref, wh_ref, wo_ref, pg_ref, y_ref, *, tiles_per_seq):
    def cat(ref):
        return jnp.concatenate([ref[s] for s in range(ATTN_WIDTH // LANES)], axis=1)

    l1, l2, l3 = cat(l1_ref), cat(l2_ref), cat(l3_ref)
    mx = jnp.maximum(jnp.maximum(l1, l2), l3)
    e1, e2, e3 = jnp.exp(l1 - mx), jnp.exp(l2 - mx), jnp.exp(l3 - mx)
    attn = (e1 * cat(o1_ref) + e2 * cat(o2_ref) + e3 * cat(o3_ref)) / (e1 + e2 + e3)
    ga = ga_ref[...].astype(F32)
    a_in = (attn * (ga * jax.nn.sigmoid(ga))).astype(BF16)
    a_br = jnp.dot(a_in, wa_ref[...], preferred_element_type=F32)
    gh = gh_ref[...].astype(F32)
    odd = (pl.program_id(0) // tiles_per_seq) % 2
    hw = _load_padded_slabs(hy_ref, gh.shape[0] // DFT_N2)
    hy = lax.bitcast_convert_type((hw << ((1 - odd) * 16).astype(jnp.uint32)) & jnp.uint32(0xFFFF0000), F32)
    h_in = (hy * (gh * jax.nn.sigmoid(gh))).astype(BF16)
    h_br = jnp.dot(h_in, wh_ref[...], preferred_element_type=F32)
    merged = jax.nn.sigmoid(ma_ref[...].astype(F32)) * a_br + jax.nn.sigmoid(mh_ref[...].astype(F32)) * h_br
    out = jnp.dot(merged.astype(BF16), wo_ref[...], preferred_element_type=F32)
    ms = jnp.mean(out * out, axis=-1, keepdims=True)
    y_ref[...] = x_ref[...] + out * lax.rsqrt(ms + EPS) * pg_ref[...]


def _final(os_, ls_, proj, hy, x2d, wa, wh, wo, pg, seq, tm=256):
    rows = x2d.shape[0]
    tm = min(tm, rows)
    tiles_per_seq = seq // tm
    hy_spec = pl.BlockSpec((HYENA_WIDTH // LANES, None, tm // DFT_N2 * PITCH_Z, LANES),
                           lambda i: (0, i // (2 * tiles_per_seq), i % tiles_per_seq, 0))
    slab = pl.BlockSpec((ATTN_WIDTH // LANES, tm, LANES), lambda i: (0, i, 0))
    a512 = lambda off: pl.BlockSpec((tm, ATTN_WIDTH), lambda i: (i, off))
    a1024 = lambda off: pl.BlockSpec((tm, D_MODEL), lambda i: (i, off))
    full = lambda shape: pl.BlockSpec(shape, lambda i: (0, 0))
    return pl.pallas_call(
        functools.partial(_final_kernel, tiles_per_seq=tiles_per_seq),
        grid=(rows // tm,),
        in_specs=[slab] * 6 + [a512(R_GA // ATTN_WIDTH), hy_spec, a1024(R_GH // D_MODEL),
                               a1024(R_MG // D_MODEL), a1024(R_MG // D_MODEL + 1), a1024(0),
                               full((ATTN_WIDTH, D_MODEL)), full((D_MODEL, D_MODEL)), full((D_MODEL, D_MODEL)),
                               full((1, D_MODEL))],
        out_specs=pl.BlockSpec((tm, D_MODEL), lambda i: (i, 0)),
        out_shape=jax.ShapeDtypeStruct((rows, D_MODEL), F32),
        compiler_params=_cparams(("parallel",)),
        name="merge_out",
    )(*os_, *ls_, proj, hy, proj, proj, proj, x2d, wa, wh, wo, pg)


def _layer(x, rel_bias, pre_g, post_g, w_in, conv_w, conv_b, fparams, skip, w_br_a, w_br_h, w_out):
    batch, seq, _ = x.shape
    x2d = x.reshape(batch * seq, D_MODEL)
    w_qkv, w_rest = w_in
    gain = pre_g.reshape(1, -1).astype(F32)
    qkv = _inproj(x2d, gain, w_qkv, tn=QKV_WIDTH // 3, slab_out=True)
    proj = _inproj(x2d, gain, w_rest, tn=REST_WIDTH // 2, slab_out=False)
    os_, ls_ = [], []
    for g in range(N_GROUPS):
        o, l = _attention_group(qkv, rel_bias, g, batch, seq)
        os_.append(o)
        ls_.append(l)
    hy = _hyena(proj, batch, seq, conv_w, conv_b, fparams, skip)
    y = _final(os_, ls_, proj, hy, x2d, w_br_a, w_br_h, w_out, post_g.reshape(1, -1).astype(F32), seq)
    return y.reshape(batch, seq, D_MODEL)


def kernel(x_prompt, x_sample, rel_bias, pre_norm_g, post_norm_g, w_in, conv_w, conv_b, filt_w1, filt_b1, filt_w2, filt_b2, filt_w3, filt_b3, filt_w4, filt_freq, hyena_skip, w_branch_a, w_branch_h, w_out):
    depth = w_in.shape[0]

    def run(x):
        for l in range(depth):
            fparams = (filt_w1[l], filt_b1[l], filt_w2[l], filt_b2[l], filt_w3[l], filt_b3[l], filt_w4[l], filt_freq[l])
            w = w_in[l]
            col_scale = jnp.where(jnp.arange(QKV_WIDTH) < QKV_WIDTH // 3, HEAD_DIM ** -0.5, 1.0).astype(F32)
            w_qkv = (w[:, :QKV_WIDTH] * col_scale).astype(BF16)
            w_rest = jnp.concatenate([w[:, U_OFF:], w[:, GA_OFF:U_OFF]], axis=1).astype(BF16)
            x = _layer(x, rel_bias, pre_norm_g[l], post_norm_g[l], (w_qkv, w_rest), conv_w[l], conv_b[l],
                       fparams, hyena_skip[l], w_branch_a[l].astype(BF16), w_branch_h[l].astype(BF16),
                       w_out[l].astype(BF16))
        return x

    return (run(x_prompt), run(x_sample))
```

```python
import functools
import math

import numpy as np
import jax
import jax.numpy as jnp
from jax import lax
from jax.experimental import pallas as pl
from jax.experimental.pallas import tpu as pltpu

F32 = jnp.float32
BF16 = jnp.bfloat16

D_MODEL = 1024
EPS = 1e-6
HEAD_DIM = 64
ATTN_GROUPS = ((128, 1), (512, 4), (2048, 16))
N_GROUPS = 3
HEADS_PER_GROUP = 8
ATTN_WIDTH = HEADS_PER_GROUP * HEAD_DIM
HYENA_WIDTH = 1024
FILTER_EMB = 33
N_BANDS = 16
FILTER_HIDDEN = 64
NUM_BUCKETS = 32
MAX_DISTANCE = 1024
NEG_INF = -1e30
QKV_WIDTH = 4608
GA_OFF, U_OFF, GH_OFF, MG_OFF = 4608, 5120, 8192, 9216
REST_WIDTH = 6656
R_U, R_GH, R_MG, R_GA = 0, 3072, 4096, 6144

WIN = 64
LANES = 128
DFT_N2 = 128
PITCH_Z = 136
PITCH_S = 264
VMEM_LIMIT = 48 * 1024 * 1024
CONV_VMEM_LIMIT = 58 * 1024 * 1024


def _cparams(sem):
    return pltpu.CompilerParams(dimension_semantics=sem, vmem_limit_bytes=VMEM_LIMIT)


def _inproj_kernel(x_ref, g_ref, w_ref, o_ref, hn_ref, *, slabs):
    @pl.when(pl.program_id(1) == 0)
    def _():
        x = x_ref[...]
        ms = jnp.mean(x * x, axis=-1, keepdims=True)
        hn_ref[...] = (x * lax.rsqrt(ms + EPS) * g_ref[...]).astype(BF16)

    acc = jnp.dot(hn_ref[...], w_ref[...], preferred_element_type=F32)
    if slabs:
        for s in range(slabs):
            o_ref[s] = _pack_bf16_pair(acc[:, (2 * s) * LANES:(2 * s + 1) * LANES],
                                       acc[:, (2 * s + 1) * LANES:(2 * s + 2) * LANES])
    else:
        o_ref[...] = acc.astype(o_ref.dtype)


def _pack_bf16_pair(a, b):
    ua = lax.bitcast_convert_type(a.astype(BF16).astype(F32), jnp.uint32)
    ub = lax.bitcast_convert_type(b.astype(BF16).astype(F32), jnp.uint32)
    return (ua >> 16) | (ub & jnp.uint32(0xFFFF0000))


def _unpack_bf16_pair(w, idx, dtype=BF16):
    bits = (w << 16) if idx == 0 else (w & jnp.uint32(0xFFFF0000))
    return lax.bitcast_convert_type(bits, F32).astype(dtype)


def _inproj(x2d, g, w_bf16, tn, slab_out, tm=1024):
    rows = x2d.shape[0]
    width = w_bf16.shape[1]
    tm = min(tm, rows)
    if slab_out:
        ns = tn // (2 * LANES)
        out_spec = pl.BlockSpec((ns, tm, LANES), lambda i, j: (j, i, 0))
        out_shape = jax.ShapeDtypeStruct((width // (2 * LANES), rows, LANES), jnp.uint32)
    else:
        ns = 0
        out_spec = pl.BlockSpec((tm, tn), lambda i, j: (i, j))
        out_shape = jax.ShapeDtypeStruct((rows, width), BF16)
    return pl.pallas_call(
        functools.partial(_inproj_kernel, slabs=ns),
        grid=(rows // tm, width // tn),
        in_specs=[
            pl.BlockSpec((tm, D_MODEL), lambda i, j: (i, 0)),
            pl.BlockSpec((1, D_MODEL), lambda i, j: (0, 0)),
            pl.BlockSpec((D_MODEL, tn), lambda i, j: (0, j)),
        ],
        out_specs=out_spec,
        out_shape=out_shape,
        scratch_shapes=[pltpu.VMEM((tm, D_MODEL), BF16)],
        compiler_params=_cparams(("parallel", "arbitrary")),
        name="inproj_qkv" if slab_out else "inproj_rest",
    )(x2d, g, w_bf16)


def _t5_bucket_np(rel):
    half = NUM_BUCKETS // 2
    max_exact = half // 2
    n = np.abs(rel)
    nf = np.maximum(n, 1).astype(np.float64)
    large = max_exact + (np.log(nf / max_exact) / math.log(MAX_DISTANCE / max_exact) * (half - max_exact)).astype(np.int64)
    large = np.minimum(large, half - 1)
    return np.where(rel > 0, half, 0) + np.where(n < max_exact, n, large)


def _bias_mask(rel_bias_g, dilation):
    qi = np.arange(2 * WIN)[:, None]
    kj = np.arange(4 * WIN)[None, :] - WIN
    delta = kj - qi
    bucket = _t5_bucket_np(delta * dilation).astype(np.int32)
    onehot = jax.nn.one_hot(jnp.asarray(bucket), NUM_BUCKETS, dtype=F32)
    bias = jnp.einsum('qkb,bh->hqk', onehot, rel_bias_g.astype(F32), precision=lax.Precision.HIGHEST)
    bias = jnp.where(jnp.asarray(np.abs(delta) <= WIN)[None], bias, NEG_INF)
    return bias.reshape(HEADS_PER_GROUP // 2, 2, 2 * WIN, 4 * WIN)


ATTN_POS_PER_STEP = 4096
ATTN_TILES_PER_ITER = 2
PAIRS_PER_SLAB = 2


def _attn_kernel(q_ref, kp_ref, kc_ref, kn_ref, vp_ref, vc_ref, vn_ref, bm_ref, o_ref, l_ref,
                 kbuf, vbuf, *, tq, dil, sub_len):
    t = pl.program_id(2)
    halo = WIN * dil
    span = tq * dil
    kbuf[0:halo] = kp_ref[...]
    kbuf[halo:halo + span] = kc_ref[...]
    kbuf[halo + span:] = kn_ref[...]
    vbuf[0:halo] = vp_ref[...]
    vbuf[halo:halo + span] = vc_ref[...]
    vbuf[halo + span:] = vn_ref[...]
    lane = lax.broadcasted_iota(jnp.int32, (1, LANES), 1)
    is_lo = lane < HEAD_DIM
    kcol = lax.broadcasted_iota(jnp.int32, (1, 4 * WIN), 1)
    qt = 2 * WIN
    nq = tq // qt

    def tile(c, i):
        row0 = i * (qt * dil) + c
        kpos = t * tq + i * qt - WIN + kcol
        pen = jnp.where((kpos >= 0) & (kpos < sub_len), 0.0, NEG_INF).astype(F32)
        qw = q_ref[pl.ds(row0, qt, stride=dil), :]
        kw = kbuf[pl.ds(row0, 2 * qt, stride=dil), :]
        vw = vbuf[pl.ds(row0, 2 * qt, stride=dil), :]
        for hp in range(PAIRS_PER_SLAB):
            q = _unpack_bf16_pair(qw, hp)
            k = _unpack_bf16_pair(kw, hp)
            v = _unpack_bf16_pair(vw, hp)
            outs, lses = [], []
            for hh in range(2):
                sel = is_lo if hh == 0 else jnp.logical_not(is_lo)
                qm = jnp.where(sel, q, jnp.zeros_like(q))
                s = lax.dot_general(qm, k, (((1,), (1,)), ((), ())), preferred_element_type=F32)
                s = s + bm_ref[hp, hh] + pen
                m = jnp.max(s, axis=-1, keepdims=True)
                p = jnp.exp(s - m)
                den = jnp.sum(p, axis=-1, keepdims=True)
                pv = jnp.dot(p.astype(BF16), v, preferred_element_type=F32)
                outs.append(pv / den)
                lses.append(m + jnp.log(den))
            o_ref[hp, pl.ds(row0, qt, stride=dil), :] = jnp.where(is_lo, outs[0], outs[1])
            l_ref[hp, pl.ds(row0, qt, stride=dil), :] = jnp.where(is_lo, lses[0], lses[1])

    def body(it, carry):
        for u in range(ATTN_TILES_PER_ITER):
            idx = it * ATTN_TILES_PER_ITER + u
            tile(idx // nq, idx % nq)
        return carry

    lax.fori_loop(0, dil * nq // ATTN_TILES_PER_ITER, body, 0)


def _attention_group(qkv, rel_bias, g, batch, seq):
    _, dil = ATTN_GROUPS[g]
    sub_len = seq // dil
    tq = min(ATTN_POS_PER_STEP // dil, sub_len)
    nt = sub_len // tq
    span = tq * dil
    halo = WIN * dil
    n_hp = ATTN_WIDTH // LANES
    n_ps = n_hp // PAIRS_PER_SLAB
    n_slab = N_GROUPS * n_ps
    bm = _bias_mask(rel_bias[:, g * HEADS_PER_GROUP:(g + 1) * HEADS_PER_GROUP], dil)
    hb = tq // WIN
    nhb = sub_len // WIN

    def cur(which):
        return pl.BlockSpec((None, span, LANES), lambda b, ps, t: (which * n_slab + g * n_ps + ps, b * nt + t, 0))

    def prev(which):
        return pl.BlockSpec((None, halo, LANES),
                            lambda b, ps, t: (which * n_slab + g * n_ps + ps, b * nhb + jnp.maximum(t * hb - 1, 0), 0))

    def nxt(which):
        return pl.BlockSpec((None, halo, LANES),
                            lambda b, ps, t: (which * n_slab + g * n_ps + ps, b * nhb + jnp.minimum((t + 1) * hb, nhb - 1), 0))

    out_spec = pl.BlockSpec((PAIRS_PER_SLAB, span, LANES), lambda b, ps, t: (ps, b * nt + t, 0))
    return pl.pallas_call(
        functools.partial(_attn_kernel, tq=tq, dil=dil, sub_len=sub_len),
        grid=(batch, n_ps, nt),
        in_specs=[cur(0), prev(1), cur(1), nxt(1), prev(2), cur(2), nxt(2),
                  pl.BlockSpec((PAIRS_PER_SLAB, 2, 2 * WIN, 4 * WIN), lambda b, ps, t: (ps, 0, 0, 0))],
        out_specs=[out_spec, out_spec],
        out_shape=[jax.ShapeDtypeStruct((n_hp, batch * seq, LANES), F32)] * 2,
        scratch_shapes=[pltpu.VMEM((span + 2 * halo, LANES), jnp.uint32)] * 2,
        compiler_params=_cparams(("parallel", "parallel", "arbitrary")),
        name=f"attn_g{g}",
    )(qkv, qkv, qkv, qkv, qkv, qkv, qkv, bm)


def _shortconv_kernel(p_ref, c_ref, n_ref, w_ref, b_ref, o_ref, x_buf, *, rows, halo):
    t = pl.program_id(1)
    nt = pl.num_programs(1)
    has_prev = (t > 0).astype(F32)
    has_next = (t < nt - 1).astype(F32)
    n_slab = c_ref.shape[2] // LANES
    for e in range(2):
        x = c_ref[e].astype(F32)
        lo = p_ref[e].astype(F32)[halo - 8:halo, :] * has_prev
        hi = n_ref[e].astype(F32)[0:8, :] * has_next
        for s in range(n_slab):
            cs = slice(s * LANES, (s + 1) * LANES)
            x_buf[e, s, 0:8, :] = lo[:, cs]
            x_buf[e, s, 8:rows + 8, :] = x[:, cs]
            x_buf[e, s, rows + 8:rows + 16, :] = hi[:, cs]
    nblk = rows // DFT_N2
    pad = jnp.zeros((PITCH_Z - DFT_N2, LANES), jnp.uint32)
    for s in range(n_slab):
        cs = slice(s * LANES, (s + 1) * LANES)
        w0, w1, w2, bias = w_ref[0:1, cs], w_ref[1:2, cs], w_ref[2:3, cs], b_ref[:, cs]
        res = []
        for e in range(2):
            xs = x_buf.at[e, s]
            res.append(xs[pl.ds(7, rows, stride=1), :] * w0 + xs[pl.ds(8, rows, stride=1), :] * w1
                       + xs[pl.ds(9, rows, stride=1), :] * w2 + bias)
        packed = _pack_bf16_pair(res[0], res[1])
        for jb in range(nblk):
            o_ref[s, jb * PITCH_Z:jb * PITCH_Z + DFT_N2, :] = packed[jb * DFT_N2:(jb + 1) * DFT_N2]
            o_ref[s, jb * PITCH_Z + DFT_N2:(jb + 1) * PITCH_Z, :] = pad


def _store_padded_slabs(o_ref, val, nblk):
    pad = jnp.zeros((PITCH_Z - DFT_N2, LANES), val.dtype)
    for s in range(val.shape[1] // LANES):
        for jb in range(nblk):
            o_ref[s, jb * PITCH_Z:jb * PITCH_Z + DFT_N2, :] = val[jb * DFT_N2:(jb + 1) * DFT_N2, s * LANES:(s + 1) * LANES]
            o_ref[s, jb * PITCH_Z + DFT_N2:(jb + 1) * PITCH_Z, :] = pad


def _load_padded_slabs(ref, nblk):
    return jnp.concatenate(
        [jnp.concatenate([ref[s, jb * PITCH_Z:jb * PITCH_Z + DFT_N2, :] for jb in range(nblk)], axis=0)
         for s in range(ref.shape[0])], axis=1)


def _shortconv(proj, conv_w, conv_b, batch, seq, rows=512):
    assert batch % 2 == 0
    rows = min(rows, seq)
    prow = rows // DFT_N2 * PITCH_Z
    n_slab = HYENA_WIDTH // LANES
    halo = 16
    pv = proj.reshape(batch, seq, REST_WIDTH)
    cb = R_U // HYENA_WIDTH
    hb = rows // halo
    nhb = seq // halo
    return pl.pallas_call(
        functools.partial(_shortconv_kernel, rows=rows, halo=halo),
        grid=(batch // 2, seq // rows, 3),
        in_specs=[
            pl.BlockSpec((2, halo, HYENA_WIDTH), lambda b, t, j: (b, jnp.maximum(t * hb - 1, 0), cb + j)),
            pl.BlockSpec((2, rows, HYENA_WIDTH), lambda b, t, j: (b, t, cb + j)),
            pl.BlockSpec((2, halo, HYENA_WIDTH), lambda b, t, j: (b, jnp.minimum((t + 1) * hb, nhb - 1), cb + j)),
            pl.BlockSpec((3, HYENA_WIDTH), lambda b, t, j: (0, j)),
            pl.BlockSpec((1, HYENA_WIDTH), lambda b, t, j: (0, j)),
        ],
        out_specs=pl.BlockSpec((None, n_slab, None, prow, LANES), lambda b, t, j: (j, 0, b, t, 0)),
        out_shape=jax.ShapeDtypeStruct((3, n_slab, batch // 2, seq // DFT_N2 * PITCH_Z, LANES), jnp.uint32),
        scratch_shapes=[pltpu.VMEM((2, n_slab, rows + 16, LANES), F32)],
        compiler_params=_cparams(("parallel", "parallel", "arbitrary")),
        name="shortconv",
    )(pv, pv, pv, conv_w, conv_b.reshape(1, -1))


def _filter_features(length):
    t = np.linspace(0.0, 1.0, length)[:, None]
    ang = (2.0 * math.pi / length) * np.arange(length, dtype=np.float64)[:, None]
    bands = np.linspace(1e-4, N_BANDS - 1, N_BANDS)[None]
    z = np.concatenate([t, np.cos(ang * bands), -np.sin(ang * bands)], axis=-1)
    zp = np.zeros((length, LANES), np.float32)
    zp[:, :FILTER_EMB] = z
    return zp


def _decay_rates():
    max_decay = math.log(1e-2) / 0.3
    min_decay = math.log(1e-2) / 1.5
    return np.abs(np.linspace(min_decay, max_decay, HYENA_WIDTH)).astype(np.float32)[None]


def _filter_kernel(z_ref, w1_ref, b1_ref, w2_ref, b2_ref, w3_ref, b3_ref, w4_ref, fr_ref, dl_ref, o_ref, *, rows):
    hi = lax.Precision.HIGHEST
    z = z_ref[...]
    fr = fr_ref[...]
    hr = rows // 2
    pre = jnp.concatenate([jnp.dot(z[:hr], w1_ref[...], precision=hi, preferred_element_type=F32),
                           jnp.dot(z[hr:], w1_ref[...], precision=hi, preferred_element_type=F32)], axis=1)
    h = jnp.sin(fr * (pre + b1_ref[...]))
    h = jnp.sin(fr * (jnp.dot(h, w2_ref[...], precision=hi, preferred_element_type=F32) + b2_ref[...]))
    h = jnp.sin(fr * (jnp.dot(h, w3_ref[...], precision=hi, preferred_element_type=F32) + b3_ref[...]))
    h = jnp.concatenate([h[:, :FILTER_HIDDEN], h[:, FILTER_HIDDEN:]], axis=0)
    hh, hl = _split(h)
    filt = jnp.dot(jnp.concatenate([hh, hl, hh], axis=1), w4_ref[...], preferred_element_type=F32)
    decay = jnp.exp(-z[:, 0:1] * dl_ref[...])
    row = pl.program_id(0) * rows + lax.broadcasted_iota(jnp.int32, (rows, 1), 0)
    for j in range(4):
        cs = slice(j * HYENA_WIDTH, (j + 1) * HYENA_WIDTH)
        val = filt[:, cs] * decay
        if j % 2 == 1:
            val = jnp.where(row == 0, 0.0, val)
        _store_padded_slabs(o_ref.at[j], val, rows // DFT_N2)


def _filters(length, fw1, fb1, fw2, fb2, fw3, fb3, fw4, ffreq, rows=256):
    rows = min(rows, length)
    n_slab = HYENA_WIDTH // LANES
    prow = rows // DFT_N2 * PITCH_Z
    w4h, w4l = _split(fw4.astype(F32))
    w4s = jnp.concatenate([w4h, w4h, w4l], axis=0)
    zfeat = jnp.asarray(_filter_features(length))
    w1p = jnp.zeros((LANES, FILTER_HIDDEN), F32).at[:FILTER_EMB].set(fw1.astype(F32))
    twice = lambda v: jnp.tile(v.astype(F32).reshape(1, -1), (1, 2))
    zero = jnp.zeros((FILTER_HIDDEN, FILTER_HIDDEN), F32)
    bdiag = lambda w: jnp.block([[w.astype(F32), zero], [zero, w.astype(F32)]])
    full = lambda shape: pl.BlockSpec(shape, lambda i: (0,) * len(shape))
    return pl.pallas_call(
        functools.partial(_filter_kernel, rows=rows),
        grid=(length // rows,),
        in_specs=[pl.BlockSpec((rows, LANES), lambda i: (i, 0)),
                  full((LANES, FILTER_HIDDEN)), full((1, LANES)),
                  full((LANES, LANES)), full((1, LANES)),
                  full((LANES, LANES)), full((1, LANES)),
                  full((3 * FILTER_HIDDEN, 4 * HYENA_WIDTH)), full((1, LANES)),
                  full((1, HYENA_WIDTH))],
        out_specs=pl.BlockSpec((4, n_slab, prow, LANES), lambda i: (0, 0, i, 0)),
        out_shape=jax.ShapeDtypeStruct((4, n_slab, length // DFT_N2 * PITCH_Z, LANES), F32),
        compiler_params=_cparams(("parallel",)),
        name="hyena_filters",
    )(zfeat, w1p, twice(fb1), bdiag(fw2), twice(fb2), bdiag(fw3), twice(fb3), w4s,
      twice(ffreq), jnp.asarray(_decay_rates()))


def _dft_tables(length):
    n = 2 * length
    n1 = n // DFT_N2
    h1 = n1 // 2
    kk = np.arange(h1)[:, None] + 0.5
    th = 2.0 * math.pi * kk * np.arange(h1)[None, :] / n1
    fa = np.concatenate([np.cos(th), -np.sin(th)], axis=0)
    ga = (2.0 / n) * fa.T
    k = np.arange(h1)[:, None, None] + n1 * np.arange(DFT_N2)[None, :, None] + 0.5
    ph = 2.0 * math.pi * k * np.arange(DFT_N2)[None, None, :] / n
    f = lambda a: jnp.asarray(a.astype(np.float32))
    er, ei = f(np.cos(ph)), f(-np.sin(ph))
    m = jnp.concatenate([jnp.concatenate([er, -ei], axis=2), jnp.concatenate([ei, er], axis=2)], axis=1)
    return f(fa), f(ga), m.astype(BF16)


def _split(a):
    hi = a.astype(BF16)
    lo = (a - hi.astype(F32)).astype(BF16)
    return hi, lo


CONV_K1_CHUNK = 64
CONV_PAIR_UNROLL = 8
CONV_FREQ_UNROLL = 8


def _resident_table_spec(h1):
    return pl.BlockSpec((h1, 2 * DFT_N2, 2 * DFT_N2), lambda *_: (0, 0, 0), pipeline_mode=pl.Buffered(1))


def _spectrum_kernel(fa_ref, m_ref, xf_ref, xb_ref, o_ref, s_ref, *, h1, kc):
    hf = pl.program_id(2)
    half = DFT_N2

    @pl.when(hf == 0)
    def _():
        fa = fa_ref[...]

        def body(n2, carry):
            x = jnp.concatenate([xf_ref[pl.ds(n2, h1, stride=PITCH_Z), :],
                                 xb_ref[pl.ds(n2, h1, stride=PITCH_Z), :]], axis=1)
            xh, xl = _split(x)
            a = jnp.dot(fa, jnp.concatenate([xh, xh, xl], axis=0), preferred_element_type=F32)
            aw = _pack_bf16_pair(a[:, :LANES], a[:, LANES:])
            for ri in range(2):
                s_ref[pl.ds(ri * half + n2, h1, stride=PITCH_S), :] = aw[ri * h1:(ri + 1) * h1]
            return carry

        lax.fori_loop(0, DFT_N2, body, 0, unroll=CONV_PAIR_UNROLL)

    def freq_body(k1, carry):
        row = pl.multiple_of((hf * kc + k1) * PITCH_S, 8)
        xw = s_ref[pl.ds(row, 2 * half), :]
        x = jnp.concatenate([_unpack_bf16_pair(xw, 0), _unpack_bf16_pair(xw, 1)], axis=1)
        z = jnp.dot(m_ref[hf * kc + k1], x, preferred_element_type=F32)
        o_ref[k1] = _pack_bf16_pair(z[:half, :LANES] + z[:half, LANES:], z[half:, :LANES] - z[half:, LANES:])
        return carry

    lax.fori_loop(0, kc, freq_body, 0, unroll=CONV_FREQ_UNROLL)


def _filter_spectrum(fa, m_bf16, filt):
    _, n_slab, prow, _ = filt.shape
    n1, h1 = fa.shape
    kc = min(CONV_K1_CHUNK, h1)
    fh, fl = _split(fa)
    fa3 = jnp.concatenate([fh, fl, fh], axis=1)
    return pl.pallas_call(
        functools.partial(_spectrum_kernel, h1=h1, kc=kc),
        grid=(2, n_slab, h1 // kc),
        in_specs=[pl.BlockSpec((n1, 3 * h1), lambda o, s, hf: (0, 0)),
                  _resident_table_spec(h1),
                  pl.BlockSpec((None, None, prow, LANES), lambda o, s, hf: (2 * o, s, 0, 0)),
                  pl.BlockSpec((None, None, prow, LANES), lambda o, s, hf: (2 * o + 1, s, 0, 0))],
        out_specs=pl.BlockSpec((None, None, kc, DFT_N2, LANES), lambda o, s, hf: (o, s, hf, 0, 0)),
        out_shape=jax.ShapeDtypeStruct((2, n_slab, h1, DFT_N2, LANES), jnp.uint32),
        scratch_shapes=[pltpu.VMEM((h1 * PITCH_S, LANES), jnp.uint32)],
        compiler_params=pltpu.CompilerParams(dimension_semantics=("parallel", "parallel", "arbitrary"),
                                             vmem_limit_bytes=CONV_VMEM_LIMIT),
        name="filter_spectrum",
    )(fa3, m_bf16, filt, filt)


def _conv_kernel(fa_ref, ga_ref, m_ref, k_ref, z_ref, hx_ref, sk_ref, o_ref, s_ref, *, h1):
    half = DFT_N2
    pairs = DFT_N2 // 2
    fa = fa_ref[...]
    ga = ga_ref[...]
    sk = sk_ref[...]

    def stage_a(j, carry):
        n2 = 2 * j
        zw = [z_ref[pl.ds(n2 + u, h1, stride=PITCH_Z), :] for u in range(2)]
        a = [jnp.dot(fa, jnp.concatenate([_unpack_bf16_pair(zw[0], e), _unpack_bf16_pair(zw[1], e)], axis=1),
                     preferred_element_type=F32) for e in range(2)]
        aw = _pack_bf16_pair(a[0], a[1])
        for ri in range(2):
            for u in range(2):
                s_ref[pl.ds(ri * half + n2 + u, h1, stride=PITCH_S), :] = (
                    aw[ri * h1:(ri + 1) * h1, u * LANES:(u + 1) * LANES])
        return carry

    lax.fori_loop(0, pairs, stage_a, 0, unroll=CONV_PAIR_UNROLL)

    def freq_body(k1, carry):
        row = pl.multiple_of(k1 * PITCH_S, 8)
        xw = s_ref[pl.ds(row, 2 * half), :]
        x = jnp.concatenate([_unpack_bf16_pair(xw, 0), _unpack_bf16_pair(xw, 1)], axis=1)
        m = m_ref[k1]
        z = jnp.dot(m, x, preferred_element_type=F32)
        zr, zi = z[:half], z[half:]
        kw = k_ref[k1]
        kr = jnp.concatenate([_unpack_bf16_pair(kw, 0, F32)] * 2, axis=1)
        ki = jnp.concatenate([_unpack_bf16_pair(kw, 1, F32)] * 2, axis=1)
        y = jnp.concatenate([zr * kr - zi * ki, zr * ki + zi * kr], axis=0).astype(BF16)
        bq = lax.dot_general(m, y, (((0,), (0,)), ((), ())), preferred_element_type=F32)
        s_ref[pl.ds(row, 2 * half), :] = _pack_bf16_pair(bq[:, :LANES], bq[:, LANES:])
        return carry

    lax.fori_loop(0, h1, freq_body, 0, unroll=CONV_FREQ_UNROLL)

    def stage_inv(j, carry):
        n2 = 2 * j
        rw = [s_ref[pl.ds(n2 + u, h1, stride=PITCH_S), :] for u in range(2)]
        iw = [s_ref[pl.ds(half + n2 + u, h1, stride=PITCH_S), :] for u in range(2)]
        zw = [z_ref[pl.ds(n2 + u, h1, stride=PITCH_Z), :] for u in range(2)]
        hw = [hx_ref[pl.ds(n2 + u, h1, stride=PITCH_Z), :] for u in range(2)]
        y = []
        for e in range(2):
            bn = jnp.concatenate(
                [jnp.concatenate([_unpack_bf16_pair(rw[u], e), _unpack_bf16_pair(iw[u], e)], axis=0)
                 for u in range(2)], axis=1)
            y.append(jnp.dot(ga, bn, preferred_element_type=F32))
        for u in range(2):
            res = [_unpack_bf16_pair(hw[u], e, F32) * (y[e][:, u * LANES:(u + 1) * LANES]
                                                       + sk * _unpack_bf16_pair(zw[u], e, F32)) for e in range(2)]
            o_ref[pl.ds(n2 + u, h1, stride=PITCH_Z), :] = _pack_bf16_pair(res[0], res[1])
        return carry

    lax.fori_loop(0, pairs, stage_inv, 0, unroll=CONV_PAIR_UNROLL // 2)
    pad = jnp.zeros((PITCH_Z - DFT_N2, LANES), jnp.uint32)
    for blk in range(h1):
        o_ref[blk * PITCH_Z + DFT_N2:(blk + 1) * PITCH_Z, :] = pad


def _long_conv_gate(tabs, kspec, order, z5, zsel, hx5, hxsel, skip):
    fa, ga, m_tab = tabs
    _, n_slab, n_pair, prow, _ = z5.shape
    n1, h1 = fa.shape
    slab = lambda sel: pl.BlockSpec((None, None, None, prow, LANES), lambda s, b: (sel, s, b, 0, 0))
    return pl.pallas_call(
        functools.partial(_conv_kernel, h1=h1),
        grid=(n_slab, n_pair),
        in_specs=[pl.BlockSpec((n1, h1), lambda s, b: (0, 0)),
                  pl.BlockSpec((h1, n1), lambda s, b: (0, 0)),
                  _resident_table_spec(h1),
                  pl.BlockSpec((None, None, h1, DFT_N2, LANES), lambda s, b: (order, s, 0, 0, 0)),
                  slab(zsel), slab(hxsel),
                  pl.BlockSpec((None, None, 1, LANES), lambda s, b: (order, s, 0, 0))],
        out_specs=pl.BlockSpec((None, None, prow, LANES), lambda s, b: (s, b, 0, 0)),
        out_shape=jax.ShapeDtypeStruct((n_slab, n_pair, prow, LANES), jnp.uint32),
        scratch_shapes=[pltpu.VMEM((h1 * PITCH_S, LANES), jnp.uint32)],
        compiler_params=pltpu.CompilerParams(dimension_semantics=("parallel", "arbitrary"),
                                             vmem_limit_bytes=CONV_VMEM_LIMIT),
        name="hyena_long_conv",
    )(fa.astype(BF16), ga.astype(BF16), m_tab, kspec, z5, hx5, skip)


def _hyena(proj, batch, seq, conv_w, conv_b, fparams, skip):
    c = HYENA_WIDTH
    fa, ga, m_tab = _dft_tables(seq)
    kspec = _filter_spectrum(fa, m_tab, _filters(seq, *fparams))
    u5 = _shortconv(proj, conv_w, conv_b, batch, seq)
    skip4 = skip.astype(F32).reshape(2, c // LANES, 1, LANES)
    tabs = (fa, ga, m_tab)
    z1 = _long_conv_gate(tabs, kspec, 0, u5, 0, u5, 1, skip4)
    return _long_conv_gate(tabs, kspec, 1, z1[None], 0, u5, 2, skip4)


def _final_kernel(o1_ref, o2_ref, o3_ref, l1_ref, l2_ref, l3_ref, ga_ref, hy_ref, gh_ref, ma_ref, mh_ref,
                  x_ref, wa_ref, wh_ref, wo_ref, pg_ref, y_ref, *, tiles_per_seq):
    def cat(ref):
        return jnp.concatenate([ref[s] for s in range(ATTN_WIDTH // LANES)], axis=1)

    l1, l2, l3 = cat(l1_ref), cat(l2_ref), cat(l3_ref)
    mx = jnp.maximum(jnp.maximum(l1, l2), l3)
    e1, e2, e3 = jnp.exp(l1 - mx), jnp.exp(l2 - mx), jnp.exp(l3 - mx)
    attn = (e1 * cat(o1_ref) + e2 * cat(o2_ref) + e3 * cat(o3_ref)) / (e1 + e2 + e3)
    ga = ga_ref[...].astype(F32)
    a_in = (attn * (ga * jax.nn.sigmoid(ga))).astype(BF16)
    a_br = jnp.dot(a_in, wa_ref[...], preferred_element_type=F32)
    gh = gh_ref[...].astype(F32)
    odd = (pl.program_id(0) // tiles_per_seq) % 2
    hw = _load_padded_slabs(hy_ref, gh.shape[0] // DFT_N2)
    hy = lax.bitcast_convert_type((hw << ((1 - odd) * 16).astype(jnp.uint32)) & jnp.uint32(0xFFFF0000), F32)
    h_in = (hy * (gh * jax.nn.sigmoid(gh))).astype(BF16)
    h_br = jnp.dot(h_in, wh_ref[...], preferred_element_type=F32)
    merged = jax.nn.sigmoid(ma_ref[...].astype(F32)) * a_br + jax.nn.sigmoid(mh_ref[...].astype(F32)) * h_br
    out = jnp.dot(merged.astype(BF16), wo_ref[...], preferred_element_type=F32)
    ms = jnp.mean(out * out, axis=-1, keepdims=True)
    y_ref[...] = x_ref[...] + out * lax.rsqrt(ms + EPS) * pg_ref[...]


def _final(os_, ls_, proj, hy, x2d, wa, wh, wo, pg, seq, tm=256):
    rows = x2d.shape[0]
    tm = min(tm, rows)
    tiles_per_seq = seq // tm
    hy_spec = pl.BlockSpec((HYENA_WIDTH // LANES, None, tm // DFT_N2 * PITCH_Z, LANES),
                           lambda i: (0, i // (2 * tiles_per_seq), i % tiles_per_seq, 0))
    slab = pl.BlockSpec((ATTN_WIDTH // LANES, tm, LANES), lambda i: (0, i, 0))
    a512 = lambda off: pl.BlockSpec((tm, ATTN_WIDTH), lambda i: (i, off))
    a1024 = lambda off: pl.BlockSpec((tm, D_MODEL), lambda i: (i, off))
    full = lambda shape: pl.BlockSpec(shape, lambda i: (0, 0))
    return pl.pallas_call(
        functools.partial(_final_kernel, tiles_per_seq=tiles_per_seq),
        grid=(rows // tm,),
        in_specs=[slab] * 6 + [a512(R_GA // ATTN_WIDTH), hy_spec, a1024(R_GH // D_MODEL),
                               a1024(R_MG // D_MODEL), a1024(R_MG // D_MODEL + 1), a1024(0),
                               full((ATTN_WIDTH, D_MODEL)), full((D_MODEL, D_MODEL)), full((D_MODEL, D_MODEL)),
                               full((1, D_MODEL))],
        out_specs=pl.BlockSpec((tm, D_MODEL), lambda i: (i, 0)),
        out_shape=jax.ShapeDtypeStruct((rows, D_MODEL), F32),
        compiler_params=_cparams(("parallel",)),
        name="merge_out",
    )(*os_, *ls_, proj, hy, proj, proj, proj, x2d, wa, wh, wo, pg)


def _layer(x, rel_bias, pre_g, post_g, w_in, conv_w, conv_b, fparams, skip, w_br_a, w_br_h, w_out):
    batch, seq, _ = x.shape
    x2d = x.reshape(batch * seq, D_MODEL)
    w_qkv, w_rest = w_in
    gain = pre_g.reshape(1, -1).astype(F32)
    qkv = _inproj(x2d, gain, w_qkv, tn=QKV_WIDTH // 3, slab_out=True)
    proj = _inproj(x2d, gain, w_rest, tn=REST_WIDTH // 2, slab_out=False)
    os_, ls_ = [], []
    for g in range(N_GROUPS):
        o, l = _attention_group(qkv, rel_bias, g, batch, seq)
        os_.append(o)
        ls_.append(l)
    hy = _hyena(proj, batch, seq, conv_w, conv_b, fparams, skip)
    y = _final(os_, ls_, proj, hy, x2d, w_br_a, w_br_h, w_out, post_g.reshape(1, -1).astype(F32), seq)
    return y.reshape(batch, seq, D_MODEL)


def kernel(x_prompt, x_sample, rel_bias, pre_norm_g, post_norm_g, w_in, conv_w, conv_b, filt_w1, filt_b1, filt_w2, filt_b2, filt_w3, filt_b3, filt_w4, filt_freq, hyena_skip, w_branch_a, w_branch_h, w_out):
    depth = w_in.shape[0]

    def run(x):
        for l in range(depth):
            fparams = (filt_w1[l], filt_b1[l], filt_w2[l], filt_b2[l], filt_w3[l], filt_b3[l], filt_w4[l], filt_freq[l])
            w = w_in[l]
            col_scale = jnp.where(jnp.arange(QKV_WIDTH) < QKV_WIDTH // 3, HEAD_DIM ** -0.5, 1.0).astype(F32)
            w_qkv = (w[:, :QKV_WIDTH] * col_scale).astype(BF16)
            w_rest = jnp.concatenate([w[:, U_OFF:], w[:, GA_OFF:U_OFF]], axis=1).astype(BF16)
            x = _layer(x, rel_bias, pre_norm_g[l], post_norm_g[l], (w_qkv, w_rest), conv_w[l], conv_b[l],
                       fparams, hyena_skip[l], w_branch_a[l].astype(BF16), w_branch_h[l].astype(BF16),
                       w_out[l].astype(BF16))
        return x

    return (run(x_prompt), run(x_sample))
```

```python
import functools
import math

import numpy as np
import jax
import jax.numpy as jnp
from jax import lax
from jax.experimental import pallas as pl
from jax.experimental.pallas import tpu as pltpu

F32 = jnp.float32
BF16 = jnp.bfloat16

D_MODEL = 1024
EPS = 1e-6
HEAD_DIM = 64
ATTN_GROUPS = ((128, 1), (512, 4), (2048, 16))
N_GROUPS = 3
HEADS_PER_GROUP = 8
ATTN_WIDTH = HEADS_PER_GROUP * HEAD_DIM
HYENA_WIDTH = 1024
FILTER_EMB = 33
N_BANDS = 16
FILTER_HIDDEN = 64
NUM_BUCKETS = 32
MAX_DISTANCE = 1024
NEG_INF = -1e30
QKV_WIDTH = 4608
GA_OFF, U_OFF, GH_OFF, MG_OFF = 4608, 5120, 8192, 9216
REST_WIDTH = 6656
R_U, R_GH, R_MG, R_GA = 0, 3072, 4096, 6144

WIN = 64
LANES = 128
DFT_N2 = 128
PITCH_Z = 136
PITCH_S = 264
VMEM_LIMIT = 48 * 1024 * 1024
CONV_VMEM_LIMIT = 58 * 1024 * 1024


def _cparams(sem):
    return pltpu.CompilerParams(dimension_semantics=sem, vmem_limit_bytes=VMEM_LIMIT)


def _inproj_kernel(x_ref, g_ref, w_ref, o_ref, hn_ref, *, slabs):
    @pl.when(pl.program_id(1) == 0)
    def _():
        x = x_ref[...]
        ms = jnp.mean(x * x, axis=-1, keepdims=True)
        hn_ref[...] = (x * lax.rsqrt(ms + EPS) * g_ref[...]).astype(BF16)

    acc = jnp.dot(hn_ref[...], w_ref[...], preferred_element_type=F32)
    if slabs:
        for s in range(slabs):
            o_ref[s] = _pack_bf16_pair(acc[:, (2 * s) * LANES:(2 * s + 1) * LANES],
                                       acc[:, (2 * s + 1) * LANES:(2 * s + 2) * LANES])
    else:
        o_ref[...] = acc.astype(o_ref.dtype)


def _pack_bf16_pair(a, b):
    ua = lax.bitcast_convert_type(a.astype(BF16).astype(F32), jnp.uint32)
    ub = lax.bitcast_convert_type(b.astype(BF16).astype(F32), jnp.uint32)
    return (ua >> 16) | (ub & jnp.uint32(0xFFFF0000))


def _unpack_bf16_pair(w, idx, dtype=BF16):
    bits = (w << 16) if idx == 0 else (w & jnp.uint32(0xFFFF0000))
    return lax.bitcast_convert_type(bits, F32).astype(dtype)


def _inproj(x2d, g, w_bf16, tn, slab_out, tm=1024):
    rows = x2d.shape[0]
    width = w_bf16.shape[1]
    tm = min(tm, rows)
    if slab_out:
        ns = tn // (2 * LANES)
        out_spec = pl.BlockSpec((ns, tm, LANES), lambda i, j: (j, i, 0))
        out_shape = jax.ShapeDtypeStruct((width // (2 * LANES), rows, LANES), jnp.uint32)
    else:
        ns = 0
        out_spec = pl.BlockSpec((tm, tn), lambda i, j: (i, j))
        out_shape = jax.ShapeDtypeStruct((rows, width), BF16)
    return pl.pallas_call(
        functools.partial(_inproj_kernel, slabs=ns),
        grid=(rows // tm, width // tn),
        in_specs=[
            pl.BlockSpec((tm, D_MODEL), lambda i, j: (i, 0)),
            pl.BlockSpec((1, D_MODEL), lambda i, j: (0, 0)),
            pl.BlockSpec((D_MODEL, tn), lambda i, j: (0, j)),
        ],
        out_specs=out_spec,
        out_shape=out_shape,
        scratch_shapes=[pltpu.VMEM((tm, D_MODEL), BF16)],
        compiler_params=_cparams(("parallel", "arbitrary")),
        name="inproj_qkv" if slab_out else "inproj_rest",
    )(x2d, g, w_bf16)


def _t5_bucket_np(rel):
    half = NUM_BUCKETS // 2
    max_exact = half // 2
    n = np.abs(rel)
    nf = np.maximum(n, 1).astype(np.float64)
    large = max_exact + (np.log(nf / max_exact) / math.log(MAX_DISTANCE / max_exact) * (half - max_exact)).astype(np.int64)
    large = np.minimum(large, half - 1)
    return np.where(rel > 0, half, 0) + np.where(n < max_exact, n, large)


def _bias_mask(rel_bias_g, dilation):
    qi = np.arange(2 * WIN)[:, None]
    kj = np.arange(4 * WIN)[None, :] - WIN
    delta = kj - qi
    bucket = _t5_bucket_np(delta * dilation).astype(np.int32)
    onehot = jax.nn.one_hot(jnp.asarray(bucket), NUM_BUCKETS, dtype=F32)
    bias = jnp.einsum('qkb,bh->hqk', onehot, rel_bias_g.astype(F32), precision=lax.Precision.HIGHEST)
    bias = jnp.where(jnp.asarray(np.abs(delta) <= WIN)[None], bias, NEG_INF)
    return bias.reshape(HEADS_PER_GROUP // 2, 2, 2 * WIN, 4 * WIN)


ATTN_POS_PER_STEP = 4096
ATTN_TILES_PER_ITER = 2
PAIRS_PER_SLAB = 2


def _attn_kernel(q_ref, kp_ref, kc_ref, kn_ref, vp_ref, vc_ref, vn_ref, bm_ref, o_ref, l_ref,
                 kbuf, vbuf, *, tq, dil, sub_len):
    t = pl.program_id(2)
    halo = WIN * dil
    span = tq * dil
    kbuf[0:halo] = kp_ref[...]
    kbuf[halo:halo + span] = kc_ref[...]
    kbuf[halo + span:] = kn_ref[...]
    vbuf[0:halo] = vp_ref[...]
    vbuf[halo:halo + span] = vc_ref[...]
    vbuf[halo + span:] = vn_ref[...]
    lane = lax.broadcasted_iota(jnp.int32, (1, LANES), 1)
    is_lo = lane < HEAD_DIM
    kcol = lax.broadcasted_iota(jnp.int32, (1, 4 * WIN), 1)
    qt = 2 * WIN
    nq = tq // qt

    def tile(c, i):
        row0 = i * (qt * dil) + c
        kpos = t * tq + i * qt - WIN + kcol
        pen = jnp.where((kpos >= 0) & (kpos < sub_len), 0.0, NEG_INF).astype(F32)
        qw = q_ref[pl.ds(row0, qt, stride=dil), :]
        kw = kbuf[pl.ds(row0, 2 * qt, stride=dil), :]
        vw = vbuf[pl.ds(row0, 2 * qt, stride=dil), :]
        o_pair = []
        for hp in range(PAIRS_PER_SLAB):
            q = _unpack_bf16_pair(qw, hp)
            k = _unpack_bf16_pair(kw, hp)
            v = _unpack_bf16_pair(vw, hp)
            outs, lses = [], []
            for hh in range(2):
                sel = is_lo if hh == 0 else jnp.logical_not(is_lo)
                qm = jnp.where(sel, q, jnp.zeros_like(q))
                s = lax.dot_general(qm, k, (((1,), (1,)), ((), ())), preferred_element_type=F32)
                s = s + bm_ref[hp, hh] + pen
                m = jnp.max(s, axis=-1, keepdims=True)
                p = jnp.exp(s - m)
                den = jnp.sum(p, axis=-1, keepdims=True)
                pv = jnp.dot(p.astype(BF16), v, preferred_element_type=F32)
                outs.append(pv / den)
                lses.append(m + jnp.log(den))
            o_pair.append(jnp.where(is_lo, outs[0], outs[1]))
            l_ref[hp, pl.ds(row0, qt, stride=dil), :] = jnp.where(is_lo, lses[0], lses[1])
        o_ref[pl.ds(row0, qt, stride=dil), :] = _pack_bf16_pair(o_pair[0], o_pair[1])

    def body(it, carry):
        for u in range(ATTN_TILES_PER_ITER):
            idx = it * ATTN_TILES_PER_ITER + u
            tile(idx // nq, idx % nq)
        return carry

    lax.fori_loop(0, dil * nq // ATTN_TILES_PER_ITER, body, 0)


def _attention_group(qkv, rel_bias, g, batch, seq):
    _, dil = ATTN_GROUPS[g]
    sub_len = seq // dil
    tq = min(ATTN_POS_PER_STEP // dil, sub_len)
    nt = sub_len // tq
    span = tq * dil
    halo = WIN * dil
    n_hp = ATTN_WIDTH // LANES
    n_ps = n_hp // PAIRS_PER_SLAB
    n_slab = N_GROUPS * n_ps
    bm = _bias_mask(rel_bias[:, g * HEADS_PER_GROUP:(g + 1) * HEADS_PER_GROUP], dil)
    hb = tq // WIN
    nhb = sub_len // WIN

    def cur(which):
        return pl.BlockSpec((None, span, LANES), lambda b, ps, t: (which * n_slab + g * n_ps + ps, b * nt + t, 0))

    def prev(which):
        return pl.BlockSpec((None, halo, LANES),
                            lambda b, ps, t: (which * n_slab + g * n_ps + ps, b * nhb + jnp.maximum(t * hb - 1, 0), 0))

    def nxt(which):
        return pl.BlockSpec((None, halo, LANES),
                            lambda b, ps, t: (which * n_slab + g * n_ps + ps, b * nhb + jnp.minimum((t + 1) * hb, nhb - 1), 0))

    out_spec = pl.BlockSpec((PAIRS_PER_SLAB, span, LANES), lambda b, ps, t: (ps, b * nt + t, 0))
    return pl.pallas_call(
        functools.partial(_attn_kernel, tq=tq, dil=dil, sub_len=sub_len),
        grid=(batch, n_ps, nt),
        in_specs=[cur(0), prev(1), cur(1), nxt(1), prev(2), cur(2), nxt(2),
                  pl.BlockSpec((PAIRS_PER_SLAB, 2, 2 * WIN, 4 * WIN), lambda b, ps, t: (ps, 0, 0, 0))],
        out_specs=[pl.BlockSpec((None, span, LANES), lambda b, ps, t: (ps, b * nt + t, 0)), out_spec],
        out_shape=[jax.ShapeDtypeStruct((n_ps, batch * seq, LANES), jnp.uint32),
                   jax.ShapeDtypeStruct((n_hp, batch * seq, LANES), F32)],
        scratch_shapes=[pltpu.VMEM((span + 2 * halo, LANES), jnp.uint32)] * 2,
        compiler_params=_cparams(("parallel", "parallel", "arbitrary")),
        name=f"attn_g{g}",
    )(qkv, qkv, qkv, qkv, qkv, qkv, qkv, bm)


def _shortconv_kernel(p_ref, c_ref, n_ref, w_ref, b_ref, o_ref, x_buf, *, rows, halo):
    t = pl.program_id(1)
    nt = pl.num_programs(1)
    has_prev = (t > 0).astype(F32)
    has_next = (t < nt - 1).astype(F32)
    n_slab = c_ref.shape[2] // LANES
    for e in range(2):
        x = c_ref[e].astype(F32)
        lo = p_ref[e].astype(F32)[halo - 8:halo, :] * has_prev
        hi = n_ref[e].astype(F32)[0:8, :] * has_next
        for s in range(n_slab):
            cs = slice(s * LANES, (s + 1) * LANES)
            x_buf[e, s, 0:8, :] = lo[:, cs]
            x_buf[e, s, 8:rows + 8, :] = x[:, cs]
            x_buf[e, s, rows + 8:rows + 16, :] = hi[:, cs]
    nblk = rows // DFT_N2
    pad = jnp.zeros((PITCH_Z - DFT_N2, LANES), jnp.uint32)
    for s in range(n_slab):
        cs = slice(s * LANES, (s + 1) * LANES)
        w0, w1, w2, bias = w_ref[0:1, cs], w_ref[1:2, cs], w_ref[2:3, cs], b_ref[:, cs]
        res = []
        for e in range(2):
            xs = x_buf.at[e, s]
            res.append(xs[pl.ds(7, rows, stride=1), :] * w0 + xs[pl.ds(8, rows, stride=1), :] * w1
                       + xs[pl.ds(9, rows, stride=1), :] * w2 + bias)
        packed = _pack_bf16_pair(res[0], res[1])
        for jb in range(nblk):
            o_ref[s, jb * PITCH_Z:jb * PITCH_Z + DFT_N2, :] = packed[jb * DFT_N2:(jb + 1) * DFT_N2]
            o_ref[s, jb * PITCH_Z + DFT_N2:(jb + 1) * PITCH_Z, :] = pad


def _store_padded_slabs(o_ref, val, nblk):
    pad = jnp.zeros((PITCH_Z - DFT_N2, LANES), val.dtype)
    for s in range(val.shape[1] // LANES):
        for jb in range(nblk):
            o_ref[s, jb * PITCH_Z:jb * PITCH_Z + DFT_N2, :] = val[jb * DFT_N2:(jb + 1) * DFT_N2, s * LANES:(s + 1) * LANES]
            o_ref[s, jb * PITCH_Z + DFT_N2:(jb + 1) * PITCH_Z, :] = pad


def _load_padded_slabs(ref, nblk):
    return jnp.concatenate(
        [jnp.concatenate([ref[s, jb * PITCH_Z:jb * PITCH_Z + DFT_N2, :] for jb in range(nblk)], axis=0)
         for s in range(ref.shape[0])], axis=1)


def _shortconv(proj, conv_w, conv_b, batch, seq, rows=512):
    assert batch % 2 == 0
    rows = min(rows, seq)
    prow = rows // DFT_N2 * PITCH_Z
    n_slab = HYENA_WIDTH // LANES
    halo = 16
    pv = proj.reshape(batch, seq, REST_WIDTH)
    cb = R_U // HYENA_WIDTH
    hb = rows // halo
    nhb = seq // halo
    return pl.pallas_call(
        functools.partial(_shortconv_kernel, rows=rows, halo=halo),
        grid=(batch // 2, seq // rows, 3),
        in_specs=[
            pl.BlockSpec((2, halo, HYENA_WIDTH), lambda b, t, j: (b, jnp.maximum(t * hb - 1, 0), cb + j)),
            pl.BlockSpec((2, rows, HYENA_WIDTH), lambda b, t, j: (b, t, cb + j)),
            pl.BlockSpec((2, halo, HYENA_WIDTH), lambda b, t, j: (b, jnp.minimum((t + 1) * hb, nhb - 1), cb + j)),
            pl.BlockSpec((3, HYENA_WIDTH), lambda b, t, j: (0, j)),
            pl.BlockSpec((1, HYENA_WIDTH), lambda b, t, j: (0, j)),
        ],
        out_specs=pl.BlockSpec((None, n_slab, None, prow, LANES), lambda b, t, j: (j, 0, b, t, 0)),
        out_shape=jax.ShapeDtypeStruct((3, n_slab, batch // 2, seq // DFT_N2 * PITCH_Z, LANES), jnp.uint32),
        scratch_shapes=[pltpu.VMEM((2, n_slab, rows + 16, LANES), F32)],
        compiler_params=_cparams(("parallel", "parallel", "arbitrary")),
        name="shortconv",
    )(pv, pv, pv, conv_w, conv_b.reshape(1, -1))


def _filter_features(length):
    t = np.linspace(0.0, 1.0, length)[:, None]
    ang = (2.0 * math.pi / length) * np.arange(length, dtype=np.float64)[:, None]
    bands = np.linspace(1e-4, N_BANDS - 1, N_BANDS)[None]
    z = np.concatenate([t, np.cos(ang * bands), -np.sin(ang * bands)], axis=-1)
    zp = np.zeros((length, LANES), np.float32)
    zp[:, :FILTER_EMB] = z
    return zp


def _decay_rates():
    max_decay = math.log(1e-2) / 0.3
    min_decay = math.log(1e-2) / 1.5
    return np.abs(np.linspace(min_decay, max_decay, HYENA_WIDTH)).astype(np.float32)[None]


def _filter_kernel(z_ref, w1_ref, b1_ref, w2_ref, b2_ref, w3_ref, b3_ref, w4_ref, fr_ref, dl_ref, o_ref, *, rows):
    hi = lax.Precision.HIGHEST
    z = z_ref[...]
    fr = fr_ref[...]
    hr = rows // 2
    pre = jnp.concatenate([jnp.dot(z[:hr], w1_ref[...], precision=hi, preferred_element_type=F32),
                           jnp.dot(z[hr:], w1_ref[...], precision=hi, preferred_element_type=F32)], axis=1)
    h = jnp.sin(fr * (pre + b1_ref[...]))
    h = jnp.sin(fr * (jnp.dot(h, w2_ref[...], precision=hi, preferred_element_type=F32) + b2_ref[...]))
    h = jnp.sin(fr * (jnp.dot(h, w3_ref[...], precision=hi, preferred_element_type=F32) + b3_ref[...]))
    h = jnp.concatenate([h[:, :FILTER_HIDDEN], h[:, FILTER_HIDDEN:]], axis=0)
    hh, hl = _split(h)
    filt = jnp.dot(jnp.concatenate([hh, hl, hh], axis=1), w4_ref[...], preferred_element_type=F32)
    decay = jnp.exp(-z[:, 0:1] * dl_ref[...])
    row = pl.program_id(0) * rows + lax.broadcasted_iota(jnp.int32, (rows, 1), 0)
    for j in range(4):
        cs = slice(j * HYENA_WIDTH, (j + 1) * HYENA_WIDTH)
        val = filt[:, cs] * decay
        if j % 2 == 1:
            val = jnp.where(row == 0, 0.0, val)
        _store_padded_slabs(o_ref.at[j], val, rows // DFT_N2)


def _filters(length, fw1, fb1, fw2, fb2, fw3, fb3, fw4, ffreq, rows=256):
    rows = min(rows, length)
    n_slab = HYENA_WIDTH // LANES
    prow = rows // DFT_N2 * PITCH_Z
    w4h, w4l = _split(fw4.astype(F32))
    w4s = jnp.concatenate([w4h, w4h, w4l], axis=0)
    zfeat = jnp.asarray(_filter_features(length))
    w1p = jnp.zeros((LANES, FILTER_HIDDEN), F32).at[:FILTER_EMB].set(fw1.astype(F32))
    twice = lambda v: jnp.tile(v.astype(F32).reshape(1, -1), (1, 2))
    zero = jnp.zeros((FILTER_HIDDEN, FILTER_HIDDEN), F32)
    bdiag = lambda w: jnp.block([[w.astype(F32), zero], [zero, w.astype(F32)]])
    full = lambda shape: pl.BlockSpec(shape, lambda i: (0,) * len(shape))
    return pl.pallas_call(
        functools.partial(_filter_kernel, rows=rows),
        grid=(length // rows,),
        in_specs=[pl.BlockSpec((rows, LANES), lambda i: (i, 0)),
                  full((LANES, FILTER_HIDDEN)), full((1, LANES)),
                  full((LANES, LANES)), full((1, LANES)),
                  full((LANES, LANES)), full((1, LANES)),
                  full((3 * FILTER_HIDDEN, 4 * HYENA_WIDTH)), full((1, LANES)),
                  full((1, HYENA_WIDTH))],
        out_specs=pl.BlockSpec((4, n_slab, prow, LANES), lambda i: (0, 0, i, 0)),
        out_shape=jax.ShapeDtypeStruct((4, n_slab, length // DFT_N2 * PITCH_Z, LANES), F32),
        compiler_params=_cparams(("parallel",)),
        name="hyena_filters",
    )(zfeat, w1p, twice(fb1), bdiag(fw2), twice(fb2), bdiag(fw3), twice(fb3), w4s,
      twice(ffreq), jnp.asarray(_decay_rates()))


def _dft_tables(length):
    n = 2 * length
    n1 = n // DFT_N2
    h1 = n1 // 2
    kk = np.arange(h1)[:, None] + 0.5
    th = 2.0 * math.pi * kk * np.arange(h1)[None, :] / n1
    fa = np.concatenate([np.cos(th), -np.sin(th)], axis=0)
    ga = (2.0 / n) * fa.T
    k = np.arange(h1)[:, None, None] + n1 * np.arange(DFT_N2)[None, :, None] + 0.5
    ph = 2.0 * math.pi * k * np.arange(DFT_N2)[None, None, :] / n
    f = lambda a: jnp.asarray(a.astype(np.float32))
    er, ei = f(np.cos(ph)), f(-np.sin(ph))
    m = jnp.concatenate([jnp.concatenate([er, -ei], axis=2), jnp.concatenate([ei, er], axis=2)], axis=1)
    return f(fa), f(ga), m.astype(BF16)


def _split(a):
    hi = a.astype(BF16)
    lo = (a - hi.astype(F32)).astype(BF16)
    return hi, lo


CONV_K1_CHUNK = 64
CONV_PAIR_UNROLL = 8
CONV_FREQ_UNROLL = 8


def _resident_table_spec(h1):
    return pl.BlockSpec((h1, 2 * DFT_N2, 2 * DFT_N2), lambda *_: (0, 0, 0), pipeline_mode=pl.Buffered(1))


def _spectrum_kernel(fa_ref, m_ref, xf_ref, xb_ref, o_ref, s_ref, *, h1, kc):
    hf = pl.program_id(2)
    half = DFT_N2

    @pl.when(hf == 0)
    def _():
        fa = fa_ref[...]

        def body(n2, carry):
            x = jnp.concatenate([xf_ref[pl.ds(n2, h1, stride=PITCH_Z), :],
                                 xb_ref[pl.ds(n2, h1, stride=PITCH_Z), :]], axis=1)
            xh, xl = _split(x)
            a = jnp.dot(fa, jnp.concatenate([xh, xh, xl], axis=0), preferred_element_type=F32)
            aw = _pack_bf16_pair(a[:, :LANES], a[:, LANES:])
            for ri in range(2):
                s_ref[pl.ds(ri * half + n2, h1, stride=PITCH_S), :] = aw[ri * h1:(ri + 1) * h1]
            return carry

        lax.fori_loop(0, DFT_N2, body, 0, unroll=CONV_PAIR_UNROLL)

    def freq_body(k1, carry):
        row = pl.multiple_of((hf * kc + k1) * PITCH_S, 8)
        xw = s_ref[pl.ds(row, 2 * half), :]
        x = jnp.concatenate([_unpack_bf16_pair(xw, 0), _unpack_bf16_pair(xw, 1)], axis=1)
        z = jnp.dot(m_ref[hf * kc + k1], x, preferred_element_type=F32)
        o_ref[k1] = _pack_bf16_pair(z[:half, :LANES] + z[:half, LANES:], z[half:, :LANES] - z[half:, LANES:])
        return carry

    lax.fori_loop(0, kc, freq_body, 0, unroll=CONV_FREQ_UNROLL)


def _filter_spectrum(fa, m_bf16, filt):
    _, n_slab, prow, _ = filt.shape
    n1, h1 = fa.shape
    kc = min(CONV_K1_CHUNK, h1)
    fh, fl = _split(fa)
    fa3 = jnp.concatenate([fh, fl, fh], axis=1)
    return pl.pallas_call(
        functools.partial(_spectrum_kernel, h1=h1, kc=kc),
        grid=(2, n_slab, h1 // kc),
        in_specs=[pl.BlockSpec((n1, 3 * h1), lambda o, s, hf: (0, 0)),
                  _resident_table_spec(h1),
                  pl.BlockSpec((None, None, prow, LANES), lambda o, s, hf: (2 * o, s, 0, 0)),
                  pl.BlockSpec((None, None, prow, LANES), lambda o, s, hf: (2 * o + 1, s, 0, 0))],
        out_specs=pl.BlockSpec((None, None, kc, DFT_N2, LANES), lambda o, s, hf: (o, s, hf, 0, 0)),
        out_shape=jax.ShapeDtypeStruct((2, n_slab, h1, DFT_N2, LANES), jnp.uint32),
        scratch_shapes=[pltpu.VMEM((h1 * PITCH_S, LANES), jnp.uint32)],
        compiler_params=pltpu.CompilerParams(dimension_semantics=("parallel", "parallel", "arbitrary"),
                                             vmem_limit_bytes=CONV_VMEM_LIMIT),
        name="filter_spectrum",
    )(fa3, m_bf16, filt, filt)


def _conv_kernel(fa_ref, ga_ref, m_ref, k_ref, z_ref, hx_ref, sk_ref, o_ref, s_ref, *, h1):
    half = DFT_N2
    pairs = DFT_N2 // 2
    fa = fa_ref[...]
    ga = ga_ref[...]
    sk = sk_ref[...]

    def stage_a(j, carry):
        n2 = 2 * j
        zw = [z_ref[pl.ds(n2 + u, h1, stride=PITCH_Z), :] for u in range(2)]
        a = [jnp.dot(fa, jnp.concatenate([_unpack_bf16_pair(zw[0], e), _unpack_bf16_pair(zw[1], e)], axis=1),
                     preferred_element_type=F32) for e in range(2)]
        aw = _pack_bf16_pair(a[0], a[1])
        for ri in range(2):
            for u in range(2):
                s_ref[pl.ds(ri * half + n2 + u, h1, stride=PITCH_S), :] = (
                    aw[ri * h1:(ri + 1) * h1, u * LANES:(u + 1) * LANES])
        return carry

    lax.fori_loop(0, pairs, stage_a, 0, unroll=CONV_PAIR_UNROLL)

    def freq_body(k1, carry):
        row = pl.multiple_of(k1 * PITCH_S, 8)
        xw = s_ref[pl.ds(row, 2 * half), :]
        x = jnp.concatenate([_unpack_bf16_pair(xw, 0), _unpack_bf16_pair(xw, 1)], axis=1)
        m = m_ref[k1]
        z = jnp.dot(m, x, preferred_element_type=F32)
        zr, zi = z[:half], z[half:]
        kw = k_ref[k1]
        kr = jnp.concatenate([_unpack_bf16_pair(kw, 0, F32)] * 2, axis=1)
        ki = jnp.concatenate([_unpack_bf16_pair(kw, 1, F32)] * 2, axis=1)
        y = jnp.concatenate([zr * kr - zi * ki, zr * ki + zi * kr], axis=0).astype(BF16)
        bq = lax.dot_general(m, y, (((0,), (0,)), ((), ())), preferred_element_type=F32)
        s_ref[pl.ds(row, 2 * half), :] = _pack_bf16_pair(bq[:, :LANES], bq[:, LANES:])
        return carry

    lax.fori_loop(0, h1, freq_body, 0, unroll=CONV_FREQ_UNROLL)

    def stage_inv(j, carry):
        n2 = 2 * j
        rw = [s_ref[pl.ds(n2 + u, h1, stride=PITCH_S), :] for u in range(2)]
        iw = [s_ref[pl.ds(half + n2 + u, h1, stride=PITCH_S), :] for u in range(2)]
        zw = [z_ref[pl.ds(n2 + u, h1, stride=PITCH_Z), :] for u in range(2)]
        hw = [hx_ref[pl.ds(n2 + u, h1, stride=PITCH_Z), :] for u in range(2)]
        y = []
        for e in range(2):
            bn = jnp.concatenate(
                [jnp.concatenate([_unpack_bf16_pair(rw[u], e), _unpack_bf16_pair(iw[u], e)], axis=0)
                 for u in range(2)], axis=1)
            y.append(jnp.dot(ga, bn, preferred_element_type=F32))
        for u in range(2):
            res = [_unpack_bf16_pair(hw[u], e, F32) * (y[e][:, u * LANES:(u + 1) * LANES]
                                                       + sk * _unpack_bf16_pair(zw[u], e, F32)) for e in range(2)]
            o_ref[pl.ds(n2 + u, h1, stride=PITCH_Z), :] = _pack_bf16_pair(res[0], res[1])
        return carry

    lax.fori_loop(0, pairs, stage_inv, 0, unroll=CONV_PAIR_UNROLL // 2)
    pad = jnp.zeros((PITCH_Z - DFT_N2, LANES), jnp.uint32)
    for blk in range(h1):
        o_ref[blk * PITCH_Z + DFT_N2:(blk + 1) * PITCH_Z, :] = pad


def _long_conv_gate(tabs, kspec, order, z5, zsel, hx5, hxsel, skip):
    fa, ga, m_tab = tabs
    _, n_slab, n_pair, prow, _ = z5.shape
    n1, h1 = fa.shape
    slab = lambda sel: pl.BlockSpec((None, None, None, prow, LANES), lambda s, b: (sel, s, b, 0, 0))
    return pl.pallas_call(
        functools.partial(_conv_kernel, h1=h1),
        grid=(n_slab, n_pair),
        in_specs=[pl.BlockSpec((n1, h1), lambda s, b: (0, 0)),
                  pl.BlockSpec((h1, n1), lambda s, b: (0, 0)),
                  _resident_table_spec(h1),
                  pl.BlockSpec((None, None, h1, DFT_N2, LANES), lambda s, b: (order, s, 0, 0, 0)),
                  slab(zsel), slab(hxsel),
                  pl.BlockSpec((None, None, 1, LANES), lambda s, b: (order, s, 0, 0))],
        out_specs=pl.BlockSpec((None, None, prow, LANES), lambda s, b: (s, b, 0, 0)),
        out_shape=jax.ShapeDtypeStruct((n_slab, n_pair, prow, LANES), jnp.uint32),
        scratch_shapes=[pltpu.VMEM((h1 * PITCH_S, LANES), jnp.uint32)],
        compiler_params=pltpu.CompilerParams(dimension_semantics=("parallel", "arbitrary"),
                                             vmem_limit_bytes=CONV_VMEM_LIMIT),
        name="hyena_long_conv",
    )(fa.astype(BF16), ga.astype(BF16), m_tab, kspec, z5, hx5, skip)


def _hyena(proj, batch, seq, conv_w, conv_b, fparams, skip):
    c = HYENA_WIDTH
    fa, ga, m_tab = _dft_tables(seq)
    kspec = _filter_spectrum(fa, m_tab, _filters(seq, *fparams))
    u5 = _shortconv(proj, conv_w, conv_b, batch, seq)
    skip4 = skip.astype(F32).reshape(2, c // LANES, 1, LANES)
    tabs = (fa, ga, m_tab)
    z1 = _long_conv_gate(tabs, kspec, 0, u5, 0, u5, 1, skip4)
    return _long_conv_gate(tabs, kspec, 1, z1[None], 0, u5, 2, skip4)


def _final_kernel(o1_ref, o2_ref, o3_ref, l1_ref, l2_ref, l3_ref, ga_ref, hy_ref, gh_ref, ma_ref, mh_ref,
                  x_ref, wa_ref, wh_ref, wo_ref, pg_ref, y_ref, *, tiles_per_seq):
    def cat(ref):
        return jnp.concatenate([ref[s] for s in range(ATTN_WIDTH // LANES)], axis=1)

    def cat_packed(ref):
        return jnp.concatenate([_unpack_bf16_pair(ref[s], e, F32)
                                for s in range(ref.shape[0]) for e in range(PAIRS_PER_SLAB)], axis=1)

    l1, l2, l3 = cat(l1_ref), cat(l2_ref), cat(l3_ref)
    mx = jnp.maximum(jnp.maximum(l1, l2), l3)
    e1, e2, e3 = jnp.exp(l1 - mx), jnp.exp(l2 - mx), jnp.exp(l3 - mx)
    attn = (e1 * cat_packed(o1_ref) + e2 * cat_packed(o2_ref) + e3 * cat_packed(o3_ref)) / (e1 + e2 + e3)
    ga = ga_ref[...].astype(F32)
    a_in = (attn * (ga * jax.nn.sigmoid(ga))).astype(BF16)
    a_br = jnp.dot(a_in, wa_ref[...], preferred_element_type=F32)
    gh = gh_ref[...].astype(F32)
    odd = (pl.program_id(0) // tiles_per_seq) % 2
    hw = _load_padded_slabs(hy_ref, gh.shape[0] // DFT_N2)
    hy = lax.bitcast_convert_type((hw << ((1 - odd) * 16).astype(jnp.uint32)) & jnp.uint32(0xFFFF0000), F32)
    h_in = (hy * (gh * jax.nn.sigmoid(gh))).astype(BF16)
    h_br = jnp.dot(h_in, wh_ref[...], preferred_element_type=F32)
    merged = jax.nn.sigmoid(ma_ref[...].astype(F32)) * a_br + jax.nn.sigmoid(mh_ref[...].astype(F32)) * h_br
    out = jnp.dot(merged.astype(BF16), wo_ref[...], preferred_element_type=F32)
    ms = jnp.mean(out * out, axis=-1, keepdims=True)
    y_ref[...] = x_ref[...] + out * lax.rsqrt(ms + EPS) * pg_ref[...]


def _final(os_, ls_, proj, hy, x2d, wa, wh, wo, pg, seq, tm=512):
    rows = x2d.shape[0]
    tm = min(tm, rows)
    tiles_per_seq = seq // tm
    hy_spec = pl.BlockSpec((HYENA_WIDTH // LANES, None, tm // DFT_N2 * PITCH_Z, LANES),
                           lambda i: (0, i // (2 * tiles_per_seq), i % tiles_per_seq, 0))
    slab = pl.BlockSpec((ATTN_WIDTH // LANES, tm, LANES), lambda i: (0, i, 0))
    oslab = pl.BlockSpec((ATTN_WIDTH // LANES // PAIRS_PER_SLAB, tm, LANES), lambda i: (0, i, 0))
    a512 =lambda off: pl.BlockSpec((tm, ATTN_WIDTH), lambda i: (i, off))
    a1024 = lambda off: pl.BlockSpec((tm, D_MODEL), lambda i: (i, off))
    full = lambda shape: pl.BlockSpec(shape, lambda i: (0, 0))
    return pl.pallas_call(
        functools.partial(_final_kernel, tiles_per_seq=tiles_per_seq),
        grid=(rows // tm,),
        in_specs=[oslab] * 3 + [slab] * 3 + [a512(R_GA // ATTN_WIDTH), hy_spec, a1024(R_GH // D_MODEL),
                               a1024(R_MG // D_MODEL), a1024(R_MG // D_MODEL + 1), a1024(0),
                               full((ATTN_WIDTH, D_MODEL)), full((D_MODEL, D_MODEL)), full((D_MODEL, D_MODEL)),
                               full((1, D_MODEL))],
        out_specs=pl.BlockSpec((tm, D_MODEL), lambda i: (i, 0)),
        out_shape=jax.ShapeDtypeStruct((rows, D_MODEL), F32),
        compiler_params=_cparams(("parallel",)),
        name="merge_out",
    )(*os_, *ls_, proj, hy, proj, proj, proj, x2d, wa, wh, wo, pg)


def _layer(x, rel_bias, pre_g, post_g, w_in, conv_w, conv_b, fparams, skip, w_br_a, w_br_h, w_out):
    batch, seq, _ = x.shape
    x2d = x.reshape(batch * seq, D_MODEL)
    w_qkv, w_rest = w_in
    gain = pre_g.reshape(1, -1).astype(F32)
    qkv = _inproj(x2d, gain, w_qkv, tn=QKV_WIDTH // 3, slab_out=True)
    proj = _inproj(x2d, gain, w_rest, tn=REST_WIDTH // 2, slab_out=False)
    os_, ls_ = [], []
    for g in range(N_GROUPS):
        o, l = _attention_group(qkv, rel_bias, g, batch, seq)
        os_.append(o)
        ls_.append(l)
    hy = _hyena(proj, batch, seq, conv_w, conv_b, fparams, skip)
    y = _final(os_, ls_, proj, hy, x2d, w_br_a, w_br_h, w_out, post_g.reshape(1, -1).astype(F32), seq)
    return y.reshape(batch, seq, D_MODEL)


def kernel(x_prompt, x_sample, rel_bias, pre_norm_g, post_norm_g, w_in, conv_w, conv_b, filt_w1, filt_b1, filt_w2, filt_b2, filt_w3, filt_b3, filt_w4, filt_freq, hyena_skip, w_branch_a, w_branch_h, w_out):
    depth = w_in.shape[0]

    def run(x):
        for l in range(depth):
            fparams = (filt_w1[l], filt_b1[l], filt_w2[l], filt_b2[l], filt_w3[l], filt_b3[l], filt_w4[l], filt_freq[l])
            w = w_in[l]
            col_scale = jnp.where(jnp.arange(QKV_WIDTH) < QKV_WIDTH // 3, HEAD_DIM ** -0.5, 1.0).astype(F32)
            w_qkv = (w[:, :QKV_WIDTH] * col_scale).astype(BF16)
            w_rest = jnp.concatenate([w[:, U_OFF:], w[:, GA_OFF:U_OFF]], axis=1).astype(BF16)
            x = _layer(x, rel_bias, pre_norm_g[l], post_norm_g[l], (w_qkv, w_rest), conv_w[l], conv_b[l],
                       fparams, hyena_skip[l], w_branch_a[l].astype(BF16), w_branch_h[l].astype(BF16),
                       w_out[l].astype(BF16))
        return x

    return (run(x_prompt), run(x_sample))
```

```python
import functools
import math

import numpy as np
import jax
import jax.numpy as jnp
from jax import lax
from jax.experimental import pallas as pl
from jax.experimental.pallas import tpu as pltpu

F32 = jnp.float32
BF16 = jnp.bfloat16

D_MODEL = 1024
EPS = 1e-6
HEAD_DIM = 64
ATTN_GROUPS = ((128, 1), (512, 4), (2048, 16))
N_GROUPS = 3
HEADS_PER_GROUP = 8
ATTN_WIDTH = HEADS_PER_GROUP * HEAD_DIM
HYENA_WIDTH = 1024
FILTER_EMB = 33
N_BANDS = 16
FILTER_HIDDEN = 64
NUM_BUCKETS = 32
MAX_DISTANCE = 1024
NEG_INF = -1e30
QKV_WIDTH = 4608
GA_OFF, U_OFF, GH_OFF, MG_OFF = 4608, 5120, 8192, 9216
U_WIDTH = 3072
GATE_WIDTH = 3584
G_GH, G_MA, G_MH, G_GA = 0, 1024, 2048, 3072

WIN = 64
LANES = 128
DFT_N2 = 128
PITCH_Z = 136
PITCH_S = 264
VMEM_LIMIT = 48 * 1024 * 1024
CONV_VMEM_LIMIT = 58 * 1024 * 1024


def _cparams(sem):
    return pltpu.CompilerParams(dimension_semantics=sem, vmem_limit_bytes=VMEM_LIMIT)


def _inproj_kernel(x_ref, g_ref, w_ref, o_ref, hn_ref, *, slabs):
    @pl.when(pl.program_id(1) == 0)
    def _():
        x = x_ref[...]
        ms = jnp.mean(x * x, axis=-1, keepdims=True)
        hn_ref[...] = (x * lax.rsqrt(ms + EPS) * g_ref[...]).astype(BF16)

    acc = jnp.dot(hn_ref[...], w_ref[...], preferred_element_type=F32)
    if slabs:
        for s in range(slabs):
            o_ref[s] = _pack_bf16_pair(acc[:, (2 * s) * LANES:(2 * s + 1) * LANES],
                                       acc[:, (2 * s + 1) * LANES:(2 * s + 2) * LANES])
    else:
        o_ref[...] = acc.astype(o_ref.dtype)


def _pack_bf16_pair(a, b):
    ua = lax.bitcast_convert_type(a.astype(BF16).astype(F32), jnp.uint32)
    ub = lax.bitcast_convert_type(b.astype(BF16).astype(F32), jnp.uint32)
    return (ua >> 16) | (ub & jnp.uint32(0xFFFF0000))


def _unpack_bf16_pair(w, idx, dtype=BF16):
    bits = (w << 16) if idx == 0 else (w & jnp.uint32(0xFFFF0000))
    return lax.bitcast_convert_type(bits, F32).astype(dtype)


def _inproj(x2d, g, w_bf16, tn, slab_out, tm=1024):
    rows = x2d.shape[0]
    width = w_bf16.shape[1]
    tm = min(tm, rows)
    if slab_out:
        ns = tn // (2 * LANES)
        out_spec = pl.BlockSpec((ns, tm, LANES), lambda i, j: (j, i, 0))
        out_shape = jax.ShapeDtypeStruct((width // (2 * LANES), rows, LANES), jnp.uint32)
    else:
        ns = 0
        out_spec = pl.BlockSpec((tm, tn), lambda i, j: (i, j))
        out_shape = jax.ShapeDtypeStruct((rows, width), BF16)
    return pl.pallas_call(
        functools.partial(_inproj_kernel, slabs=ns),
        grid=(rows // tm, width // tn),
        in_specs=[
            pl.BlockSpec((tm, D_MODEL), lambda i, j: (i, 0)),
            pl.BlockSpec((1, D_MODEL), lambda i, j: (0, 0)),
            pl.BlockSpec((D_MODEL, tn), lambda i, j: (0, j)),
        ],
        out_specs=out_spec,
        out_shape=out_shape,
        scratch_shapes=[pltpu.VMEM((tm, D_MODEL), BF16)],
        compiler_params=_cparams(("parallel", "arbitrary")),
        name="inproj_qkv" if slab_out else "inproj_u",
    )(x2d, g, w_bf16)


def _t5_bucket_np(rel):
    half = NUM_BUCKETS // 2
    max_exact = half // 2
    n = np.abs(rel)
    nf = np.maximum(n, 1).astype(np.float64)
    large = max_exact + (np.log(nf / max_exact) / math.log(MAX_DISTANCE / max_exact) * (half - max_exact)).astype(np.int64)
    large = np.minimum(large, half - 1)
    return np.where(rel > 0, half, 0) + np.where(n < max_exact, n, large)


def _bias_mask(rel_bias_g, dilation):
    qi = np.arange(2 * WIN)[:, None]
    kj = np.arange(4 * WIN)[None, :] - WIN
    delta = kj - qi
    bucket = _t5_bucket_np(delta * dilation).astype(np.int32)
    onehot = jax.nn.one_hot(jnp.asarray(bucket), NUM_BUCKETS, dtype=F32)
    bias = jnp.einsum('qkb,bh->hqk', onehot, rel_bias_g.astype(F32), precision=lax.Precision.HIGHEST)
    bias = jnp.where(jnp.asarray(np.abs(delta) <= WIN)[None], bias, NEG_INF)
    return bias.reshape(HEADS_PER_GROUP // 2, 2, 2 * WIN, 4 * WIN)


ATTN_POS_PER_STEP = 4096
ATTN_TILES_PER_ITER = 2
PAIRS_PER_SLAB = 2


def _attn_kernel(q_ref, kp_ref, kc_ref, kn_ref, vp_ref, vc_ref, vn_ref, bm_ref, o_ref, l_ref,
                 kbuf, vbuf, *, tq, dil, sub_len):
    t = pl.program_id(2)
    halo = WIN * dil
    span = tq * dil
    kbuf[0:halo] = kp_ref[...]
    kbuf[halo:halo + span] = kc_ref[...]
    kbuf[halo + span:] = kn_ref[...]
    vbuf[0:halo] = vp_ref[...]
    vbuf[halo:halo + span] = vc_ref[...]
    vbuf[halo + span:] = vn_ref[...]
    lane = lax.broadcasted_iota(jnp.int32, (1, LANES), 1)
    is_lo = lane < HEAD_DIM
    kcol = lax.broadcasted_iota(jnp.int32, (1, 4 * WIN), 1)
    qt = 2 * WIN
    nq = tq // qt

    def tile(c, i):
        row0 = i * (qt * dil) + c
        kpos = t * tq + i * qt - WIN + kcol
        pen = jnp.where((kpos >= 0) & (kpos < sub_len), 0.0, NEG_INF).astype(F32)
        qw = q_ref[pl.ds(row0, qt, stride=dil), :]
        kw = kbuf[pl.ds(row0, 2 * qt, stride=dil), :]
        vw = vbuf[pl.ds(row0, 2 * qt, stride=dil), :]
        o_pair = []
        for hp in range(PAIRS_PER_SLAB):
            q = _unpack_bf16_pair(qw, hp)
            k = _unpack_bf16_pair(kw, hp)
            v = _unpack_bf16_pair(vw, hp)
            outs, lses = [], []
            for hh in range(2):
                sel = is_lo if hh == 0 else jnp.logical_not(is_lo)
                qm = jnp.where(sel, q, jnp.zeros_like(q))
                s = lax.dot_general(qm, k, (((1,), (1,)), ((), ())), preferred_element_type=F32)
                s = s + bm_ref[hp, hh] + pen
                m = jnp.max(s, axis=-1, keepdims=True)
                p = jnp.exp(s - m)
                den = jnp.sum(p, axis=-1, keepdims=True)
                pv = jnp.dot(p.astype(BF16), v, preferred_element_type=F32)
                outs.append(pv / den)
                lses.append(m + jnp.log(den))
            o_pair.append(jnp.where(is_lo, outs[0], outs[1]))
            l_ref[hp, pl.ds(row0, qt, stride=dil), :] = jnp.where(is_lo, lses[0], lses[1])
        o_ref[pl.ds(row0, qt, stride=dil), :] = _pack_bf16_pair(o_pair[0], o_pair[1])

    def body(it, carry):
        for u in range(ATTN_TILES_PER_ITER):
            idx = it * ATTN_TILES_PER_ITER + u
            tile(idx // nq, idx % nq)
        return carry

    lax.fori_loop(0, dil * nq // ATTN_TILES_PER_ITER, body, 0)


def _attention_group(qkv, rel_bias, g, batch, seq):
    _, dil = ATTN_GROUPS[g]
    sub_len = seq // dil
    tq = min(ATTN_POS_PER_STEP // dil, sub_len)
    nt = sub_len // tq
    span = tq * dil
    halo = WIN * dil
    n_hp = ATTN_WIDTH // LANES
    n_ps = n_hp // PAIRS_PER_SLAB
    n_slab = N_GROUPS * n_ps
    bm = _bias_mask(rel_bias[:, g * HEADS_PER_GROUP:(g + 1) * HEADS_PER_GROUP], dil)
    hb = tq // WIN
    nhb = sub_len // WIN

    def cur(which):
        return pl.BlockSpec((None, span, LANES), lambda b, ps, t: (which * n_slab + g * n_ps + ps, b * nt + t, 0))

    def prev(which):
        return pl.BlockSpec((None, halo, LANES),
                            lambda b, ps, t: (which * n_slab + g * n_ps + ps, b * nhb + jnp.maximum(t * hb - 1, 0), 0))

    def nxt(which):
        return pl.BlockSpec((None, halo, LANES),
                            lambda b, ps, t: (which * n_slab + g * n_ps + ps, b * nhb + jnp.minimum((t + 1) * hb, nhb - 1), 0))

    out_spec = pl.BlockSpec((PAIRS_PER_SLAB, span, LANES), lambda b, ps, t: (ps, b * nt + t, 0))
    return pl.pallas_call(
        functools.partial(_attn_kernel, tq=tq, dil=dil, sub_len=sub_len),
        grid=(batch, n_ps, nt),
        in_specs=[cur(0), prev(1), cur(1), nxt(1), prev(2), cur(2), nxt(2),
                  pl.BlockSpec((PAIRS_PER_SLAB, 2, 2 * WIN, 4 * WIN), lambda b, ps, t: (ps, 0, 0, 0))],
        out_specs=[pl.BlockSpec((None, span, LANES), lambda b, ps, t: (ps, b * nt + t, 0)), out_spec],
        out_shape=[jax.ShapeDtypeStruct((n_ps, batch * seq, LANES), jnp.uint32),
                   jax.ShapeDtypeStruct((n_hp, batch * seq, LANES), F32)],
        scratch_shapes=[pltpu.VMEM((span + 2 * halo, LANES), jnp.uint32)] * 2,
        compiler_params=_cparams(("parallel", "parallel", "arbitrary")),
        name=f"attn_g{g}",
    )(qkv, qkv, qkv, qkv, qkv, qkv, qkv, bm)


def _shortconv_kernel(p_ref, c_ref, n_ref, w_ref, b_ref, o_ref, x_buf, *, rows, halo):
    t = pl.program_id(1)
    nt = pl.num_programs(1)
    has_prev = (t > 0).astype(F32)
    has_next = (t < nt - 1).astype(F32)
    n_slab = c_ref.shape[2] // LANES
    for e in range(2):
        x = c_ref[e].astype(F32)
        lo = p_ref[e].astype(F32)[halo - 8:halo, :] * has_prev
        hi = n_ref[e].astype(F32)[0:8, :] * has_next
        for s in range(n_slab):
            cs = slice(s * LANES, (s + 1) * LANES)
            x_buf[e, s, 0:8, :] = lo[:, cs]
            x_buf[e, s, 8:rows + 8, :] = x[:, cs]
            x_buf[e, s, rows + 8:rows + 16, :] = hi[:, cs]
    nblk = rows // DFT_N2
    pad = jnp.zeros((PITCH_Z - DFT_N2, LANES), jnp.uint32)
    for s in range(n_slab):
        cs = slice(s * LANES, (s + 1) * LANES)
        w0, w1, w2, bias = w_ref[0:1, cs], w_ref[1:2, cs], w_ref[2:3, cs], b_ref[:, cs]
        res = []
        for e in range(2):
            xs = x_buf.at[e, s]
            res.append(xs[pl.ds(7, rows, stride=1), :] * w0 + xs[pl.ds(8, rows, stride=1), :] * w1
                       + xs[pl.ds(9, rows, stride=1), :] * w2 + bias)
        packed = _pack_bf16_pair(res[0], res[1])
        for jb in range(nblk):
            o_ref[s, jb * PITCH_Z:jb * PITCH_Z + DFT_N2, :] = packed[jb * DFT_N2:(jb + 1) * DFT_N2]
            o_ref[s, jb * PITCH_Z + DFT_N2:(jb + 1) * PITCH_Z, :] = pad


def _store_padded_slabs(o_ref, val, nblk):
    pad = jnp.zeros((PITCH_Z - DFT_N2, LANES), val.dtype)
    for s in range(val.shape[1] // LANES):
        for jb in range(nblk):
            o_ref[s, jb * PITCH_Z:jb * PITCH_Z + DFT_N2, :] = val[jb * DFT_N2:(jb + 1) * DFT_N2, s * LANES:(s + 1) * LANES]
            o_ref[s, jb * PITCH_Z + DFT_N2:(jb + 1) * PITCH_Z, :] = pad


def _load_padded_slabs(ref, nblk):
    return jnp.concatenate(
        [jnp.concatenate([ref[s, jb * PITCH_Z:jb * PITCH_Z + DFT_N2, :] for jb in range(nblk)], axis=0)
         for s in range(ref.shape[0])], axis=1)


def _shortconv(proj, conv_w, conv_b, batch, seq, rows=512):
    assert batch % 2 == 0
    rows = min(rows, seq)
    prow = rows // DFT_N2 * PITCH_Z
    n_slab = HYENA_WIDTH // LANES
    halo = 16
    pv = proj.reshape(batch, seq, U_WIDTH)
    cb = 0
    hb = rows // halo
    nhb = seq // halo
    return pl.pallas_call(
        functools.partial(_shortconv_kernel, rows=rows, halo=halo),
        grid=(batch // 2, seq // rows, 3),
        in_specs=[
            pl.BlockSpec((2, halo, HYENA_WIDTH), lambda b, t, j: (b, jnp.maximum(t * hb - 1, 0), cb + j)),
            pl.BlockSpec((2, rows, HYENA_WIDTH), lambda b, t, j: (b, t, cb + j)),
            pl.BlockSpec((2, halo, HYENA_WIDTH), lambda b, t, j: (b, jnp.minimum((t + 1) * hb, nhb - 1), cb + j)),
            pl.BlockSpec((3, HYENA_WIDTH), lambda b, t, j: (0, j)),
            pl.BlockSpec((1, HYENA_WIDTH), lambda b, t, j: (0, j)),
        ],
        out_specs=pl.BlockSpec((None, n_slab, None, prow, LANES), lambda b, t, j: (j, 0, b, t, 0)),
        out_shape=jax.ShapeDtypeStruct((3, n_slab, batch // 2, seq // DFT_N2 * PITCH_Z, LANES), jnp.uint32),
        scratch_shapes=[pltpu.VMEM((2, n_slab, rows + 16, LANES), F32)],
        compiler_params=_cparams(("parallel", "parallel", "arbitrary")),
        name="shortconv",
    )(pv, pv, pv, conv_w, conv_b.reshape(1, -1))


def _filter_features(length):
    t = np.linspace(0.0, 1.0, length)[:, None]
    ang = (2.0 * math.pi / length) * np.arange(length, dtype=np.float64)[:, None]
    bands = np.linspace(1e-4, N_BANDS - 1, N_BANDS)[None]
    z = np.concatenate([t, np.cos(ang * bands), -np.sin(ang * bands)], axis=-1)
    zp = np.zeros((length, LANES), np.float32)
    zp[:, :FILTER_EMB] = z
    return zp


def _decay_rates():
    max_decay = math.log(1e-2) / 0.3
    min_decay = math.log(1e-2) / 1.5
    return np.abs(np.linspace(min_decay, max_decay, HYENA_WIDTH)).astype(np.float32)[None]


def _filter_kernel(z_ref, w1_ref, b1_ref, w2_ref, b2_ref, w3_ref, b3_ref, w4_ref, fr_ref, dl_ref, o_ref, *, rows):
    hi = lax.Precision.HIGHEST
    z = z_ref[...]
    fr = fr_ref[...]
    hr = rows // 2
    pre = jnp.concatenate([jnp.dot(z[:hr], w1_ref[...], precision=hi, preferred_element_type=F32),
                           jnp.dot(z[hr:], w1_ref[...], precision=hi, preferred_element_type=F32)], axis=1)
    h = jnp.sin(fr * (pre + b1_ref[...]))
    h = jnp.sin(fr * (jnp.dot(h, w2_ref[...], precision=hi, preferred_element_type=F32) + b2_ref[...]))
    h = jnp.sin(fr * (jnp.dot(h, w3_ref[...], precision=hi, preferred_element_type=F32) + b3_ref[...]))
    h = jnp.concatenate([h[:, :FILTER_HIDDEN], h[:, FILTER_HIDDEN:]], axis=0)
    hh, hl = _split(h)
    filt = jnp.dot(jnp.concatenate([hh, hl, hh], axis=1), w4_ref[...], preferred_element_type=F32)
    decay = jnp.exp(-z[:, 0:1] * dl_ref[...])
    row = pl.program_id(0) * rows + lax.broadcasted_iota(jnp.int32, (rows, 1), 0)
    for j in range(4):
        cs = slice(j * HYENA_WIDTH, (j + 1) * HYENA_WIDTH)
        val = filt[:, cs] * decay
        if j % 2 == 1:
            val = jnp.where(row == 0, 0.0, val)
        _store_padded_slabs(o_ref.at[j], val, rows // DFT_N2)


def _filters(length, fw1, fb1, fw2, fb2, fw3, fb3, fw4, ffreq, rows=256):
    rows = min(rows, length)
    n_slab = HYENA_WIDTH // LANES
    prow = rows // DFT_N2 * PITCH_Z
    w4h, w4l = _split(fw4.astype(F32))
    w4s = jnp.concatenate([w4h, w4h, w4l], axis=0)
    zfeat = jnp.asarray(_filter_features(length))
    w1p = jnp.zeros((LANES, FILTER_HIDDEN), F32).at[:FILTER_EMB].set(fw1.astype(F32))
    twice = lambda v: jnp.tile(v.astype(F32).reshape(1, -1), (1, 2))
    zero = jnp.zeros((FILTER_HIDDEN, FILTER_HIDDEN), F32)
    bdiag = lambda w: jnp.block([[w.astype(F32), zero], [zero, w.astype(F32)]])
    full = lambda shape: pl.BlockSpec(shape, lambda i: (0,) * len(shape))
    return pl.pallas_call(
        functools.partial(_filter_kernel, rows=rows),
        grid=(length // rows,),
        in_specs=[pl.BlockSpec((rows, LANES), lambda i: (i, 0)),
                  full((LANES, FILTER_HIDDEN)), full((1, LANES)),
                  full((LANES, LANES)), full((1, LANES)),
                  full((LANES, LANES)), full((1, LANES)),
                  full((3 * FILTER_HIDDEN, 4 * HYENA_WIDTH)), full((1, LANES)),
                  full((1, HYENA_WIDTH))],
        out_specs=pl.BlockSpec((4, n_slab, prow, LANES), lambda i: (0, 0, i, 0)),
        out_shape=jax.ShapeDtypeStruct((4, n_slab, length // DFT_N2 * PITCH_Z, LANES), F32),
        compiler_params=_cparams(("parallel",)),
        name="hyena_filters",
    )(zfeat, w1p, twice(fb1), bdiag(fw2), twice(fb2), bdiag(fw3), twice(fb3), w4s,
      twice(ffreq), jnp.asarray(_decay_rates()))


def _dft_tables(length):
    n = 2 * length
    n1 = n // DFT_N2
    h1 = n1 // 2
    kk = np.arange(h1)[:, None] + 0.5
    th = 2.0 * math.pi * kk * np.arange(h1)[None, :] / n1
    fa = np.concatenate([np.cos(th), -np.sin(th)], axis=0)
    ga = (2.0 / n) * fa.T
    k = np.arange(h1)[:, None, None] + n1 * np.arange(DFT_N2)[None, :, None] + 0.5
    ph = 2.0 * math.pi * k * np.arange(DFT_N2)[None, None, :] / n
    f = lambda a: jnp.asarray(a.astype(np.float32))
    er, ei = f(np.cos(ph)), f(-np.sin(ph))
    m = jnp.concatenate([jnp.concatenate([er, -ei], axis=2), jnp.concatenate([ei, er], axis=2)], axis=1)
    return f(fa), f(ga), m.astype(BF16)


def _split(a):
    hi = a.astype(BF16)
    lo = (a - hi.astype(F32)).astype(BF16)
    return hi, lo


CONV_K1_CHUNK = 64
CONV_PAIR_UNROLL = 8
CONV_FREQ_UNROLL = 8


def _resident_table_spec(h1):
    return pl.BlockSpec((h1, 2 * DFT_N2, 2 * DFT_N2), lambda *_: (0, 0, 0), pipeline_mode=pl.Buffered(1))


def _spectrum_kernel(fa_ref, m_ref, xf_ref, xb_ref, o_ref, s_ref, *, h1, kc):
    hf = pl.program_id(2)
    half = DFT_N2

    @pl.when(hf == 0)
    def _():
        fa = fa_ref[...]

        def body(n2, carry):
            x = jnp.concatenate([xf_ref[pl.ds(n2, h1, stride=PITCH_Z), :],
                                 xb_ref[pl.ds(n2, h1, stride=PITCH_Z), :]], axis=1)
            xh, xl = _split(x)
            a = jnp.dot(fa, jnp.concatenate([xh, xh, xl], axis=0), preferred_element_type=F32)
            aw = _pack_bf16_pair(a[:, :LANES], a[:, LANES:])
            for ri in range(2):
                s_ref[pl.ds(ri * half + n2, h1, stride=PITCH_S), :] = aw[ri * h1:(ri + 1) * h1]
            return carry

        lax.fori_loop(0, DFT_N2, body, 0, unroll=CONV_PAIR_UNROLL)

    def freq_body(k1, carry):
        row = pl.multiple_of((hf * kc + k1) * PITCH_S, 8)
        xw = s_ref[pl.ds(row, 2 * half), :]
        x = jnp.concatenate([_unpack_bf16_pair(xw, 0), _unpack_bf16_pair(xw, 1)], axis=1)
        z = jnp.dot(m_ref[hf * kc + k1], x, preferred_element_type=F32)
        o_ref[k1] = _pack_bf16_pair(z[:half, :LANES] + z[:half, LANES:], z[half:, :LANES] - z[half:, LANES:])
        return carry

    lax.fori_loop(0, kc, freq_body, 0, unroll=CONV_FREQ_UNROLL)


def _filter_spectrum(fa, m_bf16, filt):
    _, n_slab, prow, _ = filt.shape
    n1, h1 = fa.shape
    kc = min(CONV_K1_CHUNK, h1)
    fh, fl = _split(fa)
    fa3 = jnp.concatenate([fh, fl, fh], axis=1)
    return pl.pallas_call(
        functools.partial(_spectrum_kernel, h1=h1, kc=kc),
        grid=(2, n_slab, h1 // kc),
        in_specs=[pl.BlockSpec((n1, 3 * h1), lambda o, s, hf: (0, 0)),
                  _resident_table_spec(h1),
                  pl.BlockSpec((None, None, prow, LANES), lambda o, s, hf: (2 * o, s, 0, 0)),
                  pl.BlockSpec((None, None, prow, LANES), lambda o, s, hf: (2 * o + 1, s, 0, 0))],
        out_specs=pl.BlockSpec((None, None, kc, DFT_N2, LANES), lambda o, s, hf: (o, s, hf, 0, 0)),
        out_shape=jax.ShapeDtypeStruct((2, n_slab, h1, DFT_N2, LANES), jnp.uint32),
        scratch_shapes=[pltpu.VMEM((h1 * PITCH_S, LANES), jnp.uint32)],
        compiler_params=pltpu.CompilerParams(dimension_semantics=("parallel", "parallel", "arbitrary"),
                                             vmem_limit_bytes=CONV_VMEM_LIMIT),
        name="filter_spectrum",
    )(fa3, m_bf16, filt, filt)


def _conv_kernel(fa_ref, ga_ref, m_ref, k_ref, z_ref, hx_ref, sk_ref, o_ref, s_ref, *, h1):
    half = DFT_N2
    pairs = DFT_N2 // 2
    fa = fa_ref[...]
    ga = ga_ref[...]
    sk = sk_ref[...]

    def stage_a(j, carry):
        n2 = 2 * j
        zw = [z_ref[pl.ds(n2 + u, h1, stride=PITCH_Z), :] for u in range(2)]
        a = [jnp.dot(fa, jnp.concatenate([_unpack_bf16_pair(zw[0], e), _unpack_bf16_pair(zw[1], e)], axis=1),
                     preferred_element_type=F32) for e in range(2)]
        aw = _pack_bf16_pair(a[0], a[1])
        for ri in range(2):
            for u in range(2):
                s_ref[pl.ds(ri * half + n2 + u, h1, stride=PITCH_S), :] = (
                    aw[ri * h1:(ri + 1) * h1, u * LANES:(u + 1) * LANES])
        return carry

    lax.fori_loop(0, pairs, stage_a, 0, unroll=CONV_PAIR_UNROLL)

    def freq_body(k1, carry):
        row = pl.multiple_of(k1 * PITCH_S, 8)
        xw = s_ref[pl.ds(row, 2 * half), :]
        x = jnp.concatenate([_unpack_bf16_pair(xw, 0), _unpack_bf16_pair(xw, 1)], axis=1)
        m = m_ref[k1]
        z = jnp.dot(m, x, preferred_element_type=F32)
        zr, zi = z[:half], z[half:]
        kw = k_ref[k1]
        kr = jnp.concatenate([_unpack_bf16_pair(kw, 0, F32)] * 2, axis=1)
        ki = jnp.concatenate([_unpack_bf16_pair(kw, 1, F32)] * 2, axis=1)
        y = jnp.concatenate([zr * kr - zi * ki, zr * ki + zi * kr], axis=0).astype(BF16)
        bq = lax.dot_general(m, y, (((0,), (0,)), ((), ())), preferred_element_type=F32)
        s_ref[pl.ds(row, 2 * half), :] = _pack_bf16_pair(bq[:, :LANES], bq[:, LANES:])
        return carry

    lax.fori_loop(0, h1, freq_body, 0, unroll=CONV_FREQ_UNROLL)

    def stage_inv(j, carry):
        n2 = 2 * j
        rw = [s_ref[pl.ds(n2 + u, h1, stride=PITCH_S), :] for u in range(2)]
        iw = [s_ref[pl.ds(half + n2 + u, h1, stride=PITCH_S), :] for u in range(2)]
        zw = [z_ref[pl.ds(n2 + u, h1, stride=PITCH_Z), :] for u in range(2)]
        hw = [hx_ref[pl.ds(n2 + u, h1, stride=PITCH_Z), :] for u in range(2)]
        y = []
        for e in range(2):
            bn = jnp.concatenate(
                [jnp.concatenate([_unpack_bf16_pair(rw[u], e), _unpack_bf16_pair(iw[u], e)], axis=0)
                 for u in range(2)], axis=1)
            y.append(jnp.dot(ga, bn, preferred_element_type=F32))
        for u in range(2):
            res = [_unpack_bf16_pair(hw[u], e, F32) * (y[e][:, u * LANES:(u + 1) * LANES]
                                                       + sk * _unpack_bf16_pair(zw[u], e, F32)) for e in range(2)]
            o_ref[pl.ds(n2 + u, h1, stride=PITCH_Z), :] = _pack_bf16_pair(res[0], res[1])
        return carry

    lax.fori_loop(0, pairs, stage_inv, 0, unroll=CONV_PAIR_UNROLL // 2)
    pad = jnp.zeros((PITCH_Z - DFT_N2, LANES), jnp.uint32)
    for blk in range(h1):
        o_ref[blk * PITCH_Z + DFT_N2:(blk + 1) * PITCH_Z, :] = pad


def _long_conv_gate(tabs, kspec, order, z5, zsel, hx5, hxsel, skip):
    fa, ga, m_tab = tabs
    _, n_slab, n_pair, prow, _ = z5.shape
    n1, h1 = fa.shape
    slab = lambda sel: pl.BlockSpec((None, None, None, prow, LANES), lambda s, b: (sel, s, b, 0, 0))
    return pl.pallas_call(
        functools.partial(_conv_kernel, h1=h1),
        grid=(n_slab, n_pair),
        in_specs=[pl.BlockSpec((n1, h1), lambda s, b: (0, 0)),
                  pl.BlockSpec((h1, n1), lambda s, b: (0, 0)),
                  _resident_table_spec(h1),
                  pl.BlockSpec((None, None, h1, DFT_N2, LANES), lambda s, b: (order, s, 0, 0, 0)),
                  slab(zsel), slab(hxsel),
                  pl.BlockSpec((None, None, 1, LANES), lambda s, b: (order, s, 0, 0))],
        out_specs=pl.BlockSpec((None, None, prow, LANES), lambda s, b: (s, b, 0, 0)),
        out_shape=jax.ShapeDtypeStruct((n_slab, n_pair, prow, LANES), jnp.uint32),
        scratch_shapes=[pltpu.VMEM((h1 * PITCH_S, LANES), jnp.uint32)],
        compiler_params=pltpu.CompilerParams(dimension_semantics=("parallel", "arbitrary"),
                                             vmem_limit_bytes=CONV_VMEM_LIMIT),
        name="hyena_long_conv",
    )(fa.astype(BF16), ga.astype(BF16), m_tab, kspec, z5, hx5, skip)


def _hyena(proj, batch, seq, conv_w, conv_b, fparams, skip):
    c = HYENA_WIDTH
    fa, ga, m_tab = _dft_tables(seq)
    kspec = _filter_spectrum(fa, m_tab, _filters(seq, *fparams))
    u5 = _shortconv(proj, conv_w, conv_b, batch, seq)
    skip4 = skip.astype(F32).reshape(2, c // LANES, 1, LANES)
    tabs = (fa, ga, m_tab)
    z1 = _long_conv_gate(tabs, kspec, 0, u5, 0, u5, 1, skip4)
    return _long_conv_gate(tabs, kspec, 1, z1[None], 0, u5, 2, skip4)


def _final_kernel(o1_ref, o2_ref, o3_ref, l1_ref, l2_ref, l3_ref, hy_ref, x_ref, g_ref, wg_ref,
                  wa_ref, wh_ref, wo_ref, pg_ref, y_ref, *, tiles_per_seq):
    def cat(ref):
        return jnp.concatenate([ref[s] for s in range(ATTN_WIDTH // LANES)], axis=1)

    def cat_packed(ref):
        return jnp.concatenate([_unpack_bf16_pair(ref[s], e, F32)
                                for s in range(ref.shape[0]) for e in range(PAIRS_PER_SLAB)], axis=1)

    l1, l2, l3 = cat(l1_ref), cat(l2_ref), cat(l3_ref)
    mx = jnp.maximum(jnp.maximum(l1, l2), l3)
    e1, e2, e3 = jnp.exp(l1 - mx), jnp.exp(l2 - mx), jnp.exp(l3 - mx)
    attn = (e1 * cat_packed(o1_ref) + e2 * cat_packed(o2_ref) + e3 * cat_packed(o3_ref)) / (e1 + e2 + e3)
    x = x_ref[...]
    hn = (x * lax.rsqrt(jnp.mean(x * x, axis=-1, keepdims=True) + EPS) * g_ref[...]).astype(BF16)
    gates = jnp.dot(hn, wg_ref[...], preferred_element_type=F32)
    gh, ga = gates[:, G_GH:G_MA], gates[:, G_GA:]
    a_in = (attn * (ga * jax.nn.sigmoid(ga))).astype(BF16)
    a_br = jnp.dot(a_in, wa_ref[...], preferred_element_type=F32)
    odd = (pl.program_id(0) // tiles_per_seq) % 2
    hw = _load_padded_slabs(hy_ref, x.shape[0] // DFT_N2)
    hy = lax.bitcast_convert_type((hw << ((1 - odd) * 16).astype(jnp.uint32)) & jnp.uint32(0xFFFF0000), F32)
    h_in = (hy * (gh * jax.nn.sigmoid(gh))).astype(BF16)
    h_br = jnp.dot(h_in, wh_ref[...], preferred_element_type=F32)
    merged = jax.nn.sigmoid(gates[:, G_MA:G_MH]) * a_br + jax.nn.sigmoid(gates[:, G_MH:G_GA]) * h_br
    out = jnp.dot(merged.astype(BF16), wo_ref[...], preferred_element_type=F32)
    ms = jnp.mean(out * out, axis=-1, keepdims=True)
    y_ref[...] = x + out * lax.rsqrt(ms + EPS) * pg_ref[...]


def _final(os_, ls_, hy, x2d, gain, wg, wa, wh, wo, pg, seq, tm=512):
    rows = x2d.shape[0]
    tm = min(tm, rows)
    tiles_per_seq = seq // tm
    hy_spec = pl.BlockSpec((HYENA_WIDTH // LANES, None, tm // DFT_N2 * PITCH_Z, LANES),
                           lambda i: (0, i // (2 * tiles_per_seq), i % tiles_per_seq, 0))
    slab = pl.BlockSpec((ATTN_WIDTH // LANES, tm, LANES), lambda i: (0, i, 0))
    oslab = pl.BlockSpec((ATTN_WIDTH // LANES // PAIRS_PER_SLAB, tm, LANES), lambda i: (0, i, 0))
    full = lambda shape: pl.BlockSpec(shape, lambda i: (0, 0), pipeline_mode=pl.Buffered(1))
    return pl.pallas_call(
        functools.partial(_final_kernel, tiles_per_seq=tiles_per_seq),
        grid=(rows // tm,),
        in_specs=[oslab] * 3 + [slab] * 3 + [hy_spec, pl.BlockSpec((tm, D_MODEL), lambda i: (i, 0)),
                                             full((1, D_MODEL)), full((D_MODEL, GATE_WIDTH)),
                                             full((ATTN_WIDTH, D_MODEL)), full((D_MODEL, D_MODEL)),
                                             full((D_MODEL, D_MODEL)), full((1, D_MODEL))],
        out_specs=pl.BlockSpec((tm, D_MODEL), lambda i: (i, 0)),
        out_shape=jax.ShapeDtypeStruct((rows, D_MODEL), F32),
        compiler_params=_cparams(("parallel",)),
        name="merge_out",
    )(*os_, *ls_, hy, x2d, gain, wg, wa, wh, wo, pg)


def _layer(x, rel_bias, pre_g, post_g, w_in, conv_w, conv_b, fparams, skip, w_br_a, w_br_h, w_out):
    batch, seq, _ = x.shape
    x2d = x.reshape(batch * seq, D_MODEL)
    w_qkv, w_u, w_gate = w_in
    gain = pre_g.reshape(1, -1).astype(F32)
    qkv = _inproj(x2d, gain, w_qkv, tn=QKV_WIDTH // 3, slab_out=True)
    proj = _inproj(x2d, gain, w_u, tn=U_WIDTH // 2, slab_out=False)
    os_, ls_ = [], []
    for g in range(N_GROUPS):
        o, l = _attention_group(qkv, rel_bias, g, batch, seq)
        os_.append(o)
        ls_.append(l)
    hy = _hyena(proj, batch, seq, conv_w, conv_b, fparams, skip)
    y = _final(os_, ls_, hy, x2d, gain, w_gate, w_br_a, w_br_h, w_out, post_g.reshape(1, -1).astype(F32), seq)
    return y.reshape(batch, seq, D_MODEL)


def kernel(x_prompt, x_sample, rel_bias, pre_norm_g, post_norm_g, w_in, conv_w, conv_b, filt_w1, filt_b1, filt_w2, filt_b2, filt_w3, filt_b3, filt_w4, filt_freq, hyena_skip, w_branch_a, w_branch_h, w_out):
    depth = w_in.shape[0]

    def run(x):
        for l in range(depth):
            fparams = (filt_w1[l], filt_b1[l], filt_w2[l], filt_b2[l], filt_w3[l], filt_b3[l], filt_w4[l], filt_freq[l])
            w = w_in[l]
            col_scale = jnp.where(jnp.arange(QKV_WIDTH) < QKV_WIDTH // 3, HEAD_DIM ** -0.5, 1.0).astype(F32)
            w_qkv = (w[:, :QKV_WIDTH] * col_scale).astype(BF16)
            w_u = w[:, U_OFF:GH_OFF].astype(BF16)
            w_gate = jnp.concatenate([w[:, GH_OFF:], w[:, GA_OFF:U_OFF]], axis=1).astype(BF16)
            x = _layer(x, rel_bias, pre_norm_g[l], post_norm_g[l], (w_qkv, w_u, w_gate), conv_w[l], conv_b[l],
                       fparams, hyena_skip[l], w_branch_a[l].astype(BF16), w_branch_h[l].astype(BF16),
                       w_out[l].astype(BF16))
        return x

    return (run(x_prompt), run(x_sample))
```

```python
import functools
import math

import numpy as np
import jax
import jax.numpy as jnp
from jax import lax
from jax.experimental import pallas as pl
from jax.experimental.pallas import tpu as pltpu

F32 = jnp.float32
BF16 = jnp.bfloat16

D_MODEL = 1024
EPS = 1e-6
HEAD_DIM = 64
ATTN_GROUPS = ((128, 1), (512, 4), (2048, 16))
N_GROUPS = 3
HEADS_PER_GROUP = 8
ATTN_WIDTH = HEADS_PER_GROUP * HEAD_DIM
HYENA_WIDTH = 1024
FILTER_EMB = 33
N_BANDS = 16
FILTER_HIDDEN = 64
NUM_BUCKETS = 32
MAX_DISTANCE = 1024
NEG_INF = -1e30
QKV_WIDTH = 4608
GA_OFF, U_OFF, GH_OFF = 4608, 5120, 8192
U_WIDTH = 3072
GATE_WIDTH = 3584
G_GH, G_MA, G_MH, G_GA = 0, 1024, 2048, 3072

WIN = 64
LANES = 128
DFT_N2 = 128
PITCH_Z = 136
PITCH_S = 264
VMEM_LIMIT = 48 * 1024 * 1024
CONV_VMEM_LIMIT = 58 * 1024 * 1024


def _cparams(sem):
    return pltpu.CompilerParams(dimension_semantics=sem, vmem_limit_bytes=VMEM_LIMIT)


def _inproj_kernel(x_ref, g_ref, w_ref, o_ref, hn_ref, *, slabs):
    @pl.when(pl.program_id(1) == 0)
    def _():
        x = x_ref[...]
        ms = jnp.mean(x * x, axis=-1, keepdims=True)
        hn_ref[...] = (x * lax.rsqrt(ms + EPS) * g_ref[...]).astype(BF16)

    acc = jnp.dot(hn_ref[...], w_ref[...], preferred_element_type=F32)
    if slabs:
        for s in range(slabs):
            o_ref[s] = _pack_bf16_pair(acc[:, (2 * s) * LANES:(2 * s + 1) * LANES],
                                       acc[:, (2 * s + 1) * LANES:(2 * s + 2) * LANES])
    else:
        o_ref[...] = acc.astype(o_ref.dtype)


def _pack_bf16_pair(a, b):
    ua = lax.bitcast_convert_type(a.astype(BF16).astype(F32), jnp.uint32)
    ub = lax.bitcast_convert_type(b.astype(BF16).astype(F32), jnp.uint32)
    return (ua >> 16) | (ub & jnp.uint32(0xFFFF0000))


def _unpack_bf16_pair(w, idx, dtype=BF16):
    bits = (w << 16) if idx == 0 else (w & jnp.uint32(0xFFFF0000))
    return lax.bitcast_convert_type(bits, F32).astype(dtype)


def _inproj(x2d, g, w_bf16, tn, slab_out, tm=1024):
    rows = x2d.shape[0]
    width = w_bf16.shape[1]
    tm = min(tm, rows)
    if slab_out:
        ns = tn // (2 * LANES)
        out_spec = pl.BlockSpec((ns, tm, LANES), lambda i, j: (j, i, 0))
        out_shape = jax.ShapeDtypeStruct((width // (2 * LANES), rows, LANES), jnp.uint32)
    else:
        ns = 0
        out_spec = pl.BlockSpec((tm, tn), lambda i, j: (i, j))
        out_shape = jax.ShapeDtypeStruct((rows, width), BF16)
    return pl.pallas_call(
        functools.partial(_inproj_kernel, slabs=ns),
        grid=(rows // tm, width // tn),
        in_specs=[
            pl.BlockSpec((tm, D_MODEL), lambda i, j: (i, 0)),
            pl.BlockSpec((1, D_MODEL), lambda i, j: (0, 0)),
            pl.BlockSpec((D_MODEL, tn), lambda i, j: (0, j)),
        ],
        out_specs=out_spec,
        out_shape=out_shape,
        scratch_shapes=[pltpu.VMEM((tm, D_MODEL), BF16)],
        compiler_params=_cparams(("parallel", "arbitrary")),
        name="inproj_qkv" if slab_out else "inproj_u",
    )(x2d, g, w_bf16)


def _t5_bucket_np(rel):
    half = NUM_BUCKETS // 2
    max_exact = half // 2
    n = np.abs(rel)
    nf = np.maximum(n, 1).astype(np.float64)
    large = max_exact + (np.log(nf / max_exact) / math.log(MAX_DISTANCE / max_exact) * (half - max_exact)).astype(np.int64)
    large = np.minimum(large, half - 1)
    return np.where(rel > 0, half, 0) + np.where(n < max_exact, n, large)


def _bias_mask(rel_bias_g, dilation):
    qi = np.arange(2 * WIN)[:, None]
    kj = np.arange(4 * WIN)[None, :] - WIN
    delta = kj - qi
    bucket = _t5_bucket_np(delta * dilation).astype(np.int32)
    onehot = jax.nn.one_hot(jnp.asarray(bucket), NUM_BUCKETS, dtype=F32)
    bias = jnp.einsum('qkb,bh->hqk', onehot, rel_bias_g.astype(F32), precision=lax.Precision.HIGHEST)
    bias = jnp.where(jnp.asarray(np.abs(delta) <= WIN)[None], bias, NEG_INF)
    return bias.reshape(HEADS_PER_GROUP // 2, 2, 2 * WIN, 4 * WIN)


ATTN_POS_PER_STEP = 4096
ATTN_TILES_PER_ITER = 2
PAIRS_PER_SLAB = 2


def _attn_kernel(q_ref, kp_ref, kc_ref, kn_ref, vp_ref, vc_ref, vn_ref, bm_ref, o_ref, l_ref,
                 kbuf, vbuf, *, tq, dil, sub_len):
    t = pl.program_id(2)
    halo = WIN * dil
    span = tq * dil
    kbuf[0:halo] = kp_ref[...]
    kbuf[halo:halo + span] = kc_ref[...]
    kbuf[halo + span:] = kn_ref[...]
    vbuf[0:halo] = vp_ref[...]
    vbuf[halo:halo + span] = vc_ref[...]
    vbuf[halo + span:] = vn_ref[...]
    lane = lax.broadcasted_iota(jnp.int32, (1, LANES), 1)
    is_lo = lane < HEAD_DIM
    kcol = lax.broadcasted_iota(jnp.int32, (1, 4 * WIN), 1)
    qt = 2 * WIN
    nq = tq // qt

    def tile(c, i):
        row0 = i * (qt * dil) + c
        kpos = t * tq + i * qt - WIN + kcol
        pen = jnp.where((kpos >= 0) & (kpos < sub_len), 0.0, NEG_INF).astype(F32)
        qw = q_ref[pl.ds(row0, qt, stride=dil), :]
        kw = kbuf[pl.ds(row0, 2 * qt, stride=dil), :]
        vw = vbuf[pl.ds(row0, 2 * qt, stride=dil), :]
        o_pair = []
        for hp in range(PAIRS_PER_SLAB):
            q = _unpack_bf16_pair(qw, hp)
            k = _unpack_bf16_pair(kw, hp)
            v = _unpack_bf16_pair(vw, hp)
            outs, lses = [], []
            for hh in range(2):
                sel = is_lo if hh == 0 else jnp.logical_not(is_lo)
                qm = jnp.where(sel, q, jnp.zeros_like(q))
                s = lax.dot_general(qm, k, (((1,), (1,)), ((), ())), preferred_element_type=F32)
                s = s + bm_ref[hp, hh] + pen
                m = jnp.max(s, axis=-1, keepdims=True)
                p = jnp.exp(s - m)
                den = jnp.sum(p, axis=-1, keepdims=True)
                pv = jnp.dot(p.astype(BF16), v, preferred_element_type=F32)
                outs.append(pv / den)
                lses.append(m + jnp.log(den))
            o_pair.append(jnp.where(is_lo, outs[0], outs[1]))
            l_ref[hp, pl.ds(row0, qt, stride=dil), :] = jnp.where(is_lo, lses[0], lses[1])
        o_ref[pl.ds(row0, qt, stride=dil), :] = _pack_bf16_pair(o_pair[0], o_pair[1])

    def body(it, carry):
        for u in range(ATTN_TILES_PER_ITER):
            idx = it * ATTN_TILES_PER_ITER + u
            tile(idx // nq, idx % nq)
        return carry

    lax.fori_loop(0, dil * nq // ATTN_TILES_PER_ITER, body, 0)


def _attention_group(qkv, rel_bias, g, batch, seq):
    _, dil = ATTN_GROUPS[g]
    sub_len = seq // dil
    tq = min(ATTN_POS_PER_STEP // dil, sub_len)
    nt = sub_len // tq
    span = tq * dil
    halo = WIN * dil
    n_hp = ATTN_WIDTH // LANES
    n_ps = n_hp // PAIRS_PER_SLAB
    n_slab = N_GROUPS * n_ps
    bm = _bias_mask(rel_bias[:, g * HEADS_PER_GROUP:(g + 1) * HEADS_PER_GROUP], dil)
    hb = tq // WIN
    nhb = sub_len // WIN

    def cur(which):
        return pl.BlockSpec((None, span, LANES), lambda b, ps, t: (which * n_slab + g * n_ps + ps, b * nt + t, 0))

    def prev(which):
        return pl.BlockSpec((None, halo, LANES),
                            lambda b, ps, t: (which * n_slab + g * n_ps + ps, b * nhb + jnp.maximum(t * hb - 1, 0), 0))

    def nxt(which):
        return pl.BlockSpec((None, halo, LANES),
                            lambda b, ps, t: (which * n_slab + g * n_ps + ps, b * nhb + jnp.minimum((t + 1) * hb, nhb - 1), 0))

    out_spec = pl.BlockSpec((PAIRS_PER_SLAB, span, LANES), lambda b, ps, t: (ps, b * nt + t, 0))
    return pl.pallas_call(
        functools.partial(_attn_kernel, tq=tq, dil=dil, sub_len=sub_len),
        grid=(batch, n_ps, nt),
        in_specs=[cur(0), prev(1), cur(1), nxt(1), prev(2), cur(2), nxt(2),
                  pl.BlockSpec((PAIRS_PER_SLAB, 2, 2 * WIN, 4 * WIN), lambda b, ps, t: (ps, 0, 0, 0))],
        out_specs=[pl.BlockSpec((None, span, LANES), lambda b, ps, t: (ps, b * nt + t, 0)), out_spec],
        out_shape=[jax.ShapeDtypeStruct((n_ps, batch * seq, LANES), jnp.uint32),
                   jax.ShapeDtypeStruct((n_hp, batch * seq, LANES), F32)],
        scratch_shapes=[pltpu.VMEM((span + 2 * halo, LANES), jnp.uint32)] * 2,
        compiler_params=_cparams(("parallel", "parallel", "arbitrary")),
        name=f"attn_g{g}",
    )(qkv, qkv, qkv, qkv, qkv, qkv, qkv, bm)


def _shortconv_kernel(p_ref, c_ref, n_ref, w_ref, b_ref, o_ref, x_buf, *, rows, halo):
    t = pl.program_id(1)
    nt = pl.num_programs(1)
    has_prev = (t > 0).astype(F32)
    has_next = (t < nt - 1).astype(F32)
    n_slab = c_ref.shape[2] // LANES
    for e in range(2):
        x = c_ref[e].astype(F32)
        lo = p_ref[e].astype(F32)[halo - 8:halo, :] * has_prev
        hi = n_ref[e].astype(F32)[0:8, :] * has_next
        for s in range(n_slab):
            cs = slice(s * LANES, (s + 1) * LANES)
            x_buf[e, s, 0:8, :] = lo[:, cs]
            x_buf[e, s, 8:rows + 8, :] = x[:, cs]
            x_buf[e, s, rows + 8:rows + 16, :] = hi[:, cs]
    nblk = rows // DFT_N2
    pad = jnp.zeros((PITCH_Z - DFT_N2, LANES), jnp.uint32)
    for s in range(n_slab):
        cs = slice(s * LANES, (s + 1) * LANES)
        w0, w1, w2, bias = w_ref[0:1, cs], w_ref[1:2, cs], w_ref[2:3, cs], b_ref[:, cs]
        res = []
        for e in range(2):
            xs = x_buf.at[e, s]
            res.append(xs[pl.ds(7, rows, stride=1), :] * w0 + xs[pl.ds(8, rows, stride=1), :] * w1
                       + xs[pl.ds(9, rows, stride=1), :] * w2 + bias)
        packed = _pack_bf16_pair(res[0], res[1])
        for jb in range(nblk):
            o_ref[s, jb * PITCH_Z:jb * PITCH_Z + DFT_N2, :] = packed[jb * DFT_N2:(jb + 1) * DFT_N2]
            o_ref[s, jb * PITCH_Z + DFT_N2:(jb + 1) * PITCH_Z, :] = pad


def _store_padded_slabs(o_ref, val, nblk):
    pad = jnp.zeros((PITCH_Z - DFT_N2, LANES), val.dtype)
    for s in range(val.shape[1] // LANES):
        for jb in range(nblk):
            o_ref[s, jb * PITCH_Z:jb * PITCH_Z + DFT_N2, :] = val[jb * DFT_N2:(jb + 1) * DFT_N2, s * LANES:(s + 1) * LANES]
            o_ref[s, jb * PITCH_Z + DFT_N2:(jb + 1) * PITCH_Z, :] = pad


def _load_padded_slabs(ref, nblk):
    return jnp.concatenate(
        [jnp.concatenate([ref[s, jb * PITCH_Z:jb * PITCH_Z + DFT_N2, :] for jb in range(nblk)], axis=0)
         for s in range(ref.shape[0])], axis=1)


def _shortconv(proj, conv_w, conv_b, batch, seq, rows=512):
    assert batch % 2 == 0
    rows = min(rows, seq)
    prow = rows // DFT_N2 * PITCH_Z
    n_slab = HYENA_WIDTH // LANES
    halo = 16
    pv = proj.reshape(batch, seq, U_WIDTH)
    cb = 0
    hb = rows // halo
    nhb = seq // halo
    return pl.pallas_call(
        functools.partial(_shortconv_kernel, rows=rows, halo=halo),
        grid=(batch // 2, seq // rows, 3),
        in_specs=[
            pl.BlockSpec((2, halo, HYENA_WIDTH), lambda b, t, j: (b, jnp.maximum(t * hb - 1, 0), cb + j)),
            pl.BlockSpec((2, rows, HYENA_WIDTH), lambda b, t, j: (b, t, cb + j)),
            pl.BlockSpec((2, halo, HYENA_WIDTH), lambda b, t, j: (b, jnp.minimum((t + 1) * hb, nhb - 1), cb + j)),
            pl.BlockSpec((3, HYENA_WIDTH), lambda b, t, j: (0, j)),
            pl.BlockSpec((1, HYENA_WIDTH), lambda b, t, j: (0, j)),
        ],
        out_specs=pl.BlockSpec((None, n_slab, None, prow, LANES), lambda b, t, j: (j, 0, b, t, 0)),
        out_shape=jax.ShapeDtypeStruct((3, n_slab, batch // 2, seq // DFT_N2 * PITCH_Z, LANES), jnp.uint32),
        scratch_shapes=[pltpu.VMEM((2, n_slab, rows + 16, LANES), F32)],
        compiler_params=_cparams(("parallel", "parallel", "arbitrary")),
        name="shortconv",
    )(pv, pv, pv, conv_w, conv_b.reshape(1, -1))


def _filter_features(length):
    t = np.linspace(0.0, 1.0, length)[:, None]
    ang = (2.0 * math.pi / length) * np.arange(length, dtype=np.float64)[:, None]
    bands = np.linspace(1e-4, N_BANDS - 1, N_BANDS)[None]
    z = np.concatenate([t, np.cos(ang * bands), -np.sin(ang * bands)], axis=-1)
    zp = np.zeros((length, LANES), np.float32)
    zp[:, :FILTER_EMB] = z
    return zp


def _decay_rates():
    max_decay = math.log(1e-2) / 0.3
    min_decay = math.log(1e-2) / 1.5
    return np.abs(np.linspace(min_decay, max_decay, HYENA_WIDTH)).astype(np.float32)[None]


def _filter_kernel(z_ref, w1_ref, b1_ref, w2_ref, b2_ref, w3_ref, b3_ref, w4_ref, fr_ref, dl_ref, o_ref, *, rows):
    hi = lax.Precision.HIGHEST
    z = z_ref[...]
    fr = fr_ref[...]
    hr = rows // 2
    pre = jnp.concatenate([jnp.dot(z[:hr], w1_ref[...], precision=hi, preferred_element_type=F32),
                           jnp.dot(z[hr:], w1_ref[...], precision=hi, preferred_element_type=F32)], axis=1)
    h = jnp.sin(fr * (pre + b1_ref[...]))
    h = jnp.sin(fr * (jnp.dot(h, w2_ref[...], precision=hi, preferred_element_type=F32) + b2_ref[...]))
    h = jnp.sin(fr * (jnp.dot(h, w3_ref[...], precision=hi, preferred_element_type=F32) + b3_ref[...]))
    h = jnp.concatenate([h[:, :FILTER_HIDDEN], h[:, FILTER_HIDDEN:]], axis=0)
    hh, hl = _split(h)
    filt = jnp.dot(jnp.concatenate([hh, hl, hh], axis=1), w4_ref[...], preferred_element_type=F32)
    decay = jnp.exp(-z[:, 0:1] * dl_ref[...])
    row = pl.program_id(0) * rows + lax.broadcasted_iota(jnp.int32, (rows, 1), 0)
    for j in range(4):
        cs = slice(j * HYENA_WIDTH, (j + 1) * HYENA_WIDTH)
        val = filt[:, cs] * decay
        if j % 2 == 1:
            val = jnp.where(row == 0, 0.0, val)
        _store_padded_slabs(o_ref.at[j], val, rows // DFT_N2)


def _filters(length, fw1, fb1, fw2, fb2, fw3, fb3, fw4, ffreq, rows=256):
    rows = min(rows, length)
    n_slab = HYENA_WIDTH // LANES
    prow = rows // DFT_N2 * PITCH_Z
    w4h, w4l = _split(fw4.astype(F32))
    w4s = jnp.concatenate([w4h, w4h, w4l], axis=0)
    zfeat = jnp.asarray(_filter_features(length))
    w1p = jnp.zeros((LANES, FILTER_HIDDEN), F32).at[:FILTER_EMB].set(fw1.astype(F32))
    twice = lambda v: jnp.tile(v.astype(F32).reshape(1, -1), (1, 2))
    zero = jnp.zeros((FILTER_HIDDEN, FILTER_HIDDEN), F32)
    bdiag = lambda w: jnp.block([[w.astype(F32), zero], [zero, w.astype(F32)]])
    full = lambda shape: pl.BlockSpec(shape, lambda i: (0,) * len(shape))
    return pl.pallas_call(
        functools.partial(_filter_kernel, rows=rows),
        grid=(length // rows,),
        in_specs=[pl.BlockSpec((rows, LANES), lambda i: (i, 0)),
                  full((LANES, FILTER_HIDDEN)), full((1, LANES)),
                  full((LANES, LANES)), full((1, LANES)),
                  full((LANES, LANES)), full((1, LANES)),
                  full((3 * FILTER_HIDDEN, 4 * HYENA_WIDTH)), full((1, LANES)),
                  full((1, HYENA_WIDTH))],
        out_specs=pl.BlockSpec((4, n_slab, prow, LANES), lambda i: (0, 0, i, 0)),
        out_shape=jax.ShapeDtypeStruct((4, n_slab, length // DFT_N2 * PITCH_Z, LANES), F32),
        compiler_params=_cparams(("parallel",)),
        name="hyena_filters",
    )(zfeat, w1p, twice(fb1), bdiag(fw2), twice(fb2), bdiag(fw3), twice(fb3), w4s,
      twice(ffreq), jnp.asarray(_decay_rates()))


def _dft_tables(length):
    n = 2 * length
    n1 = n // DFT_N2
    h1 = n1 // 2
    kk = np.arange(h1)[:, None] + 0.5
    th = 2.0 * math.pi * kk * np.arange(h1)[None, :] / n1
    fa = np.concatenate([np.cos(th), -np.sin(th)], axis=0)
    ga = (2.0 / n) * fa.T
    k = np.arange(h1)[:, None, None] + n1 * np.arange(DFT_N2)[None, :, None] + 0.5
    ph = 2.0 * math.pi * k * np.arange(DFT_N2)[None, None, :] / n
    f = lambda a: jnp.asarray(a.astype(np.float32))
    er, ei = f(np.cos(ph)), f(-np.sin(ph))
    m = jnp.concatenate([jnp.concatenate([er, -ei], axis=2), jnp.concatenate([ei, er], axis=2)], axis=1)
    return f(fa), f(ga), m.astype(BF16)


def _split(a):
    hi = a.astype(BF16)
    lo = (a - hi.astype(F32)).astype(BF16)
    return hi, lo


CONV_K1_CHUNK = 64
CONV_PAIR_UNROLL = 8
CONV_FREQ_UNROLL = 8


def _resident_table_spec(h1):
    return pl.BlockSpec((h1, 2 * DFT_N2, 2 * DFT_N2), lambda *_: (0, 0, 0), pipeline_mode=pl.Buffered(1))


def _spectrum_kernel(fa_ref, m_ref, xf_ref, xb_ref, o_ref, s_ref, *, h1, kc):
    hf = pl.program_id(2)
    half = DFT_N2

    @pl.when(hf == 0)
    def _():
        fa = fa_ref[...]

        def body(n2, carry):
            x = jnp.concatenate([xf_ref[pl.ds(n2, h1, stride=PITCH_Z), :],
                                 xb_ref[pl.ds(n2, h1, stride=PITCH_Z), :]], axis=1)
            xh, xl = _split(x)
            a = jnp.dot(fa, jnp.concatenate([xh, xh, xl], axis=0), preferred_element_type=F32)
            aw = _pack_bf16_pair(a[:, :LANES], a[:, LANES:])
            for ri in range(2):
                s_ref[pl.ds(ri * half + n2, h1, stride=PITCH_S), :] = aw[ri * h1:(ri + 1) * h1]
            return carry

        lax.fori_loop(0, DFT_N2, body, 0, unroll=CONV_PAIR_UNROLL)

    def freq_body(k1, carry):
        row = pl.multiple_of((hf * kc + k1) * PITCH_S, 8)
        xw = s_ref[pl.ds(row, 2 * half), :]
        x = jnp.concatenate([_unpack_bf16_pair(xw, 0), _unpack_bf16_pair(xw, 1)], axis=1)
        z = jnp.dot(m_ref[hf * kc + k1], x, preferred_element_type=F32)
        o_ref[k1] = _pack_bf16_pair(z[:half, :LANES] + z[:half, LANES:], z[half:, :LANES] - z[half:, LANES:])
        return carry

    lax.fori_loop(0, kc, freq_body, 0, unroll=CONV_FREQ_UNROLL)


def _filter_spectrum(fa, m_bf16, filt):
    _, n_slab, prow, _ = filt.shape
    n1, h1 = fa.shape
    kc = min(CONV_K1_CHUNK, h1)
    fh, fl = _split(fa)
    fa3 = jnp.concatenate([fh, fl, fh], axis=1)
    return pl.pallas_call(
        functools.partial(_spectrum_kernel, h1=h1, kc=kc),
        grid=(2, n_slab, h1 // kc),
        in_specs=[pl.BlockSpec((n1, 3 * h1), lambda o, s, hf: (0, 0)),
                  _resident_table_spec(h1),
                  pl.BlockSpec((None, None, prow, LANES), lambda o, s, hf: (2 * o, s, 0, 0)),
                  pl.BlockSpec((None, None, prow, LANES), lambda o, s, hf: (2 * o + 1, s, 0, 0))],
        out_specs=pl.BlockSpec((None, None, kc, DFT_N2, LANES), lambda o, s, hf: (o, s, hf, 0, 0)),
        out_shape=jax.ShapeDtypeStruct((2, n_slab, h1, DFT_N2, LANES), jnp.uint32),
        scratch_shapes=[pltpu.VMEM((h1 * PITCH_S, LANES), jnp.uint32)],
        compiler_params=pltpu.CompilerParams(dimension_semantics=("parallel", "parallel", "arbitrary"),
                                             vmem_limit_bytes=CONV_VMEM_LIMIT),
        name="filter_spectrum",
    )(fa3, m_bf16, filt, filt)


def _conv_kernel(fa_ref, ga_ref, m_ref, k_ref, z_ref, hx_ref, sk_ref, o_ref, s_ref, *, h1):
    half = DFT_N2
    pairs = DFT_N2 // 2
    fa = fa_ref[...]
    ga = ga_ref[...]
    sk = sk_ref[...]

    def stage_a(j, carry):
        n2 = 2 * j
        zw = [z_ref[pl.ds(n2 + u, h1, stride=PITCH_Z), :] for u in range(2)]
        a = [jnp.dot(fa, jnp.concatenate([_unpack_bf16_pair(zw[0], e), _unpack_bf16_pair(zw[1], e)], axis=1),
                     preferred_element_type=F32) for e in range(2)]
        aw = _pack_bf16_pair(a[0], a[1])
        for ri in range(2):
            for u in range(2):
                s_ref[pl.ds(ri * half + n2 + u, h1, stride=PITCH_S), :] = (
                    aw[ri * h1:(ri + 1) * h1, u * LANES:(u + 1) * LANES])
        return carry

    lax.fori_loop(0, pairs, stage_a, 0, unroll=CONV_PAIR_UNROLL)

    def freq_body(k1, carry):
        row = pl.multiple_of(k1 * PITCH_S, 8)
        xw = s_ref[pl.ds(row, 2 * half), :]
        x = jnp.concatenate([_unpack_bf16_pair(xw, 0), _unpack_bf16_pair(xw, 1)], axis=1)
        m = m_ref[k1]
        z = jnp.dot(m, x, preferred_element_type=F32)
        zr, zi = z[:half], z[half:]
        kw = k_ref[k1]
        kr = jnp.concatenate([_unpack_bf16_pair(kw, 0, F32)] * 2, axis=1)
        ki = jnp.concatenate([_unpack_bf16_pair(kw, 1, F32)] * 2, axis=1)
        y = jnp.concatenate([zr * kr - zi * ki, zr * ki + zi * kr], axis=0).astype(BF16)
        bq = lax.dot_general(m, y, (((0,), (0,)), ((), ())), preferred_element_type=F32)
        s_ref[pl.ds(row, 2 * half), :] = _pack_bf16_pair(bq[:, :LANES], bq[:, LANES:])
        return carry

    lax.fori_loop(0, h1, freq_body, 0, unroll=CONV_FREQ_UNROLL)

    def stage_inv(j, carry):
        n2 = 2 * j
        rw = [s_ref[pl.ds(n2 + u, h1, stride=PITCH_S), :] for u in range(2)]
        iw = [s_ref[pl.ds(half + n2 + u, h1, stride=PITCH_S), :] for u in range(2)]
        zw = [z_ref[pl.ds(n2 + u, h1, stride=PITCH_Z), :] for u in range(2)]
        hw = [hx_ref[pl.ds(n2 + u, h1, stride=PITCH_Z), :] for u in range(2)]
        y = []
        for e in range(2):
            bn = jnp.concatenate(
                [jnp.concatenate([_unpack_bf16_pair(rw[u], e), _unpack_bf16_pair(iw[u], e)], axis=0)
                 for u in range(2)], axis=1)
            y.append(jnp.dot(ga, bn, preferred_element_type=F32))
        for u in range(2):
            res = [_unpack_bf16_pair(hw[u], e, F32) * (y[e][:, u * LANES:(u + 1) * LANES]
                                                       + sk * _unpack_bf16_pair(zw[u], e, F32)) for e in range(2)]
            o_ref[pl.ds(n2 + u, h1, stride=PITCH_Z), :] = _pack_bf16_pair(res[0], res[1])
        return carry

    lax.fori_loop(0, pairs, stage_inv, 0, unroll=CONV_PAIR_UNROLL // 2)
    pad = jnp.zeros((PITCH_Z - DFT_N2, LANES), jnp.uint32)
    for blk in range(h1):
        o_ref[blk * PITCH_Z + DFT_N2:(blk + 1) * PITCH_Z, :] = pad


def _long_conv_gate(tabs, kspec, order, z5, zsel, hx5, hxsel, skip):
    fa, ga, m_tab = tabs
    _, n_slab, n_pair, prow, _ = z5.shape
    n1, h1 = fa.shape
    slab = lambda sel: pl.BlockSpec((None, None, None, prow, LANES), lambda s, b: (sel, s, b, 0, 0))
    return pl.pallas_call(
        functools.partial(_conv_kernel, h1=h1),
        grid=(n_slab, n_pair),
        in_specs=[pl.BlockSpec((n1, h1), lambda s, b: (0, 0)),
                  pl.BlockSpec((h1, n1), lambda s, b: (0, 0)),
                  _resident_table_spec(h1),
                  pl.BlockSpec((None, None, h1, DFT_N2, LANES), lambda s, b: (order, s, 0, 0, 0)),
                  slab(zsel), slab(hxsel),
                  pl.BlockSpec((None, None, 1, LANES), lambda s, b: (order, s, 0, 0))],
        out_specs=pl.BlockSpec((None, None, prow, LANES), lambda s, b: (s, b, 0, 0)),
        out_shape=jax.ShapeDtypeStruct((n_slab, n_pair, prow, LANES), jnp.uint32),
        scratch_shapes=[pltpu.VMEM((h1 * PITCH_S, LANES), jnp.uint32)],
        compiler_params=pltpu.CompilerParams(dimension_semantics=("parallel", "arbitrary"),
                                             vmem_limit_bytes=CONV_VMEM_LIMIT),
        name="hyena_long_conv",
    )(fa.astype(BF16), ga.astype(BF16), m_tab, kspec, z5, hx5, skip)


def _hyena(proj, batch, seq, conv_w, conv_b, fparams, skip):
    c = HYENA_WIDTH
    fa, ga, m_tab = _dft_tables(seq)
    kspec = _filter_spectrum(fa, m_tab, _filters(seq, *fparams))
    u5 = _shortconv(proj, conv_w, conv_b, batch, seq)
    skip4 = skip.astype(F32).reshape(2, c // LANES, 1, LANES)
    tabs = (fa, ga, m_tab)
    z1 = _long_conv_gate(tabs, kspec, 0, u5, 0, u5, 1, skip4)
    return _long_conv_gate(tabs, kspec, 1, z1[None], 0, u5, 2, skip4)


def _final_kernel(o1_ref, o2_ref, o3_ref, l1_ref, l2_ref, l3_ref, hy_ref, x_ref, g_ref, wg_ref,
                  wa_ref, wh_ref, wo_ref, pg_ref, y_ref, *, tiles_per_seq):
    def cat(ref):
        return jnp.concatenate([ref[s] for s in range(ATTN_WIDTH // LANES)], axis=1)

    def cat_packed(ref):
        return jnp.concatenate([_unpack_bf16_pair(ref[s], e, F32)
                                for s in range(ref.shape[0]) for e in range(PAIRS_PER_SLAB)], axis=1)

    l1, l2, l3 = cat(l1_ref), cat(l2_ref), cat(l3_ref)
    mx = jnp.maximum(jnp.maximum(l1, l2), l3)
    e1, e2, e3 = jnp.exp(l1 - mx), jnp.exp(l2 - mx), jnp.exp(l3 - mx)
    attn = (e1 * cat_packed(o1_ref) + e2 * cat_packed(o2_ref) + e3 * cat_packed(o3_ref)) / (e1 + e2 + e3)
    x = x_ref[...]
    hn = (x * lax.rsqrt(jnp.mean(x * x, axis=-1, keepdims=True) + EPS) * g_ref[...]).astype(BF16)
    gates = jnp.dot(hn, wg_ref[...], preferred_element_type=F32)
    gh, ga = gates[:, G_GH:G_MA], gates[:, G_GA:]
    a_in = (attn * (ga * jax.nn.sigmoid(ga))).astype(BF16)
    a_br = jnp.dot(a_in, wa_ref[...], preferred_element_type=F32)
    odd = (pl.program_id(0) // tiles_per_seq) % 2
    hw = _load_padded_slabs(hy_ref, x.shape[0] // DFT_N2)
    hy = lax.bitcast_convert_type((hw << ((1 - odd) * 16).astype(jnp.uint32)) & jnp.uint32(0xFFFF0000), F32)
    h_in = (hy * (gh * jax.nn.sigmoid(gh))).astype(BF16)
    h_br = jnp.dot(h_in, wh_ref[...], preferred_element_type=F32)
    merged = jax.nn.sigmoid(gates[:, G_MA:G_MH]) * a_br + jax.nn.sigmoid(gates[:, G_MH:G_GA]) * h_br
    out = jnp.dot(merged.astype(BF16), wo_ref[...], preferred_element_type=F32)
    ms = jnp.mean(out * out, axis=-1, keepdims=True)
    y_ref[...] = x + out * lax.rsqrt(ms + EPS) * pg_ref[...]


def _final(os_, ls_, hy, x2d, gain, wg, wa, wh, wo, pg, seq, tm=512):
    rows = x2d.shape[0]
    tm = min(tm, rows)
    tiles_per_seq = seq // tm
    hy_spec = pl.BlockSpec((HYENA_WIDTH // LANES, None, tm // DFT_N2 * PITCH_Z, LANES),
                           lambda i: (0, i // (2 * tiles_per_seq), i % tiles_per_seq, 0))
    slab = pl.BlockSpec((ATTN_WIDTH // LANES, tm, LANES), lambda i: (0, i, 0))
    oslab = pl.BlockSpec((ATTN_WIDTH // LANES // PAIRS_PER_SLAB, tm, LANES), lambda i: (0, i, 0))
    full = lambda shape: pl.BlockSpec(shape, lambda i: (0, 0), pipeline_mode=pl.Buffered(1))
    return pl.pallas_call(
        functools.partial(_final_kernel, tiles_per_seq=tiles_per_seq),
        grid=(rows // tm,),
        in_specs=[oslab] * 3 + [slab] * 3 + [hy_spec, pl.BlockSpec((tm, D_MODEL), lambda i: (i, 0)),
                                             full((1, D_MODEL)), full((D_MODEL, GATE_WIDTH)),
                                             full((ATTN_WIDTH, D_MODEL)), full((D_MODEL, D_MODEL)),
                                             full((D_MODEL, D_MODEL)), full((1, D_MODEL))],
        out_specs=pl.BlockSpec((tm, D_MODEL), lambda i: (i, 0)),
        out_shape=jax.ShapeDtypeStruct((rows, D_MODEL), F32),
        compiler_params=_cparams(("parallel",)),
        name="merge_out",
    )(*os_, *ls_, hy, x2d, gain, wg, wa, wh, wo, pg)


def _layer(x, rel_bias, pre_g, post_g, w_in, conv_w, conv_b, fparams, skip, w_br_a, w_br_h, w_out):
    batch, seq, _ = x.shape
    x2d = x.reshape(batch * seq, D_MODEL)
    w_qkv, w_u, w_gate = w_in
    gain = pre_g.reshape(1, -1).astype(F32)
    qkv = _inproj(x2d, gain, w_qkv, tn=QKV_WIDTH // 2, slab_out=True)
    proj = _inproj(x2d, gain, w_u, tn=U_WIDTH, slab_out=False)
    os_, ls_ = [], []
    for g in range(N_GROUPS):
        o, l = _attention_group(qkv, rel_bias, g, batch, seq)
        os_.append(o)
        ls_.append(l)
    hy = _hyena(proj, batch, seq, conv_w, conv_b, fparams, skip)
    y = _final(os_, ls_, hy, x2d, gain, w_gate, w_br_a, w_br_h, w_out, post_g.reshape(1, -1).astype(F32), seq)
    return y.reshape(batch, seq, D_MODEL)


def kernel(x_prompt, x_sample, rel_bias, pre_norm_g, post_norm_g, w_in, conv_w, conv_b, filt_w1, filt_b1, filt_w2, filt_b2, filt_w3, filt_b3, filt_w4, filt_freq, hyena_skip, w_branch_a, w_branch_h, w_out):
    depth = w_in.shape[0]

    def run(x):
        for l in range(depth):
            fparams = (filt_w1[l], filt_b1[l], filt_w2[l], filt_b2[l], filt_w3[l], filt_b3[l], filt_w4[l], filt_freq[l])
            w = w_in[l]
            col_scale = jnp.where(jnp.arange(QKV_WIDTH) < QKV_WIDTH // 3, HEAD_DIM ** -0.5, 1.0).astype(F32)
            w_qkv = (w[:, :QKV_WIDTH] * col_scale).astype(BF16)
            w_u = w[:, U_OFF:GH_OFF].astype(BF16)
            w_gate = jnp.concatenate([w[:, GH_OFF:], w[:, GA_OFF:U_OFF]], axis=1).astype(BF16)
            x = _layer(x, rel_bias, pre_norm_g[l], post_norm_g[l], (w_qkv, w_u, w_gate), conv_w[l], conv_b[l],
                       fparams, hyena_skip[l], w_branch_a[l].astype(BF16), w_branch_h[l].astype(BF16),
                       w_out[l].astype(BF16))
        return x

    return (run(x_prompt), run(x_sample))
```

```python
import functools
import math

import numpy as np
import jax
import jax.numpy as jnp
from jax import lax
from jax.experimental import pallas as pl
from jax.experimental.pallas import tpu as pltpu

F32 = jnp.float32
BF16 = jnp.bfloat16

D_MODEL = 1024
EPS = 1e-6
HEAD_DIM = 64
ATTN_GROUPS = ((128, 1), (512, 4), (2048, 16))
N_GROUPS = 3
HEADS_PER_GROUP = 8
ATTN_WIDTH = HEADS_PER_GROUP * HEAD_DIM
HYENA_WIDTH = 1024
FILTER_EMB = 33
N_BANDS = 16
FILTER_HIDDEN = 64
NUM_BUCKETS = 32
MAX_DISTANCE = 1024
NEG_INF = -1e30
QKV_WIDTH = 4608
GA_OFF, U_OFF, GH_OFF = 4608, 5120, 8192
U_WIDTH = 3072
GATE_WIDTH = 3584
G_GH, G_MA, G_MH, G_GA = 0, 1024, 2048, 3072

WIN = 64
LANES = 128
DFT_N2 = 128
PITCH_Z = 136
PITCH_S = 264
VMEM_LIMIT = 48 * 1024 * 1024
CONV_VMEM_LIMIT = 58 * 1024 * 1024


def _cparams(sem):
    return pltpu.CompilerParams(dimension_semantics=sem, vmem_limit_bytes=VMEM_LIMIT)


def _inproj_kernel(x_ref, g_ref, w_ref, o_ref, hn_ref, *, slabs):
    @pl.when(pl.program_id(1) == 0)
    def _():
        x = x_ref[...]
        ms = jnp.mean(x * x, axis=-1, keepdims=True)
        hn_ref[...] = (x * lax.rsqrt(ms + EPS) * g_ref[...]).astype(BF16)

    acc = jnp.dot(hn_ref[...], w_ref[...], preferred_element_type=F32)
    if slabs:
        for s in range(slabs):
            o_ref[s] = _pack_bf16_pair(acc[:, (2 * s) * LANES:(2 * s + 1) * LANES],
                                       acc[:, (2 * s + 1) * LANES:(2 * s + 2) * LANES])
    else:
        o_ref[...] = acc.astype(o_ref.dtype)


def _pack_bf16_pair(a, b):
    ua = lax.bitcast_convert_type(a.astype(BF16).astype(F32), jnp.uint32)
    ub = lax.bitcast_convert_type(b.astype(BF16).astype(F32), jnp.uint32)
    return (ua >> 16) | (ub & jnp.uint32(0xFFFF0000))


def _unpack_bf16_pair(w, idx, dtype=BF16):
    bits = (w << 16) if idx == 0 else (w & jnp.uint32(0xFFFF0000))
    return lax.bitcast_convert_type(bits, F32).astype(dtype)


def _inproj(x2d, g, w_bf16, tn, slab_out, tm=1024):
    rows = x2d.shape[0]
    width = w_bf16.shape[1]
    tm = min(tm, rows)
    if slab_out:
        ns = tn // (2 * LANES)
        out_spec = pl.BlockSpec((ns, tm, LANES), lambda i, j: (j, i, 0))
        out_shape = jax.ShapeDtypeStruct((width // (2 * LANES), rows, LANES), jnp.uint32)
    else:
        ns = 0
        out_spec = pl.BlockSpec((tm, tn), lambda i, j: (i, j))
        out_shape = jax.ShapeDtypeStruct((rows, width), BF16)
    return pl.pallas_call(
        functools.partial(_inproj_kernel, slabs=ns),
        grid=(rows // tm, width // tn),
        in_specs=[
            pl.BlockSpec((tm, D_MODEL), lambda i, j: (i, 0)),
            pl.BlockSpec((1, D_MODEL), lambda i, j: (0, 0)),
            pl.BlockSpec((D_MODEL, tn), lambda i, j: (0, j)),
        ],
        out_specs=out_spec,
        out_shape=out_shape,
        scratch_shapes=[pltpu.VMEM((tm, D_MODEL), BF16)],
        compiler_params=_cparams(("parallel", "arbitrary")),
        name="inproj_qkv" if slab_out else "inproj_u",
    )(x2d, g, w_bf16)


def _t5_bucket_np(rel):
    half = NUM_BUCKETS // 2
    max_exact = half // 2
    n = np.abs(rel)
    nf = np.maximum(n, 1).astype(np.float64)
    large = max_exact + (np.log(nf / max_exact) / math.log(MAX_DISTANCE / max_exact) * (half - max_exact)).astype(np.int64)
    large = np.minimum(large, half - 1)
    return np.where(rel > 0, half, 0) + np.where(n < max_exact, n, large)


def _bias_mask(rel_bias_g, dilation):
    qi = np.arange(2 * WIN)[:, None]
    kj = np.arange(4 * WIN)[None, :] - WIN
    delta = kj - qi
    bucket = _t5_bucket_np(delta * dilation).astype(np.int32)
    onehot = jax.nn.one_hot(jnp.asarray(bucket), NUM_BUCKETS, dtype=F32)
    bias = jnp.einsum('qkb,bh->hqk', onehot, rel_bias_g.astype(F32), precision=lax.Precision.HIGHEST)
    bias = jnp.where(jnp.asarray(np.abs(delta) <= WIN)[None], bias, NEG_INF)
    return bias.reshape(HEADS_PER_GROUP // 2, 2, 2 * WIN, 4 * WIN)


ATTN_POS_PER_STEP = 4096
ATTN_TILES_PER_ITER = 2
PAIRS_PER_SLAB = 2


def _attn_kernel(q_ref, kp_ref, kc_ref, kn_ref, vp_ref, vc_ref, vn_ref, bm_ref, o_ref, l_ref,
                 kbuf, vbuf, *, tq, dil, sub_len):
    t = pl.program_id(2)
    halo = WIN * dil
    span = tq * dil
    kbuf[0:halo] = kp_ref[...]
    kbuf[halo:halo + span] = kc_ref[...]
    kbuf[halo + span:] = kn_ref[...]
    vbuf[0:halo] = vp_ref[...]
    vbuf[halo:halo + span] = vc_ref[...]
    vbuf[halo + span:] = vn_ref[...]
    lane = lax.broadcasted_iota(jnp.int32, (1, LANES), 1)
    is_lo = lane < HEAD_DIM
    kcol = lax.broadcasted_iota(jnp.int32, (1, 4 * WIN), 1)
    qt = 2 * WIN
    nq = tq // qt

    def tile(c, i):
        row0 = i * (qt * dil) + c
        kpos = t * tq + i * qt - WIN + kcol
        pen = jnp.where((kpos >= 0) & (kpos < sub_len), 0.0, NEG_INF).astype(F32)
        qw = q_ref[pl.ds(row0, qt, stride=dil), :]
        kw = kbuf[pl.ds(row0, 2 * qt, stride=dil), :]
        vw = vbuf[pl.ds(row0, 2 * qt, stride=dil), :]
        o_pair = []
        for hp in range(PAIRS_PER_SLAB):
            q = _unpack_bf16_pair(qw, hp)
            k = _unpack_bf16_pair(kw, hp)
            v = _unpack_bf16_pair(vw, hp)
            outs, lses = [], []
            for hh in range(2):
                sel = is_lo if hh == 0 else jnp.logical_not(is_lo)
                qm = jnp.where(sel, q, jnp.zeros_like(q))
                s = lax.dot_general(qm, k, (((1,), (1,)), ((), ())), preferred_element_type=F32)
                s = s + bm_ref[hp, hh] + pen
                m = jnp.max(s, axis=-1, keepdims=True)
                p = jnp.exp(s - m)
                den = jnp.sum(p, axis=-1, keepdims=True)
                pv = jnp.dot(p.astype(BF16), v, preferred_element_type=F32)
                outs.append(pv / den)
                lses.append(m + jnp.log(den))
            o_pair.append(jnp.where(is_lo, outs[0], outs[1]))
            l_ref[hp, pl.ds(row0, qt, stride=dil), :] = jnp.where(is_lo, lses[0], lses[1])
        o_ref[pl.ds(row0, qt, stride=dil), :] = _pack_bf16_pair(o_pair[0], o_pair[1])

    def body(it, carry):
        for u in range(ATTN_TILES_PER_ITER):
            idx = it * ATTN_TILES_PER_ITER + u
            tile(idx // nq, idx % nq)
        return carry

    lax.fori_loop(0, dil * nq // ATTN_TILES_PER_ITER, body, 0)


def _attention_group(qkv, rel_bias, g, batch, seq):
    _, dil = ATTN_GROUPS[g]
    sub_len = seq // dil
    tq = min(ATTN_POS_PER_STEP // dil, sub_len)
    nt = sub_len // tq
    span = tq * dil
    halo = WIN * dil
    n_hp = ATTN_WIDTH // LANES
    n_ps = n_hp // PAIRS_PER_SLAB
    n_slab = N_GROUPS * n_ps
    bm = _bias_mask(rel_bias[:, g * HEADS_PER_GROUP:(g + 1) * HEADS_PER_GROUP], dil)
    hb = tq // WIN
    nhb = sub_len // WIN

    def cur(which):
        return pl.BlockSpec((None, span, LANES), lambda b, ps, t: (which * n_slab + g * n_ps + ps, b * nt + t, 0))

    def prev(which):
        return pl.BlockSpec((None, halo, LANES),
                            lambda b, ps, t: (which * n_slab + g * n_ps + ps, b * nhb + jnp.maximum(t * hb - 1, 0), 0))

    def nxt(which):
        return pl.BlockSpec((None, halo, LANES),
                            lambda b, ps, t: (which * n_slab + g * n_ps + ps, b * nhb + jnp.minimum((t + 1) * hb, nhb - 1), 0))

    out_spec = pl.BlockSpec((PAIRS_PER_SLAB, span, LANES), lambda b, ps, t: (ps, b * nt + t, 0))
    return pl.pallas_call(
        functools.partial(_attn_kernel, tq=tq, dil=dil, sub_len=sub_len),
        grid=(batch, n_ps, nt),
        in_specs=[cur(0), prev(1), cur(1), nxt(1), prev(2), cur(2), nxt(2),
                  pl.BlockSpec((PAIRS_PER_SLAB, 2, 2 * WIN, 4 * WIN), lambda b, ps, t: (ps, 0, 0, 0))],
        out_specs=[pl.BlockSpec((None, span, LANES), lambda b, ps, t: (ps, b * nt + t, 0)), out_spec],
        out_shape=[jax.ShapeDtypeStruct((n_ps, batch * seq, LANES), jnp.uint32),
                   jax.ShapeDtypeStruct((n_hp, batch * seq, LANES), F32)],
        scratch_shapes=[pltpu.VMEM((span + 2 * halo, LANES), jnp.uint32)] * 2,
        compiler_params=_cparams(("parallel", "parallel", "arbitrary")),
        name=f"attn_g{g}",
    )(qkv, qkv, qkv, qkv, qkv, qkv, qkv, bm)


def _shortconv_kernel(p_ref, c_ref, n_ref, w_ref, b_ref, o_ref, x_buf, *, rows, halo):
    t = pl.program_id(1)
    nt = pl.num_programs(1)
    has_prev = (t > 0).astype(F32)
    has_next = (t < nt - 1).astype(F32)
    n_slab = c_ref.shape[2] // LANES
    for e in range(2):
        x = c_ref[e].astype(F32)
        lo = p_ref[e].astype(F32)[halo - 8:halo, :] * has_prev
        hi = n_ref[e].astype(F32)[0:8, :] * has_next
        for s in range(n_slab):
            cs = slice(s * LANES, (s + 1) * LANES)
            x_buf[e, s, 0:8, :] = lo[:, cs]
            x_buf[e, s, 8:rows + 8, :] = x[:, cs]
            x_buf[e, s, rows + 8:rows + 16, :] = hi[:, cs]
    nblk = rows // DFT_N2
    pad = jnp.zeros((PITCH_Z - DFT_N2, LANES), jnp.uint32)
    for s in range(n_slab):
        cs = slice(s * LANES, (s + 1) * LANES)
        w0, w1, w2, bias = w_ref[0:1, cs], w_ref[1:2, cs], w_ref[2:3, cs], b_ref[:, cs]
        res = []
        for e in range(2):
            xs = x_buf.at[e, s]
            res.append(xs[pl.ds(7, rows, stride=1), :] * w0 + xs[pl.ds(8, rows, stride=1), :] * w1
                       + xs[pl.ds(9, rows, stride=1), :] * w2 + bias)
        packed = _pack_bf16_pair(res[0], res[1])
        for jb in range(nblk):
            o_ref[s, jb * PITCH_Z:jb * PITCH_Z + DFT_N2, :] = packed[jb * DFT_N2:(jb + 1) * DFT_N2]
            o_ref[s, jb * PITCH_Z + DFT_N2:(jb + 1) * PITCH_Z, :] = pad


def _store_padded_slabs(o_ref, val, nblk):
    pad = jnp.zeros((PITCH_Z - DFT_N2, LANES), val.dtype)
    for s in range(val.shape[1] // LANES):
        for jb in range(nblk):
            o_ref[s, jb * PITCH_Z:jb * PITCH_Z + DFT_N2, :] = val[jb * DFT_N2:(jb + 1) * DFT_N2, s * LANES:(s + 1) * LANES]
            o_ref[s, jb * PITCH_Z + DFT_N2:(jb + 1) * PITCH_Z, :] = pad


def _load_padded_slabs(ref, nblk):
    return jnp.concatenate(
        [jnp.concatenate([ref[s, jb * PITCH_Z:jb * PITCH_Z + DFT_N2, :] for jb in range(nblk)], axis=0)
         for s in range(ref.shape[0])], axis=1)


def _shortconv(proj, conv_w, conv_b, batch, seq, rows=512):
    assert batch % 2 == 0
    rows = min(rows, seq)
    prow = rows // DFT_N2 * PITCH_Z
    n_slab = HYENA_WIDTH // LANES
    halo = 16
    pv = proj.reshape(batch, seq, U_WIDTH)
    cb = 0
    hb = rows // halo
    nhb = seq // halo
    return pl.pallas_call(
        functools.partial(_shortconv_kernel, rows=rows, halo=halo),
        grid=(batch // 2, seq // rows, 3),
        in_specs=[
            pl.BlockSpec((2, halo, HYENA_WIDTH), lambda b, t, j: (b, jnp.maximum(t * hb - 1, 0), cb + j)),
            pl.BlockSpec((2, rows, HYENA_WIDTH), lambda b, t, j: (b, t, cb + j)),
            pl.BlockSpec((2, halo, HYENA_WIDTH), lambda b, t, j: (b, jnp.minimum((t + 1) * hb, nhb - 1), cb + j)),
            pl.BlockSpec((3, HYENA_WIDTH), lambda b, t, j: (0, j)),
            pl.BlockSpec((1, HYENA_WIDTH), lambda b, t, j: (0, j)),
        ],
        out_specs=pl.BlockSpec((None, n_slab, None, prow, LANES), lambda b, t, j: (j, 0, b, t, 0)),
        out_shape=jax.ShapeDtypeStruct((3, n_slab, batch // 2, seq // DFT_N2 * PITCH_Z, LANES), jnp.uint32),
        scratch_shapes=[pltpu.VMEM((2, n_slab, rows + 16, LANES), F32)],
        compiler_params=_cparams(("parallel", "parallel", "arbitrary")),
        name="shortconv",
    )(pv, pv, pv, conv_w, conv_b.reshape(1, -1))


def _filter_features(length):
    t = np.linspace(0.0, 1.0, length)[:, None]
    ang = (2.0 * math.pi / length) * np.arange(length, dtype=np.float64)[:, None]
    bands = np.linspace(1e-4, N_BANDS - 1, N_BANDS)[None]
    z = np.concatenate([t, np.cos(ang * bands), -np.sin(ang * bands)], axis=-1)
    zp = np.zeros((length, LANES), np.float32)
    zp[:, :FILTER_EMB] = z
    return zp


def _decay_rates():
    max_decay = math.log(1e-2) / 0.3
    min_decay = math.log(1e-2) / 1.5
    return np.abs(np.linspace(min_decay, max_decay, HYENA_WIDTH)).astype(np.float32)[None]


def _filter_kernel(z_ref, w1_ref, b1_ref, w2_ref, b2_ref, w3_ref, b3_ref, w4_ref, fr_ref, dl_ref, o_ref, *, rows):
    hi = lax.Precision.HIGHEST
    z = z_ref[...]
    fr = fr_ref[...]
    hr = rows // 2
    pre = jnp.concatenate([jnp.dot(z[:hr], w1_ref[...], precision=hi, preferred_element_type=F32),
                           jnp.dot(z[hr:], w1_ref[...], precision=hi, preferred_element_type=F32)], axis=1)
    h = jnp.sin(fr * (pre + b1_ref[...]))
    h = jnp.sin(fr * (jnp.dot(h, w2_ref[...], precision=hi, preferred_element_type=F32) + b2_ref[...]))
    h = jnp.sin(fr * (jnp.dot(h, w3_ref[...], precision=hi, preferred_element_type=F32) + b3_ref[...]))
    h = jnp.concatenate([h[:, :FILTER_HIDDEN], h[:, FILTER_HIDDEN:]], axis=0)
    hh, hl = _split(h)
    filt = jnp.dot(jnp.concatenate([hh, hl, hh], axis=1), w4_ref[...], preferred_element_type=F32)
    decay = jnp.exp(-z[:, 0:1] * dl_ref[...])
    row = pl.program_id(0) * rows + lax.broadcasted_iota(jnp.int32, (rows, 1), 0)
    for j in range(4):
        cs = slice(j * HYENA_WIDTH, (j + 1) * HYENA_WIDTH)
        val = filt[:, cs] * decay
        if j % 2 == 1:
            val = jnp.where(row == 0, 0.0, val)
        _store_padded_slabs(o_ref.at[j], val, rows // DFT_N2)


def _filters(length, fw1, fb1, fw2, fb2, fw3, fb3, fw4, ffreq, rows=256):
    rows = min(rows, length)
    n_slab = HYENA_WIDTH // LANES
    prow = rows // DFT_N2 * PITCH_Z
    w4h, w4l = _split(fw4.astype(F32))
    w4s = jnp.concatenate([w4h, w4h, w4l], axis=0)
    zfeat = jnp.asarray(_filter_features(length))
    w1p = jnp.zeros((LANES, FILTER_HIDDEN), F32).at[:FILTER_EMB].set(fw1.astype(F32))
    twice = lambda v: jnp.tile(v.astype(F32).reshape(1, -1), (1, 2))
    zero = jnp.zeros((FILTER_HIDDEN, FILTER_HIDDEN), F32)
    bdiag = lambda w: jnp.block([[w.astype(F32), zero], [zero, w.astype(F32)]])
    full = lambda shape: pl.BlockSpec(shape, lambda i: (0,) * len(shape))
    return pl.pallas_call(
        functools.partial(_filter_kernel, rows=rows),
        grid=(length // rows,),
        in_specs=[pl.BlockSpec((rows, LANES), lambda i: (i, 0)),
                  full((LANES, FILTER_HIDDEN)), full((1, LANES)),
                  full((LANES, LANES)), full((1, LANES)),
                  full((LANES, LANES)), full((1, LANES)),
                  full((3 * FILTER_HIDDEN, 4 * HYENA_WIDTH)), full((1, LANES)),
                  full((1, HYENA_WIDTH))],
        out_specs=pl.BlockSpec((4, n_slab, prow, LANES), lambda i: (0, 0, i, 0)),
        out_shape=jax.ShapeDtypeStruct((4, n_slab, length // DFT_N2 * PITCH_Z, LANES), F32),
        compiler_params=_cparams(("parallel",)),
        name="hyena_filters",
    )(zfeat, w1p, twice(fb1), bdiag(fw2), twice(fb2), bdiag(fw3), twice(fb3), w4s,
      twice(ffreq), jnp.asarray(_decay_rates()))


def _dft_tables(length):
    n = 2 * length
    n1 = n // DFT_N2
    h1 = n1 // 2
    kk = np.arange(h1)[:, None] + 0.5
    th = 2.0 * math.pi * kk * np.arange(h1)[None, :] / n1
    fa = np.concatenate([np.cos(th), -np.sin(th)], axis=0)
    ga = (2.0 / n) * fa.T
    k = np.arange(h1)[:, None, None] + n1 * np.arange(DFT_N2)[None, :, None] + 0.5
    ph = 2.0 * math.pi * k * np.arange(DFT_N2)[None, None, :] / n
    f = lambda a: jnp.asarray(a.astype(np.float32))
    er, ei = f(np.cos(ph)), f(-np.sin(ph))
    m = jnp.concatenate([jnp.concatenate([er, -ei], axis=2), jnp.concatenate([ei, er], axis=2)], axis=1)
    return f(fa), f(ga), m.astype(BF16)


def _split(a):
    hi = a.astype(BF16)
    lo = (a - hi.astype(F32)).astype(BF16)
    return hi, lo


CONV_K1_CHUNK = 64
CONV_PAIR_UNROLL = 32
CONV_FREQ_UNROLL = 16


def _resident_table_spec(h1):
    return pl.BlockSpec((h1, 2 * DFT_N2, 2 * DFT_N2), lambda *_: (0, 0, 0), pipeline_mode=pl.Buffered(1))


def _spectrum_kernel(fa_ref, m_ref, xf_ref, xb_ref, o_ref, s_ref, *, h1, kc):
    hf = pl.program_id(2)
    half = DFT_N2

    @pl.when(hf == 0)
    def _():
        fa = fa_ref[...]

        def body(n2, carry):
            x = jnp.concatenate([xf_ref[pl.ds(n2, h1, stride=PITCH_Z), :],
                                 xb_ref[pl.ds(n2, h1, stride=PITCH_Z), :]], axis=1)
            xh, xl = _split(x)
            a = jnp.dot(fa, jnp.concatenate([xh, xh, xl], axis=0), preferred_element_type=F32)
            aw = _pack_bf16_pair(a[:, :LANES], a[:, LANES:])
            for ri in range(2):
                s_ref[pl.ds(ri * half + n2, h1, stride=PITCH_S), :] = aw[ri * h1:(ri + 1) * h1]
            return carry

        lax.fori_loop(0, DFT_N2, body, 0, unroll=CONV_PAIR_UNROLL)

    def freq_body(k1, carry):
        row = pl.multiple_of((hf * kc + k1) * PITCH_S, 8)
        xw = s_ref[pl.ds(row, 2 * half), :]
        x = jnp.concatenate([_unpack_bf16_pair(xw, 0), _unpack_bf16_pair(xw, 1)], axis=1)
        z = jnp.dot(m_ref[hf * kc + k1], x, preferred_element_type=F32)
        o_ref[k1] = _pack_bf16_pair(z[:half, :LANES] + z[:half, LANES:], z[half:, :LANES] - z[half:, LANES:])
        return carry

    lax.fori_loop(0, kc, freq_body, 0, unroll=CONV_FREQ_UNROLL)


def _filter_spectrum(fa, m_bf16, filt):
    _, n_slab, prow, _ = filt.shape
    n1, h1 = fa.shape
    kc = min(CONV_K1_CHUNK, h1)
    fh, fl = _split(fa)
    fa3 = jnp.concatenate([fh, fl, fh], axis=1)
    return pl.pallas_call(
        functools.partial(_spectrum_kernel, h1=h1, kc=kc),
        grid=(2, n_slab, h1 // kc),
        in_specs=[pl.BlockSpec((n1, 3 * h1), lambda o, s, hf: (0, 0)),
                  _resident_table_spec(h1),
                  pl.BlockSpec((None, None, prow, LANES), lambda o, s, hf: (2 * o, s, 0, 0)),
                  pl.BlockSpec((None, None, prow, LANES), lambda o, s, hf: (2 * o + 1, s, 0, 0))],
        out_specs=pl.BlockSpec((None, None, kc, DFT_N2, LANES), lambda o, s, hf: (o, s, hf, 0, 0)),
        out_shape=jax.ShapeDtypeStruct((2, n_slab, h1, DFT_N2, LANES), jnp.uint32),
        scratch_shapes=[pltpu.VMEM((h1 * PITCH_S, LANES), jnp.uint32)],
        compiler_params=pltpu.CompilerParams(dimension_semantics=("parallel", "parallel", "arbitrary"),
                                             vmem_limit_bytes=CONV_VMEM_LIMIT),
        name="filter_spectrum",
    )(fa3, m_bf16, filt, filt)


def _conv_kernel(fa_ref, ga_ref, m_ref, k_ref, z_ref, hx_ref, sk_ref, o_ref, s_ref, *, h1):
    half = DFT_N2
    pairs = DFT_N2 // 2
    fa = fa_ref[...]
    ga = ga_ref[...]
    sk = sk_ref[...]

    def stage_a(j, carry):
        n2 = 2 * j
        zw = [z_ref[pl.ds(n2 + u, h1, stride=PITCH_Z), :] for u in range(2)]
        a = [jnp.dot(fa, jnp.concatenate([_unpack_bf16_pair(zw[0], e), _unpack_bf16_pair(zw[1], e)], axis=1),
                     preferred_element_type=F32) for e in range(2)]
        aw = _pack_bf16_pair(a[0], a[1])
        for ri in range(2):
            for u in range(2):
                s_ref[pl.ds(ri * half + n2 + u, h1, stride=PITCH_S), :] = (
                    aw[ri * h1:(ri + 1) * h1, u * LANES:(u + 1) * LANES])
        return carry

    lax.fori_loop(0, pairs, stage_a, 0, unroll=CONV_PAIR_UNROLL)

    def freq_body(k1, carry):
        row = pl.multiple_of(k1 * PITCH_S, 8)
        xw = s_ref[pl.ds(row, 2 * half), :]
        x = jnp.concatenate([_unpack_bf16_pair(xw, 0), _unpack_bf16_pair(xw, 1)], axis=1)
        m = m_ref[k1]
        z = jnp.dot(m, x, preferred_element_type=F32)
        zr, zi = z[:half], z[half:]
        kw = k_ref[k1]
        kr = jnp.concatenate([_unpack_bf16_pair(kw, 0, F32)] * 2, axis=1)
        ki = jnp.concatenate([_unpack_bf16_pair(kw, 1, F32)] * 2, axis=1)
        y = jnp.concatenate([zr * kr - zi * ki, zr * ki + zi * kr], axis=0).astype(BF16)
        bq = lax.dot_general(m, y, (((0,), (0,)), ((), ())), preferred_element_type=F32)
        s_ref[pl.ds(row, 2 * half), :] = _pack_bf16_pair(bq[:, :LANES], bq[:, LANES:])
        return carry

    lax.fori_loop(0, h1, freq_body, 0, unroll=CONV_FREQ_UNROLL)

    def stage_inv(j, carry):
        n2 = 2 * j
        rw = [s_ref[pl.ds(n2 + u, h1, stride=PITCH_S), :] for u in range(2)]
        iw = [s_ref[pl.ds(half + n2 + u, h1, stride=PITCH_S), :] for u in range(2)]
        zw = [z_ref[pl.ds(n2 + u, h1, stride=PITCH_Z), :] for u in range(2)]
        hw = [hx_ref[pl.ds(n2 + u, h1, stride=PITCH_Z), :] for u in range(2)]
        y = []
        for e in range(2):
            bn = jnp.concatenate(
                [jnp.concatenate([_unpack_bf16_pair(rw[u], e), _unpack_bf16_pair(iw[u], e)], axis=0)
                 for u in range(2)], axis=1)
            y.append(jnp.dot(ga, bn, preferred_element_type=F32))
        for u in range(2):
            res = [_unpack_bf16_pair(hw[u], e, F32) * (y[e][:, u * LANES:(u + 1) * LANES]
                                                       + sk * _unpack_bf16_pair(zw[u], e, F32)) for e in range(2)]
            o_ref[pl.ds(n2 + u, h1, stride=PITCH_Z), :] = _pack_bf16_pair(res[0], res[1])
        return carry

    lax.fori_loop(0, pairs, stage_inv, 0, unroll=CONV_PAIR_UNROLL // 2)
    pad = jnp.zeros((PITCH_Z - DFT_N2, LANES), jnp.uint32)
    for blk in range(h1):
        o_ref[blk * PITCH_Z + DFT_N2:(blk + 1) * PITCH_Z, :] = pad


def _long_conv_gate(tabs, kspec, order, z5, zsel, hx5, hxsel, skip):
    fa, ga, m_tab = tabs
    _, n_slab, n_pair, prow, _ = z5.shape
    n1, h1 = fa.shape
    slab = lambda sel: pl.BlockSpec((None, None, None, prow, LANES), lambda s, b: (sel, s, b, 0, 0))
    return pl.pallas_call(
        functools.partial(_conv_kernel, h1=h1),
        grid=(n_slab, n_pair),
        in_specs=[pl.BlockSpec((n1, h1), lambda s, b: (0, 0)),
                  pl.BlockSpec((h1, n1), lambda s, b: (0, 0)),
                  _resident_table_spec(h1),
                  pl.BlockSpec((None, None, h1, DFT_N2, LANES), lambda s, b: (order, s, 0, 0, 0)),
                  slab(zsel), slab(hxsel),
                  pl.BlockSpec((None, None, 1, LANES), lambda s, b: (order, s, 0, 0))],
        out_specs=pl.BlockSpec((None, None, prow, LANES), lambda s, b: (s, b, 0, 0)),
        out_shape=jax.ShapeDtypeStruct((n_slab, n_pair, prow, LANES), jnp.uint32),
        scratch_shapes=[pltpu.VMEM((h1 * PITCH_S, LANES), jnp.uint32)],
        compiler_params=pltpu.CompilerParams(dimension_semantics=("parallel", "arbitrary"),
                                             vmem_limit_bytes=CONV_VMEM_LIMIT),
        name="hyena_long_conv",
    )(fa.astype(BF16), ga.astype(BF16), m_tab, kspec, z5, hx5, skip)


def _hyena(proj, batch, seq, conv_w, conv_b, fparams, skip):
    c = HYENA_WIDTH
    fa, ga, m_tab = _dft_tables(seq)
    kspec = _filter_spectrum(fa, m_tab, _filters(seq, *fparams))
    u5 = _shortconv(proj, conv_w, conv_b, batch, seq)
    skip4 = skip.astype(F32).reshape(2, c // LANES, 1, LANES)
    tabs = (fa, ga, m_tab)
    z1 = _long_conv_gate(tabs, kspec, 0, u5, 0, u5, 1, skip4)
    return _long_conv_gate(tabs, kspec, 1, z1[None], 0, u5, 2, skip4)


def _final_kernel(o1_ref, o2_ref, o3_ref, l1_ref, l2_ref, l3_ref, hy_ref, x_ref, g_ref, wg_ref,
                  wa_ref, wh_ref, wo_ref, pg_ref, y_ref, *, tiles_per_seq):
    def cat(ref):
        return jnp.concatenate([ref[s] for s in range(ATTN_WIDTH // LANES)], axis=1)

    def cat_packed(ref):
        return jnp.concatenate([_unpack_bf16_pair(ref[s], e, F32)
                                for s in range(ref.shape[0]) for e in range(PAIRS_PER_SLAB)], axis=1)

    l1, l2, l3 = cat(l1_ref), cat(l2_ref), cat(l3_ref)
    mx = jnp.maximum(jnp.maximum(l1, l2), l3)
    e1, e2, e3 = jnp.exp(l1 - mx), jnp.exp(l2 - mx), jnp.exp(l3 - mx)
    attn = (e1 * cat_packed(o1_ref) + e2 * cat_packed(o2_ref) + e3 * cat_packed(o3_ref)) / (e1 + e2 + e3)
    x = x_ref[...]
    hn = (x * lax.rsqrt(jnp.mean(x * x, axis=-1, keepdims=True) + EPS) * g_ref[...]).astype(BF16)
    gates = jnp.dot(hn, wg_ref[...], preferred_element_type=F32)
    gh, ga = gates[:, G_GH:G_MA], gates[:, G_GA:]
    a_in = (attn * (ga * jax.nn.sigmoid(ga))).astype(BF16)
    a_br = jnp.dot(a_in, wa_ref[...], preferred_element_type=F32)
    odd = (pl.program_id(0) // tiles_per_seq) % 2
    hw = _load_padded_slabs(hy_ref, x.shape[0] // DFT_N2)
    hy = lax.bitcast_convert_type((hw << ((1 - odd) * 16).astype(jnp.uint32)) & jnp.uint32(0xFFFF0000), F32)
    h_in = (hy * (gh * jax.nn.sigmoid(gh))).astype(BF16)
    h_br = jnp.dot(h_in, wh_ref[...], preferred_element_type=F32)
    merged = jax.nn.sigmoid(gates[:, G_MA:G_MH]) * a_br + jax.nn.sigmoid(gates[:, G_MH:G_GA]) * h_br
    out = jnp.dot(merged.astype(BF16), wo_ref[...], preferred_element_type=F32)
    ms = jnp.mean(out * out, axis=-1, keepdims=True)
    y_ref[...] = x + out * lax.rsqrt(ms + EPS) * pg_ref[...]


def _final(os_, ls_, hy, x2d, gain, wg, wa, wh, wo, pg, seq, tm=512):
    rows = x2d.shape[0]
    tm = min(tm, rows)
    tiles_per_seq = seq // tm
    hy_spec = pl.BlockSpec((HYENA_WIDTH // LANES, None, tm // DFT_N2 * PITCH_Z, LANES),
                           lambda i: (0, i // (2 * tiles_per_seq), i % tiles_per_seq, 0))
    slab = pl.BlockSpec((ATTN_WIDTH // LANES, tm, LANES), lambda i: (0, i, 0))
    oslab = pl.BlockSpec((ATTN_WIDTH // LANES // PAIRS_PER_SLAB, tm, LANES), lambda i: (0, i, 0))
    full = lambda shape: pl.BlockSpec(shape, lambda i: (0, 0), pipeline_mode=pl.Buffered(1))
    return pl.pallas_call(
        functools.partial(_final_kernel, tiles_per_seq=tiles_per_seq),
        grid=(rows // tm,),
        in_specs=[oslab] * 3 + [slab] * 3 + [hy_spec, pl.BlockSpec((tm, D_MODEL), lambda i: (i, 0)),
                                             full((1, D_MODEL)), full((D_MODEL, GATE_WIDTH)),
                                             full((ATTN_WIDTH, D_MODEL)), full((D_MODEL, D_MODEL)),
                                             full((D_MODEL, D_MODEL)), full((1, D_MODEL))],
        out_specs=pl.BlockSpec((tm, D_MODEL), lambda i: (i, 0)),
        out_shape=jax.ShapeDtypeStruct((rows, D_MODEL), F32),
        compiler_params=_cparams(("parallel",)),
        name="merge_out",
    )(*os_, *ls_, hy, x2d, gain, wg, wa, wh, wo, pg)


def _layer(x, rel_bias, pre_g, post_g, w_in, conv_w, conv_b, fparams, skip, w_br_a, w_br_h, w_out):
    batch, seq, _ = x.shape
    x2d = x.reshape(batch * seq, D_MODEL)
    w_qkv, w_u, w_gate = w_in
    gain = pre_g.reshape(1, -1).astype(F32)
    qkv = _inproj(x2d, gain, w_qkv, tn=QKV_WIDTH // 2, slab_out=True)
    proj = _inproj(x2d, gain, w_u, tn=U_WIDTH, slab_out=False)
    os_, ls_ = [], []
    for g in range(N_GROUPS):
        o, l = _attention_group(qkv, rel_bias, g, batch, seq)
        os_.append(o)
        ls_.append(l)
    hy = _hyena(proj, batch, seq, conv_w, conv_b, fparams, skip)
    y = _final(os_, ls_, hy, x2d, gain, w_gate, w_br_a, w_br_h, w_out, post_g.reshape(1, -1).astype(F32), seq)
    return y.reshape(batch, seq, D_MODEL)


def kernel(x_prompt, x_sample, rel_bias, pre_norm_g, post_norm_g, w_in, conv_w, conv_b, filt_w1, filt_b1, filt_w2, filt_b2, filt_w3, filt_b3, filt_w4, filt_freq, hyena_skip, w_branch_a, w_branch_h, w_out):
    depth = w_in.shape[0]

    def run(x):
        for l in range(depth):
            fparams = (filt_w1[l], filt_b1[l], filt_w2[l], filt_b2[l], filt_w3[l], filt_b3[l], filt_w4[l], filt_freq[l])
            w = w_in[l]
            col_scale = jnp.where(jnp.arange(QKV_WIDTH) < QKV_WIDTH // 3, HEAD_DIM ** -0.5, 1.0).astype(F32)
            w_qkv = (w[:, :QKV_WIDTH] * col_scale).astype(BF16)
            w_u = w[:, U_OFF:GH_OFF].astype(BF16)
            w_gate = jnp.concatenate([w[:, GH_OFF:], w[:, GA_OFF:U_OFF]], axis=1).astype(BF16)
            x = _layer(x, rel_bias, pre_norm_g[l], post_norm_g[l], (w_qkv, w_u, w_gate), conv_w[l], conv_b[l],
                       fparams, hyena_skip[l], w_branch_a[l].astype(BF16), w_branch_h[l].astype(BF16),
                       w_out[l].astype(BF16))
        return x

    return (run(x_prompt), run(x_sample))
```

```python
import functools
import math

import numpy as np
import jax
import jax.numpy as jnp
from jax import lax
from jax.experimental import pallas as pl
from jax.experimental.pallas import tpu as pltpu

F32 = jnp.float32
BF16 = jnp.bfloat16

D_MODEL = 1024
EPS = 1e-6
HEAD_DIM = 64
ATTN_GROUPS = ((128, 1), (512, 4), (2048, 16))
N_GROUPS = 3
HEADS_PER_GROUP = 8
ATTN_WIDTH = HEADS_PER_GROUP * HEAD_DIM
HYENA_WIDTH = 1024
FILTER_EMB = 33
N_BANDS = 16
FILTER_HIDDEN = 64
NUM_BUCKETS = 32
MAX_DISTANCE = 1024
NEG_INF = -1e30
QKV_WIDTH = 4608
GA_OFF, U_OFF, GH_OFF = 4608, 5120, 8192
U_WIDTH = 3072
GATE_WIDTH = 3584
G_GH, G_MA, G_MH, G_GA = 0, 1024, 2048, 3072

WIN = 64
LANES = 128
DFT_N2 = 128
PITCH_Z = 136
PITCH_S = 264
VMEM_LIMIT = 48 * 1024 * 1024
CONV_VMEM_LIMIT = 58 * 1024 * 1024


def _cparams(sem):
    return pltpu.CompilerParams(dimension_semantics=sem, vmem_limit_bytes=VMEM_LIMIT)


def _inproj_kernel(x_ref, g_ref, w_ref, o_ref, hn_ref, *, slabs):
    @pl.when(pl.program_id(1) == 0)
    def _():
        x = x_ref[...]
        ms = jnp.mean(x * x, axis=-1, keepdims=True)
        hn_ref[...] = (x * lax.rsqrt(ms + EPS) * g_ref[...]).astype(BF16)

    acc = jnp.dot(hn_ref[...], w_ref[...], preferred_element_type=F32)
    if slabs:
        for s in range(slabs):
            o_ref[s] = _pack_bf16_pair(acc[:, (2 * s) * LANES:(2 * s + 1) * LANES],
                                       acc[:, (2 * s + 1) * LANES:(2 * s + 2) * LANES])
    else:
        o_ref[...] = acc.astype(o_ref.dtype)


def _pack_bf16_pair(a, b):
    ua = lax.bitcast_convert_type(a.astype(BF16).astype(F32), jnp.uint32)
    ub = lax.bitcast_convert_type(b.astype(BF16).astype(F32), jnp.uint32)
    return (ua >> 16) | (ub & jnp.uint32(0xFFFF0000))


def _unpack_bf16_pair(w, idx, dtype=BF16):
    bits = (w << 16) if idx == 0 else (w & jnp.uint32(0xFFFF0000))
    return lax.bitcast_convert_type(bits, F32).astype(dtype)


def _inproj(x2d, g, w_bf16, tn, slab_out, tm=1024):
    rows = x2d.shape[0]
    width = w_bf16.shape[1]
    tm = min(tm, rows)
    if slab_out:
        ns = tn // (2 * LANES)
        out_spec = pl.BlockSpec((ns, tm, LANES), lambda i, j: (j, i, 0))
        out_shape = jax.ShapeDtypeStruct((width // (2 * LANES), rows, LANES), jnp.uint32)
    else:
        ns = 0
        out_spec = pl.BlockSpec((tm, tn), lambda i, j: (i, j))
        out_shape = jax.ShapeDtypeStruct((rows, width), BF16)
    return pl.pallas_call(
        functools.partial(_inproj_kernel, slabs=ns),
        grid=(rows // tm, width // tn),
        in_specs=[
            pl.BlockSpec((tm, D_MODEL), lambda i, j: (i, 0)),
            pl.BlockSpec((1, D_MODEL), lambda i, j: (0, 0)),
            pl.BlockSpec((D_MODEL, tn), lambda i, j: (0, j)),
        ],
        out_specs=out_spec,
        out_shape=out_shape,
        scratch_shapes=[pltpu.VMEM((tm, D_MODEL), BF16)],
        compiler_params=_cparams(("parallel", "arbitrary")),
        name="inproj_qkv" if slab_out else "inproj_u",
    )(x2d, g, w_bf16)


def _t5_bucket_np(rel):
    half = NUM_BUCKETS // 2
    max_exact = half // 2
    n = np.abs(rel)
    nf = np.maximum(n, 1).astype(np.float64)
    large = max_exact + (np.log(nf / max_exact) / math.log(MAX_DISTANCE / max_exact) * (half - max_exact)).astype(np.int64)
    large = np.minimum(large, half - 1)
    return np.where(rel > 0, half, 0) + np.where(n < max_exact, n, large)


def _bias_mask(rel_bias_g, dilation):
    qi = np.arange(2 * WIN)[:, None]
    kj = np.arange(4 * WIN)[None, :] - WIN
    delta = kj - qi
    bucket = _t5_bucket_np(delta * dilation).astype(np.int32)
    onehot = jax.nn.one_hot(jnp.asarray(bucket), NUM_BUCKETS, dtype=F32)
    bias = jnp.einsum('qkb,bh->hqk', onehot, rel_bias_g.astype(F32), precision=lax.Precision.HIGHEST)
    bias = jnp.where(jnp.asarray(np.abs(delta) <= WIN)[None], bias, NEG_INF)
    return bias.reshape(HEADS_PER_GROUP // 2, 2, 2 * WIN, 4 * WIN)


ATTN_POS_PER_STEP = 4096
ATTN_TILES_PER_ITER = 8
PAIRS_PER_SLAB = 2


def _attn_kernel(q_ref, kp_ref, kc_ref, kn_ref, vp_ref, vc_ref, vn_ref, bm_ref, o_ref, l_ref,
                 kbuf, vbuf, *, tq, dil, sub_len):
    t = pl.program_id(2)
    halo = WIN * dil
    span = tq * dil
    kbuf[0:halo] = kp_ref[...]
    kbuf[halo:halo + span] = kc_ref[...]
    kbuf[halo + span:] = kn_ref[...]
    vbuf[0:halo] = vp_ref[...]
    vbuf[halo:halo + span] = vc_ref[...]
    vbuf[halo + span:] = vn_ref[...]
    lane = lax.broadcasted_iota(jnp.int32, (1, LANES), 1)
    is_lo = lane < HEAD_DIM
    kcol = lax.broadcasted_iota(jnp.int32, (1, 4 * WIN), 1)
    qt = 2 * WIN
    nq = tq // qt

    def tile(c, i):
        row0 = i * (qt * dil) + c
        kpos = t * tq + i * qt - WIN + kcol
        pen = jnp.where((kpos >= 0) & (kpos < sub_len), 0.0, NEG_INF).astype(F32)
        qw = q_ref[pl.ds(row0, qt, stride=dil), :]
        kw = kbuf[pl.ds(row0, 2 * qt, stride=dil), :]
        vw = vbuf[pl.ds(row0, 2 * qt, stride=dil), :]
        o_pair = []
        for hp in range(PAIRS_PER_SLAB):
            q = _unpack_bf16_pair(qw, hp)
            k = _unpack_bf16_pair(kw, hp)
            v = _unpack_bf16_pair(vw, hp)
            outs, lses = [], []
            for hh in range(2):
                sel = is_lo if hh == 0 else jnp.logical_not(is_lo)
                qm = jnp.where(sel, q, jnp.zeros_like(q))
                s = lax.dot_general(qm, k, (((1,), (1,)), ((), ())), preferred_element_type=F32)
                s = s + bm_ref[hp, hh] + pen
                m = jnp.max(s, axis=-1, keepdims=True)
                p = jnp.exp(s - m)
                den = jnp.sum(p, axis=-1, keepdims=True)
                pv = jnp.dot(p.astype(BF16), v, preferred_element_type=F32)
                outs.append(pv / den)
                lses.append(m + jnp.log(den))
            o_pair.append(jnp.where(is_lo, outs[0], outs[1]))
            l_ref[hp, pl.ds(row0, qt, stride=dil), :] = jnp.where(is_lo, lses[0], lses[1])
        o_ref[pl.ds(row0, qt, stride=dil), :] = _pack_bf16_pair(o_pair[0], o_pair[1])

    def body(it, carry):
        for u in range(ATTN_TILES_PER_ITER):
            idx = it * ATTN_TILES_PER_ITER + u
            tile(idx // nq, idx % nq)
        return carry

    lax.fori_loop(0, dil * nq // ATTN_TILES_PER_ITER, body, 0)


def _attention_group(qkv, rel_bias, g, batch, seq):
    _, dil = ATTN_GROUPS[g]
    sub_len = seq // dil
    tq = min(ATTN_POS_PER_STEP // dil, sub_len)
    nt = sub_len // tq
    span = tq * dil
    halo = WIN * dil
    n_hp = ATTN_WIDTH // LANES
    n_ps = n_hp // PAIRS_PER_SLAB
    n_slab = N_GROUPS * n_ps
    bm = _bias_mask(rel_bias[:, g * HEADS_PER_GROUP:(g + 1) * HEADS_PER_GROUP], dil)
    hb = tq // WIN
    nhb = sub_len // WIN

    def cur(which):
        return pl.BlockSpec((None, span, LANES), lambda b, ps, t: (which * n_slab + g * n_ps + ps, b * nt + t, 0))

    def prev(which):
        return pl.BlockSpec((None, halo, LANES),
                            lambda b, ps, t: (which * n_slab + g * n_ps + ps, b * nhb + jnp.maximum(t * hb - 1, 0), 0))

    def nxt(which):
        return pl.BlockSpec((None, halo, LANES),
                            lambda b, ps, t: (which * n_slab + g * n_ps + ps, b * nhb + jnp.minimum((t + 1) * hb, nhb - 1), 0))

    out_spec = pl.BlockSpec((PAIRS_PER_SLAB, span, LANES), lambda b, ps, t: (ps, b * nt + t, 0))
    return pl.pallas_call(
        functools.partial(_attn_kernel, tq=tq, dil=dil, sub_len=sub_len),
        grid=(batch, n_ps, nt),
        in_specs=[cur(0), prev(1), cur(1), nxt(1), prev(2), cur(2), nxt(2),
                  pl.BlockSpec((PAIRS_PER_SLAB, 2, 2 * WIN, 4 * WIN), lambda b, ps, t: (ps, 0, 0, 0))],
        out_specs=[pl.BlockSpec((None, span, LANES), lambda b, ps, t: (ps, b * nt + t, 0)), out_spec],
        out_shape=[jax.ShapeDtypeStruct((n_ps, batch * seq, LANES), jnp.uint32),
                   jax.ShapeDtypeStruct((n_hp, batch * seq, LANES), F32)],
        scratch_shapes=[pltpu.VMEM((span + 2 * halo, LANES), jnp.uint32)] * 2,
        compiler_params=_cparams(("parallel", "parallel", "arbitrary")),
        name=f"attn_g{g}",
    )(qkv, qkv, qkv, qkv, qkv, qkv, qkv, bm)


def _shortconv_kernel(p_ref, c_ref, n_ref, w_ref, b_ref, o_ref, x_buf, *, rows, halo):
    t = pl.program_id(1)
    nt = pl.num_programs(1)
    has_prev = (t > 0).astype(F32)
    has_next = (t < nt - 1).astype(F32)
    n_slab = c_ref.shape[2] // LANES
    for e in range(2):
        x = c_ref[e].astype(F32)
        lo = p_ref[e].astype(F32)[halo - 8:halo, :] * has_prev
        hi = n_ref[e].astype(F32)[0:8, :] * has_next
        for s in range(n_slab):
            cs = slice(s * LANES, (s + 1) * LANES)
            x_buf[e, s, 0:8, :] = lo[:, cs]
            x_buf[e, s, 8:rows + 8, :] = x[:, cs]
            x_buf[e, s, rows + 8:rows + 16, :] = hi[:, cs]
    nblk = rows // DFT_N2
    pad = jnp.zeros((PITCH_Z - DFT_N2, LANES), jnp.uint32)
    for s in range(n_slab):
        cs = slice(s * LANES, (s + 1) * LANES)
        w0, w1, w2, bias = w_ref[0:1, cs], w_ref[1:2, cs], w_ref[2:3, cs], b_ref[:, cs]
        res = []
        for e in range(2):
            xs = x_buf.at[e, s]
            res.append(xs[pl.ds(7, rows, stride=1), :] * w0 + xs[pl.ds(8, rows, stride=1), :] * w1
                       + xs[pl.ds(9, rows, stride=1), :] * w2 + bias)
        packed = _pack_bf16_pair(res[0], res[1])
        for jb in range(nblk):
            o_ref[s, jb * PITCH_Z:jb * PITCH_Z + DFT_N2, :] = packed[jb * DFT_N2:(jb + 1) * DFT_N2]
            o_ref[s, jb * PITCH_Z + DFT_N2:(jb + 1) * PITCH_Z, :] = pad


def _store_padded_slabs(o_ref, val, nblk):
    pad = jnp.zeros((PITCH_Z - DFT_N2, LANES), val.dtype)
    for s in range(val.shape[1] // LANES):
        for jb in range(nblk):
            o_ref[s, jb * PITCH_Z:jb * PITCH_Z + DFT_N2, :] = val[jb * DFT_N2:(jb + 1) * DFT_N2, s * LANES:(s + 1) * LANES]
            o_ref[s, jb * PITCH_Z + DFT_N2:(jb + 1) * PITCH_Z, :] = pad


def _load_padded_slabs(ref, nblk):
    return jnp.concatenate(
        [jnp.concatenate([ref[s, jb * PITCH_Z:jb * PITCH_Z + DFT_N2, :] for jb in range(nblk)], axis=0)
         for s in range(ref.shape[0])], axis=1)


def _shortconv(proj, conv_w, conv_b, batch, seq, rows=512):
    assert batch % 2 == 0
    rows = min(rows, seq)
    prow = rows // DFT_N2 * PITCH_Z
    n_slab = HYENA_WIDTH // LANES
    halo = 16
    pv = proj.reshape(batch, seq, U_WIDTH)
    cb = 0
    hb = rows // halo
    nhb = seq // halo
    return pl.pallas_call(
        functools.partial(_shortconv_kernel, rows=rows, halo=halo),
        grid=(batch // 2, seq // rows, 3),
        in_specs=[
            pl.BlockSpec((2, halo, HYENA_WIDTH), lambda b, t, j: (b, jnp.maximum(t * hb - 1, 0), cb + j)),
            pl.BlockSpec((2, rows, HYENA_WIDTH), lambda b, t, j: (b, t, cb + j)),
            pl.BlockSpec((2, halo, HYENA_WIDTH), lambda b, t, j: (b, jnp.minimum((t + 1) * hb, nhb - 1), cb + j)),
            pl.BlockSpec((3, HYENA_WIDTH), lambda b, t, j: (0, j)),
            pl.BlockSpec((1, HYENA_WIDTH), lambda b, t, j: (0, j)),
        ],
        out_specs=pl.BlockSpec((None, n_slab, None, prow, LANES), lambda b, t, j: (j, 0, b, t, 0)),
        out_shape=jax.ShapeDtypeStruct((3, n_slab, batch // 2, seq // DFT_N2 * PITCH_Z, LANES), jnp.uint32),
        scratch_shapes=[pltpu.VMEM((2, n_slab, rows + 16, LANES), F32)],
        compiler_params=_cparams(("parallel", "parallel", "arbitrary")),
        name="shortconv",
    )(pv, pv, pv, conv_w, conv_b.reshape(1, -1))


def _filter_features(length):
    t = np.linspace(0.0, 1.0, length)[:, None]
    ang = (2.0 * math.pi / length) * np.arange(length, dtype=np.float64)[:, None]
    bands = np.linspace(1e-4, N_BANDS - 1, N_BANDS)[None]
    z = np.concatenate([t, np.cos(ang * bands), -np.sin(ang * bands)], axis=-1)
    zp = np.zeros((length, LANES), np.float32)
    zp[:, :FILTER_EMB] = z
    return zp


def _decay_rates():
    max_decay = math.log(1e-2) / 0.3
    min_decay = math.log(1e-2) / 1.5
    return np.abs(np.linspace(min_decay, max_decay, HYENA_WIDTH)).astype(np.float32)[None]


def _filter_kernel(z_ref, w1_ref, b1_ref, w2_ref, b2_ref, w3_ref, b3_ref, w4_ref, fr_ref, dl_ref, o_ref, *, rows):
    hi = lax.Precision.HIGHEST
    z = z_ref[...]
    fr = fr_ref[...]
    hr = rows // 2
    pre = jnp.concatenate([jnp.dot(z[:hr], w1_ref[...], precision=hi, preferred_element_type=F32),
                           jnp.dot(z[hr:], w1_ref[...], precision=hi, preferred_element_type=F32)], axis=1)
    h = jnp.sin(fr * (pre + b1_ref[...]))
    h = jnp.sin(fr * (jnp.dot(h, w2_ref[...], precision=hi, preferred_element_type=F32) + b2_ref[...]))
    h = jnp.sin(fr * (jnp.dot(h, w3_ref[...], precision=hi, preferred_element_type=F32) + b3_ref[...]))
    h = jnp.concatenate([h[:, :FILTER_HIDDEN], h[:, FILTER_HIDDEN:]], axis=0)
    hh, hl = _split(h)
    filt = jnp.dot(jnp.concatenate([hh, hl, hh], axis=1), w4_ref[...], preferred_element_type=F32)
    decay = jnp.exp(-z[:, 0:1] * dl_ref[...])
    row = pl.program_id(0) * rows + lax.broadcasted_iota(jnp.int32, (rows, 1), 0)
    for j in range(4):
        cs = slice(j * HYENA_WIDTH, (j + 1) * HYENA_WIDTH)
        val = filt[:, cs] * decay
        if j % 2 == 1:
            val = jnp.where(row == 0, 0.0, val)
        _store_padded_slabs(o_ref.at[j], val, rows // DFT_N2)


def _filters(length, fw1, fb1, fw2, fb2, fw3, fb3, fw4, ffreq, rows=256):
    rows = min(rows, length)
    n_slab = HYENA_WIDTH // LANES
    prow = rows // DFT_N2 * PITCH_Z
    w4h, w4l = _split(fw4.astype(F32))
    w4s = jnp.concatenate([w4h, w4h, w4l], axis=0)
    zfeat = jnp.asarray(_filter_features(length))
    w1p = jnp.zeros((LANES, FILTER_HIDDEN), F32).at[:FILTER_EMB].set(fw1.astype(F32))
    twice = lambda v: jnp.tile(v.astype(F32).reshape(1, -1), (1, 2))
    zero = jnp.zeros((FILTER_HIDDEN, FILTER_HIDDEN), F32)
    bdiag = lambda w: jnp.block([[w.astype(F32), zero], [zero, w.astype(F32)]])
    full = lambda shape: pl.BlockSpec(shape, lambda i: (0,) * len(shape))
    return pl.pallas_call(
        functools.partial(_filter_kernel, rows=rows),
        grid=(length // rows,),
        in_specs=[pl.BlockSpec((rows, LANES), lambda i: (i, 0)),
                  full((LANES, FILTER_HIDDEN)), full((1, LANES)),
                  full((LANES, LANES)), full((1, LANES)),
                  full((LANES, LANES)), full((1, LANES)),
                  full((3 * FILTER_HIDDEN, 4 * HYENA_WIDTH)), full((1, LANES)),
                  full((1, HYENA_WIDTH))],
        out_specs=pl.BlockSpec((4, n_slab, prow, LANES), lambda i: (0, 0, i, 0)),
        out_shape=jax.ShapeDtypeStruct((4, n_slab, length // DFT_N2 * PITCH_Z, LANES), F32),
        compiler_params=_cparams(("parallel",)),
        name="hyena_filters",
    )(zfeat, w1p, twice(fb1), bdiag(fw2), twice(fb2), bdiag(fw3), twice(fb3), w4s,
      twice(ffreq), jnp.asarray(_decay_rates()))


def _dft_tables(length):
    n = 2 * length
    n1 = n // DFT_N2
    h1 = n1 // 2
    kk = np.arange(h1)[:, None] + 0.5
    th = 2.0 * math.pi * kk * np.arange(h1)[None, :] / n1
    fa = np.concatenate([np.cos(th), -np.sin(th)], axis=0)
    ga = (2.0 / n) * fa.T
    k = np.arange(h1)[:, None, None] + n1 * np.arange(DFT_N2)[None, :, None] + 0.5
    ph = 2.0 * math.pi * k * np.arange(DFT_N2)[None, None, :] / n
    f = lambda a: jnp.asarray(a.astype(np.float32))
    er, ei = f(np.cos(ph)), f(-np.sin(ph))
    m = jnp.concatenate([jnp.concatenate([er, -ei], axis=2), jnp.concatenate([ei, er], axis=2)], axis=1)
    return f(fa), f(ga), m.astype(BF16)


def _split(a):
    hi = a.astype(BF16)
    lo = (a - hi.astype(F32)).astype(BF16)
    return hi, lo


CONV_K1_CHUNK = 64
CONV_PAIR_UNROLL = 64
CONV_FREQ_UNROLL = 32


def _resident_table_spec(h1):
    return pl.BlockSpec((h1, 2 * DFT_N2, 2 * DFT_N2), lambda *_: (0, 0, 0), pipeline_mode=pl.Buffered(1))


def _spectrum_kernel(fa_ref, m_ref, xf_ref, xb_ref, o_ref, s_ref, *, h1, kc):
    hf = pl.program_id(2)
    half = DFT_N2

    @pl.when(hf == 0)
    def _():
        fa = fa_ref[...]

        def body(n2, carry):
            x = jnp.concatenate([xf_ref[pl.ds(n2, h1, stride=PITCH_Z), :],
                                 xb_ref[pl.ds(n2, h1, stride=PITCH_Z), :]], axis=1)
            xh, xl = _split(x)
            a = jnp.dot(fa, jnp.concatenate([xh, xh, xl], axis=0), preferred_element_type=F32)
            aw = _pack_bf16_pair(a[:, :LANES], a[:, LANES:])
            for ri in range(2):
                s_ref[pl.ds(ri * half + n2, h1, stride=PITCH_S), :] = aw[ri * h1:(ri + 1) * h1]
            return carry

        lax.fori_loop(0, DFT_N2, body, 0, unroll=CONV_PAIR_UNROLL)

    def freq_body(k1, carry):
        row = pl.multiple_of((hf * kc + k1) * PITCH_S, 8)
        xw = s_ref[pl.ds(row, 2 * half), :]
        x = jnp.concatenate([_unpack_bf16_pair(xw, 0), _unpack_bf16_pair(xw, 1)], axis=1)
        z = jnp.dot(m_ref[hf * kc + k1], x, preferred_element_type=F32)
        o_ref[k1] = _pack_bf16_pair(z[:half, :LANES] + z[:half, LANES:], z[half:, :LANES] - z[half:, LANES:])
        return carry

    lax.fori_loop(0, kc, freq_body, 0, unroll=CONV_FREQ_UNROLL)


def _filter_spectrum(fa, m_bf16, filt):
    _, n_slab, prow, _ = filt.shape
    n1, h1 = fa.shape
    kc = min(CONV_K1_CHUNK, h1)
    fh, fl = _split(fa)
    fa3 = jnp.concatenate([fh, fl, fh], axis=1)
    return pl.pallas_call(
        functools.partial(_spectrum_kernel, h1=h1, kc=kc),
        grid=(2, n_slab, h1 // kc),
        in_specs=[pl.BlockSpec((n1, 3 * h1), lambda o, s, hf: (0, 0)),
                  _resident_table_spec(h1),
                  pl.BlockSpec((None, None, prow, LANES), lambda o, s, hf: (2 * o, s, 0, 0)),
                  pl.BlockSpec((None, None, prow, LANES), lambda o, s, hf: (2 * o + 1, s, 0, 0))],
        out_specs=pl.BlockSpec((None, None, kc, DFT_N2, LANES), lambda o, s, hf: (o, s, hf, 0, 0)),
        out_shape=jax.ShapeDtypeStruct((2, n_slab, h1, DFT_N2, LANES), jnp.uint32),
        scratch_shapes=[pltpu.VMEM((h1 * PITCH_S, LANES), jnp.uint32)],
        compiler_params=pltpu.CompilerParams(dimension_semantics=("parallel", "parallel", "arbitrary"),
                                             vmem_limit_bytes=CONV_VMEM_LIMIT),
        name="filter_spectrum",
    )(fa3, m_bf16, filt, filt)


def _conv_kernel(fa_ref, ga_ref, m_ref, k_ref, z_ref, hx_ref, sk_ref, o_ref, s_ref, *, h1):
    half = DFT_N2
    pairs = DFT_N2 // 2
    fa = fa_ref[...]
    ga = ga_ref[...]
    sk = sk_ref[...]

    def stage_a(j, carry):
        n2 = 2 * j
        zw = [z_ref[pl.ds(n2 + u, h1, stride=PITCH_Z), :] for u in range(2)]
        a = [jnp.dot(fa, jnp.concatenate([_unpack_bf16_pair(zw[0], e), _unpack_bf16_pair(zw[1], e)], axis=1),
                     preferred_element_type=F32) for e in range(2)]
        aw = _pack_bf16_pair(a[0], a[1])
        for ri in range(2):
            for u in range(2):
                s_ref[pl.ds(ri * half + n2 + u, h1, stride=PITCH_S), :] = (
                    aw[ri * h1:(ri + 1) * h1, u * LANES:(u + 1) * LANES])
        return carry

    lax.fori_loop(0, pairs, stage_a, 0, unroll=CONV_PAIR_UNROLL)

    def freq_body(k1, carry):
        row = pl.multiple_of(k1 * PITCH_S, 8)
        xw = s_ref[pl.ds(row, 2 * half), :]
        x = jnp.concatenate([_unpack_bf16_pair(xw, 0), _unpack_bf16_pair(xw, 1)], axis=1)
        m = m_ref[k1]
        z = jnp.dot(m, x, preferred_element_type=F32)
        zr, zi = z[:half], z[half:]
        kw = k_ref[k1]
        kr = jnp.concatenate([_unpack_bf16_pair(kw, 0, F32)] * 2, axis=1)
        ki = jnp.concatenate([_unpack_bf16_pair(kw, 1, F32)] * 2, axis=1)
        y = jnp.concatenate([zr * kr - zi * ki, zr * ki + zi * kr], axis=0).astype(BF16)
        bq = lax.dot_general(m, y, (((0,), (0,)), ((), ())), preferred_element_type=F32)
        s_ref[pl.ds(row, 2 * half), :] = _pack_bf16_pair(bq[:, :LANES], bq[:, LANES:])
        return carry

    lax.fori_loop(0, h1, freq_body, 0, unroll=CONV_FREQ_UNROLL)

    def stage_inv(j, carry):
        n2 = 2 * j
        rw = [s_ref[pl.ds(n2 + u, h1, stride=PITCH_S), :] for u in range(2)]
        iw = [s_ref[pl.ds(half + n2 + u, h1, stride=PITCH_S), :] for u in range(2)]
        zw = [z_ref[pl.ds(n2 + u, h1, stride=PITCH_Z), :] for u in range(2)]
        hw = [hx_ref[pl.ds(n2 + u, h1, stride=PITCH_Z), :] for u in range(2)]
        y = []
        for e in range(2):
            bn = jnp.concatenate(
                [jnp.concatenate([_unpack_bf16_pair(rw[u], e), _unpack_bf16_pair(iw[u], e)], axis=0)
                 for u in range(2)], axis=1)
            y.append(jnp.dot(ga, bn, preferred_element_type=F32))
        for u in range(2):
            res = [_unpack_bf16_pair(hw[u], e, F32) * (y[e][:, u * LANES:(u + 1) * LANES]
                                                       + sk * _unpack_bf16_pair(zw[u], e, F32)) for e in range(2)]
            o_ref[pl.ds(n2 + u, h1, stride=PITCH_Z), :] = _pack_bf16_pair(res[0], res[1])
        return carry

    lax.fori_loop(0, pairs, stage_inv, 0, unroll=CONV_PAIR_UNROLL // 2)
    pad = jnp.zeros((PITCH_Z - DFT_N2, LANES), jnp.uint32)
    for blk in range(h1):
        o_ref[blk * PITCH_Z + DFT_N2:(blk + 1) * PITCH_Z, :] = pad


def _long_conv_gate(tabs, kspec, order, z5, zsel, hx5, hxsel, skip):
    fa, ga, m_tab = tabs
    _, n_slab, n_pair, prow, _ = z5.shape
    n1, h1 = fa.shape
    slab = lambda sel: pl.BlockSpec((None, None, None, prow, LANES), lambda s, b: (sel, s, b, 0, 0))
    return pl.pallas_call(
        functools.partial(_conv_kernel, h1=h1),
        grid=(n_slab, n_pair),
        in_specs=[pl.BlockSpec((n1, h1), lambda s, b: (0, 0)),
                  pl.BlockSpec((h1, n1), lambda s, b: (0, 0)),
                  _resident_table_spec(h1),
                  pl.BlockSpec((None, None, h1, DFT_N2, LANES), lambda s, b: (order, s, 0, 0, 0)),
                  slab(zsel), slab(hxsel),
                  pl.BlockSpec((None, None, 1, LANES), lambda s, b: (order, s, 0, 0))],
        out_specs=pl.BlockSpec((None, None, prow, LANES), lambda s, b: (s, b, 0, 0)),
        out_shape=jax.ShapeDtypeStruct((n_slab, n_pair, prow, LANES), jnp.uint32),
        scratch_shapes=[pltpu.VMEM((h1 * PITCH_S, LANES), jnp.uint32)],
        compiler_params=pltpu.CompilerParams(dimension_semantics=("parallel", "arbitrary"),
                                             vmem_limit_bytes=CONV_VMEM_LIMIT),
        name="hyena_long_conv",
    )(fa.astype(BF16), ga.astype(BF16), m_tab, kspec, z5, hx5, skip)


def _hyena(proj, batch, seq, conv_w, conv_b, fparams, skip):
    c = HYENA_WIDTH
    fa, ga, m_tab = _dft_tables(seq)
    kspec = _filter_spectrum(fa, m_tab, _filters(seq, *fparams))
    u5 = _shortconv(proj, conv_w, conv_b, batch, seq)
    skip4 = skip.astype(F32).reshape(2, c // LANES, 1, LANES)
    tabs = (fa, ga, m_tab)
    z1 = _long_conv_gate(tabs, kspec, 0, u5, 0, u5, 1, skip4)
    return _long_conv_gate(tabs, kspec, 1, z1[None], 0, u5, 2, skip4)


def _final_kernel(o1_ref, o2_ref, o3_ref, l1_ref, l2_ref, l3_ref, hy_ref, x_ref, g_ref, wg_ref,
                  wa_ref, wh_ref, wo_ref, pg_ref, y_ref, *, tiles_per_seq):
    def cat(ref):
        return jnp.concatenate([ref[s] for s in range(ATTN_WIDTH // LANES)], axis=1)

    def cat_packed(ref):
        return jnp.concatenate([_unpack_bf16_pair(ref[s], e, F32)
                                for s in range(ref.shape[0]) for e in range(PAIRS_PER_SLAB)], axis=1)

    l1, l2, l3 = cat(l1_ref), cat(l2_ref), cat(l3_ref)
    mx = jnp.maximum(jnp.maximum(l1, l2), l3)
    e1, e2, e3 = jnp.exp(l1 - mx), jnp.exp(l2 - mx), jnp.exp(l3 - mx)
    attn = (e1 * cat_packed(o1_ref) + e2 * cat_packed(o2_ref) + e3 * cat_packed(o3_ref)) / (e1 + e2 + e3)
    x = x_ref[...]
    hn = (x * lax.rsqrt(jnp.mean(x * x, axis=-1, keepdims=True) + EPS) * g_ref[...]).astype(BF16)
    gates = jnp.dot(hn, wg_ref[...], preferred_element_type=F32)
    gh, ga = gates[:, G_GH:G_MA], gates[:, G_GA:]
    a_in = (attn * (ga * jax.nn.sigmoid(ga))).astype(BF16)
    a_br = jnp.dot(a_in, wa_ref[...], preferred_element_type=F32)
    odd = (pl.program_id(0) // tiles_per_seq) % 2
    hw = _load_padded_slabs(hy_ref, x.shape[0] // DFT_N2)
    hy = lax.bitcast_convert_type((hw << ((1 - odd) * 16).astype(jnp.uint32)) & jnp.uint32(0xFFFF0000), F32)
    h_in = (hy * (gh * jax.nn.sigmoid(gh))).astype(BF16)
    h_br = jnp.dot(h_in, wh_ref[...], preferred_element_type=F32)
    merged = jax.nn.sigmoid(gates[:, G_MA:G_MH]) * a_br + jax.nn.sigmoid(gates[:, G_MH:G_GA]) * h_br
    out = jnp.dot(merged.astype(BF16), wo_ref[...], preferred_element_type=F32)
    ms = jnp.mean(out * out, axis=-1, keepdims=True)
    y_ref[...] = x + out * lax.rsqrt(ms + EPS) * pg_ref[...]


def _final(os_, ls_, hy, x2d, gain, wg, wa, wh, wo, pg, seq, tm=512):
    rows = x2d.shape[0]
    tm = min(tm, rows)
    tiles_per_seq = seq // tm
    hy_spec = pl.BlockSpec((HYENA_WIDTH // LANES, None, tm // DFT_N2 * PITCH_Z, LANES),
                           lambda i: (0, i // (2 * tiles_per_seq), i % tiles_per_seq, 0))
    slab = pl.BlockSpec((ATTN_WIDTH // LANES, tm, LANES), lambda i: (0, i, 0))
    oslab = pl.BlockSpec((ATTN_WIDTH // LANES // PAIRS_PER_SLAB, tm, LANES), lambda i: (0, i, 0))
    full = lambda shape: pl.BlockSpec(shape, lambda i: (0, 0), pipeline_mode=pl.Buffered(1))
    return pl.pallas_call(
        functools.partial(_final_kernel, tiles_per_seq=tiles_per_seq),
        grid=(rows // tm,),
        in_specs=[oslab] * 3 + [slab] * 3 + [hy_spec, pl.BlockSpec((tm, D_MODEL), lambda i: (i, 0)),
                                             full((1, D_MODEL)), full((D_MODEL, GATE_WIDTH)),
                                             full((ATTN_WIDTH, D_MODEL)), full((D_MODEL, D_MODEL)),
                                             full((D_MODEL, D_MODEL)), full((1, D_MODEL))],
        out_specs=pl.BlockSpec((tm, D_MODEL), lambda i: (i, 0)),
        out_shape=jax.ShapeDtypeStruct((rows, D_MODEL), F32),
        compiler_params=_cparams(("parallel",)),
        name="merge_out",
    )(*os_, *ls_, hy, x2d, gain, wg, wa, wh, wo, pg)


def _layer(x, rel_bias, pre_g, post_g, w_in, conv_w, conv_b, fparams, skip, w_br_a, w_br_h, w_out):
    batch, seq, _ = x.shape
    x2d = x.reshape(batch * seq, D_MODEL)
    w_qkv, w_u, w_gate = w_in
    gain = pre_g.reshape(1, -1).astype(F32)
    qkv = _inproj(x2d, gain, w_qkv, tn=QKV_WIDTH // 2, slab_out=True)
    proj = _inproj(x2d, gain, w_u, tn=U_WIDTH, slab_out=False)
    os_, ls_ = [], []
    for g in range(N_GROUPS):
        o, l = _attention_group(qkv, rel_bias, g, batch, seq)
        os_.append(o)
        ls_.append(l)
    hy = _hyena(proj, batch, seq, conv_w, conv_b, fparams, skip)
    y = _final(os_, ls_, hy, x2d, gain, w_gate, w_br_a, w_br_h, w_out, post_g.reshape(1, -1).astype(F32), seq)
    return y.reshape(batch, seq, D_MODEL)


def kernel(x_prompt, x_sample, rel_bias, pre_norm_g, post_norm_g, w_in, conv_w, conv_b, filt_w1, filt_b1, filt_w2, filt_b2, filt_w3, filt_b3, filt_w4, filt_freq, hyena_skip, w_branch_a, w_branch_h, w_out):
    depth = w_in.shape[0]

    def run(x):
        for l in range(depth):
            fparams = (filt_w1[l], filt_b1[l], filt_w2[l], filt_b2[l], filt_w3[l], filt_b3[l], filt_w4[l], filt_freq[l])
            w = w_in[l]
            col_scale = jnp.where(jnp.arange(QKV_WIDTH) < QKV_WIDTH // 3, HEAD_DIM ** -0.5, 1.0).astype(F32)
            w_qkv = (w[:, :QKV_WIDTH] * col_scale).astype(BF16)
            w_u = w[:, U_OFF:GH_OFF].astype(BF16)
            w_gate = jnp.concatenate([w[:, GH_OFF:], w[:, GA_OFF:U_OFF]], axis=1).astype(BF16)
            x = _layer(x, rel_bias, pre_norm_g[l], post_norm_g[l], (w_qkv, w_u, w_gate), conv_w[l], conv_b[l],
                       fparams, hyena_skip[l], w_branch_a[l].astype(BF16), w_branch_h[l].astype(BF16),
                       w_out[l].astype(BF16))
        return x

    return (run(x_prompt), run(x_sample))
```

```python
import functools
import math

import numpy as np
import jax
import jax.numpy as jnp
from jax import lax
from jax.experimental import pallas as pl
from jax.experimental.pallas import tpu as pltpu

F32 = jnp.float32
BF16 = jnp.bfloat16

D_MODEL = 1024
EPS = 1e-6
HEAD_DIM = 64
ATTN_GROUPS = ((128, 1), (512, 4), (2048, 16))
N_GROUPS = 3
HEADS_PER_GROUP = 8
ATTN_WIDTH = HEADS_PER_GROUP * HEAD_DIM
HYENA_WIDTH = 1024
FILTER_EMB = 33
N_BANDS = 16
FILTER_HIDDEN = 64
NUM_BUCKETS = 32
MAX_DISTANCE = 1024
NEG_INF = -1e30
QKV_WIDTH = 4608
GA_OFF, U_OFF, GH_OFF = 4608, 5120, 8192
U_WIDTH = 3072
GATE_WIDTH = 3584
G_GH, G_MA, G_MH, G_GA = 0, 1024, 2048, 3072

WIN = 64
LANES = 128
DFT_N2 = 128
PITCH_Z = 136
PITCH_S = 264
VMEM_LIMIT = 48 * 1024 * 1024
CONV_VMEM_LIMIT = 58 * 1024 * 1024


def _cparams(sem):
    return pltpu.CompilerParams(dimension_semantics=sem, vmem_limit_bytes=VMEM_LIMIT)


def _inproj_kernel(x_ref, g_ref, w_ref, o_ref, hn_ref, *, slabs):
    @pl.when(pl.program_id(1) == 0)
    def _():
        x = x_ref[...]
        ms = jnp.mean(x * x, axis=-1, keepdims=True)
        hn_ref[...] = (x * lax.rsqrt(ms + EPS) * g_ref[...]).astype(BF16)

    acc = jnp.dot(hn_ref[...], w_ref[...], preferred_element_type=F32)
    if slabs:
        for s in range(slabs):
            o_ref[s] = _pack_bf16_pair(acc[:, (2 * s) * LANES:(2 * s + 1) * LANES],
                                       acc[:, (2 * s + 1) * LANES:(2 * s + 2) * LANES])
    else:
        o_ref[...] = acc.astype(o_ref.dtype)


def _pack_bf16_pair(a, b):
    ua = lax.bitcast_convert_type(a.astype(BF16).astype(F32), jnp.uint32)
    ub = lax.bitcast_convert_type(b.astype(BF16).astype(F32), jnp.uint32)
    return (ua >> 16) | (ub & jnp.uint32(0xFFFF0000))


def _unpack_bf16_pair(w, idx, dtype=BF16):
    bits = (w << 16) if idx == 0 else (w & jnp.uint32(0xFFFF0000))
    return lax.bitcast_convert_type(bits, F32).astype(dtype)


def _inproj(x2d, g, w_bf16, tn, slab_out, tm=1024):
    rows = x2d.shape[0]
    width = w_bf16.shape[1]
    tm = min(tm, rows)
    if slab_out:
        ns = tn // (2 * LANES)
        out_spec = pl.BlockSpec((ns, tm, LANES), lambda i, j: (j, i, 0))
        out_shape = jax.ShapeDtypeStruct((width // (2 * LANES), rows, LANES), jnp.uint32)
    else:
        ns = 0
        out_spec = pl.BlockSpec((tm, tn), lambda i, j: (i, j))
        out_shape = jax.ShapeDtypeStruct((rows, width), BF16)
    return pl.pallas_call(
        functools.partial(_inproj_kernel, slabs=ns),
        grid=(rows // tm, width // tn),
        in_specs=[
            pl.BlockSpec((tm, D_MODEL), lambda i, j: (i, 0)),
            pl.BlockSpec((1, D_MODEL), lambda i, j: (0, 0)),
            pl.BlockSpec((D_MODEL, tn), lambda i, j: (0, j)),
        ],
        out_specs=out_spec,
        out_shape=out_shape,
        scratch_shapes=[pltpu.VMEM((tm, D_MODEL), BF16)],
        compiler_params=_cparams(("parallel", "arbitrary")),
        name="inproj_qkv" if slab_out else "inproj_u",
    )(x2d, g, w_bf16)


def _t5_bucket_np(rel):
    half = NUM_BUCKETS // 2
    max_exact = half // 2
    n = np.abs(rel)
    nf = np.maximum(n, 1).astype(np.float64)
    large = max_exact + (np.log(nf / max_exact) / math.log(MAX_DISTANCE / max_exact) * (half - max_exact)).astype(np.int64)
    large = np.minimum(large, half - 1)
    return np.where(rel > 0, half, 0) + np.where(n < max_exact, n, large)


def _bias_mask(rel_bias_g, dilation):
    qi = np.arange(2 * WIN)[:, None]
    kj = np.arange(4 * WIN)[None, :] - WIN
    delta = kj - qi
    bucket = _t5_bucket_np(delta * dilation).astype(np.int32)
    onehot = jax.nn.one_hot(jnp.asarray(bucket), NUM_BUCKETS, dtype=F32)
    bias = jnp.einsum('qkb,bh->hqk', onehot, rel_bias_g.astype(F32), precision=lax.Precision.HIGHEST)
    bias = jnp.where(jnp.asarray(np.abs(delta) <= WIN)[None], bias, NEG_INF)
    return bias.reshape(HEADS_PER_GROUP // 2, 2, 2 * WIN, 4 * WIN)


ATTN_POS_PER_STEP = 4096
ATTN_TILES_PER_ITER = 8
PAIRS_PER_SLAB = 2


def _attn_kernel(q_ref, kp_ref, kc_ref, kn_ref, vp_ref, vc_ref, vn_ref, bm_ref, o_ref, l_ref,
                 kbuf, vbuf, *, tq, dil, sub_len):
    t = pl.program_id(2)
    halo = WIN * dil
    span = tq * dil
    kbuf[0:halo] = kp_ref[...]
    kbuf[halo:halo + span] = kc_ref[...]
    kbuf[halo + span:] = kn_ref[...]
    vbuf[0:halo] = vp_ref[...]
    vbuf[halo:halo + span] = vc_ref[...]
    vbuf[halo + span:] = vn_ref[...]
    lane = lax.broadcasted_iota(jnp.int32, (1, LANES), 1)
    is_lo = lane < HEAD_DIM
    kcol = lax.broadcasted_iota(jnp.int32, (1, 4 * WIN), 1)
    qt = 2 * WIN
    nq = tq // qt

    def tile(c, i):
        row0 = i * (qt * dil) + c
        kpos = t * tq + i * qt - WIN + kcol
        pen = jnp.where((kpos >= 0) & (kpos < sub_len), 0.0, NEG_INF).astype(F32)
        qw = q_ref[pl.ds(row0, qt, stride=dil), :]
        kw = kbuf[pl.ds(row0, 2 * qt, stride=dil), :]
        vw = vbuf[pl.ds(row0, 2 * qt, stride=dil), :]
        o_pair = []
        for hp in range(PAIRS_PER_SLAB):
            q = _unpack_bf16_pair(qw, hp)
            k = _unpack_bf16_pair(kw, hp)
            v = _unpack_bf16_pair(vw, hp)
            outs, lses = [], []
            for hh in range(2):
                sel = is_lo if hh == 0 else jnp.logical_not(is_lo)
                qm = jnp.where(sel, q, jnp.zeros_like(q))
                s = lax.dot_general(qm, k, (((1,), (1,)), ((), ())), preferred_element_type=F32)
                s = s + bm_ref[hp, hh] + pen
                m = jnp.max(s, axis=-1, keepdims=True)
                p = jnp.exp(s - m)
                den = jnp.sum(p, axis=-1, keepdims=True)
                pv = jnp.dot(p.astype(BF16), v, preferred_element_type=F32)
                outs.append(pv / den)
                lses.append(m + jnp.log(den))
            o_pair.append(jnp.where(is_lo, outs[0], outs[1]))
            l_ref[hp, pl.ds(row0, qt, stride=dil), :] = jnp.where(is_lo, lses[0], lses[1])
        o_ref[pl.ds(row0, qt, stride=dil), :] = _pack_bf16_pair(o_pair[0], o_pair[1])

    def body(it, carry):
        for u in range(ATTN_TILES_PER_ITER):
            idx = it * ATTN_TILES_PER_ITER + u
            tile(idx // nq, idx % nq)
        return carry

    lax.fori_loop(0, dil * nq // ATTN_TILES_PER_ITER, body, 0)


def _attention_group(qkv, rel_bias, g, batch, seq):
    _, dil = ATTN_GROUPS[g]
    sub_len = seq // dil
    tq = min(ATTN_POS_PER_STEP // dil, sub_len)
    nt = sub_len // tq
    span = tq * dil
    halo = WIN * dil
    n_hp = ATTN_WIDTH // LANES
    n_ps = n_hp // PAIRS_PER_SLAB
    n_slab = N_GROUPS * n_ps
    bm = _bias_mask(rel_bias[:, g * HEADS_PER_GROUP:(g + 1) * HEADS_PER_GROUP], dil)
    hb = tq // WIN
    nhb = sub_len // WIN

    def cur(which):
        return pl.BlockSpec((None, span, LANES), lambda b, ps, t: (which * n_slab + g * n_ps + ps, b * nt + t, 0))

    def prev(which):
        return pl.BlockSpec((None, halo, LANES),
                            lambda b, ps, t: (which * n_slab + g * n_ps + ps, b * nhb + jnp.maximum(t * hb - 1, 0), 0))

    def nxt(which):
        return pl.BlockSpec((None, halo, LANES),
                            lambda b, ps, t: (which * n_slab + g * n_ps + ps, b * nhb + jnp.minimum((t + 1) * hb, nhb - 1), 0))

    out_spec = pl.BlockSpec((PAIRS_PER_SLAB, span, LANES), lambda b, ps, t: (ps, b * nt + t, 0))
    return pl.pallas_call(
        functools.partial(_attn_kernel, tq=tq, dil=dil, sub_len=sub_len),
        grid=(batch, n_ps, nt),
        in_specs=[cur(0), prev(1), cur(1), nxt(1), prev(2), cur(2), nxt(2),
                  pl.BlockSpec((PAIRS_PER_SLAB, 2, 2 * WIN, 4 * WIN), lambda b, ps, t: (ps, 0, 0, 0))],
        out_specs=[pl.BlockSpec((None, span, LANES), lambda b, ps, t: (ps, b * nt + t, 0)), out_spec],
        out_shape=[jax.ShapeDtypeStruct((n_ps, batch * seq, LANES), jnp.uint32),
                   jax.ShapeDtypeStruct((n_hp, batch * seq, LANES), F32)],
        scratch_shapes=[pltpu.VMEM((span + 2 * halo, LANES), jnp.uint32)] * 2,
        compiler_params=_cparams(("parallel", "parallel", "arbitrary")),
        name=f"attn_g{g}",
    )(qkv, qkv, qkv, qkv, qkv, qkv, qkv, bm)


def _shortconv_kernel(p_ref, c_ref, n_ref, w_ref, b_ref, o_ref, x_buf, *, rows, halo):
    t = pl.program_id(1)
    nt = pl.num_programs(1)
    has_prev = (t > 0).astype(F32)
    has_next = (t < nt - 1).astype(F32)
    n_slab = c_ref.shape[2] // LANES
    for e in range(2):
        x = c_ref[e].astype(F32)
        lo = p_ref[e].astype(F32)[halo - 8:halo, :] * has_prev
        hi = n_ref[e].astype(F32)[0:8, :] * has_next
        for s in range(n_slab):
            cs = slice(s * LANES, (s + 1) * LANES)
            x_buf[e, s, 0:8, :] = lo[:, cs]
            x_buf[e, s, 8:rows + 8, :] = x[:, cs]
            x_buf[e, s, rows + 8:rows + 16, :] = hi[:, cs]
    nblk = rows // DFT_N2
    pad = jnp.zeros((PITCH_Z - DFT_N2, LANES), jnp.uint32)
    for s in range(n_slab):
        cs = slice(s * LANES, (s + 1) * LANES)
        w0, w1, w2, bias = w_ref[0:1, cs], w_ref[1:2, cs], w_ref[2:3, cs], b_ref[:, cs]
        res = []
        for e in range(2):
            xs = x_buf.at[e, s]
            res.append(xs[pl.ds(7, rows, stride=1), :] * w0 + xs[pl.ds(8, rows, stride=1), :] * w1
                       + xs[pl.ds(9, rows, stride=1), :] * w2 + bias)
        packed = _pack_bf16_pair(res[0], res[1])
        for jb in range(nblk):
            o_ref[s, jb * PITCH_Z:jb * PITCH_Z + DFT_N2, :] = packed[jb * DFT_N2:(jb + 1) * DFT_N2]
            o_ref[s, jb * PITCH_Z + DFT_N2:(jb + 1) * PITCH_Z, :] = pad


def _store_padded_slabs(o_ref, val, nblk):
    pad = jnp.zeros((PITCH_Z - DFT_N2, LANES), val.dtype)
    for s in range(val.shape[1] // LANES):
        for jb in range(nblk):
            o_ref[s, jb * PITCH_Z:jb * PITCH_Z + DFT_N2, :] = val[jb * DFT_N2:(jb + 1) * DFT_N2, s * LANES:(s + 1) * LANES]
            o_ref[s, jb * PITCH_Z + DFT_N2:(jb + 1) * PITCH_Z, :] = pad


def _load_padded_slabs(ref, nblk):
    return jnp.concatenate(
        [jnp.concatenate([ref[s, jb * PITCH_Z:jb * PITCH_Z + DFT_N2, :] for jb in range(nblk)], axis=0)
         for s in range(ref.shape[0])], axis=1)


def _shortconv(proj, conv_w, conv_b, batch, seq, rows=512):
    assert batch % 2 == 0
    rows = min(rows, seq)
    prow = rows // DFT_N2 * PITCH_Z
    n_slab = HYENA_WIDTH // LANES
    halo = 16
    pv = proj.reshape(batch, seq, U_WIDTH)
    cb = 0
    hb = rows // halo
    nhb = seq // halo
    return pl.pallas_call(
        functools.partial(_shortconv_kernel, rows=rows, halo=halo),
        grid=(batch // 2, seq // rows, 3),
        in_specs=[
            pl.BlockSpec((2, halo, HYENA_WIDTH), lambda b, t, j: (b, jnp.maximum(t * hb - 1, 0), cb + j)),
            pl.BlockSpec((2, rows, HYENA_WIDTH), lambda b, t, j: (b, t, cb + j)),
            pl.BlockSpec((2, halo, HYENA_WIDTH), lambda b, t, j: (b, jnp.minimum((t + 1) * hb, nhb - 1), cb + j)),
            pl.BlockSpec((3, HYENA_WIDTH), lambda b, t, j: (0, j)),
            pl.BlockSpec((1, HYENA_WIDTH), lambda b, t, j: (0, j)),
        ],
        out_specs=pl.BlockSpec((None, n_slab, None, prow, LANES), lambda b, t, j: (j, 0, b, t, 0)),
        out_shape=jax.ShapeDtypeStruct((3, n_slab, batch // 2, seq // DFT_N2 * PITCH_Z, LANES), jnp.uint32),
        scratch_shapes=[pltpu.VMEM((2, n_slab, rows + 16, LANES), F32)],
        compiler_params=_cparams(("parallel", "parallel", "arbitrary")),
        name="shortconv",
    )(pv, pv, pv, conv_w, conv_b.reshape(1, -1))


def _filter_features(length):
    t = np.linspace(0.0, 1.0, length)[:, None]
    ang = (2.0 * math.pi / length) * np.arange(length, dtype=np.float64)[:, None]
    bands = np.linspace(1e-4, N_BANDS - 1, N_BANDS)[None]
    z = np.concatenate([t, np.cos(ang * bands), -np.sin(ang * bands)], axis=-1)
    zp = np.zeros((length, LANES), np.float32)
    zp[:, :FILTER_EMB] = z
    return zp


def _decay_rates():
    max_decay = math.log(1e-2) / 0.3
    min_decay = math.log(1e-2) / 1.5
    return np.abs(np.linspace(min_decay, max_decay, HYENA_WIDTH)).astype(np.float32)[None]


def _filter_kernel(z_ref, w1_ref, b1_ref, w2_ref, b2_ref, w3_ref, b3_ref, w4_ref, fr_ref, dl_ref, o_ref, *, rows):
    hi = lax.Precision.HIGHEST
    z = z_ref[...]
    fr = fr_ref[...]
    hr = rows // 2
    pre = jnp.concatenate([jnp.dot(z[:hr], w1_ref[...], precision=hi, preferred_element_type=F32),
                           jnp.dot(z[hr:], w1_ref[...], precision=hi, preferred_element_type=F32)], axis=1)
    h = jnp.sin(fr * (pre + b1_ref[...]))
    h = jnp.sin(fr * (jnp.dot(h, w2_ref[...], precision=hi, preferred_element_type=F32) + b2_ref[...]))
    h = jnp.sin(fr * (jnp.dot(h, w3_ref[...], precision=hi, preferred_element_type=F32) + b3_ref[...]))
    h = jnp.concatenate([h[:, :FILTER_HIDDEN], h[:, FILTER_HIDDEN:]], axis=0)
    hh, hl = _split(h)
    filt = jnp.dot(jnp.concatenate([hh, hl, hh], axis=1), w4_ref[...], preferred_element_type=F32)
    decay = jnp.exp(-z[:, 0:1] * dl_ref[...])
    row = pl.program_id(0) * rows + lax.broadcasted_iota(jnp.int32, (rows, 1), 0)
    for j in range(4):
        cs = slice(j * HYENA_WIDTH, (j + 1) * HYENA_WIDTH)
        val = filt[:, cs] * decay
        if j % 2 == 1:
            val = jnp.where(row == 0, 0.0, val)
        _store_padded_slabs(o_ref.at[j], val, rows // DFT_N2)


def _filters(length, fw1, fb1, fw2, fb2, fw3, fb3, fw4, ffreq, rows=512):
    rows = min(rows, length)
    n_slab = HYENA_WIDTH // LANES
    prow = rows // DFT_N2 * PITCH_Z
    w4h, w4l = _split(fw4.astype(F32))
    w4s = jnp.concatenate([w4h, w4h, w4l], axis=0)
    zfeat = jnp.asarray(_filter_features(length))
    w1p = jnp.zeros((LANES, FILTER_HIDDEN), F32).at[:FILTER_EMB].set(fw1.astype(F32))
    twice = lambda v: jnp.tile(v.astype(F32).reshape(1, -1), (1, 2))
    zero = jnp.zeros((FILTER_HIDDEN, FILTER_HIDDEN), F32)
    bdiag = lambda w: jnp.block([[w.astype(F32), zero], [zero, w.astype(F32)]])
    full = lambda shape: pl.BlockSpec(shape, lambda i: (0,) * len(shape))
    return pl.pallas_call(
        functools.partial(_filter_kernel, rows=rows),
        grid=(length // rows,),
        in_specs=[pl.BlockSpec((rows, LANES), lambda i: (i, 0)),
                  full((LANES, FILTER_HIDDEN)), full((1, LANES)),
                  full((LANES, LANES)), full((1, LANES)),
                  full((LANES, LANES)), full((1, LANES)),
                  full((3 * FILTER_HIDDEN, 4 * HYENA_WIDTH)), full((1, LANES)),
                  full((1, HYENA_WIDTH))],
        out_specs=pl.BlockSpec((4, n_slab, prow, LANES), lambda i: (0, 0, i, 0)),
        out_shape=jax.ShapeDtypeStruct((4, n_slab, length // DFT_N2 * PITCH_Z, LANES), F32),
        compiler_params=_cparams(("parallel",)),
        name="hyena_filters",
    )(zfeat, w1p, twice(fb1), bdiag(fw2), twice(fb2), bdiag(fw3), twice(fb3), w4s,
      twice(ffreq), jnp.asarray(_decay_rates()))


def _dft_tables(length):
    n = 2 * length
    n1 = n // DFT_N2
    h1 = n1 // 2
    kk = np.arange(h1)[:, None] + 0.5
    th = 2.0 * math.pi * kk * np.arange(h1)[None, :] / n1
    fa = np.concatenate([np.cos(th), -np.sin(th)], axis=0)
    ga = (2.0 / n) * fa.T
    k = np.arange(h1)[:, None, None] + n1 * np.arange(DFT_N2)[None, :, None] + 0.5
    ph = 2.0 * math.pi * k * np.arange(DFT_N2)[None, None, :] / n
    f = lambda a: jnp.asarray(a.astype(np.float32))
    er, ei = f(np.cos(ph)), f(-np.sin(ph))
    m = jnp.concatenate([jnp.concatenate([er, -ei], axis=2), jnp.concatenate([ei, er], axis=2)], axis=1)
    return f(fa), f(ga), m.astype(BF16)


def _split(a):
    hi = a.astype(BF16)
    lo = (a - hi.astype(F32)).astype(BF16)
    return hi, lo


CONV_K1_CHUNK = 64
CONV_PAIR_UNROLL = 64
CONV_FREQ_UNROLL = 32


def _resident_table_spec(h1):
    return pl.BlockSpec((h1, 2 * DFT_N2, 2 * DFT_N2), lambda *_: (0, 0, 0), pipeline_mode=pl.Buffered(1))


def _spectrum_kernel(fa_ref, m_ref, xf_ref, xb_ref, o_ref, s_ref, *, h1, kc):
    hf = pl.program_id(2)
    half = DFT_N2

    @pl.when(hf == 0)
    def _():
        fa = fa_ref[...]

        def body(n2, carry):
            x = jnp.concatenate([xf_ref[pl.ds(n2, h1, stride=PITCH_Z), :],
                                 xb_ref[pl.ds(n2, h1, stride=PITCH_Z), :]], axis=1).astype(BF16)
            a = jnp.dot(fa, x, preferred_element_type=F32)
            aw = _pack_bf16_pair(a[:, :LANES], a[:, LANES:])
            for ri in range(2):
                s_ref[pl.ds(ri * half + n2, h1, stride=PITCH_S), :] = aw[ri * h1:(ri + 1) * h1]
            return carry

        lax.fori_loop(0, DFT_N2, body, 0, unroll=CONV_PAIR_UNROLL)

    def freq_body(k1, carry):
        row = pl.multiple_of((hf * kc + k1) * PITCH_S, 8)
        xw = s_ref[pl.ds(row, 2 * half), :]
        x = jnp.concatenate([_unpack_bf16_pair(xw, 0), _unpack_bf16_pair(xw, 1)], axis=1)
        z = jnp.dot(m_ref[hf * kc + k1], x, preferred_element_type=F32)
        o_ref[k1] = _pack_bf16_pair(z[:half, :LANES] + z[:half, LANES:], z[half:, :LANES] - z[half:, LANES:])
        return carry

    lax.fori_loop(0, kc, freq_body, 0, unroll=CONV_FREQ_UNROLL)


def _filter_spectrum(fa, m_bf16, filt):
    _, n_slab, prow, _ = filt.shape
    n1, h1 = fa.shape
    kc = min(CONV_K1_CHUNK, h1)
    return pl.pallas_call(
        functools.partial(_spectrum_kernel, h1=h1, kc=kc),
        grid=(2, n_slab, h1 // kc),
        in_specs=[pl.BlockSpec((n1, h1), lambda o, s, hf: (0, 0)),
                  _resident_table_spec(h1),
                  pl.BlockSpec((None, None, prow, LANES), lambda o, s, hf: (2 * o, s, 0, 0)),
                  pl.BlockSpec((None, None, prow, LANES), lambda o, s, hf: (2 * o + 1, s, 0, 0))],
        out_specs=pl.BlockSpec((None, None, kc, DFT_N2, LANES), lambda o, s, hf: (o, s, hf, 0, 0)),
        out_shape=jax.ShapeDtypeStruct((2, n_slab, h1, DFT_N2, LANES), jnp.uint32),
        scratch_shapes=[pltpu.VMEM((h1 * PITCH_S, LANES), jnp.uint32)],
        compiler_params=pltpu.CompilerParams(dimension_semantics=("parallel", "parallel", "arbitrary"),
                                             vmem_limit_bytes=CONV_VMEM_LIMIT),
        name="filter_spectrum",
    )(fa.astype(BF16), m_bf16, filt, filt)


def _conv_kernel(fa_ref, ga_ref, m_ref, k_ref, z_ref, hx_ref, sk_ref, o_ref, s_ref, *, h1):
    half = DFT_N2
    pairs = DFT_N2 // 2
    fa = fa_ref[...]
    ga = ga_ref[...]
    sk = sk_ref[...]

    def stage_a(j, carry):
        n2 = 2 * j
        zw = [z_ref[pl.ds(n2 + u, h1, stride=PITCH_Z), :] for u in range(2)]
        a = [jnp.dot(fa, jnp.concatenate([_unpack_bf16_pair(zw[0], e), _unpack_bf16_pair(zw[1], e)], axis=1),
                     preferred_element_type=F32) for e in range(2)]
        aw = _pack_bf16_pair(a[0], a[1])
        for ri in range(2):
            for u in range(2):
                s_ref[pl.ds(ri * half + n2 + u, h1, stride=PITCH_S), :] = (
                    aw[ri * h1:(ri + 1) * h1, u * LANES:(u + 1) * LANES])
        return carry

    lax.fori_loop(0, pairs, stage_a, 0, unroll=CONV_PAIR_UNROLL)

    def freq_body(k1, carry):
        row = pl.multiple_of(k1 * PITCH_S, 8)
        xw = s_ref[pl.ds(row, 2 * half), :]
        x = jnp.concatenate([_unpack_bf16_pair(xw, 0), _unpack_bf16_pair(xw, 1)], axis=1)
        m = m_ref[k1]
        z = jnp.dot(m, x, preferred_element_type=F32)
        zr, zi = z[:half], z[half:]
        kw = k_ref[k1]
        kr = jnp.concatenate([_unpack_bf16_pair(kw, 0, F32)] * 2, axis=1)
        ki = jnp.concatenate([_unpack_bf16_pair(kw, 1, F32)] * 2, axis=1)
        y = jnp.concatenate([zr * kr - zi * ki, zr * ki + zi * kr], axis=0).astype(BF16)
        bq = lax.dot_general(m, y, (((0,), (0,)), ((), ())), preferred_element_type=F32)
        s_ref[pl.ds(row, 2 * half), :] = _pack_bf16_pair(bq[:, :LANES], bq[:, LANES:])
        return carry

    lax.fori_loop(0, h1, freq_body, 0, unroll=CONV_FREQ_UNROLL)

    def stage_inv(j, carry):
        n2 = 2 * j
        rw = [s_ref[pl.ds(n2 + u, h1, stride=PITCH_S), :] for u in range(2)]
        iw = [s_ref[pl.ds(half + n2 + u, h1, stride=PITCH_S), :] for u in range(2)]
        zw = [z_ref[pl.ds(n2 + u, h1, stride=PITCH_Z), :] for u in range(2)]
        hw = [hx_ref[pl.ds(n2 + u, h1, stride=PITCH_Z), :] for u in range(2)]
        y = []
        for e in range(2):
            bn = jnp.concatenate(
                [jnp.concatenate([_unpack_bf16_pair(rw[u], e), _unpack_bf16_pair(iw[u], e)], axis=0)
                 for u in range(2)], axis=1)
            y.append(jnp.dot(ga, bn, preferred_element_type=F32))
        for u in range(2):
            res = [_unpack_bf16_pair(hw[u], e, F32) * (y[e][:, u * LANES:(u + 1) * LANES]
                                                       + sk * _unpack_bf16_pair(zw[u], e, F32)) for e in range(2)]
            o_ref[pl.ds(n2 + u, h1, stride=PITCH_Z), :] = _pack_bf16_pair(res[0], res[1])
        return carry

    lax.fori_loop(0, pairs, stage_inv, 0, unroll=CONV_PAIR_UNROLL // 2)
    pad = jnp.zeros((PITCH_Z - DFT_N2, LANES), jnp.uint32)
    for blk in range(h1):
        o_ref[blk * PITCH_Z + DFT_N2:(blk + 1) * PITCH_Z, :] = pad


def _long_conv_gate(tabs, kspec, order, z5, zsel, hx5, hxsel, skip):
    fa, ga, m_tab = tabs
    _, n_slab, n_pair, prow, _ = z5.shape
    n1, h1 = fa.shape
    slab = lambda sel: pl.BlockSpec((None, None, None, prow, LANES), lambda s, b: (sel, s, b, 0, 0))
    return pl.pallas_call(
        functools.partial(_conv_kernel, h1=h1),
        grid=(n_slab, n_pair),
        in_specs=[pl.BlockSpec((n1, h1), lambda s, b: (0, 0)),
                  pl.BlockSpec((h1, n1), lambda s, b: (0, 0)),
                  _resident_table_spec(h1),
                  pl.BlockSpec((None, None, h1, DFT_N2, LANES), lambda s, b: (order, s, 0, 0, 0)),
                  slab(zsel), slab(hxsel),
                  pl.BlockSpec((None, None, 1, LANES), lambda s, b: (order, s, 0, 0))],
        out_specs=pl.BlockSpec((None, None, prow, LANES), lambda s, b: (s, b, 0, 0)),
        out_shape=jax.ShapeDtypeStruct((n_slab, n_pair, prow, LANES), jnp.uint32),
        scratch_shapes=[pltpu.VMEM((h1 * PITCH_S, LANES), jnp.uint32)],
        compiler_params=pltpu.CompilerParams(dimension_semantics=("parallel", "arbitrary"),
                                             vmem_limit_bytes=CONV_VMEM_LIMIT),
        name="hyena_long_conv",
    )(fa.astype(BF16), ga.astype(BF16), m_tab, kspec, z5, hx5, skip)


def _hyena(proj, batch, seq, conv_w, conv_b, fparams, skip):
    c = HYENA_WIDTH
    fa, ga, m_tab = _dft_tables(seq)
    kspec = _filter_spectrum(fa, m_tab, _filters(seq, *fparams))
    u5 = _shortconv(proj, conv_w, conv_b, batch, seq)
    skip4 = skip.astype(F32).reshape(2, c // LANES, 1, LANES)
    tabs = (fa, ga, m_tab)
    z1 = _long_conv_gate(tabs, kspec, 0, u5, 0, u5, 1, skip4)
    return _long_conv_gate(tabs, kspec, 1, z1[None], 0, u5, 2, skip4)


def _final_kernel(o1_ref, o2_ref, o3_ref, l1_ref, l2_ref, l3_ref, hy_ref, x_ref, g_ref, wg_ref,
                  wa_ref, wh_ref, wo_ref, pg_ref, y_ref, *, tiles_per_seq):
    def cat(ref):
        return jnp.concatenate([ref[s] for s in range(ATTN_WIDTH // LANES)], axis=1)

    def cat_packed(ref):
        return jnp.concatenate([_unpack_bf16_pair(ref[s], e, F32)
                                for s in range(ref.shape[0]) for e in range(PAIRS_PER_SLAB)], axis=1)

    l1, l2, l3 = cat(l1_ref), cat(l2_ref), cat(l3_ref)
    mx = jnp.maximum(jnp.maximum(l1, l2), l3)
    e1, e2, e3 = jnp.exp(l1 - mx), jnp.exp(l2 - mx), jnp.exp(l3 - mx)
    attn = (e1 * cat_packed(o1_ref) + e2 * cat_packed(o2_ref) + e3 * cat_packed(o3_ref)) / (e1 + e2 + e3)
    x = x_ref[...]
    hn = (x * lax.rsqrt(jnp.mean(x * x, axis=-1, keepdims=True) + EPS) * g_ref[...]).astype(BF16)
    gates = jnp.dot(hn, wg_ref[...], preferred_element_type=F32)
    gh, ga = gates[:, G_GH:G_MA], gates[:, G_GA:]
    a_in = (attn * (ga * jax.nn.sigmoid(ga))).astype(BF16)
    a_br = jnp.dot(a_in, wa_ref[...], preferred_element_type=F32)
    odd = (pl.program_id(0) // tiles_per_seq) % 2
    hw = _load_padded_slabs(hy_ref, x.shape[0] // DFT_N2)
    hy = lax.bitcast_convert_type((hw << ((1 - odd) * 16).astype(jnp.uint32)) & jnp.uint32(0xFFFF0000), F32)
    h_in = (hy * (gh * jax.nn.sigmoid(gh))).astype(BF16)
    h_br = jnp.dot(h_in, wh_ref[...], preferred_element_type=F32)
    merged = jax.nn.sigmoid(gates[:, G_MA:G_MH]) * a_br + jax.nn.sigmoid(gates[:, G_MH:G_GA]) * h_br
    out = jnp.dot(merged.astype(BF16), wo_ref[...], preferred_element_type=F32)
    ms = jnp.mean(out * out, axis=-1, keepdims=True)
    y_ref[...] = x + out * lax.rsqrt(ms + EPS) * pg_ref[...]


def _final(os_, ls_, hy, x2d, gain, wg, wa, wh, wo, pg, seq, tm=512):
    rows = x2d.shape[0]
    tm = min(tm, rows)
    tiles_per_seq = seq // tm
    hy_spec = pl.BlockSpec((HYENA_WIDTH // LANES, None, tm // DFT_N2 * PITCH_Z, LANES),
                           lambda i: (0, i // (2 * tiles_per_seq), i % tiles_per_seq, 0))
    slab = pl.BlockSpec((ATTN_WIDTH // LANES, tm, LANES), lambda i: (0, i, 0))
    oslab = pl.BlockSpec((ATTN_WIDTH // LANES // PAIRS_PER_SLAB, tm, LANES), lambda i: (0, i, 0))
    full = lambda shape: pl.BlockSpec(shape, lambda i: (0, 0), pipeline_mode=pl.Buffered(1))
    return pl.pallas_call(
        functools.partial(_final_kernel, tiles_per_seq=tiles_per_seq),
        grid=(rows // tm,),
        in_specs=[oslab] * 3 + [slab] * 3 + [hy_spec, pl.BlockSpec((tm, D_MODEL), lambda i: (i, 0)),
                                             full((1, D_MODEL)), full((D_MODEL, GATE_WIDTH)),
                                             full((ATTN_WIDTH, D_MODEL)), full((D_MODEL, D_MODEL)),
                                             full((D_MODEL, D_MODEL)), full((1, D_MODEL))],
        out_specs=pl.BlockSpec((tm, D_MODEL), lambda i: (i, 0)),
        out_shape=jax.ShapeDtypeStruct((rows, D_MODEL), F32),
        compiler_params=_cparams(("parallel",)),
        name="merge_out",
    )(*os_, *ls_, hy, x2d, gain, wg, wa, wh, wo, pg)


def _layer(x, rel_bias, pre_g, post_g, w_in, conv_w, conv_b, fparams, skip, w_br_a, w_br_h, w_out):
    batch, seq, _ = x.shape
    x2d = x.reshape(batch * seq, D_MODEL)
    w_qkv, w_u, w_gate = w_in
    gain = pre_g.reshape(1, -1).astype(F32)
    qkv = _inproj(x2d, gain, w_qkv, tn=QKV_WIDTH // 2, slab_out=True)
    proj = _inproj(x2d, gain, w_u, tn=U_WIDTH, slab_out=False)
    os_, ls_ = [], []
    for g in range(N_GROUPS):
        o, l = _attention_group(qkv, rel_bias, g, batch, seq)
        os_.append(o)
        ls_.append(l)
    hy = _hyena(proj, batch, seq, conv_w, conv_b, fparams, skip)
    y = _final(os_, ls_, hy, x2d, gain, w_gate, w_br_a, w_br_h, w_out, post_g.reshape(1, -1).astype(F32), seq)
    return y.reshape(batch, seq, D_MODEL)


def kernel(x_prompt, x_sample, rel_bias, pre_norm_g, post_norm_g, w_in, conv_w, conv_b, filt_w1, filt_b1, filt_w2, filt_b2, filt_w3, filt_b3, filt_w4, filt_freq, hyena_skip, w_branch_a, w_branch_h, w_out):
    depth = w_in.shape[0]

    def run(x):
        for l in range(depth):
            fparams = (filt_w1[l], filt_b1[l], filt_w2[l], filt_b2[l], filt_w3[l], filt_b3[l], filt_w4[l], filt_freq[l])
            w = w_in[l]
            col_scale = jnp.where(jnp.arange(QKV_WIDTH) < QKV_WIDTH // 3, HEAD_DIM ** -0.5, 1.0).astype(F32)
            w_qkv = (w[:, :QKV_WIDTH] * col_scale).astype(BF16)
            w_u = w[:, U_OFF:GH_OFF].astype(BF16)
            w_gate = jnp.concatenate([w[:, GH_OFF:], w[:, GA_OFF:U_OFF]], axis=1).astype(BF16)
            x = _layer(x, rel_bias, pre_norm_g[l], post_norm_g[l], (w_qkv, w_u, w_gate), conv_w[l], conv_b[l],
                       fparams, hyena_skip[l], w_branch_a[l].astype(BF16), w_branch_h[l].astype(BF16),
                       w_out[l].astype(BF16))
        return x

    return (run(x_prompt), run(x_sample))
```

```python
import functools
import math

import numpy as np
import jax
import jax.numpy as jnp
from jax import lax
from jax.experimental import pallas as pl
from jax.experimental.pallas import tpu as pltpu

F32 = jnp.float32
BF16 = jnp.bfloat16

D_MODEL = 1024
EPS = 1e-6
HEAD_DIM = 64
ATTN_GROUPS = ((128, 1), (512, 4), (2048, 16))
N_GROUPS = 3
HEADS_PER_GROUP = 8
ATTN_WIDTH = HEADS_PER_GROUP * HEAD_DIM
HYENA_WIDTH = 1024
FILTER_EMB = 33
N_BANDS = 16
FILTER_HIDDEN = 64
NUM_BUCKETS = 32
MAX_DISTANCE = 1024
NEG_INF = -1e30
QKV_WIDTH = 4608
GA_OFF, U_OFF, GH_OFF = 4608, 5120, 8192
U_WIDTH = 3072
GATE_WIDTH = 3584
G_GH, G_MA, G_MH, G_GA = 0, 1024, 2048, 3072

WIN = 64
LANES = 128
DFT_N2 = 128
PITCH_Z = 136
PITCH_S = 264
VMEM_LIMIT = 48 * 1024 * 1024
CONV_VMEM_LIMIT = 58 * 1024 * 1024


def _cparams(sem):
    return pltpu.CompilerParams(dimension_semantics=sem, vmem_limit_bytes=VMEM_LIMIT)


def _inproj_kernel(x_ref, g_ref, w_ref, o_ref, hn_ref, *, slabs):
    @pl.when(pl.program_id(1) == 0)
    def _():
        x = x_ref[...]
        ms = jnp.mean(x * x, axis=-1, keepdims=True)
        hn_ref[...] = (x * lax.rsqrt(ms + EPS) * g_ref[...]).astype(BF16)

    acc = jnp.dot(hn_ref[...], w_ref[...], preferred_element_type=F32)
    if slabs:
        for s in range(slabs):
            o_ref[s] = _pack_bf16_pair(acc[:, (2 * s) * LANES:(2 * s + 1) * LANES],
                                       acc[:, (2 * s + 1) * LANES:(2 * s + 2) * LANES])
    else:
        o_ref[...] = acc.astype(o_ref.dtype)


def _pack_bf16_pair(a, b):
    ua = lax.bitcast_convert_type(a.astype(BF16).astype(F32), jnp.uint32)
    ub = lax.bitcast_convert_type(b.astype(BF16).astype(F32), jnp.uint32)
    return (ua >> 16) | (ub & jnp.uint32(0xFFFF0000))


def _unpack_bf16_pair(w, idx, dtype=BF16):
    bits = (w << 16) if idx == 0 else (w & jnp.uint32(0xFFFF0000))
    return lax.bitcast_convert_type(bits, F32).astype(dtype)


def _inproj(x2d, g, w_bf16, tn, slab_out, tm=1024):
    rows = x2d.shape[0]
    width = w_bf16.shape[1]
    tm = min(tm, rows)
    if slab_out:
        ns = tn // (2 * LANES)
        out_spec = pl.BlockSpec((ns, tm, LANES), lambda i, j: (j, i, 0))
        out_shape = jax.ShapeDtypeStruct((width // (2 * LANES), rows, LANES), jnp.uint32)
    else:
        ns = 0
        out_spec = pl.BlockSpec((tm, tn), lambda i, j: (i, j))
        out_shape = jax.ShapeDtypeStruct((rows, width), BF16)
    return pl.pallas_call(
        functools.partial(_inproj_kernel, slabs=ns),
        grid=(rows // tm, width // tn),
        in_specs=[
            pl.BlockSpec((tm, D_MODEL), lambda i, j: (i, 0)),
            pl.BlockSpec((1, D_MODEL), lambda i, j: (0, 0)),
            pl.BlockSpec((D_MODEL, tn), lambda i, j: (0, j)),
        ],
        out_specs=out_spec,
        out_shape=out_shape,
        scratch_shapes=[pltpu.VMEM((tm, D_MODEL), BF16)],
        compiler_params=_cparams(("parallel", "arbitrary")),
        name="inproj_qkv" if slab_out else "inproj_u",
    )(x2d, g, w_bf16)


def _t5_bucket_np(rel):
    half = NUM_BUCKETS // 2
    max_exact = half // 2
    n = np.abs(rel)
    nf = np.maximum(n, 1).astype(np.float64)
    large = max_exact + (np.log(nf / max_exact) / math.log(MAX_DISTANCE / max_exact) * (half - max_exact)).astype(np.int64)
    large = np.minimum(large, half - 1)
    return np.where(rel > 0, half, 0) + np.where(n < max_exact, n, large)


def _bias_mask(rel_bias_g, dilation):
    qi = np.arange(2 * WIN)[:, None]
    kj = np.arange(4 * WIN)[None, :] - WIN
    delta = kj - qi
    bucket = _t5_bucket_np(delta * dilation).astype(np.int32)
    onehot = jax.nn.one_hot(jnp.asarray(bucket), NUM_BUCKETS, dtype=F32)
    bias = jnp.einsum('qkb,bh->hqk', onehot, rel_bias_g.astype(F32), precision=lax.Precision.HIGHEST)
    bias = jnp.where(jnp.asarray(np.abs(delta) <= WIN)[None], bias, NEG_INF)
    return bias.reshape(HEADS_PER_GROUP // 2, 2, 2 * WIN, 4 * WIN)


ATTN_POS_PER_STEP = 4096
ATTN_TILES_PER_ITER = 8
PAIRS_PER_SLAB = 2


def _attn_kernel(q_ref, kp_ref, kc_ref, kn_ref, vp_ref, vc_ref, vn_ref, bm_ref, o_ref, l_ref,
                 kbuf, vbuf, *, tq, dil, sub_len):
    t = pl.program_id(2)
    halo = WIN * dil
    span = tq * dil
    kbuf[0:halo] = kp_ref[...]
    kbuf[halo:halo + span] = kc_ref[...]
    kbuf[halo + span:] = kn_ref[...]
    vbuf[0:halo] = vp_ref[...]
    vbuf[halo:halo + span] = vc_ref[...]
    vbuf[halo + span:] = vn_ref[...]
    lane = lax.broadcasted_iota(jnp.int32, (1, LANES), 1)
    is_lo = lane < HEAD_DIM
    kcol = lax.broadcasted_iota(jnp.int32, (1, 4 * WIN), 1)
    qt = 2 * WIN
    nq = tq // qt

    def tile(c, i):
        row0 = i * (qt * dil) + c
        kpos = t * tq + i * qt - WIN + kcol
        pen = jnp.where((kpos >= 0) & (kpos < sub_len), 0.0, NEG_INF).astype(F32)
        qw = q_ref[pl.ds(row0, qt, stride=dil), :]
        kw = kbuf[pl.ds(row0, 2 * qt, stride=dil), :]
        vw = vbuf[pl.ds(row0, 2 * qt, stride=dil), :]
        o_pair = []
        for hp in range(PAIRS_PER_SLAB):
            q = _unpack_bf16_pair(qw, hp)
            k = _unpack_bf16_pair(kw, hp)
            v = _unpack_bf16_pair(vw, hp)
            outs, lses = [], []
            for hh in range(2):
                sel = is_lo if hh == 0 else jnp.logical_not(is_lo)
                qm = jnp.where(sel, q, jnp.zeros_like(q))
                s = lax.dot_general(qm, k, (((1,), (1,)), ((), ())), preferred_element_type=F32)
                s = s + bm_ref[hp, hh] + pen
                m = jnp.max(s, axis=-1, keepdims=True)
                p = jnp.exp(s - m)
                den = jnp.sum(p, axis=-1, keepdims=True)
                pv = jnp.dot(p.astype(BF16), v, preferred_element_type=F32)
                outs.append(pv / den)
                lses.append(m + jnp.log(den))
            o_pair.append(jnp.where(is_lo, outs[0], outs[1]))
            l_ref[hp, pl.ds(row0, qt, stride=dil), :] = jnp.where(is_lo, lses[0], lses[1])
        o_ref[pl.ds(row0, qt, stride=dil), :] = _pack_bf16_pair(o_pair[0], o_pair[1])

    def body(it, carry):
        for u in range(ATTN_TILES_PER_ITER):
            idx = it * ATTN_TILES_PER_ITER + u
            tile(idx // nq, idx % nq)
        return carry

    lax.fori_loop(0, dil * nq // ATTN_TILES_PER_ITER, body, 0)


def _attention_group(qkv, rel_bias, g, batch, seq):
    _, dil = ATTN_GROUPS[g]
    sub_len = seq // dil
    tq = min(ATTN_POS_PER_STEP // dil, sub_len)
    nt = sub_len // tq
    span = tq * dil
    halo = WIN * dil
    n_hp = ATTN_WIDTH // LANES
    n_ps = n_hp // PAIRS_PER_SLAB
    n_slab = N_GROUPS * n_ps
    bm = _bias_mask(rel_bias[:, g * HEADS_PER_GROUP:(g + 1) * HEADS_PER_GROUP], dil)
    hb = tq // WIN
    nhb = sub_len // WIN

    def cur(which):
        return pl.BlockSpec((None, span, LANES), lambda b, ps, t: (which * n_slab + g * n_ps + ps, b * nt + t, 0))

    def prev(which):
        return pl.BlockSpec((None, halo, LANES),
                            lambda b, ps, t: (which * n_slab + g * n_ps + ps, b * nhb + jnp.maximum(t * hb - 1, 0), 0))

    def nxt(which):
        return pl.BlockSpec((None, halo, LANES),
                            lambda b, ps, t: (which * n_slab + g * n_ps + ps, b * nhb + jnp.minimum((t + 1) * hb, nhb - 1), 0))

    out_spec = pl.BlockSpec((PAIRS_PER_SLAB, span, LANES), lambda b, ps, t: (ps, b * nt + t, 0))
    return pl.pallas_call(
        functools.partial(_attn_kernel, tq=tq, dil=dil, sub_len=sub_len),
        grid=(batch, n_ps, nt),
        in_specs=[cur(0), prev(1), cur(1), nxt(1), prev(2), cur(2), nxt(2),
                  pl.BlockSpec((PAIRS_PER_SLAB, 2, 2 * WIN, 4 * WIN), lambda b, ps, t: (ps, 0, 0, 0))],
        out_specs=[pl.BlockSpec((None, span, LANES), lambda b, ps, t: (ps, b * nt + t, 0)), out_spec],
        out_shape=[jax.ShapeDtypeStruct((n_ps, batch * seq, LANES), jnp.uint32),
                   jax.ShapeDtypeStruct((n_hp, batch * seq, LANES), F32)],
        scratch_shapes=[pltpu.VMEM((span + 2 * halo, LANES), jnp.uint32)] * 2,
        compiler_params=_cparams(("parallel", "parallel", "arbitrary")),
        name=f"attn_g{g}",
    )(qkv, qkv, qkv, qkv, qkv, qkv, qkv, bm)


def _shortconv_kernel(p_ref, c_ref, n_ref, w_ref, b_ref, o_ref, x_buf, *, rows, halo):
    t = pl.program_id(1)
    nt = pl.num_programs(1)
    has_prev = (t > 0).astype(F32)
    has_next = (t < nt - 1).astype(F32)
    n_slab = c_ref.shape[2] // LANES
    for e in range(2):
        x = c_ref[e].astype(F32)
        lo = p_ref[e].astype(F32)[halo - 8:halo, :] * has_prev
        hi = n_ref[e].astype(F32)[0:8, :] * has_next
        for s in range(n_slab):
            cs = slice(s * LANES, (s + 1) * LANES)
            x_buf[e, s, 0:8, :] = lo[:, cs]
            x_buf[e, s, 8:rows + 8, :] = x[:, cs]
            x_buf[e, s, rows + 8:rows + 16, :] = hi[:, cs]
    nblk = rows // DFT_N2
    pad = jnp.zeros((PITCH_Z - DFT_N2, LANES), jnp.uint32)
    for s in range(n_slab):
        cs = slice(s * LANES, (s + 1) * LANES)
        w0, w1, w2, bias = w_ref[0:1, cs], w_ref[1:2, cs], w_ref[2:3, cs], b_ref[:, cs]
        res = []
        for e in range(2):
            xs = x_buf.at[e, s]
            res.append(xs[pl.ds(7, rows, stride=1), :] * w0 + xs[pl.ds(8, rows, stride=1), :] * w1
                       + xs[pl.ds(9, rows, stride=1), :] * w2 + bias)
        packed = _pack_bf16_pair(res[0], res[1])
        for jb in range(nblk):
            o_ref[s, jb * PITCH_Z:jb * PITCH_Z + DFT_N2, :] = packed[jb * DFT_N2:(jb + 1) * DFT_N2]
            o_ref[s, jb * PITCH_Z + DFT_N2:(jb + 1) * PITCH_Z, :] = pad


def _store_padded_slabs(o_ref, val, nblk):
    pad = jnp.zeros((PITCH_Z - DFT_N2, LANES), val.dtype)
    for s in range(val.shape[1] // LANES):
        for jb in range(nblk):
            o_ref[s, jb * PITCH_Z:jb * PITCH_Z + DFT_N2, :] = val[jb * DFT_N2:(jb + 1) * DFT_N2, s * LANES:(s + 1) * LANES]
            o_ref[s, jb * PITCH_Z + DFT_N2:(jb + 1) * PITCH_Z, :] = pad


def _load_padded_slabs(ref, nblk):
    return jnp.concatenate(
        [jnp.concatenate([ref[s, jb * PITCH_Z:jb * PITCH_Z + DFT_N2, :] for jb in range(nblk)], axis=0)
         for s in range(ref.shape[0])], axis=1)


def _shortconv(proj, conv_w, conv_b, batch, seq, rows=1024):
    assert batch % 2 == 0
    rows = min(rows, seq)
    prow = rows // DFT_N2 * PITCH_Z
    n_slab = HYENA_WIDTH // LANES
    halo = 16
    pv = proj.reshape(batch, seq, U_WIDTH)
    cb = 0
    hb = rows // halo
    nhb = seq // halo
    return pl.pallas_call(
        functools.partial(_shortconv_kernel, rows=rows, halo=halo),
        grid=(batch // 2, seq // rows, 3),
        in_specs=[
            pl.BlockSpec((2, halo, HYENA_WIDTH), lambda b, t, j: (b, jnp.maximum(t * hb - 1, 0), cb + j)),
            pl.BlockSpec((2, rows, HYENA_WIDTH), lambda b, t, j: (b, t, cb + j)),
            pl.BlockSpec((2, halo, HYENA_WIDTH), lambda b, t, j: (b, jnp.minimum((t + 1) * hb, nhb - 1), cb + j)),
            pl.BlockSpec((3, HYENA_WIDTH), lambda b, t, j: (0, j)),
            pl.BlockSpec((1, HYENA_WIDTH), lambda b, t, j: (0, j)),
        ],
        out_specs=pl.BlockSpec((None, n_slab, None, prow, LANES), lambda b, t, j: (j, 0, b, t, 0)),
        out_shape=jax.ShapeDtypeStruct((3, n_slab, batch // 2, seq // DFT_N2 * PITCH_Z, LANES), jnp.uint32),
        scratch_shapes=[pltpu.VMEM((2, n_slab, rows + 16, LANES), F32)],
        compiler_params=_cparams(("parallel", "parallel", "arbitrary")),
        name="shortconv",
    )(pv, pv, pv, conv_w, conv_b.reshape(1, -1))


def _filter_features(length):
    t = np.linspace(0.0, 1.0, length)[:, None]
    ang = (2.0 * math.pi / length) * np.arange(length, dtype=np.float64)[:, None]
    bands = np.linspace(1e-4, N_BANDS - 1, N_BANDS)[None]
    z = np.concatenate([t, np.cos(ang * bands), -np.sin(ang * bands)], axis=-1)
    zp = np.zeros((length, LANES), np.float32)
    zp[:, :FILTER_EMB] = z
    return zp


def _decay_rates():
    max_decay = math.log(1e-2) / 0.3
    min_decay = math.log(1e-2) / 1.5
    return np.abs(np.linspace(min_decay, max_decay, HYENA_WIDTH)).astype(np.float32)[None]


def _filter_kernel(z_ref, w1_ref, b1_ref, w2_ref, b2_ref, w3_ref, b3_ref, w4_ref, fr_ref, dl_ref, o_ref, *, rows):
    hi = lax.Precision.HIGHEST
    z = z_ref[...]
    fr = fr_ref[...]
    hr = rows // 2
    pre = jnp.concatenate([jnp.dot(z[:hr], w1_ref[...], precision=hi, preferred_element_type=F32),
                           jnp.dot(z[hr:], w1_ref[...], precision=hi, preferred_element_type=F32)], axis=1)
    h = jnp.sin(fr * (pre + b1_ref[...]))
    h = jnp.sin(fr * (jnp.dot(h, w2_ref[...], precision=hi, preferred_element_type=F32) + b2_ref[...]))
    h = jnp.sin(fr * (jnp.dot(h, w3_ref[...], precision=hi, preferred_element_type=F32) + b3_ref[...]))
    h = jnp.concatenate([h[:, :FILTER_HIDDEN], h[:, FILTER_HIDDEN:]], axis=0)
    hh, hl = _split(h)
    filt = jnp.dot(jnp.concatenate([hh, hl, hh], axis=1), w4_ref[...], preferred_element_type=F32)
    decay = jnp.exp(-z[:, 0:1] * dl_ref[...])
    row = pl.program_id(0) * rows + lax.broadcasted_iota(jnp.int32, (rows, 1), 0)
    for j in range(4):
        cs = slice(j * HYENA_WIDTH, (j + 1) * HYENA_WIDTH)
        val = filt[:, cs] * decay
        if j % 2 == 1:
            val = jnp.where(row == 0, 0.0, val)
        _store_padded_slabs(o_ref.at[j], val, rows // DFT_N2)


def _filters(length, fw1, fb1, fw2, fb2, fw3, fb3, fw4, ffreq, rows=1024):
    rows = min(rows, length)
    n_slab = HYENA_WIDTH // LANES
    prow = rows // DFT_N2 * PITCH_Z
    w4h, w4l = _split(fw4.astype(F32))
    w4s = jnp.concatenate([w4h, w4h, w4l], axis=0)
    zfeat = jnp.asarray(_filter_features(length))
    w1p = jnp.zeros((LANES, FILTER_HIDDEN), F32).at[:FILTER_EMB].set(fw1.astype(F32))
    twice = lambda v: jnp.tile(v.astype(F32).reshape(1, -1), (1, 2))
    zero = jnp.zeros((FILTER_HIDDEN, FILTER_HIDDEN), F32)
    bdiag = lambda w: jnp.block([[w.astype(F32), zero], [zero, w.astype(F32)]])
    full = lambda shape: pl.BlockSpec(shape, lambda i: (0,) * len(shape))
    return pl.pallas_call(
        functools.partial(_filter_kernel, rows=rows),
        grid=(length // rows,),
        in_specs=[pl.BlockSpec((rows, LANES), lambda i: (i, 0)),
                  full((LANES, FILTER_HIDDEN)), full((1, LANES)),
                  full((LANES, LANES)), full((1, LANES)),
                  full((LANES, LANES)), full((1, LANES)),
                  full((3 * FILTER_HIDDEN, 4 * HYENA_WIDTH)), full((1, LANES)),
                  full((1, HYENA_WIDTH))],
        out_specs=pl.BlockSpec((4, n_slab, prow, LANES), lambda i: (0, 0, i, 0)),
        out_shape=jax.ShapeDtypeStruct((4, n_slab, length // DFT_N2 * PITCH_Z, LANES), F32),
        compiler_params=_cparams(("parallel",)),
        name="hyena_filters",
    )(zfeat, w1p, twice(fb1), bdiag(fw2), twice(fb2), bdiag(fw3), twice(fb3), w4s,
      twice(ffreq), jnp.asarray(_decay_rates()))


def _dft_tables(length):
    n = 2 * length
    n1 = n // DFT_N2
    h1 = n1 // 2
    kk = np.arange(h1)[:, None] + 0.5
    th = 2.0 * math.pi * kk * np.arange(h1)[None, :] / n1
    fa = np.concatenate([np.cos(th), -np.sin(th)], axis=0)
    ga = (2.0 / n) * fa.T
    k = np.arange(h1)[:, None, None] + n1 * np.arange(DFT_N2)[None, :, None] + 0.5
    ph = 2.0 * math.pi * k * np.arange(DFT_N2)[None, None, :] / n
    f = lambda a: jnp.asarray(a.astype(np.float32))
    er, ei = f(np.cos(ph)), f(-np.sin(ph))
    m = jnp.concatenate([jnp.concatenate([er, -ei], axis=2), jnp.concatenate([ei, er], axis=2)], axis=1)
    return f(fa), f(ga), m.astype(BF16)


def _split(a):
    hi = a.astype(BF16)
    lo = (a - hi.astype(F32)).astype(BF16)
    return hi, lo


CONV_K1_CHUNK = 64
CONV_PAIR_UNROLL = 64
CONV_FREQ_UNROLL = 32


def _resident_table_spec(h1):
    return pl.BlockSpec((h1, 2 * DFT_N2, 2 * DFT_N2), lambda *_: (0, 0, 0), pipeline_mode=pl.Buffered(1))


def _spectrum_kernel(fa_ref, m_ref, xf_ref, xb_ref, o_ref, s_ref, *, h1, kc):
    hf = pl.program_id(2)
    half = DFT_N2

    @pl.when(hf == 0)
    def _():
        fa = fa_ref[...]

        def body(n2, carry):
            x = jnp.concatenate([xf_ref[pl.ds(n2, h1, stride=PITCH_Z), :],
                                 xb_ref[pl.ds(n2, h1, stride=PITCH_Z), :]], axis=1).astype(BF16)
            a = jnp.dot(fa, x, preferred_element_type=F32)
            aw = _pack_bf16_pair(a[:, :LANES], a[:, LANES:])
            for ri in range(2):
                s_ref[pl.ds(ri * half + n2, h1, stride=PITCH_S), :] = aw[ri * h1:(ri + 1) * h1]
            return carry

        lax.fori_loop(0, DFT_N2, body, 0, unroll=CONV_PAIR_UNROLL)

    def freq_body(k1, carry):
        row = pl.multiple_of((hf * kc + k1) * PITCH_S, 8)
        xw = s_ref[pl.ds(row, 2 * half), :]
        x = jnp.concatenate([_unpack_bf16_pair(xw, 0), _unpack_bf16_pair(xw, 1)], axis=1)
        z = jnp.dot(m_ref[hf * kc + k1], x, preferred_element_type=F32)
        o_ref[k1] = _pack_bf16_pair(z[:half, :LANES] + z[:half, LANES:], z[half:, :LANES] - z[half:, LANES:])
        return carry

    lax.fori_loop(0, kc, freq_body, 0, unroll=CONV_FREQ_UNROLL)


def _filter_spectrum(fa, m_bf16, filt):
    _, n_slab, prow, _ = filt.shape
    n1, h1 = fa.shape
    kc = min(CONV_K1_CHUNK, h1)
    return pl.pallas_call(
        functools.partial(_spectrum_kernel, h1=h1, kc=kc),
        grid=(2, n_slab, h1 // kc),
        in_specs=[pl.BlockSpec((n1, h1), lambda o, s, hf: (0, 0)),
                  _resident_table_spec(h1),
                  pl.BlockSpec((None, None, prow, LANES), lambda o, s, hf: (2 * o, s, 0, 0)),
                  pl.BlockSpec((None, None, prow, LANES), lambda o, s, hf: (2 * o + 1, s, 0, 0))],
        out_specs=pl.BlockSpec((None, None, kc, DFT_N2, LANES), lambda o, s, hf: (o, s, hf, 0, 0)),
        out_shape=jax.ShapeDtypeStruct((2, n_slab, h1, DFT_N2, LANES), jnp.uint32),
        scratch_shapes=[pltpu.VMEM((h1 * PITCH_S, LANES), jnp.uint32)],
        compiler_params=pltpu.CompilerParams(dimension_semantics=("parallel", "parallel", "arbitrary"),
                                             vmem_limit_bytes=CONV_VMEM_LIMIT),
        name="filter_spectrum",
    )(fa.astype(BF16), m_bf16, filt, filt)


def _conv_kernel(fa_ref, ga_ref, m_ref, k_ref, z_ref, hx_ref, sk_ref, o_ref, s_ref, *, h1):
    half = DFT_N2
    pairs = DFT_N2 // 2
    fa = fa_ref[...]
    ga = ga_ref[...]
    sk = sk_ref[...]

    def stage_a(j, carry):
        n2 = 2 * j
        zw = [z_ref[pl.ds(n2 + u, h1, stride=PITCH_Z), :] for u in range(2)]
        a = [jnp.dot(fa, jnp.concatenate([_unpack_bf16_pair(zw[0], e), _unpack_bf16_pair(zw[1], e)], axis=1),
                     preferred_element_type=F32) for e in range(2)]
        aw = _pack_bf16_pair(a[0], a[1])
        for ri in range(2):
            for u in range(2):
                s_ref[pl.ds(ri * half + n2 + u, h1, stride=PITCH_S), :] = (
                    aw[ri * h1:(ri + 1) * h1, u * LANES:(u + 1) * LANES])
        return carry

    lax.fori_loop(0, pairs, stage_a, 0, unroll=CONV_PAIR_UNROLL)

    def freq_body(k1, carry):
        row = pl.multiple_of(k1 * PITCH_S, 8)
        xw = s_ref[pl.ds(row, 2 * half), :]
        x = jnp.concatenate([_unpack_bf16_pair(xw, 0), _unpack_bf16_pair(xw, 1)], axis=1)
        m = m_ref[k1]
        z = jnp.dot(m, x, preferred_element_type=F32)
        zr, zi = z[:half], z[half:]
        kw = k_ref[k1]
        kr = jnp.concatenate([_unpack_bf16_pair(kw, 0, F32)] * 2, axis=1)
        ki = jnp.concatenate([_unpack_bf16_pair(kw, 1, F32)] * 2, axis=1)
        y = jnp.concatenate([zr * kr - zi * ki, zr * ki + zi * kr], axis=0).astype(BF16)
        bq = lax.dot_general(m, y, (((0,), (0,)), ((), ())), preferred_element_type=F32)
        s_ref[pl.ds(row, 2 * half), :] = _pack_bf16_pair(bq[:, :LANES], bq[:, LANES:])
        return carry

    lax.fori_loop(0, h1, freq_body, 0, unroll=CONV_FREQ_UNROLL)

    def stage_inv(j, carry):
        n2 = 2 * j
        rw = [s_ref[pl.ds(n2 + u, h1, stride=PITCH_S), :] for u in range(2)]
        iw = [s_ref[pl.ds(half + n2 + u, h1, stride=PITCH_S), :] for u in range(2)]
        zw = [z_ref[pl.ds(n2 + u, h1, stride=PITCH_Z), :] for u in range(2)]
        hw = [hx_ref[pl.ds(n2 + u, h1, stride=PITCH_Z), :] for u in range(2)]
        y = []
        for e in range(2):
            bn = jnp.concatenate(
                [jnp.concatenate([_unpack_bf16_pair(rw[u], e), _unpack_bf16_pair(iw[u], e)], axis=0)
                 for u in range(2)], axis=1)
            y.append(jnp.dot(ga, bn, preferred_element_type=F32))
        for u in range(2):
            res = [_unpack_bf16_pair(hw[u], e, F32) * (y[e][:, u * LANES:(u + 1) * LANES]
                                                       + sk * _unpack_bf16_pair(zw[u], e, F32)) for e in range(2)]
            o_ref[pl.ds(n2 + u, h1, stride=PITCH_Z), :] = _pack_bf16_pair(res[0], res[1])
        return carry

    lax.fori_loop(0, pairs, stage_inv, 0, unroll=CONV_PAIR_UNROLL // 2)
    pad = jnp.zeros((PITCH_Z - DFT_N2, LANES), jnp.uint32)
    for blk in range(h1):
        o_ref[blk * PITCH_Z + DFT_N2:(blk + 1) * PITCH_Z, :] = pad


def _long_conv_gate(tabs, kspec, order, z5, zsel, hx5, hxsel, skip):
    fa, ga, m_tab = tabs
    _, n_slab, n_pair, prow, _ = z5.shape
    n1, h1 = fa.shape
    slab = lambda sel: pl.BlockSpec((None, None, None, prow, LANES), lambda s, b: (sel, s, b, 0, 0))
    return pl.pallas_call(
        functools.partial(_conv_kernel, h1=h1),
        grid=(n_slab, n_pair),
        in_specs=[pl.BlockSpec((n1, h1), lambda s, b: (0, 0)),
                  pl.BlockSpec((h1, n1), lambda s, b: (0, 0)),
                  _resident_table_spec(h1),
                  pl.BlockSpec((None, None, h1, DFT_N2, LANES), lambda s, b: (order, s, 0, 0, 0)),
                  slab(zsel), slab(hxsel),
                  pl.BlockSpec((None, None, 1, LANES), lambda s, b: (order, s, 0, 0))],
        out_specs=pl.BlockSpec((None, None, prow, LANES), lambda s, b: (s, b, 0, 0)),
        out_shape=jax.ShapeDtypeStruct((n_slab, n_pair, prow, LANES), jnp.uint32),
        scratch_shapes=[pltpu.VMEM((h1 * PITCH_S, LANES), jnp.uint32)],
        compiler_params=pltpu.CompilerParams(dimension_semantics=("parallel", "arbitrary"),
                                             vmem_limit_bytes=CONV_VMEM_LIMIT),
        name="hyena_long_conv",
    )(fa.astype(BF16), ga.astype(BF16), m_tab, kspec, z5, hx5, skip)


def _hyena(proj, batch, seq, conv_w, conv_b, fparams, skip):
    c = HYENA_WIDTH
    fa, ga, m_tab = _dft_tables(seq)
    kspec = _filter_spectrum(fa, m_tab, _filters(seq, *fparams))
    u5 = _shortconv(proj, conv_w, conv_b, batch, seq)
    skip4 = skip.astype(F32).reshape(2, c // LANES, 1, LANES)
    tabs = (fa, ga, m_tab)
    z1 = _long_conv_gate(tabs, kspec, 0, u5, 0, u5, 1, skip4)
    return _long_conv_gate(tabs, kspec, 1, z1[None], 0, u5, 2, skip4)


def _final_kernel(o1_ref, o2_ref, o3_ref, l1_ref, l2_ref, l3_ref, hy_ref, x_ref, g_ref, wg_ref,
                  wa_ref, wh_ref, wo_ref, pg_ref, y_ref, *, tiles_per_seq):
    def cat(ref):
        return jnp.concatenate([ref[s] for s in range(ATTN_WIDTH // LANES)], axis=1)

    def cat_packed(ref):
        return jnp.concatenate([_unpack_bf16_pair(ref[s], e, F32)
                                for s in range(ref.shape[0]) for e in range(PAIRS_PER_SLAB)], axis=1)

    l1, l2, l3 = cat(l1_ref), cat(l2_ref), cat(l3_ref)
    mx = jnp.maximum(jnp.maximum(l1, l2), l3)
    e1, e2, e3 = jnp.exp(l1 - mx), jnp.exp(l2 - mx), jnp.exp(l3 - mx)
    attn = (e1 * cat_packed(o1_ref) + e2 * cat_packed(o2_ref) + e3 * cat_packed(o3_ref)) / (e1 + e2 + e3)
    x = x_ref[...]
    hn = (x * lax.rsqrt(jnp.mean(x * x, axis=-1, keepdims=True) + EPS) * g_ref[...]).astype(BF16)
    gates = jnp.dot(hn, wg_ref[...], preferred_element_type=F32)
    gh, ga = gates[:, G_GH:G_MA], gates[:, G_GA:]
    a_in = (attn * (ga * jax.nn.sigmoid(ga))).astype(BF16)
    a_br = jnp.dot(a_in, wa_ref[...], preferred_element_type=F32)
    odd = (pl.program_id(0) // tiles_per_seq) % 2
    hw = _load_padded_slabs(hy_ref, x.shape[0] // DFT_N2)
    hy = lax.bitcast_convert_type((hw << ((1 - odd) * 16).astype(jnp.uint32)) & jnp.uint32(0xFFFF0000), F32)
    h_in = (hy * (gh * jax.nn.sigmoid(gh))).astype(BF16)
    h_br = jnp.dot(h_in, wh_ref[...], preferred_element_type=F32)
    merged = jax.nn.sigmoid(gates[:, G_MA:G_MH]) * a_br + jax.nn.sigmoid(gates[:, G_MH:G_GA]) * h_br
    out = jnp.dot(merged.astype(BF16), wo_ref[...], preferred_element_type=F32)
    ms = jnp.mean(out * out, axis=-1, keepdims=True)
    y_ref[...] = x + out * lax.rsqrt(ms + EPS) * pg_ref[...]


def _final(os_, ls_, hy, x2d, gain, wg, wa, wh, wo, pg, seq, tm=512):
    rows = x2d.shape[0]
    tm = min(tm, rows)
    tiles_per_seq = seq // tm
    hy_spec = pl.BlockSpec((HYENA_WIDTH // LANES, None, tm // DFT_N2 * PITCH_Z, LANES),
                           lambda i: (0, i // (2 * tiles_per_seq), i % tiles_per_seq, 0))
    slab = pl.BlockSpec((ATTN_WIDTH // LANES, tm, LANES), lambda i: (0, i, 0))
    oslab = pl.BlockSpec((ATTN_WIDTH // LANES // PAIRS_PER_SLAB, tm, LANES), lambda i: (0, i, 0))
    full = lambda shape: pl.BlockSpec(shape, lambda i: (0, 0), pipeline_mode=pl.Buffered(1))
    return pl.pallas_call(
        functools.partial(_final_kernel, tiles_per_seq=tiles_per_seq),
        grid=(rows // tm,),
        in_specs=[oslab] * 3 + [slab] * 3 + [hy_spec, pl.BlockSpec((tm, D_MODEL), lambda i: (i, 0)),
                                             full((1, D_MODEL)), full((D_MODEL, GATE_WIDTH)),
                                             full((ATTN_WIDTH, D_MODEL)), full((D_MODEL, D_MODEL)),
                                             full((D_MODEL, D_MODEL)), full((1, D_MODEL))],
        out_specs=pl.BlockSpec((tm, D_MODEL), lambda i: (i, 0)),
        out_shape=jax.ShapeDtypeStruct((rows, D_MODEL), F32),
        compiler_params=_cparams(("parallel",)),
        name="merge_out",
    )(*os_, *ls_, hy, x2d, gain, wg, wa, wh, wo, pg)


def _layer(x, rel_bias, pre_g, post_g, w_in, conv_w, conv_b, fparams, skip, w_br_a, w_br_h, w_out):
    batch, seq, _ = x.shape
    x2d = x.reshape(batch * seq, D_MODEL)
    w_qkv, w_u, w_gate = w_in
    gain = pre_g.reshape(1, -1).astype(F32)
    qkv = _inproj(x2d, gain, w_qkv, tn=QKV_WIDTH // 2, slab_out=True)
    proj = _inproj(x2d, gain, w_u, tn=U_WIDTH, slab_out=False)
    os_, ls_ = [], []
    for g in range(N_GROUPS):
        o, l = _attention_group(qkv, rel_bias, g, batch, seq)
        os_.append(o)
        ls_.append(l)
    hy = _hyena(proj, batch, seq, conv_w, conv_b, fparams, skip)
    y = _final(os_, ls_, hy, x2d, gain, w_gate, w_br_a, w_br_h, w_out, post_g.reshape(1, -1).astype(F32), seq)
    return y.reshape(batch, seq, D_MODEL)


def kernel(x_prompt, x_sample, rel_bias, pre_norm_g, post_norm_g, w_in, conv_w, conv_b, filt_w1, filt_b1, filt_w2, filt_b2, filt_w3, filt_b3, filt_w4, filt_freq, hyena_skip, w_branch_a, w_branch_h, w_out):
    depth = w_in.shape[0]

    def run(x):
        for l in range(depth):
            fparams = (filt_w1[l], filt_b1[l], filt_w2[l], filt_b2[l], filt_w3[l], filt_b3[l], filt_w4[l], filt_freq[l])
            w = w_in[l]
            col_scale = jnp.where(jnp.arange(QKV_WIDTH) < QKV_WIDTH // 3, HEAD_DIM ** -0.5, 1.0).astype(F32)
            w_qkv = (w[:, :QKV_WIDTH] * col_scale).astype(BF16)
            w_u = w[:, U_OFF:GH_OFF].astype(BF16)
            w_gate = jnp.concatenate([w[:, GH_OFF:], w[:, GA_OFF:U_OFF]], axis=1).astype(BF16)
            x = _layer(x, rel_bias, pre_norm_g[l], post_norm_g[l], (w_qkv, w_u, w_gate), conv_w[l], conv_b[l],
                       fparams, hyena_skip[l], w_branch_a[l].astype(BF16), w_branch_h[l].astype(BF16),
                       w_out[l].astype(BF16))
        return x

    return (run(x_prompt), run(x_sample))
```

```python
import functools
import math

import numpy as np
import jax
import jax.numpy as jnp
from jax import lax
from jax.experimental import pallas as pl
from jax.experimental.pallas import tpu as pltpu

F32 = jnp.float32
BF16 = jnp.bfloat16

D_MODEL = 1024
EPS = 1e-6
HEAD_DIM = 64
ATTN_GROUPS = ((128, 1), (512, 4), (2048, 16))
N_GROUPS = 3
HEADS_PER_GROUP = 8
ATTN_WIDTH = HEADS_PER_GROUP * HEAD_DIM
HYENA_WIDTH = 1024
FILTER_EMB = 33
N_BANDS = 16
FILTER_HIDDEN = 64
NUM_BUCKETS = 32
MAX_DISTANCE = 1024
NEG_INF = -1e30
QKV_WIDTH = 4608
GA_OFF, U_OFF, GH_OFF = 4608, 5120, 8192
U_WIDTH = 3072
GATE_WIDTH = 3584
G_GH, G_MA, G_MH, G_GA = 0, 1024, 2048, 3072

WIN = 64
LANES = 128
DFT_N2 = 128
PITCH_Z = 136
PITCH_S = 264
VMEM_LIMIT = 48 * 1024 * 1024
CONV_VMEM_LIMIT = 58 * 1024 * 1024


def _cparams(sem):
    return pltpu.CompilerParams(dimension_semantics=sem, vmem_limit_bytes=VMEM_LIMIT)


def _inproj_kernel(x_ref, g_ref, w_ref, o_ref, hn_ref, *, slabs):
    @pl.when(pl.program_id(1) == 0)
    def _():
        x = x_ref[...]
        ms = jnp.mean(x * x, axis=-1, keepdims=True)
        hn_ref[...] = (x * lax.rsqrt(ms + EPS) * g_ref[...]).astype(BF16)

    acc = jnp.dot(hn_ref[...], w_ref[...], preferred_element_type=F32)
    if slabs:
        for s in range(slabs):
            o_ref[s] = _pack_bf16_pair(acc[:, (2 * s) * LANES:(2 * s + 1) * LANES],
                                       acc[:, (2 * s + 1) * LANES:(2 * s + 2) * LANES])
    else:
        o_ref[...] = acc.astype(o_ref.dtype)


def _pack_bf16_pair(a, b):
    ua = lax.bitcast_convert_type(a.astype(BF16).astype(F32), jnp.uint32)
    ub = lax.bitcast_convert_type(b.astype(BF16).astype(F32), jnp.uint32)
    return (ua >> 16) | (ub & jnp.uint32(0xFFFF0000))


def _unpack_bf16_pair(w, idx, dtype=BF16):
    bits = (w << 16) if idx == 0 else (w & jnp.uint32(0xFFFF0000))
    return lax.bitcast_convert_type(bits, F32).astype(dtype)


def _inproj(x2d, g, w_bf16, tn, slab_out, tm=1024):
    rows = x2d.shape[0]
    width = w_bf16.shape[1]
    tm = min(tm, rows)
    if slab_out:
        ns = tn // (2 * LANES)
        out_spec = pl.BlockSpec((ns, tm, LANES), lambda i, j: (j, i, 0))
        out_shape = jax.ShapeDtypeStruct((width // (2 * LANES), rows, LANES), jnp.uint32)
    else:
        ns = 0
        out_spec = pl.BlockSpec((tm, tn), lambda i, j: (i, j))
        out_shape = jax.ShapeDtypeStruct((rows, width), BF16)
    return pl.pallas_call(
        functools.partial(_inproj_kernel, slabs=ns),
        grid=(rows // tm, width // tn),
        in_specs=[
            pl.BlockSpec((tm, D_MODEL), lambda i, j: (i, 0)),
            pl.BlockSpec((1, D_MODEL), lambda i, j: (0, 0)),
            pl.BlockSpec((D_MODEL, tn), lambda i, j: (0, j)),
        ],
        out_specs=out_spec,
        out_shape=out_shape,
        scratch_shapes=[pltpu.VMEM((tm, D_MODEL), BF16)],
        compiler_params=_cparams(("parallel", "arbitrary")),
        name="inproj_qkv" if slab_out else "inproj_u",
    )(x2d, g, w_bf16)


def _t5_bucket_np(rel):
    half = NUM_BUCKETS // 2
    max_exact = half // 2
    n = np.abs(rel)
    nf = np.maximum(n, 1).astype(np.float64)
    large = max_exact + (np.log(nf / max_exact) / math.log(MAX_DISTANCE / max_exact) * (half - max_exact)).astype(np.int64)
    large = np.minimum(large, half - 1)
    return np.where(rel > 0, half, 0) + np.where(n < max_exact, n, large)


def _bias_mask(rel_bias_g, dilation):
    qi = np.arange(2 * WIN)[:, None]
    kj = np.arange(4 * WIN)[None, :] - WIN
    delta = kj - qi
    bucket = _t5_bucket_np(delta * dilation).astype(np.int32)
    onehot = jax.nn.one_hot(jnp.asarray(bucket), NUM_BUCKETS, dtype=F32)
    bias = jnp.einsum('qkb,bh->hqk', onehot, rel_bias_g.astype(F32), precision=lax.Precision.HIGHEST)
    bias = jnp.where(jnp.asarray(np.abs(delta) <= WIN)[None], bias, NEG_INF)
    return bias.reshape(HEADS_PER_GROUP // 2, 2, 2 * WIN, 4 * WIN)


ATTN_POS_PER_STEP = 4096
ATTN_TILES_PER_ITER = 8
PAIRS_PER_SLAB = 2


def _attn_kernel(q_ref, kp_ref, kc_ref, kn_ref, vp_ref, vc_ref, vn_ref, bm_ref, o_ref, l_ref,
                 kbuf, vbuf, *, tq, dil, sub_len):
    t = pl.program_id(2)
    halo = WIN * dil
    span = tq * dil
    kbuf[0:halo] = kp_ref[...]
    kbuf[halo:halo + span] = kc_ref[...]
    kbuf[halo + span:] = kn_ref[...]
    vbuf[0:halo] = vp_ref[...]
    vbuf[halo:halo + span] = vc_ref[...]
    vbuf[halo + span:] = vn_ref[...]
    lane = lax.broadcasted_iota(jnp.int32, (1, LANES), 1)
    is_lo = lane < HEAD_DIM
    kcol = lax.broadcasted_iota(jnp.int32, (1, 4 * WIN), 1)
    qt = 2 * WIN
    nq = tq // qt

    def tile(c, i):
        row0 = i * (qt * dil) + c
        kpos = t * tq + i * qt - WIN + kcol
        pen = jnp.where((kpos >= 0) & (kpos < sub_len), 0.0, NEG_INF).astype(F32)
        qw = q_ref[pl.ds(row0, qt, stride=dil), :]
        kw = kbuf[pl.ds(row0, 2 * qt, stride=dil), :]
        vw = vbuf[pl.ds(row0, 2 * qt, stride=dil), :]
        o_pair = []
        for hp in range(PAIRS_PER_SLAB):
            q = _unpack_bf16_pair(qw, hp)
            k = _unpack_bf16_pair(kw, hp)
            v = _unpack_bf16_pair(vw, hp)
            outs, lses = [], []
            for hh in range(2):
                sel = is_lo if hh == 0 else jnp.logical_not(is_lo)
                qm = jnp.where(sel, q, jnp.zeros_like(q))
                s = lax.dot_general(qm, k, (((1,), (1,)), ((), ())), preferred_element_type=F32)
                s = s + bm_ref[hp, hh] + pen
                m = jnp.max(s, axis=-1, keepdims=True)
                p = jnp.exp(s - m)
                den = jnp.sum(p, axis=-1, keepdims=True)
                pv = jnp.dot(p.astype(BF16), v, preferred_element_type=F32)
                outs.append(pv / den)
                lses.append(m + jnp.log(den))
            o_pair.append(jnp.where(is_lo, outs[0], outs[1]))
            l_ref[hp, pl.ds(row0, qt, stride=dil), :] = jnp.where(is_lo, lses[0], lses[1])
        o_ref[pl.ds(row0, qt, stride=dil), :] = _pack_bf16_pair(o_pair[0], o_pair[1])

    def body(it, carry):
        for u in range(ATTN_TILES_PER_ITER):
            idx = it * ATTN_TILES_PER_ITER + u
            tile(idx // nq, idx % nq)
        return carry

    lax.fori_loop(0, dil * nq // ATTN_TILES_PER_ITER, body, 0)


def _attention_group(qkv, rel_bias, g, batch, seq):
    _, dil = ATTN_GROUPS[g]
    sub_len = seq // dil
    tq = min(ATTN_POS_PER_STEP // dil, sub_len)
    nt = sub_len // tq
    span = tq * dil
    halo = WIN * dil
    n_hp = ATTN_WIDTH // LANES
    n_ps = n_hp // PAIRS_PER_SLAB
    n_slab = N_GROUPS * n_ps
    bm = _bias_mask(rel_bias[:, g * HEADS_PER_GROUP:(g + 1) * HEADS_PER_GROUP], dil)
    hb = tq // WIN
    nhb = sub_len // WIN

    def cur(which):
        return pl.BlockSpec((None, span, LANES), lambda b, ps, t: (which * n_slab + g * n_ps + ps, b * nt + t, 0))

    def prev(which):
        return pl.BlockSpec((None, halo, LANES),
                            lambda b, ps, t: (which * n_slab + g * n_ps + ps, b * nhb + jnp.maximum(t * hb - 1, 0), 0))

    def nxt(which):
        return pl.BlockSpec((None, halo, LANES),
                            lambda b, ps, t: (which * n_slab + g * n_ps + ps, b * nhb + jnp.minimum((t + 1) * hb, nhb - 1), 0))

    out_spec = pl.BlockSpec((PAIRS_PER_SLAB, span, LANES), lambda b, ps, t: (ps, b * nt + t, 0))
    return pl.pallas_call(
        functools.partial(_attn_kernel, tq=tq, dil=dil, sub_len=sub_len),
        grid=(batch, n_ps, nt),
        in_specs=[cur(0), prev(1), cur(1), nxt(1), prev(2), cur(2), nxt(2),
                  pl.BlockSpec((PAIRS_PER_SLAB, 2, 2 * WIN, 4 * WIN), lambda b, ps, t: (ps, 0, 0, 0))],
        out_specs=[pl.BlockSpec((None, span, LANES), lambda b, ps, t: (ps, b * nt + t, 0)), out_spec],
        out_shape=[jax.ShapeDtypeStruct((n_ps, batch * seq, LANES), jnp.uint32),
                   jax.ShapeDtypeStruct((n_hp, batch * seq, LANES), F32)],
        scratch_shapes=[pltpu.VMEM((span + 2 * halo, LANES), jnp.uint32)] * 2,
        compiler_params=_cparams(("parallel", "parallel", "arbitrary")),
        name=f"attn_g{g}",
    )(qkv, qkv, qkv, qkv, qkv, qkv, qkv, bm)


def _shortconv_kernel(p_ref, c_ref, n_ref, w_ref, b_ref, o_ref, x_buf, *, rows, halo):
    t = pl.program_id(1)
    nt = pl.num_programs(1)
    has_prev = (t > 0).astype(F32)
    has_next = (t < nt - 1).astype(F32)
    n_slab = c_ref.shape[2] // LANES
    for e in range(2):
        x = c_ref[e].astype(F32)
        lo = p_ref[e].astype(F32)[halo - 8:halo, :] * has_prev
        hi = n_ref[e].astype(F32)[0:8, :] * has_next
        for s in range(n_slab):
            cs = slice(s * LANES, (s + 1) * LANES)
            x_buf[e, s, 0:8, :] = lo[:, cs]
            x_buf[e, s, 8:rows + 8, :] = x[:, cs]
            x_buf[e, s, rows + 8:rows + 16, :] = hi[:, cs]
    nblk = rows // DFT_N2
    pad = jnp.zeros((PITCH_Z - DFT_N2, LANES), jnp.uint32)
    for s in range(n_slab):
        cs = slice(s * LANES, (s + 1) * LANES)
        w0, w1, w2, bias = w_ref[0:1, cs], w_ref[1:2, cs], w_ref[2:3, cs], b_ref[:, cs]
        res = []
        for e in range(2):
            xs = x_buf.at[e, s]
            res.append(xs[pl.ds(7, rows, stride=1), :] * w0 + xs[pl.ds(8, rows, stride=1), :] * w1
                       + xs[pl.ds(9, rows, stride=1), :] * w2 + bias)
        packed = _pack_bf16_pair(res[0], res[1])
        for jb in range(nblk):
            o_ref[s, jb * PITCH_Z:jb * PITCH_Z + DFT_N2, :] = packed[jb * DFT_N2:(jb + 1) * DFT_N2]
            o_ref[s, jb * PITCH_Z + DFT_N2:(jb + 1) * PITCH_Z, :] = pad


def _store_padded_slabs(o_ref, val, nblk):
    pad = jnp.zeros((PITCH_Z - DFT_N2, LANES), val.dtype)
    for s in range(val.shape[1] // LANES):
        for jb in range(nblk):
            o_ref[s, jb * PITCH_Z:jb * PITCH_Z + DFT_N2, :] = val[jb * DFT_N2:(jb + 1) * DFT_N2, s * LANES:(s + 1) * LANES]
            o_ref[s, jb * PITCH_Z + DFT_N2:(jb + 1) * PITCH_Z, :] = pad


def _load_padded_slabs(ref, nblk):
    return jnp.concatenate(
        [jnp.concatenate([ref[s, jb * PITCH_Z:jb * PITCH_Z + DFT_N2, :] for jb in range(nblk)], axis=0)
         for s in range(ref.shape[0])], axis=1)


def _shortconv(proj, conv_w, conv_b, batch, seq, rows=1024):
    assert batch % 2 == 0
    rows = min(rows, seq)
    prow = rows // DFT_N2 * PITCH_Z
    n_slab = HYENA_WIDTH // LANES
    halo = 16
    pv = proj.reshape(batch, seq, U_WIDTH)
    cb = 0
    hb = rows // halo
    nhb = seq // halo
    return pl.pallas_call(
        functools.partial(_shortconv_kernel, rows=rows, halo=halo),
        grid=(batch // 2, seq // rows, 3),
        in_specs=[
            pl.BlockSpec((2, halo, HYENA_WIDTH), lambda b, t, j: (b, jnp.maximum(t * hb - 1, 0), cb + j)),
            pl.BlockSpec((2, rows, HYENA_WIDTH), lambda b, t, j: (b, t, cb + j)),
            pl.BlockSpec((2, halo, HYENA_WIDTH), lambda b, t, j: (b, jnp.minimum((t + 1) * hb, nhb - 1), cb + j)),
            pl.BlockSpec((3, HYENA_WIDTH), lambda b, t, j: (0, j)),
            pl.BlockSpec((1, HYENA_WIDTH), lambda b, t, j: (0, j)),
        ],
        out_specs=pl.BlockSpec((None, n_slab, None, prow, LANES), lambda b, t, j: (j, 0, b, t, 0)),
        out_shape=jax.ShapeDtypeStruct((3, n_slab, batch // 2, seq // DFT_N2 * PITCH_Z, LANES), jnp.uint32),
        scratch_shapes=[pltpu.VMEM((2, n_slab, rows + 16, LANES), F32)],
        compiler_params=_cparams(("parallel", "parallel", "arbitrary")),
        name="shortconv",
    )(pv, pv, pv, conv_w, conv_b.reshape(1, -1))


def _filter_features(length):
    t = np.linspace(0.0, 1.0, length)[:, None]
    ang = (2.0 * math.pi / length) * np.arange(length, dtype=np.float64)[:, None]
    bands = np.linspace(1e-4, N_BANDS - 1, N_BANDS)[None]
    z = np.concatenate([t, np.cos(ang * bands), -np.sin(ang * bands)], axis=-1)
    zp = np.zeros((length, LANES), np.float32)
    zp[:, :FILTER_EMB] = z
    return zp


def _decay_rates():
    max_decay = math.log(1e-2) / 0.3
    min_decay = math.log(1e-2) / 1.5
    return np.abs(np.linspace(min_decay, max_decay, HYENA_WIDTH)).astype(np.float32)[None]


def _filter_kernel(z_ref, w1_ref, b1_ref, w2_ref, b2_ref, w3_ref, b3_ref, w4_ref, fr_ref, dl_ref, o_ref, *, rows):
    hi = lax.Precision.HIGHEST
    z = z_ref[...]
    fr = fr_ref[...]
    hr = rows // 2
    pre = jnp.concatenate([jnp.dot(z[:hr], w1_ref[...], precision=hi, preferred_element_type=F32),
                           jnp.dot(z[hr:], w1_ref[...], precision=hi, preferred_element_type=F32)], axis=1)
    h = jnp.sin(fr * (pre + b1_ref[...]))
    h = jnp.sin(fr * (jnp.dot(h, w2_ref[...], precision=hi, preferred_element_type=F32) + b2_ref[...]))
    h = jnp.sin(fr * (jnp.dot(h, w3_ref[...], precision=hi, preferred_element_type=F32) + b3_ref[...]))
    h = jnp.concatenate([h[:, :FILTER_HIDDEN], h[:, FILTER_HIDDEN:]], axis=0)
    hh, hl = _split(h)
    filt = jnp.dot(jnp.concatenate([hh, hl, hh], axis=1), w4_ref[...], preferred_element_type=F32)
    decay = jnp.exp(-z[:, 0:1] * dl_ref[...])
    row = pl.program_id(0) * rows + lax.broadcasted_iota(jnp.int32, (rows, 1), 0)
    for j in range(4):
        cs = slice(j * HYENA_WIDTH, (j + 1) * HYENA_WIDTH)
        val = filt[:, cs] * decay
        if j % 2 == 1:
            val = jnp.where(row == 0, 0.0, val)
        _store_padded_slabs(o_ref.at[j], val, rows // DFT_N2)


def _filters(length, fw1, fb1, fw2, fb2, fw3, fb3, fw4, ffreq, rows=1024):
    rows = min(rows, length)
    n_slab = HYENA_WIDTH // LANES
    prow = rows // DFT_N2 * PITCH_Z
    w4h, w4l = _split(fw4.astype(F32))
    w4s = jnp.concatenate([w4h, w4h, w4l], axis=0)
    zfeat = jnp.asarray(_filter_features(length))
    w1p = jnp.zeros((LANES, FILTER_HIDDEN), F32).at[:FILTER_EMB].set(fw1.astype(F32))
    twice = lambda v: jnp.tile(v.astype(F32).reshape(1, -1), (1, 2))
    zero = jnp.zeros((FILTER_HIDDEN, FILTER_HIDDEN), F32)
    bdiag = lambda w: jnp.block([[w.astype(F32), zero], [zero, w.astype(F32)]])
    full = lambda shape: pl.BlockSpec(shape, lambda i: (0,) * len(shape))
    return pl.pallas_call(
        functools.partial(_filter_kernel, rows=rows),
        grid=(length // rows,),
        in_specs=[pl.BlockSpec((rows, LANES), lambda i: (i, 0)),
                  full((LANES, FILTER_HIDDEN)), full((1, LANES)),
                  full((LANES, LANES)), full((1, LANES)),
                  full((LANES, LANES)), full((1, LANES)),
                  full((3 * FILTER_HIDDEN, 4 * HYENA_WIDTH)), full((1, LANES)),
                  full((1, HYENA_WIDTH))],
        out_specs=pl.BlockSpec((4, n_slab, prow, LANES), lambda i: (0, 0, i, 0)),
        out_shape=jax.ShapeDtypeStruct((4, n_slab, length // DFT_N2 * PITCH_Z, LANES), F32),
        compiler_params=_cparams(("parallel",)),
        name="hyena_filters",
    )(zfeat, w1p, twice(fb1), bdiag(fw2), twice(fb2), bdiag(fw3), twice(fb3), w4s,
      twice(ffreq), jnp.asarray(_decay_rates()))


def _dft_tables(length):
    n = 2 * length
    n1 = n // DFT_N2
    h1 = n1 // 2
    kk = np.arange(h1)[:, None] + 0.5
    th = 2.0 * math.pi * kk * np.arange(h1)[None, :] / n1
    fa = np.concatenate([np.cos(th), -np.sin(th)], axis=0)
    ga = (2.0 / n) * fa.T
    k = np.arange(h1)[:, None, None] + n1 * np.arange(DFT_N2)[None, :, None] + 0.5
    ph = 2.0 * math.pi * k * np.arange(DFT_N2)[None, None, :] / n
    f = lambda a: jnp.asarray(a.astype(np.float32))
    er, ei = f(np.cos(ph)), f(-np.sin(ph))
    m = jnp.concatenate([jnp.concatenate([er, -ei], axis=2), jnp.concatenate([ei, er], axis=2)], axis=1)
    return f(fa), f(ga), m.astype(BF16)


def _split(a):
    hi = a.astype(BF16)
    lo = (a - hi.astype(F32)).astype(BF16)
    return hi, lo


CONV_K1_CHUNK = 64
CONV_PAIR_UNROLL = 64
CONV_FREQ_UNROLL = 32


def _resident_table_spec(h1):
    return pl.BlockSpec((h1, 2 * DFT_N2, 2 * DFT_N2), lambda *_: (0, 0, 0), pipeline_mode=pl.Buffered(1))


def _spectrum_kernel(fa_ref, m_ref, xf_ref, xb_ref, sk_ref, o_ref, s_ref, *, h1, kc):
    hf = pl.program_id(2)
    half = DFT_N2

    @pl.when(hf == 0)
    def _():
        fa = fa_ref[...]

        def body(n2, carry):
            x = jnp.concatenate([xf_ref[pl.ds(n2, h1, stride=PITCH_Z), :],
                                 xb_ref[pl.ds(n2, h1, stride=PITCH_Z), :]], axis=1).astype(BF16)
            a = jnp.dot(fa, x, preferred_element_type=F32)
            aw = _pack_bf16_pair(a[:, :LANES], a[:, LANES:])
            for ri in range(2):
                s_ref[pl.ds(ri * half + n2, h1, stride=PITCH_S), :] = aw[ri * h1:(ri + 1) * h1]
            return carry

        lax.fori_loop(0, DFT_N2, body, 0, unroll=CONV_PAIR_UNROLL)

    def freq_body(k1, carry):
        row = pl.multiple_of((hf * kc + k1) * PITCH_S, 8)
        xw = s_ref[pl.ds(row, 2 * half), :]
        x = jnp.concatenate([_unpack_bf16_pair(xw, 0), _unpack_bf16_pair(xw, 1)], axis=1)
        z = jnp.dot(m_ref[hf * kc + k1], x, preferred_element_type=F32)
        o_ref[k1] = _pack_bf16_pair(z[:half, :LANES] + z[:half, LANES:] + sk_ref[...],
                                    z[half:, :LANES] - z[half:, LANES:])
        return carry

    lax.fori_loop(0, kc, freq_body, 0, unroll=CONV_FREQ_UNROLL)


def _filter_spectrum(fa, m_bf16, filt, skip4):
    _, n_slab, prow, _ = filt.shape
    n1, h1 = fa.shape
    kc = min(CONV_K1_CHUNK, h1)
    return pl.pallas_call(
        functools.partial(_spectrum_kernel, h1=h1, kc=kc),
        grid=(2, n_slab, h1 // kc),
        in_specs=[pl.BlockSpec((n1, h1), lambda o, s, hf: (0, 0)),
                  _resident_table_spec(h1),
                  pl.BlockSpec((None, None, prow, LANES), lambda o, s, hf: (2 * o, s, 0, 0)),
                  pl.BlockSpec((None, None, prow, LANES), lambda o, s, hf: (2 * o + 1, s, 0, 0)),
                  pl.BlockSpec((None, None, 1, LANES), lambda o, s, hf: (o, s, 0, 0))],
        out_specs=pl.BlockSpec((None, None, kc, DFT_N2, LANES), lambda o, s, hf: (o, s, hf, 0, 0)),
        out_shape=jax.ShapeDtypeStruct((2, n_slab, h1, DFT_N2, LANES), jnp.uint32),
        scratch_shapes=[pltpu.VMEM((h1 * PITCH_S, LANES), jnp.uint32)],
        compiler_params=pltpu.CompilerParams(dimension_semantics=("parallel", "parallel", "arbitrary"),
                                             vmem_limit_bytes=CONV_VMEM_LIMIT),
        name="filter_spectrum",
    )(fa.astype(BF16), m_bf16, filt, filt, skip4)


def _conv_kernel(fa_ref, ga_ref, m_ref, k_ref, z_ref, hx_ref, o_ref, s_ref, *, h1):
    half = DFT_N2
    pairs = DFT_N2 // 2
    fa = fa_ref[...]
    ga = ga_ref[...]

    def stage_a(j, carry):
        n2 = 2 * j
        zw = [z_ref[pl.ds(n2 + u, h1, stride=PITCH_Z), :] for u in range(2)]
        a = [jnp.dot(fa, jnp.concatenate([_unpack_bf16_pair(zw[0], e), _unpack_bf16_pair(zw[1], e)], axis=1),
                     preferred_element_type=F32) for e in range(2)]
        aw = _pack_bf16_pair(a[0], a[1])
        for ri in range(2):
            for u in range(2):
                s_ref[pl.ds(ri * half + n2 + u, h1, stride=PITCH_S), :] = (
                    aw[ri * h1:(ri + 1) * h1, u * LANES:(u + 1) * LANES])
        return carry

    lax.fori_loop(0, pairs, stage_a, 0, unroll=CONV_PAIR_UNROLL)

    def freq_body(k1, carry):
        row = pl.multiple_of(k1 * PITCH_S, 8)
        xw = s_ref[pl.ds(row, 2 * half), :]
        x = jnp.concatenate([_unpack_bf16_pair(xw, 0), _unpack_bf16_pair(xw, 1)], axis=1)
        m = m_ref[k1]
        z = jnp.dot(m, x, preferred_element_type=F32)
        zr, zi = z[:half], z[half:]
        kw = k_ref[k1]
        kr = jnp.concatenate([_unpack_bf16_pair(kw, 0, F32)] * 2, axis=1)
        ki = jnp.concatenate([_unpack_bf16_pair(kw, 1, F32)] * 2, axis=1)
        y = jnp.concatenate([zr * kr - zi * ki, zr * ki + zi * kr], axis=0).astype(BF16)
        bq = lax.dot_general(m, y, (((0,), (0,)), ((), ())), preferred_element_type=F32)
        s_ref[pl.ds(row, 2 * half), :] = _pack_bf16_pair(bq[:, :LANES], bq[:, LANES:])
        return carry

    lax.fori_loop(0, h1, freq_body, 0, unroll=CONV_FREQ_UNROLL)

    def stage_inv(j, carry):
        n2 = 2 * j
        rw = [s_ref[pl.ds(n2 + u, h1, stride=PITCH_S), :] for u in range(2)]
        iw = [s_ref[pl.ds(half + n2 + u, h1, stride=PITCH_S), :] for u in range(2)]
        hw = [hx_ref[pl.ds(n2 + u, h1, stride=PITCH_Z), :] for u in range(2)]
        y = []
        for e in range(2):
            bn = jnp.concatenate(
                [jnp.concatenate([_unpack_bf16_pair(rw[u], e), _unpack_bf16_pair(iw[u], e)], axis=0)
                 for u in range(2)], axis=1)
            y.append(jnp.dot(ga, bn, preferred_element_type=F32))
        for u in range(2):
            res = [_unpack_bf16_pair(hw[u], e, F32) * y[e][:, u * LANES:(u + 1) * LANES] for e in range(2)]
            o_ref[pl.ds(n2 + u, h1, stride=PITCH_Z), :] = _pack_bf16_pair(res[0], res[1])
        return carry

    lax.fori_loop(0, pairs, stage_inv, 0, unroll=CONV_PAIR_UNROLL // 2)
    pad = jnp.zeros((PITCH_Z - DFT_N2, LANES), jnp.uint32)
    for blk in range(h1):
        o_ref[blk * PITCH_Z + DFT_N2:(blk + 1) * PITCH_Z, :] = pad


def _long_conv_gate(tabs, kspec, order, z5, zsel, hx5, hxsel):
    fa, ga, m_tab = tabs
    _, n_slab, n_pair, prow, _ = z5.shape
    n1, h1 = fa.shape
    slab = lambda sel: pl.BlockSpec((None, None, None, prow, LANES), lambda s, b: (sel, s, b, 0, 0))
    return pl.pallas_call(
        functools.partial(_conv_kernel, h1=h1),
        grid=(n_slab, n_pair),
        in_specs=[pl.BlockSpec((n1, h1), lambda s, b: (0, 0)),
                  pl.BlockSpec((h1, n1), lambda s, b: (0, 0)),
                  _resident_table_spec(h1),
                  pl.BlockSpec((None, None, h1, DFT_N2, LANES), lambda s, b: (order, s, 0, 0, 0)),
                  slab(zsel), slab(hxsel)],
        out_specs=pl.BlockSpec((None, None, prow, LANES), lambda s, b: (s, b, 0, 0)),
        out_shape=jax.ShapeDtypeStruct((n_slab, n_pair, prow, LANES), jnp.uint32),
        scratch_shapes=[pltpu.VMEM((h1 * PITCH_S, LANES), jnp.uint32)],
        compiler_params=pltpu.CompilerParams(dimension_semantics=("parallel", "arbitrary"),
                                             vmem_limit_bytes=CONV_VMEM_LIMIT),
        name="hyena_long_conv",
    )(fa.astype(BF16), ga.astype(BF16), m_tab, kspec, z5, hx5)


def _hyena(proj, batch, seq, conv_w, conv_b, fparams, skip):
    c = HYENA_WIDTH
    fa, ga, m_tab = _dft_tables(seq)
    skip4 = skip.astype(F32).reshape(2, c // LANES, 1, LANES)
    kspec = _filter_spectrum(fa, m_tab, _filters(seq, *fparams), skip4)
    u5 = _shortconv(proj, conv_w, conv_b, batch, seq)
    tabs = (fa, ga, m_tab)
    z1 = _long_conv_gate(tabs, kspec, 0, u5, 0, u5, 1)
    return _long_conv_gate(tabs, kspec, 1, z1[None], 0, u5, 2)


def _final_kernel(o1_ref, o2_ref, o3_ref, l1_ref, l2_ref, l3_ref, hy_ref, x_ref, g_ref, wg_ref,
                  wa_ref, wh_ref, wo_ref, pg_ref, y_ref, *, tiles_per_seq):
    def cat(ref):
        return jnp.concatenate([ref[s] for s in range(ATTN_WIDTH // LANES)], axis=1)

    def cat_packed(ref):
        return jnp.concatenate([_unpack_bf16_pair(ref[s], e, F32)
                                for s in range(ref.shape[0]) for e in range(PAIRS_PER_SLAB)], axis=1)

    l1, l2, l3 = cat(l1_ref), cat(l2_ref), cat(l3_ref)
    mx = jnp.maximum(jnp.maximum(l1, l2), l3)
    e1, e2, e3 = jnp.exp(l1 - mx), jnp.exp(l2 - mx), jnp.exp(l3 - mx)
    attn = (e1 * cat_packed(o1_ref) + e2 * cat_packed(o2_ref) + e3 * cat_packed(o3_ref)) / (e1 + e2 + e3)
    x = x_ref[...]
    hn = (x * lax.rsqrt(jnp.mean(x * x, axis=-1, keepdims=True) + EPS) * g_ref[...]).astype(BF16)
    gates = jnp.dot(hn, wg_ref[...], preferred_element_type=F32)
    gh, ga = gates[:, G_GH:G_MA], gates[:, G_GA:]
    a_in = (attn * (ga * jax.nn.sigmoid(ga))).astype(BF16)
    a_br = jnp.dot(a_in, wa_ref[...], preferred_element_type=F32)
    odd = (pl.program_id(0) // tiles_per_seq) % 2
    hw = _load_padded_slabs(hy_ref, x.shape[0] // DFT_N2)
    hy = lax.bitcast_convert_type((hw << ((1 - odd) * 16).astype(jnp.uint32)) & jnp.uint32(0xFFFF0000), F32)
    h_in = (hy * (gh * jax.nn.sigmoid(gh))).astype(BF16)
    h_br = jnp.dot(h_in, wh_ref[...], preferred_element_type=F32)
    merged = jax.nn.sigmoid(gates[:, G_MA:G_MH]) * a_br + jax.nn.sigmoid(gates[:, G_MH:G_GA]) * h_br
    out = jnp.dot(merged.astype(BF16), wo_ref[...], preferred_element_type=F32)
    ms = jnp.mean(out * out, axis=-1, keepdims=True)
    y_ref[...] = x + out * lax.rsqrt(ms + EPS) * pg_ref[...]


def _final(os_, ls_, hy, x2d, gain, wg, wa, wh, wo, pg, seq, tm=512):
    rows = x2d.shape[0]
    tm = min(tm, rows)
    tiles_per_seq = seq // tm
    hy_spec = pl.BlockSpec((HYENA_WIDTH // LANES, None, tm // DFT_N2 * PITCH_Z, LANES),
                           lambda i: (0, i // (2 * tiles_per_seq), i % tiles_per_seq, 0))
    slab = pl.BlockSpec((ATTN_WIDTH // LANES, tm, LANES), lambda i: (0, i, 0))
    oslab = pl.BlockSpec((ATTN_WIDTH // LANES // PAIRS_PER_SLAB, tm, LANES), lambda i: (0, i, 0))
    full = lambda shape: pl.BlockSpec(shape, lambda i: (0, 0), pipeline_mode=pl.Buffered(1))
    return pl.pallas_call(
        functools.partial(_final_kernel, tiles_per_seq=tiles_per_seq),
        grid=(rows // tm,),
        in_specs=[oslab] * 3 + [slab] * 3 + [hy_spec, pl.BlockSpec((tm, D_MODEL), lambda i: (i, 0)),
                                             full((1, D_MODEL)), full((D_MODEL, GATE_WIDTH)),
                                             full((ATTN_WIDTH, D_MODEL)), full((D_MODEL, D_MODEL)),
                                             full((D_MODEL, D_MODEL)), full((1, D_MODEL))],
        out_specs=pl.BlockSpec((tm, D_MODEL), lambda i: (i, 0)),
        out_shape=jax.ShapeDtypeStruct((rows, D_MODEL), F32),
        compiler_params=_cparams(("parallel",)),
        name="merge_out",
    )(*os_, *ls_, hy, x2d, gain, wg, wa, wh, wo, pg)


def _layer(x, rel_bias, pre_g, post_g, w_in, conv_w, conv_b, fparams, skip, w_br_a, w_br_h, w_out):
    batch, seq, _ = x.shape
    x2d = x.reshape(batch * seq, D_MODEL)
    w_qkv, w_u, w_gate = w_in
    gain = pre_g.reshape(1, -1).astype(F32)
    qkv = _inproj(x2d, gain, w_qkv, tn=QKV_WIDTH // 2, slab_out=True)
    proj = _inproj(x2d, gain, w_u, tn=U_WIDTH, slab_out=False)
    os_, ls_ = [], []
    for g in range(N_GROUPS):
        o, l = _attention_group(qkv, rel_bias, g, batch, seq)
        os_.append(o)
        ls_.append(l)
    hy = _hyena(proj, batch, seq, conv_w, conv_b, fparams, skip)
    y = _final(os_, ls_, hy, x2d, gain, w_gate, w_br_a, w_br_h, w_out, post_g.reshape(1, -1).astype(F32), seq)
    return y.reshape(batch, seq, D_MODEL)


def kernel(x_prompt, x_sample, rel_bias, pre_norm_g, post_norm_g, w_in, conv_w, conv_b, filt_w1, filt_b1, filt_w2, filt_b2, filt_w3, filt_b3, filt_w4, filt_freq, hyena_skip, w_branch_a, w_branch_h, w_out):
    depth = w_in.shape[0]

    def run(x):
        for l in range(depth):
            fparams = (filt_w1[l], filt_b1[l], filt_w2[l], filt_b2[l], filt_w3[l], filt_b3[l], filt_w4[l], filt_freq[l])
            w = w_in[l]
            col_scale = jnp.where(jnp.arange(QKV_WIDTH) < QKV_WIDTH // 3, HEAD_DIM ** -0.5, 1.0).astype(F32)
            w_qkv = (w[:, :QKV_WIDTH] * col_scale).astype(BF16)
            w_u = w[:, U_OFF:GH_OFF].astype(BF16)
            w_gate = jnp.concatenate([w[:, GH_OFF:], w[:, GA_OFF:U_OFF]], axis=1).astype(BF16)
            x = _layer(x, rel_bias, pre_norm_g[l], post_norm_g[l], (w_qkv, w_u, w_gate), conv_w[l], conv_b[l],
                       fparams, hyena_skip[l], w_branch_a[l].astype(BF16), w_branch_h[l].astype(BF16),
                       w_out[l].astype(BF16))
        return x

    return (run(x_prompt), run(x_sample))
```
